```python
import jax, jax.numpy as jnp
from jax import lax
import numpy as np

D_MODEL = 1024
BATCH = 2
SEQ = 8192
DEPTH = 1
DEC_BATCH = 128
DEC_SEQ = 4
PAST_LEN = 8192
PAGE_SIZE = 128

HEAD_DIM = 64
CONV_W = D_MODEL // 4
ATTN_W = D_MODEL // 2
MEM_W = D_MODEL // 4
ATTN_HEADS = ATTN_W // HEAD_DIM
MEM_HEADS = MEM_W // HEAD_DIM
CONV_GROUPS = CONV_W // HEAD_DIM
D_MIX = CONV_W + ATTN_W + MEM_W
CONV_WIDTH = 3
N_MEM = 256
DILATED_CONFIGS = ((128, 1), (512, 4), (2048, 16))
WINDOW_MAX = 2048
ATTN_BLOCK = 128
ALIBI_MAX = 8.0
RMS_EPS = 1e-6
SPLITS = (CONV_W, CONV_W, CONV_W, ATTN_W, ATTN_W, ATTN_W, MEM_W, D_MIX)
D_IN = sum(SPLITS)

kernel_name = "hymba_conv_dilated_alibi_memxattn_step"

F32 = jnp.float32


def _rmsnorm(x, g):
    xf = x.astype(F32)
    r = lax.rsqrt(jnp.mean(xf * xf, axis=-1, keepdims=True) + RMS_EPS)
    return (xf * r * g.astype(F32)).astype(x.dtype)


def _alibi_slopes():
    h = jnp.arange(ATTN_HEADS, dtype=F32) + 1.0
    return jnp.exp2(-ALIBI_MAX * h / ATTN_HEADS)


def _project(x, g_in, w_in):
    h = _rmsnorm(x, g_in)
    p = h @ w_in
    idx = np.cumsum(SPLITS)[:-1].tolist()
    cb, cc, ch, q, k, v, mq, z = jnp.split(p, idx, axis=-1)
    lead = x.shape[:2]
    q = q.reshape(lead + (ATTN_HEADS, HEAD_DIM))
    k = k.reshape(lead + (ATTN_HEADS, HEAD_DIM))
    v = v.reshape(lead + (ATTN_HEADS, HEAD_DIM))
    mq = mq.reshape(lead + (MEM_HEADS, HEAD_DIM))
    return cb, cc, ch, q, k, v, mq, z


def _conv3(full, w):
    return w[0] * full[:, :-2] + w[1] * full[:, 1:-1] + w[2] * full[:, 2:]


def _dilated_band(q, k, v, win, dil, slopes):
    B, T, H, E = q.shape
    span = win // dil
    L = T // dil
    nb = -(-L // ATTN_BLOCK)
    Lp = nb * ATTN_BLOCK

    def fold(a):
        a = a.reshape(B, L, dil, H, E).transpose(0, 2, 1, 3, 4)
        a = jnp.pad(a, ((0, 0), (0, 0), (0, Lp - L), (0, 0), (0, 0)))
        return a.reshape(B, dil, nb, ATTN_BLOCK, H, E)

    def with_prev(a):
        prev = jnp.pad(a, ((0, 0), (0, 0), (1, 0), (0, 0), (0, 0), (0, 0)))[:, :, :-1]
        return jnp.concatenate([prev, a], axis=3)

    qb = fold(q)
    kk = with_prev(fold(k))
    vv = with_prev(fold(v))
    s = jnp.einsum('brnqhe,brnkhe->brnhqk', qb, kk, preferred_element_type=F32) * (HEAD_DIM ** -0.5)
    qi = jnp.arange(ATTN_BLOCK)[:, None] + ATTN_BLOCK
    ki = jnp.arange(2 * ATTN_BLOCK)[None, :]
    dist = qi - ki
    kglob = jnp.arange(nb)[:, None, None] * ATTN_BLOCK + ki - ATTN_BLOCK
    valid = (dist >= 0) & (dist <= span) & (kglob >= 0)
    bias = -(slopes * dil)[:, None, None] * dist
    s = jnp.where(valid[:, None], s + bias, -jnp.inf)
    m = jnp.max(s, axis=-1, keepdims=True)
    p = jnp.exp(s - m)
    l = jnp.sum(p, axis=-1, keepdims=True)
    o = jnp.einsum('brnhqk,brnkhe->brnqhe', p, vv.astype(F32))
    l_t = jnp.transpose(l[..., 0], (0, 1, 2, 4, 3))[..., None]
    o = o / l_t
    lse = jnp.transpose((m + jnp.log(l))[..., 0], (0, 1, 2, 4, 3))
    o = o.reshape(B, dil, Lp, H, E)[:, :, :L].transpose(0, 2, 1, 3, 4).reshape(B, T, H, E)
    lse = lse.reshape(B, dil, Lp, H)[:, :, :L].transpose(0, 2, 1, 3).reshape(B, T, H)
    return o, lse


def _dilated_gather(q, kfull, vfull, win, dil, slopes, n_buf):
    S = q.shape[1]
    span = win // dil
    j = jnp.arange(S)[:, None]
    kd = jnp.arange(span + 1)[None, :]
    idx = n_buf + j - kd * dil
    valid = idx >= 0
    idx = jnp.maximum(idx, 0)
    kg = kfull[:, idx]
    vg = vfull[:, idx]
    s = jnp.einsum('bshe,bskhe->bhsk', q, kg, preferred_element_type=F32) * (HEAD_DIM ** -0.5)
    s = s - (slopes * dil)[:, None, None] * kd
    s = jnp.where(valid, s, -jnp.inf)
    m = jnp.max(s, axis=-1, keepdims=True)
    p = jnp.exp(s - m)
    l = jnp.sum(p, axis=-1, keepdims=True)
    o = jnp.einsum('bhsk,bskhe->bshe', p, vg.astype(F32))
    o = o / jnp.transpose(l[..., 0], (0, 2, 1))[..., None]
    lse = jnp.transpose((m + jnp.log(l))[..., 0], (0, 2, 1))
    return o, lse


def _merge_by_denominator(outs, lses):
    o = jnp.stack(outs, 0)
    a = jax.nn.softmax(jnp.stack(lses, 0), axis=0)
    return jnp.sum(a[..., None] * o, axis=0)


def _mem_attn(qm, mk, mv):
    s = jnp.einsum('bthe,bmhe->bhtm', qm, mk, preferred_element_type=F32) * (HEAD_DIM ** -0.5)
    p = jax.nn.softmax(s, axis=-1)
    return jnp.einsum('bhtm,bmhe->bthe', p, mv.astype(F32))


def _mem_kv(mem, g_mem, w_mem_kv):
    kv = _rmsnorm(mem, g_mem) @ w_mem_kv
    mk, mv = jnp.split(kv, 2, axis=-1)
    lead = mem.shape[:2]
    return mk.reshape(lead + (MEM_HEADS, HEAD_DIM)), mv.reshape(lead + (MEM_HEADS, HEAD_DIM))


def _finish(x, conv_y, attn_o, mem_o, z, w_out):
    lead = x.shape[:2]
    mix = jnp.concatenate([conv_y,
                           attn_o.reshape(lead + (ATTN_W,)).astype(x.dtype),
                           mem_o.reshape(lead + (MEM_W,)).astype(x.dtype)], axis=-1)
    mix = mix * jax.nn.silu(z)
    return x + mix @ w_out


def setup_inputs(seed: int = 0) -> dict:
    key = jax.random.key(seed)
    ks = jax.random.split(key, 20)
    n_buf = min(WINDOW_MAX, PAST_LEN)
    nrm = jax.random.normal
    return {
        "x_prompt": nrm(ks[0], (BATCH, SEQ, D_MODEL), F32),
        "x_sample": nrm(ks[1], (DEC_BATCH, DEC_SEQ, D_MODEL), F32),
        "mem_prompt": nrm(ks[2], (BATCH, N_MEM, D_MODEL), F32),
        "cache_win_k": nrm(ks[3], (DEPTH, DEC_BATCH, n_buf, ATTN_HEADS, HEAD_DIM), F32),
        "cache_win_v": nrm(ks[4], (DEPTH, DEC_BATCH, n_buf, ATTN_HEADS, HEAD_DIM), F32),
        "cache_conv": nrm(ks[5], (DEPTH, DEC_BATCH, CONV_WIDTH - 1, CONV_W), F32),
        "cache_mem_k": nrm(ks[6], (DEPTH, DEC_BATCH, N_MEM, MEM_HEADS, HEAD_DIM), F32),
        "cache_mem_v": nrm(ks[7], (DEPTH, DEC_BATCH, N_MEM, MEM_HEADS, HEAD_DIM), F32),
        "g_in": 1.0 + 0.02 * nrm(ks[8], (DEPTH, D_MODEL), F32),
        "w_in": nrm(ks[9], (DEPTH, D_MODEL, D_IN), F32) * D_MODEL ** -0.5,
        "conv_w": nrm(ks[10], (DEPTH, CONV_WIDTH, CONV_W), F32) * CONV_WIDTH ** -0.5,
        "g_mem": 1.0 + 0.02 * nrm(ks[11], (DEPTH, D_MODEL), F32),
        "w_mem_kv": nrm(ks[12], (DEPTH, D_MODEL, 2 * MEM_W), F32) * D_MODEL ** -0.5,
        "w_out": nrm(ks[13], (DEPTH, D_MIX, D_MODEL), F32) * D_MIX ** -0.5,
        "g_final": 1.0 + 0.02 * nrm(ks[14], (D_MODEL,), F32),
    }


def reference(x_prompt, x_sample, mem_prompt, cache_win_k, cache_win_v, cache_conv,
              cache_mem_k, cache_mem_v, g_in, w_in, conv_w, g_mem, w_mem_kv, w_out, g_final):
    slopes = _alibi_slopes()
    xp, xs = x_prompt, x_sample
    n_buf = cache_win_k.shape[2]
    p_wk, p_wv, p_cv, p_mk, p_mv, s_wk, s_wv, s_cv = [], [], [], [], [], [], [], []
    for l in range(DEPTH):
        cb, cc, ch, q, k, v, mq, z = _project(xp, g_in[l], w_in[l])
        u = cc * ch
        up = jnp.pad(u, ((0, 0), (CONV_WIDTH - 1, 0), (0, 0)))
        conv_y = cb * _conv3(up, conv_w[l])
        outs, lses = [], []
        for win, dil in DILATED_CONFIGS:
            o, lse = _dilated_band(q, k, v, win, dil, slopes)
            outs.append(o)
            lses.append(lse)
        attn_o = _merge_by_denominator(outs, lses)
        mk, mv = _mem_kv(mem_prompt, g_mem[l], w_mem_kv[l])
        mem_o = _mem_attn(mq, mk, mv)
        n_keep = min(WINDOW_MAX, xp.shape[1])
        p_wk.append(k[:, -n_keep:])
        p_wv.append(v[:, -n_keep:])
        p_cv.append(u[:, -(CONV_WIDTH - 1):])
        p_mk.append(mk)
        p_mv.append(mv)
        xp = _finish(xp, conv_y, attn_o, mem_o, z, w_out[l])

        cb, cc, ch, q, k, v, mq, z = _project(xs, g_in[l], w_in[l])
        u = cc * ch
        full = jnp.concatenate([cache_conv[l].astype(u.dtype), u], axis=1)
        conv_y = cb * _conv3(full, conv_w[l])
        kfull = jnp.concatenate([cache_win_k[l].astype(k.dtype), k], axis=1)
        vfull = jnp.concatenate([cache_win_v[l].astype(v.dtype), v], axis=1)
        outs, lses = [], []
        for win, dil in DILATED_CONFIGS:
            o, lse = _dilated_gather(q, kfull, vfull, win, dil, slopes, n_buf)
            outs.append(o)
            lses.append(lse)
        attn_o = _merge_by_denominator(outs, lses)
        mem_o = _mem_attn(mq, cache_mem_k[l].astype(mq.dtype), cache_mem_v[l].astype(mq.dtype))
        s_wk.append(k)
        s_wv.append(v)
        s_cv.append(full[:, -(CONV_WIDTH - 1):])
        xs = _finish(xs, conv_y, attn_o, mem_o, z, w_out[l])

    y_prompt = _rmsnorm(xp, g_final)
    y_sample = _rmsnorm(xs, g_final)
    return (y_prompt, y_sample,
            jnp.stack(p_wk, 0), jnp.stack(p_wv, 0), jnp.stack(p_cv, 0),
            jnp.stack(p_mk, 0), jnp.stack(p_mv, 0),
            jnp.stack(s_wk, 0), jnp.stack(s_wv, 0), jnp.stack(s_cv, 0))
```

```python
import functools

import numpy as np
import jax
import jax.numpy as jnp
from jax import lax
from jax.experimental import pallas as pl
from jax.experimental.pallas import tpu as pltpu

F32 = jnp.float32
BF16 = jnp.bfloat16

HEAD_DIM = 64
ATTN_HEADS = 8
MEM_HEADS = 4
CONV_W = 256
ATTN_W = 512
MEM_W = 256
D_MIX = 1024
CONV_WIDTH = 3
DILATED_CONFIGS = ((128, 1), (512, 4), (2048, 16))
SPAN = 128
ALIBI_MAX = 8.0
RMS_EPS = 1e-6
Q_SCALE = HEAD_DIM ** -0.5

LANES = 128
SUBLANES = 8
VMEM_LIMIT = 56 * 1024 * 1024

_C_CB, _C_CC, _C_Q, _C_K, _C_V, _C_MQ, _C_Z, _C_END = 0, 256, 768, 1280, 1792, 2304, 2560, 3584

ROW_TILE = 512
ATTN_TILE = 512


def _slopes():
    h = np.arange(ATTN_HEADS, dtype=np.float64) + 1.0
    return np.exp2(-ALIBI_MAX * h / ATTN_HEADS)


def _params(n_axes):
    return pltpu.CompilerParams(
        dimension_semantics=("arbitrary",) * n_axes,
        vmem_limit_bytes=VMEM_LIMIT)


def _proj_kernel(x_ref, g_ref, w_ref, cb_ref, u_ref, q_ref, k_ref, v_ref,
                 k32_ref, v32_ref, mq_ref, z_ref):
    x = x_ref[0]
    r = lax.rsqrt(jnp.mean(x * x, axis=-1, keepdims=True) + RMS_EPS)
    h = (x * r * g_ref[...]).astype(BF16)

    def seg(lo, hi):
        return jnp.dot(h, w_ref[:, lo:hi], preferred_element_type=F32)

    cb_ref[0] = seg(_C_CB, _C_CC).astype(BF16)
    cch = seg(_C_CC, _C_Q)
    u_ref[0] = cch[:, :CONV_W] * cch[:, CONV_W:]
    q_ref[0] = (seg(_C_Q, _C_K) * Q_SCALE).astype(BF16)
    k = seg(_C_K, _C_V)
    k32_ref[0] = k
    k_ref[0] = k.astype(BF16)
    v = seg(_C_V, _C_MQ)
    v32_ref[0] = v
    v_ref[0] = v.astype(BF16)
    mq_ref[0] = (seg(_C_MQ, _C_Z) * Q_SCALE).astype(BF16)
    z_ref[0] = seg(_C_Z, _C_END).astype(BF16)


def _project(x, g, w_bf16, n_keep):
    B, T, D = x.shape
    tm = ROW_TILE
    nt = T // tm
    first = (T - n_keep) // tm

    def row(width):
        return pl.BlockSpec((1, tm, width), lambda b, i: (b, i, 0))

    keep = pl.BlockSpec((1, tm, ATTN_W), lambda b, i: (b, jnp.maximum(i - first, 0), 0))
    out_shape = (
        jax.ShapeDtypeStruct((B, T, CONV_W), BF16),
        jax.ShapeDtypeStruct((B, T, CONV_W), F32),
        jax.ShapeDtypeStruct((B, T, ATTN_W), BF16),
        jax.ShapeDtypeStruct((B, T, ATTN_W), BF16),
        jax.ShapeDtypeStruct((B, T, ATTN_W), BF16),
        jax.ShapeDtypeStruct((B, n_keep, ATTN_W), F32),
        jax.ShapeDtypeStruct((B, n_keep, ATTN_W), F32),
        jax.ShapeDtypeStruct((B, T, MEM_W), BF16),
        jax.ShapeDtypeStruct((B, T, D_MIX), BF16),
    )
    return pl.pallas_call(
        _proj_kernel,
        grid=(B, nt),
        in_specs=[row(D),
                  pl.BlockSpec((1, D), lambda b, i: (0, 0)),
                  pl.BlockSpec(w_bf16.shape, lambda b, i: (0, 0))],
        out_specs=(row(CONV_W), row(CONV_W), row(ATTN_W), row(ATTN_W), row(ATTN_W),
                   keep, keep, row(MEM_W), row(D_MIX)),
        out_shape=out_shape,
        compiler_params=_params(2),
        name="proj",
    )(x, g.reshape(1, D), w_bf16)


def _memkv_kernel(x_ref, g_ref, w_ref, k32_ref, v32_ref, k_ref, v_ref):
    x = x_ref[...]
    r = lax.rsqrt(jnp.mean(x * x, axis=-1, keepdims=True) + RMS_EPS)
    h = (x * r * g_ref[...]).astype(BF16)
    kv = jnp.dot(h, w_ref[...], preferred_element_type=F32)
    k32_ref[...] = kv[:, :MEM_W]
    v32_ref[...] = kv[:, MEM_W:]
    k_ref[...] = kv[:, :MEM_W].astype(BF16)
    v_ref[...] = kv[:, MEM_W:].astype(BF16)


def _mem_kv(mem, g, w_bf16):
    B, M, D = mem.shape
    rows = B * M
    outs = pl.pallas_call(
        _memkv_kernel,
        out_shape=(jax.ShapeDtypeStruct((rows, MEM_W), F32),
                   jax.ShapeDtypeStruct((rows, MEM_W), F32),
                   jax.ShapeDtypeStruct((rows, MEM_W), BF16),
                   jax.ShapeDtypeStruct((rows, MEM_W), BF16)),
        compiler_params=pltpu.CompilerParams(vmem_limit_bytes=VMEM_LIMIT),
        name="memkv",
    )(mem.reshape(rows, D), g.reshape(1, D), w_bf16)
    return tuple(o.reshape(B, M, MEM_W) for o in outs)


def _band_bias_table():
    qi = np.arange(SPAN)[:, None] + SPAN
    ki = np.arange(2 * SPAN)[None, :]
    dist = qi - ki
    valid = (dist >= 0) & (dist <= SPAN)
    slopes = _slopes()
    tabs = []
    for _, dil in DILATED_CONFIGS:
        for first in (False, True):
            ok = valid & (ki >= SPAN) if first else valid
            for h in range(ATTN_HEADS):
                bias = -(slopes[h] * dil) * dist
                tabs.append(np.where(ok, bias, -np.inf))
    return np.stack(tabs).astype(np.float32)


def _pair_attention(qb, kb, vb, bias_lo, bias_hi):
    lane = lax.broadcasted_iota(jnp.int32, qb.shape, 1)
    lo = lane < HEAD_DIM
    zero = jnp.zeros_like(qb)
    outs, lses = [], []
    for sel, bias in ((lo, bias_lo), (~lo, bias_hi)):
        qm = jnp.where(sel, qb, zero)
        s = lax.dot_general(qm, kb, (((1,), (1,)), ((), ())), preferred_element_type=F32)
        if bias is not None:
            s = s + bias
        m = jnp.max(s, axis=-1, keepdims=True)
        p = jnp.exp(s - m)
        l = jnp.sum(p, axis=-1, keepdims=True)
        o = jnp.dot(p.astype(BF16), vb, preferred_element_type=F32)
        outs.append(o / l)
        lses.append(m + jnp.log(l))
    o = jnp.where(lo, outs[0], outs[1])
    lse = jnp.where(lo, lses[0], lses[1])
    return o, lse


def _attn_kernel(*refs, steps):
    ins = refs[:15]
    tab_ref = refs[15]
    outs = refs[16:22]
    kcat, vcat = refs[22], refs[23]
    s = pl.program_id(1)
    nblk = ATTN_TILE // SPAN
    for c in range(3):
        _, kp, ko, vp, vo = ins[5 * c:5 * c + 5]
        kcat[c, 0:SPAN] = kp[0]
        kcat[c, SPAN:] = ko[0]
        vcat[c, 0:SPAN] = vp[0]
        vcat[c, SPAN:] = vo[0]

    def body(j, carry):
        row0 = pl.multiple_of(j * SPAN, SPAN)
        for c, (_, dil) in enumerate(DILATED_CONFIGS):
            q_ref = ins[5 * c]
            o_ref, l_ref = outs[2 * c], outs[2 * c + 1]
            units_per_residue = steps // dil
            first = jnp.logical_and(s % units_per_residue == 0, j == 0).astype(jnp.int32)
            base = (c * 2 + first) * ATTN_HEADS
            for p in range(ATTN_HEADS // 2):
                lanes = slice(p * LANES, (p + 1) * LANES)
                qb = q_ref[0, pl.ds(row0, SPAN), lanes]
                kb = kcat[c, pl.ds(row0, 2 * SPAN), lanes]
                vb = vcat[c, pl.ds(row0, 2 * SPAN), lanes]
                o, lse = _pair_attention(qb, kb, vb, tab_ref[base + 2 * p], tab_ref[base + 2 * p + 1])
                o_ref[0, pl.ds(row0, SPAN), lanes] = o.astype(BF16)
                l_ref[0, pl.ds(row0, SPAN), lanes] = lse
        return carry

    lax.fori_loop(0, nblk, body, 0)


def _prompt_attention(q, k, v):
    B, T, W = q.shape
    steps = T // ATTN_TILE
    prev_per_tile = ATTN_TILE // SPAN
    args, in_specs, out_specs, out_shape = [], [], [], []
    for _, dil in DILATED_CONFIGS:
        L = T // dil
        upr = L // ATTN_TILE

        def own(b, s, upr=upr):
            return (b, s % upr, s // upr)

        def prev(b, s, upr=upr):
            return (b, jnp.maximum((s % upr) * prev_per_tile - 1, 0), s // upr)

        shape = (B, L, dil * W)
        qv, kv, vv = (a.reshape(shape) for a in (q, k, v))
        args += [qv, kv, kv, vv, vv]
        in_specs += [pl.BlockSpec((1, ATTN_TILE, W), own),
                     pl.BlockSpec((1, SPAN, W), prev),
                     pl.BlockSpec((1, ATTN_TILE, W), own),
                     pl.BlockSpec((1, SPAN, W), prev),
                     pl.BlockSpec((1, ATTN_TILE, W), own)]
        out_specs += [pl.BlockSpec((1, ATTN_TILE, W), own),
                      pl.BlockSpec((1, ATTN_TILE, W), own)]
        out_shape += [jax.ShapeDtypeStruct(shape, BF16), jax.ShapeDtypeStruct(shape, F32)]
    tab = jnp.asarray(_band_bias_table())
    args.append(tab)
    in_specs.append(pl.BlockSpec(tab.shape, lambda b, s: (0, 0, 0)))
    res = pl.pallas_call(
        functools.partial(_attn_kernel, steps=steps),
        grid=(B, steps),
        in_specs=in_specs,
        out_specs=tuple(out_specs),
        out_shape=tuple(out_shape),
        scratch_shapes=[pltpu.VMEM((3, ATTN_TILE + SPAN, W), BF16),
                        pltpu.VMEM((3, ATTN_TILE + SPAN, W), BF16)],
        compiler_params=_params(2),
        name="prompt_attn",
    )(*args)
    return [r.reshape(B, T, W) for r in res]


def _silu(z):
    return z / (1.0 + jnp.exp(-z))


def _gate_project_norm(x, mix_ref, wout_ref, gf_ref):
    y = x + jnp.dot(mix_ref[...], wout_ref[...], preferred_element_type=F32)
    r = lax.rsqrt(jnp.mean(y * y, axis=-1, keepdims=True) + RMS_EPS)
    return y * r * gf_ref[...]


def _finish_kernel(x_ref, cb_ref, u_ref, up_ref, o1_ref, l1_ref, o4_ref, l4_ref, o16_ref, l16_ref,
                   mq_ref, z_ref, mk_ref, mv_ref, cw_ref, wout_ref, gf_ref, y_ref, ucat, mix):
    i = pl.program_id(1)
    tm = ROW_TILE
    ucat[0:SUBLANES] = jnp.where(i == 0, 0.0, up_ref[0])
    ucat[SUBLANES:] = u_ref[0]
    cw = cw_ref[...]
    conv = (cw[0:1] * ucat[pl.ds(SUBLANES - 2, tm)]
            + cw[1:2] * ucat[pl.ds(SUBLANES - 1, tm)]
            + cw[2:3] * ucat[pl.ds(SUBLANES, tm)])
    conv_y = cb_ref[0].astype(F32) * conv
    z = z_ref[0].astype(F32)
    mix[:, 0:CONV_W] = (conv_y * _silu(z[:, 0:CONV_W])).astype(BF16)
    l1, l4, l16 = l1_ref[0], l4_ref[0], l16_ref[0]
    mx = jnp.maximum(jnp.maximum(l1, l4), l16)
    e1, e4, e16 = jnp.exp(l1 - mx), jnp.exp(l4 - mx), jnp.exp(l16 - mx)
    den = e1 + e4 + e16
    attn = (e1 * o1_ref[0].astype(F32) + e4 * o4_ref[0].astype(F32)
            + e16 * o16_ref[0].astype(F32)) / den
    lo, hi = CONV_W, CONV_W + ATTN_W
    mix[:, lo:hi] = (attn * _silu(z[:, lo:hi])).astype(BF16)
    for p in range(MEM_HEADS // 2):
        lanes = slice(p * LANES, (p + 1) * LANES)
        o, _ = _pair_attention(mq_ref[0, :, lanes], mk_ref[0, :, lanes], mv_ref[0, :, lanes],
                               None, None)
        a = hi + p * LANES
        mix[:, a:a + LANES] = (o * _silu(z[:, a:a + LANES])).astype(BF16)
    y_ref[0] = _gate_project_norm(x_ref[0], mix, wout_ref, gf_ref)


def _prompt_finish(x, cb, u, attn, mq, z, mk, mv, conv_w, w_out_bf16, g_final):
    B, T, D = x.shape
    tm = ROW_TILE

    def row(width):
        return pl.BlockSpec((1, tm, width), lambda b, i: (b, i, 0))

    def const(shape):
        return pl.BlockSpec(shape, lambda b, i: (0,) * len(shape))

    prev8 = pl.BlockSpec((1, SUBLANES, CONV_W),
                         lambda b, i: (b, jnp.maximum(i * (tm // SUBLANES) - 1, 0), 0))
    memspec = pl.BlockSpec((1,) + mk.shape[1:], lambda b, i: (b, 0, 0))
    in_specs = [row(D), row(CONV_W), row(CONV_W), prev8]
    for _ in range(3):
        in_specs += [row(ATTN_W), row(ATTN_W)]
    in_specs += [row(MEM_W), row(D_MIX), memspec, memspec,
                 const(conv_w.shape), const(w_out_bf16.shape), const((1, D))]
    return pl.pallas_call(
        _finish_kernel,
        grid=(B, T // tm),
        in_specs=in_specs,
        out_specs=row(D),
        out_shape=jax.ShapeDtypeStruct((B, T, D), F32),
        scratch_shapes=[pltpu.VMEM((tm + SUBLANES, CONV_W), F32),
                        pltpu.VMEM((tm, D_MIX), BF16)],
        compiler_params=_params(2),
        name="prompt_finish",
    )(x, cb, u, u, *attn, mq, z, mk, mv, conv_w, w_out_bf16, g_final.reshape(1, D))


def _sample_tables(S, n_buf):
    slopes = _slopes()
    rows = S * ATTN_HEADS
    j = np.repeat(np.arange(S), ATTN_HEADS)[:, None]
    sl = np.tile(slopes, S)[:, None]
    na = 4 * SPAN
    r = np.arange(na)[None, :]
    gap = na + j - r
    t1 = np.where(gap <= SPAN, -sl * gap, -np.inf)
    t4 = np.where((gap % 4 == 0) & (gap // 4 <= SPAN), -sl * gap, -np.inf)
    m = np.arange(SPAN)[None, :]
    t16 = -sl * 16.0 * (SPAN - m) + 0.0 * j
    jp = np.arange(S)[None, :]
    tn = np.where(jp <= j, -sl * (j - jp), -np.inf)
    return (t1.astype(np.float32), t4.astype(np.float32), t16.astype(np.float32),
            tn.astype(np.float32))


def _sample_attn_kernel(q_ref, kn_ref, vn_ref, ka_ref, kb_ref, va_ref, vb_ref,
                        mq_ref, mk_ref, mv_ref, t1_ref, t4_ref, t16_ref, tn_ref,
                        hm_ref, hmm_ref, o_ref, mo_ref):
    S = q_ref.shape[1]
    H = ATTN_HEADS
    rows = S * H
    nt = (((1,), (1,)), ((), ()))
    hm = hm_ref[...]
    q = q_ref[0]
    qblk = (jnp.broadcast_to(q[:, None, :], (S, H, ATTN_W)) * hm[None]).reshape(rows, ATTN_W)
    qb = qblk.astype(BF16)
    rowj = lax.broadcasted_iota(jnp.int32, (rows, 1), 0) // H

    ka = ka_ref[0].astype(BF16)
    s_a = lax.dot_general(qb, ka, nt, preferred_element_type=F32)
    s1 = s_a + t1_ref[...]
    s4 = s_a + t4_ref[...]
    s16 = []
    for j in range(S):
        kbj = kb_ref[0, :, j * ATTN_W:(j + 1) * ATTN_W].astype(BF16)
        s16.append(lax.dot_general(qb, kbj, nt, preferred_element_type=F32) + t16_ref[...])
    kn = kn_ref[0]
    vn = vn_ref[0]
    tn = tn_ref[...]
    sn = [jnp.sum(qblk * kn[j:j + 1], axis=-1, keepdims=True) for j in range(S)]
    sself = sn[0]
    for j in range(1, S):
        sself = jnp.where(rowj == j, sn[j], sself)
    sn1 = [sn[j] + tn[:, j:j + 1] for j in range(S)]

    m = jnp.maximum(jnp.max(s1, axis=-1, keepdims=True), jnp.max(s4, axis=-1, keepdims=True))
    for j in range(S):
        mj = jnp.max(s16[j], axis=-1, keepdims=True)
        m = jnp.maximum(m, jnp.where(rowj == j, mj, -jnp.inf))
        m = jnp.maximum(m, sn1[j])
    m = jnp.maximum(m, sself)

    pa = jnp.exp(s1 - m) + jnp.exp(s4 - m)
    l = jnp.sum(pa, axis=-1, keepdims=True)
    acc = jnp.dot(pa.astype(BF16), va_ref[0].astype(BF16), preferred_element_type=F32)
    for j in range(S):
        pj = jnp.where(rowj == j, jnp.exp(s16[j] - m), 0.0)
        l = l + jnp.sum(pj, axis=-1, keepdims=True)
        vbj = vb_ref[0, :, j * ATTN_W:(j + 1) * ATTN_W].astype(BF16)
        acc = acc + jnp.dot(pj.astype(BF16), vbj, preferred_element_type=F32)
    pself = jnp.exp(sself - m)
    l = l + 2.0 * pself
    for j in range(S):
        cj = jnp.exp(sn1[j] - m) + jnp.where(rowj == j, 2.0 * pself, 0.0)
        l = l + jnp.exp(sn1[j] - m)
        acc = acc + cj * vn[j:j + 1]
    acc = acc / l
    o_ref[0] = jnp.sum(acc.reshape(S, H, ATTN_W) * hm[None], axis=1)

    hmm = hmm_ref[...]
    mq = mq_ref[0]
    mqb = (jnp.broadcast_to(mq[:, None, :], (S, H, MEM_W)) * hmm[None]).reshape(rows, MEM_W)
    sm = lax.dot_general(mqb.astype(BF16), mk_ref[0].astype(BF16), nt, preferred_element_type=F32)
    mm = jnp.max(sm, axis=-1, keepdims=True)
    pm = jnp.exp(sm - mm)
    lm = jnp.sum(pm, axis=-1, keepdims=True)
    om = jnp.dot(pm.astype(BF16), mv_ref[0].astype(BF16), preferred_element_type=F32) / lm
    mo_ref[0] = jnp.sum(om.reshape(S, H, MEM_W) * hmm[None], axis=1)


def _sample_attention(q, k_new, v_new, cache_k, cache_v, mq, mem_k, mem_v):
    DB, S, W = q.shape
    n_buf = cache_k.shape[1]
    assert n_buf == 16 * SPAN and S <= 4, "sample path assumes a full 2048-row window buffer"
    t1, t4, t16, tn = (jnp.asarray(t) for t in _sample_tables(S, n_buf))
    lane = np.arange(W)[None, :] // HEAD_DIM
    hm = jnp.asarray((lane == np.arange(ATTN_HEADS)[:, None]).astype(np.float32))
    lane_m = np.arange(MEM_W)[None, :] // HEAD_DIM
    hmm = jnp.asarray((lane_m == np.arange(ATTN_HEADS)[:, None]).astype(np.float32))
    na = 4 * SPAN
    cache_k16 = cache_k.reshape(DB, SPAN, 16 * W)
    cache_v16 = cache_v.reshape(DB, SPAN, 16 * W)

    def tok(width):
        return pl.BlockSpec((1, S, width), lambda b: (b, 0, 0))

    tail = pl.BlockSpec((1, na, W), lambda b: (b, n_buf // na - 1, 0))
    strided = pl.BlockSpec((1, SPAN, S * W), lambda b: (b, 0, 0))
    memspec = pl.BlockSpec((1,) + mem_k.shape[1:], lambda b: (b, 0, 0))

    def const(a):
        return pl.BlockSpec(a.shape, lambda b: (0,) * a.ndim)

    return pl.pallas_call(
        _sample_attn_kernel,
        grid=(DB,),
        in_specs=[tok(W), tok(W), tok(W), tail, strided, tail, strided,
                  tok(MEM_W), memspec, memspec,
                  const(t1), const(t4), const(t16), const(tn), const(hm), const(hmm)],
        out_specs=(tok(W), tok(MEM_W)),
        out_shape=(jax.ShapeDtypeStruct((DB, S, W), F32),
                   jax.ShapeDtypeStruct((DB, S, MEM_W), F32)),
        compiler_params=_params(1),
        name="sample_attn",
    )(q, k_new, v_new, cache_k, cache_k16, cache_v, cache_v16, mq, mem_k, mem_v,
      t1, t4, t16, tn, hm, hmm)


def _sample_finish_kernel(x_ref, cb_ref, u0_ref, u1_ref, u2_ref, attn_ref, mem_ref, z_ref,
                          cw_ref, wout_ref, gf_ref, y_ref, mix):
    cw = cw_ref[...]
    conv = cw[0:1] * u0_ref[...] + cw[1:2] * u1_ref[...] + cw[2:3] * u2_ref[...]
    z = z_ref[...].astype(F32)
    lo, hi = CONV_W, CONV_W + ATTN_W
    mix[:, 0:lo] = (cb_ref[...].astype(F32) * conv * _silu(z[:, 0:lo])).astype(BF16)
    mix[:, lo:hi] = (attn_ref[...] * _silu(z[:, lo:hi])).astype(BF16)
    mix[:, hi:] = (mem_ref[...] * _silu(z[:, hi:])).astype(BF16)
    y_ref[...] = _gate_project_norm(x_ref[...], mix, wout_ref, gf_ref)


def _sample_finish(x, cb, u_taps, attn, mem, z, conv_w, w_out_bf16, g_final):
    rows, D = x.shape
    return pl.pallas_call(
        _sample_finish_kernel,
        out_shape=jax.ShapeDtypeStruct((rows, D), F32),
        scratch_shapes=[pltpu.VMEM((rows, D_MIX), BF16)],
        compiler_params=pltpu.CompilerParams(vmem_limit_bytes=VMEM_LIMIT),
        name="sample_finish",
    )(x, cb, *u_taps, attn, mem, z, conv_w, w_out_bf16, g_final.reshape(1, D))


def kernel(x_prompt, x_sample, mem_prompt, cache_win_k, cache_win_v, cache_conv, cache_mem_k,
           cache_mem_v, g_in, w_in, conv_w, g_mem, w_mem_kv, w_out, g_final):
    assert g_in.shape[0] == 1, "the final RMSNorm is fused into the single layer's tail"
    B, T, D = x_prompt.shape
    DB, S, _ = x_sample.shape
    n_buf = cache_win_k.shape[2]
    n_keep = min(DILATED_CONFIGS[-1][0], T)
    w_in_b = w_in[0].astype(BF16)
    w_out_b = w_out[0].astype(BF16)
    cb, u, q, k, v, k32, v32, mq, z = _project(x_prompt, g_in[0], w_in_b, n_keep)
    mk32, mv32, mk, mv = _mem_kv(mem_prompt, g_mem[0], w_mem_kv[0].astype(BF16))
    attn = _prompt_attention(q, k, v)
    y_prompt = _prompt_finish(x_prompt, cb, u, attn, mq, z, mk, mv, conv_w[0], w_out_b, g_final)
    p_win_k = k32.reshape(1, B, n_keep, ATTN_HEADS, HEAD_DIM)
    p_win_v = v32.reshape(1, B, n_keep, ATTN_HEADS, HEAD_DIM)
    p_conv = u[None, :, -(CONV_WIDTH - 1):]
    p_mem_k = mk32.reshape(1, B, -1, MEM_HEADS, HEAD_DIM)
    p_mem_v = mv32.reshape(1, B, -1, MEM_HEADS, HEAD_DIM)
    rows = DB * S
    cb, u, q, k, v, k32, v32, mq, z = _project(x_sample.reshape(1, rows, D), g_in[0], w_in_b, rows)
    k32 = k32.reshape(DB, S, ATTN_W)
    v32 = v32.reshape(DB, S, ATTN_W)
    attn_s, mem_s = _sample_attention(
        q.reshape(DB, S, ATTN_W).astype(F32), k32, v32,
        cache_win_k[0].reshape(DB, n_buf, ATTN_W), cache_win_v[0].reshape(DB, n_buf, ATTN_W),
        mq.reshape(DB, S, MEM_W).astype(F32),
        cache_mem_k[0].reshape(DB, -1, MEM_W), cache_mem_v[0].reshape(DB, -1, MEM_W))
    full = jnp.concatenate([cache_conv[0], u.reshape(DB, S, CONV_W)], axis=1)
    taps = [full[:, t:t + S].reshape(rows, CONV_W) for t in range(CONV_WIDTH)]
    y_sample = _sample_finish(x_sample.reshape(rows, D), cb.reshape(rows, CONV_W), taps,
                              attn_s.reshape(rows, ATTN_W), mem_s.reshape(rows, MEM_W),
                              z.reshape(rows, D_MIX), conv_w[0], w_out_b, g_final).reshape(DB, S, D)
    s_win_k = k32.reshape(1, DB, S, ATTN_HEADS, HEAD_DIM)
    s_win_v = v32.reshape(1, DB, S, ATTN_HEADS, HEAD_DIM)
    s_conv = full[None, :, -(CONV_WIDTH - 1):]
    return (y_prompt, y_sample, p_win_k, p_win_v, p_conv, p_mem_k, p_mem_v,
            s_win_k, s_win_v, s_conv)
```

```python
import functools

import numpy as np
import jax
import jax.numpy as jnp
from jax import lax
from jax.experimental import pallas as pl
from jax.experimental.pallas import tpu as pltpu

F32 = jnp.float32
BF16 = jnp.bfloat16

HEAD_DIM = 64
ATTN_HEADS = 8
MEM_HEADS = 4
CONV_W = 256
ATTN_W = 512
MEM_W = 256
D_MIX = 1024
CONV_WIDTH = 3
DILATED_CONFIGS = ((128, 1), (512, 4), (2048, 16))
SPAN = 128
ALIBI_MAX = 8.0
RMS_EPS = 1e-6
Q_SCALE = HEAD_DIM ** -0.5

LANES = 128
SUBLANES = 8
VMEM_LIMIT = 56 * 1024 * 1024

_C_CB, _C_CC, _C_Q, _C_K, _C_V, _C_MQ, _C_Z, _C_END = 0, 256, 768, 1280, 1792, 2304, 2560, 3584

ROW_TILE = 512
ATTN_TILE = 512


def _slopes():
    h = np.arange(ATTN_HEADS, dtype=np.float64) + 1.0
    return np.exp2(-ALIBI_MAX * h / ATTN_HEADS)


def _params(n_axes):
    return pltpu.CompilerParams(
        dimension_semantics=("arbitrary",) * n_axes,
        vmem_limit_bytes=VMEM_LIMIT)


def _proj_kernel(x_ref, g_ref, w_ref, cb_ref, u_ref, q_ref, k_ref, v_ref,
                 k32_ref, v32_ref, mq_ref, z_ref, *, first, keep_transposed):
    x = x_ref[0]
    r = lax.rsqrt(jnp.mean(x * x, axis=-1, keepdims=True) + RMS_EPS)
    h = (x * r * g_ref[...]).astype(BF16)

    def seg(lo, hi):
        return jnp.dot(h, w_ref[:, lo:hi], preferred_element_type=F32)

    cb_ref[0] = seg(_C_CB, _C_CC).astype(BF16)
    cch = seg(_C_CC, _C_Q)
    u_ref[0] = cch[:, :CONV_W] * cch[:, CONV_W:]
    q_ref[0] = (seg(_C_Q, _C_K) * Q_SCALE).astype(BF16)
    k = seg(_C_K, _C_V)
    k_ref[0] = k.astype(BF16)
    v = seg(_C_V, _C_MQ)
    v_ref[0] = v.astype(BF16)
    if keep_transposed:
        @pl.when(pl.program_id(1) >= first)
        def _():
            k32_ref[0] = k.T
            v32_ref[0] = v.T
    else:
        k32_ref[0] = k
        v32_ref[0] = v
    mq_ref[0] = (seg(_C_MQ, _C_Z) * Q_SCALE).astype(BF16)
    z_ref[0] = seg(_C_Z, _C_END).astype(BF16)


def _project(x, g, w_bf16, n_keep, keep_transposed):
    B, T, D = x.shape
    tm = ROW_TILE
    nt = T // tm
    first = (T - n_keep) // tm

    def row(width):
        return pl.BlockSpec((1, tm, width), lambda b, i: (b, i, 0))

    if keep_transposed:
        keep = pl.BlockSpec((1, ATTN_W, tm), lambda b, i: (b, 0, jnp.maximum(i - first, 0)))
        keep_shape = (B, ATTN_W, n_keep)
    else:
        keep = pl.BlockSpec((1, tm, ATTN_W), lambda b, i: (b, jnp.maximum(i - first, 0), 0))
        keep_shape = (B, n_keep, ATTN_W)
    out_shape = (
        jax.ShapeDtypeStruct((B, T, CONV_W), BF16),
        jax.ShapeDtypeStruct((B, T, CONV_W), F32),
        jax.ShapeDtypeStruct((B, T, ATTN_W), BF16),
        jax.ShapeDtypeStruct((B, T, ATTN_W), BF16),
        jax.ShapeDtypeStruct((B, T, ATTN_W), BF16),
        jax.ShapeDtypeStruct(keep_shape, F32),
        jax.ShapeDtypeStruct(keep_shape, F32),
        jax.ShapeDtypeStruct((B, T, MEM_W), BF16),
        jax.ShapeDtypeStruct((B, T, D_MIX), BF16),
    )
    return pl.pallas_call(
        functools.partial(_proj_kernel, first=first, keep_transposed=keep_transposed),
        grid=(B, nt),
        in_specs=[row(D),
                  pl.BlockSpec((1, D), lambda b, i: (0, 0)),
                  pl.BlockSpec(w_bf16.shape, lambda b, i: (0, 0))],
        out_specs=(row(CONV_W), row(CONV_W), row(ATTN_W), row(ATTN_W), row(ATTN_W),
                   keep, keep, row(MEM_W), row(D_MIX)),
        out_shape=out_shape,
        compiler_params=_params(2),
        name="proj",
    )(x, g.reshape(1, D), w_bf16)


def _memkv_kernel(x_ref, g_ref, w_ref, k32_ref, v32_ref, k_ref, v_ref):
    x = x_ref[...]
    r = lax.rsqrt(jnp.mean(x * x, axis=-1, keepdims=True) + RMS_EPS)
    h = (x * r * g_ref[...]).astype(BF16)
    kv = jnp.dot(h, w_ref[...], preferred_element_type=F32)
    k_ref[...] = kv[:, :MEM_W].astype(BF16)
    v_ref[...] = kv[:, MEM_W:].astype(BF16)
    n_mem = k32_ref.shape[2]
    for b in range(k32_ref.shape[0]):
        kvt = kv[b * n_mem:(b + 1) * n_mem].T
        k32_ref[b] = kvt[:MEM_W]
        v32_ref[b] = kvt[MEM_W:]


def _mem_kv(mem, g, w_bf16):
    B, M, D = mem.shape
    rows = B * M
    k32, v32, k, v = pl.pallas_call(
        _memkv_kernel,
        out_shape=(jax.ShapeDtypeStruct((B, MEM_W, M), F32),
                   jax.ShapeDtypeStruct((B, MEM_W, M), F32),
                   jax.ShapeDtypeStruct((rows, MEM_W), BF16),
                   jax.ShapeDtypeStruct((rows, MEM_W), BF16)),
        compiler_params=pltpu.CompilerParams(vmem_limit_bytes=VMEM_LIMIT),
        name="memkv",
    )(mem.reshape(rows, D), g.reshape(1, D), w_bf16)
    return k32, v32, k.reshape(B, M, MEM_W), v.reshape(B, M, MEM_W)


def _band_bias_table():
    qi = np.arange(SPAN)[:, None] + SPAN
    ki = np.arange(2 * SPAN)[None, :]
    dist = qi - ki
    valid = (dist >= 0) & (dist <= SPAN)
    slopes = _slopes()
    tabs = []
    for _, dil in DILATED_CONFIGS:
        for first in (False, True):
            ok = valid & (ki >= SPAN) if first else valid
            for h in range(ATTN_HEADS):
                bias = -(slopes[h] * dil) * dist
                tabs.append(np.where(ok, bias, -np.inf))
    return np.stack(tabs).astype(np.float32)


def _pair_attention(qb, kb, vb, bias_lo, bias_hi):
    lane = lax.broadcasted_iota(jnp.int32, qb.shape, 1)
    lo = lane < HEAD_DIM
    zero = jnp.zeros_like(qb)
    outs, lses = [], []
    for sel, bias in ((lo, bias_lo), (~lo, bias_hi)):
        qm = jnp.where(sel, qb, zero)
        s = lax.dot_general(qm, kb, (((1,), (1,)), ((), ())), preferred_element_type=F32)
        if bias is not None:
            s = s + bias
        m = jnp.max(s, axis=-1, keepdims=True)
        p = jnp.exp(s - m)
        l = jnp.sum(p, axis=-1, keepdims=True)
        o = jnp.dot(p.astype(BF16), vb, preferred_element_type=F32)
        outs.append(o / l)
        lses.append(m + jnp.log(l))
    o = jnp.where(lo, outs[0], outs[1])
    lse = jnp.where(lo, lses[0], lses[1])
    return o, lse


def _attn_kernel(*refs, steps):
    ins = refs[:15]
    tab_ref = refs[15]
    outs = refs[16:22]
    kcat, vcat = refs[22], refs[23]
    s = pl.program_id(1)
    nblk = ATTN_TILE // SPAN
    for c in range(3):
        _, kp, ko, vp, vo = ins[5 * c:5 * c + 5]
        kcat[c, 0:SPAN] = kp[0]
        kcat[c, SPAN:] = ko[0]
        vcat[c, 0:SPAN] = vp[0]
        vcat[c, SPAN:] = vo[0]

    def body(j, carry):
        row0 = pl.multiple_of(j * SPAN, SPAN)
        for c, (_, dil) in enumerate(DILATED_CONFIGS):
            q_ref = ins[5 * c]
            o_ref, l_ref = outs[2 * c], outs[2 * c + 1]
            units_per_residue = steps // dil
            first = jnp.logical_and(s % units_per_residue == 0, j == 0).astype(jnp.int32)
            base = (c * 2 + first) * ATTN_HEADS
            for p in range(ATTN_HEADS // 2):
                lanes = slice(p * LANES, (p + 1) * LANES)
                qb = q_ref[0, pl.ds(row0, SPAN), lanes]
                kb = kcat[c, pl.ds(row0, 2 * SPAN), lanes]
                vb = vcat[c, pl.ds(row0, 2 * SPAN), lanes]
                o, lse = _pair_attention(qb, kb, vb, tab_ref[base + 2 * p], tab_ref[base + 2 * p + 1])
                o_ref[0, pl.ds(row0, SPAN), lanes] = o.astype(BF16)
                l_ref[0, pl.ds(row0, SPAN), lanes] = lse
        return carry

    lax.fori_loop(0, nblk, body, 0)


def _prompt_attention(q, k, v):
    B, T, W = q.shape
    steps = T // ATTN_TILE
    prev_per_tile = ATTN_TILE // SPAN
    args, in_specs, out_specs, out_shape = [], [], [], []
    for _, dil in DILATED_CONFIGS:
        L = T // dil
        upr = L // ATTN_TILE

        def own(b, s, upr=upr):
            return (b, s % upr, s // upr)

        def prev(b, s, upr=upr):
            return (b, jnp.maximum((s % upr) * prev_per_tile - 1, 0), s // upr)

        shape = (B, L, dil * W)
        qv, kv, vv = (a.reshape(shape) for a in (q, k, v))
        args += [qv, kv, kv, vv, vv]
        in_specs += [pl.BlockSpec((1, ATTN_TILE, W), own),
                     pl.BlockSpec((1, SPAN, W), prev),
                     pl.BlockSpec((1, ATTN_TILE, W), own),
                     pl.BlockSpec((1, SPAN, W), prev),
                     pl.BlockSpec((1, ATTN_TILE, W), own)]
        out_specs += [pl.BlockSpec((1, ATTN_TILE, W), own),
                      pl.BlockSpec((1, ATTN_TILE, W), own)]
        out_shape += [jax.ShapeDtypeStruct(shape, BF16), jax.ShapeDtypeStruct(shape, F32)]
    tab = jnp.asarray(_band_bias_table())
    args.append(tab)
    in_specs.append(pl.BlockSpec(tab.shape, lambda b, s: (0, 0, 0)))
    res = pl.pallas_call(
        functools.partial(_attn_kernel, steps=steps),
        grid=(B, steps),
        in_specs=in_specs,
        out_specs=tuple(out_specs),
        out_shape=tuple(out_shape),
        scratch_shapes=[pltpu.VMEM((3, ATTN_TILE + SPAN, W), BF16),
                        pltpu.VMEM((3, ATTN_TILE + SPAN, W), BF16)],
        compiler_params=_params(2),
        name="prompt_attn",
    )(*args)
    return [r.reshape(B, T, W) for r in res]


def _silu(z):
    return z / (1.0 + jnp.exp(-z))


def _gate_project_norm(x, mix_ref, wout_ref, gf_ref):
    y = x + jnp.dot(mix_ref[...], wout_ref[...], preferred_element_type=F32)
    r = lax.rsqrt(jnp.mean(y * y, axis=-1, keepdims=True) + RMS_EPS)
    return y * r * gf_ref[...]


def _finish_kernel(x_ref, cb_ref, u_ref, up_ref, o1_ref, l1_ref, o4_ref, l4_ref, o16_ref, l16_ref,
                   mq_ref, z_ref, mk_ref, mv_ref, cw_ref, wout_ref, gf_ref, y_ref, ucat, mix):
    i = pl.program_id(1)
    tm = ROW_TILE
    ucat[0:SUBLANES] = jnp.where(i == 0, 0.0, up_ref[0])
    ucat[SUBLANES:] = u_ref[0]
    cw = cw_ref[...]
    conv = (cw[0:1] * ucat[pl.ds(SUBLANES - 2, tm)]
            + cw[1:2] * ucat[pl.ds(SUBLANES - 1, tm)]
            + cw[2:3] * ucat[pl.ds(SUBLANES, tm)])
    conv_y = cb_ref[0].astype(F32) * conv
    z = z_ref[0].astype(F32)
    mix[:, 0:CONV_W] = (conv_y * _silu(z[:, 0:CONV_W])).astype(BF16)
    l1, l4, l16 = l1_ref[0], l4_ref[0], l16_ref[0]
    mx = jnp.maximum(jnp.maximum(l1, l4), l16)
    e1, e4, e16 = jnp.exp(l1 - mx), jnp.exp(l4 - mx), jnp.exp(l16 - mx)
    den = e1 + e4 + e16
    attn = (e1 * o1_ref[0].astype(F32) + e4 * o4_ref[0].astype(F32)
            + e16 * o16_ref[0].astype(F32)) / den
    lo, hi = CONV_W, CONV_W + ATTN_W
    mix[:, lo:hi] = (attn * _silu(z[:, lo:hi])).astype(BF16)
    for p in range(MEM_HEADS // 2):
        lanes = slice(p * LANES, (p + 1) * LANES)
        o, _ = _pair_attention(mq_ref[0, :, lanes], mk_ref[0, :, lanes], mv_ref[0, :, lanes],
                               None, None)
        a = hi + p * LANES
        mix[:, a:a + LANES] = (o * _silu(z[:, a:a + LANES])).astype(BF16)
    y_ref[0] = _gate_project_norm(x_ref[0], mix, wout_ref, gf_ref)


def _prompt_finish(x, cb, u, attn, mq, z, mk, mv, conv_w, w_out_bf16, g_final):
    B, T, D = x.shape
    tm = ROW_TILE

    def row(width):
        return pl.BlockSpec((1, tm, width), lambda b, i: (b, i, 0))

    def const(shape):
        return pl.BlockSpec(shape, lambda b, i: (0,) * len(shape))

    prev8 = pl.BlockSpec((1, SUBLANES, CONV_W),
                         lambda b, i: (b, jnp.maximum(i * (tm // SUBLANES) - 1, 0), 0))
    memspec = pl.BlockSpec((1,) + mk.shape[1:], lambda b, i: (b, 0, 0))
    in_specs = [row(D), row(CONV_W), row(CONV_W), prev8]
    for _ in range(3):
        in_specs += [row(ATTN_W), row(ATTN_W)]
    in_specs += [row(MEM_W), row(D_MIX), memspec, memspec,
                 const(conv_w.shape), const(w_out_bf16.shape), const((1, D))]
    return pl.pallas_call(
        _finish_kernel,
        grid=(B, T // tm),
        in_specs=in_specs,
        out_specs=row(D),
        out_shape=jax.ShapeDtypeStruct((B, T, D), F32),
        scratch_shapes=[pltpu.VMEM((tm + SUBLANES, CONV_W), F32),
                        pltpu.VMEM((tm, D_MIX), BF16)],
        compiler_params=_params(2),
        name="prompt_finish",
    )(x, cb, u, u, *attn, mq, z, mk, mv, conv_w, w_out_bf16, g_final.reshape(1, D))


def _sample_tables(S, n_buf):
    j = np.arange(S)[:, None]
    t = np.arange(n_buf)[None, :]
    dist = n_buf + j - t
    count = np.zeros(dist.shape)
    for win, dil in DILATED_CONFIGS:
        count += (dist % dil == 0) & (dist <= win)
    bias = np.where(count > 0, -_slopes()[:, None, None] * dist, -np.inf)
    return count.astype(np.float32), bias.astype(np.float32)


def _sample_attn_kernel(q_ref, kn_ref, vn_ref, kt_ref, vt_ref, mq_ref, mkt_ref, mvt_ref,
                        cnt_ref, bias_ref, o_ref, mo_ref):
    S = q_ref.shape[1]
    nt = (((1,), (1,)), ((), ()))
    slopes = _slopes()
    n_cfg = float(len(DILATED_CONFIGS))
    rowj = lax.broadcasted_iota(jnp.int32, (S, 1), 0)
    q, kn, vn = q_ref[0], kn_ref[0], vn_ref[0]
    cnt = cnt_ref[...]
    outs = []
    for h in range(ATTN_HEADS):
        lanes = slice(h * HEAD_DIM, (h + 1) * HEAD_DIM)
        qh, knh, vnh = q[:, lanes], kn[:, lanes], vn[:, lanes]
        s = jnp.dot(qh.astype(BF16), kt_ref[0, h].astype(BF16),
                    preferred_element_type=F32) + bias_ref[h]
        m = jnp.max(s, axis=-1, keepdims=True)
        s_new = []
        for jp in range(S):
            d = rowj - jp
            sj = jnp.sum(qh * knh[jp:jp + 1], axis=-1, keepdims=True)
            sj = jnp.where(d >= 0, sj - float(slopes[h]) * d.astype(F32), -jnp.inf)
            s_new.append((sj, jnp.where(d == 0, n_cfg, 1.0)))
            m = jnp.maximum(m, sj)
        p = cnt * jnp.exp(s - m)
        l = jnp.sum(p, axis=-1, keepdims=True)
        acc = lax.dot_general(p.astype(BF16), vt_ref[0, h].astype(BF16), nt,
                              preferred_element_type=F32)
        for jp in range(S):
            sj, mult = s_new[jp]
            w = mult * jnp.exp(sj - m)
            l = l + w
            acc = acc + w * vnh[jp:jp + 1]
        outs.append(acc / l)
    o_ref[0] = jnp.concatenate(outs, axis=-1)

    mq = mq_ref[0]
    outs = []
    for h in range(MEM_HEADS):
        lanes = slice(h * HEAD_DIM, (h + 1) * HEAD_DIM)
        s = jnp.dot(mq[:, lanes].astype(BF16), mkt_ref[0, h].astype(BF16),
                    preferred_element_type=F32)
        p = jnp.exp(s - jnp.max(s, axis=-1, keepdims=True))
        l = jnp.sum(p, axis=-1, keepdims=True)
        o = lax.dot_general(p.astype(BF16), mvt_ref[0, h].astype(BF16), nt,
                            preferred_element_type=F32)
        outs.append(o / l)
    mo_ref[0] = jnp.concatenate(outs, axis=-1)


def _sample_attention(q, k_new, v_new, cache_kt, cache_vt, mq, mem_kt, mem_vt):
    DB, S, W = q.shape
    n_buf = cache_kt.shape[-1]
    assert n_buf >= DILATED_CONFIGS[-1][0] and S <= DILATED_CONFIGS[1][1], \
        "new tokens reach each other through dilation 1 only; the buffer covers every window"
    cnt, bias = (jnp.asarray(t) for t in _sample_tables(S, n_buf))

    def tok(width):
        return pl.BlockSpec((1, S, width), lambda b: (b, 0, 0))

    def per_batch(a):
        return pl.BlockSpec((1,) + a.shape[1:], lambda b: (b,) + (0,) * (a.ndim - 1))

    def const(a):
        return pl.BlockSpec(a.shape, lambda b: (0,) * a.ndim)

    return pl.pallas_call(
        _sample_attn_kernel,
        grid=(DB,),
        in_specs=[tok(W), tok(W), tok(W), per_batch(cache_kt), per_batch(cache_vt),
                  tok(MEM_W), per_batch(mem_kt), per_batch(mem_vt), const(cnt), const(bias)],
        out_specs=(tok(W), tok(MEM_W)),
        out_shape=(jax.ShapeDtypeStruct((DB, S, W), F32),
                   jax.ShapeDtypeStruct((DB, S, MEM_W), F32)),
        compiler_params=_params(1),
        name="sample_attn",
    )(q, k_new, v_new, cache_kt, cache_vt, mq, mem_kt, mem_vt, cnt, bias)


def _sample_finish_kernel(x_ref, cb_ref, u0_ref, u1_ref, u2_ref, attn_ref, mem_ref, z_ref,
                          cw_ref, wout_ref, gf_ref, y_ref, mix):
    cw = cw_ref[...]
    conv = cw[0:1] * u0_ref[...] + cw[1:2] * u1_ref[...] + cw[2:3] * u2_ref[...]
    z = z_ref[...].astype(F32)
    lo, hi = CONV_W, CONV_W + ATTN_W
    mix[:, 0:lo] = (cb_ref[...].astype(F32) * conv * _silu(z[:, 0:lo])).astype(BF16)
    mix[:, lo:hi] = (attn_ref[...] * _silu(z[:, lo:hi])).astype(BF16)
    mix[:, hi:] = (mem_ref[...] * _silu(z[:, hi:])).astype(BF16)
    y_ref[...] = _gate_project_norm(x_ref[...], mix, wout_ref, gf_ref)


def _sample_finish(x, cb, u_taps, attn, mem, z, conv_w, w_out_bf16, g_final):
    rows, D = x.shape
    return pl.pallas_call(
        _sample_finish_kernel,
        out_shape=jax.ShapeDtypeStruct((rows, D), F32),
        scratch_shapes=[pltpu.VMEM((rows, D_MIX), BF16)],
        compiler_params=pltpu.CompilerParams(vmem_limit_bytes=VMEM_LIMIT),
        name="sample_finish",
    )(x, cb, *u_taps, attn, mem, z, conv_w, w_out_bf16, g_final.reshape(1, D))


def kernel(x_prompt, x_sample, mem_prompt, cache_win_k, cache_win_v, cache_conv, cache_mem_k,
           cache_mem_v, g_in, w_in, conv_w, g_mem, w_mem_kv, w_out, g_final):
    assert g_in.shape[0] == 1, "the final RMSNorm is fused into the single layer's tail"
    B, T, D = x_prompt.shape
    DB, S, _ = x_sample.shape
    n_buf = cache_win_k.shape[2]
    n_keep = min(DILATED_CONFIGS[-1][0], T)
    w_in_b = w_in[0].astype(BF16)
    w_out_b = w_out[0].astype(BF16)
    cb, u, q, k, v, kt32, vt32, mq, z = _project(x_prompt, g_in[0], w_in_b, n_keep, True)
    mkt32, mvt32, mk, mv = _mem_kv(mem_prompt, g_mem[0], w_mem_kv[0].astype(BF16))
    attn = _prompt_attention(q, k, v)
    y_prompt = _prompt_finish(x_prompt, cb, u, attn, mq, z, mk, mv, conv_w[0], w_out_b, g_final)

    def heads_last(t, heads):
        return jnp.transpose(t.reshape(t.shape[0], heads, HEAD_DIM, -1), (0, 3, 1, 2))[None]

    def heads_first(c):
        return jnp.transpose(c, (0, 2, 3, 1))

    p_win_k = heads_last(kt32, ATTN_HEADS)
    p_win_v = heads_last(vt32, ATTN_HEADS)
    p_conv = u[None, :, -(CONV_WIDTH - 1):]
    p_mem_k = heads_last(mkt32, MEM_HEADS)
    p_mem_v = heads_last(mvt32, MEM_HEADS)
    rows = DB * S
    cb, u, q, k, v, k32, v32, mq, z = _project(x_sample.reshape(1, rows, D), g_in[0], w_in_b,
                                               rows, False)
    k32 = k32.reshape(DB, S, ATTN_W)
    v32 = v32.reshape(DB, S, ATTN_W)
    attn_s, mem_s = _sample_attention(
        q.reshape(DB, S, ATTN_W).astype(F32), k32, v32,
        heads_first(cache_win_k[0]), heads_first(cache_win_v[0]),
        mq.reshape(DB, S, MEM_W).astype(F32),
        heads_first(cache_mem_k[0]), heads_first(cache_mem_v[0]))
    full = jnp.concatenate([cache_conv[0], u.reshape(DB, S, CONV_W)], axis=1)
    taps = [full[:, t:t + S].reshape(rows, CONV_W) for t in range(CONV_WIDTH)]
    y_sample = _sample_finish(x_sample.reshape(rows, D), cb.reshape(rows, CONV_W), taps,
                              attn_s.reshape(rows, ATTN_W), mem_s.reshape(rows, MEM_W),
                              z.reshape(rows, D_MIX), conv_w[0], w_out_b, g_final).reshape(DB, S, D)
    s_win_k = k32.reshape(1, DB, S, ATTN_HEADS, HEAD_DIM)
    s_win_v = v32.reshape(1, DB, S, ATTN_HEADS, HEAD_DIM)
    s_conv = full[None, :, -(CONV_WIDTH - 1):]
    return (y_prompt, y_sample, p_win_k, p_win_v, p_conv, p_mem_k, p_mem_v,
            s_win_k, s_win_v, s_conv)
```

```python
import functools

import numpy as np
import jax
import jax.numpy as jnp
from jax import lax
from jax.experimental import pallas as pl
from jax.experimental.pallas import tpu as pltpu

F32 = jnp.float32
BF16 = jnp.bfloat16

HEAD_DIM = 64
ATTN_HEADS = 8
MEM_HEADS = 4
CONV_W = 256
ATTN_W = 512
MEM_W = 256
D_MIX = 1024
CONV_WIDTH = 3
DILATED_CONFIGS = ((128, 1), (512, 4), (2048, 16))
SPAN = 128
ALIBI_MAX = 8.0
RMS_EPS = 1e-6
Q_SCALE = HEAD_DIM ** -0.5

LANES = 128
SUBLANES = 8
VMEM_LIMIT = 56 * 1024 * 1024

_C_CB, _C_CC, _C_Q, _C_K, _C_V, _C_MQ, _C_Z, _C_END = 0, 256, 768, 1280, 1792, 2304, 2560, 3584

ROW_TILE = 512
ATTN_TILE = 512


def _slopes():
    h = np.arange(ATTN_HEADS, dtype=np.float64) + 1.0
    return np.exp2(-ALIBI_MAX * h / ATTN_HEADS)


def _params(n_axes):
    return pltpu.CompilerParams(
        dimension_semantics=("arbitrary",) * n_axes,
        vmem_limit_bytes=VMEM_LIMIT)


def _fold_store(res, out_refs, slabs, dils):
    rows = res.shape[0]
    out_refs[0][0, 0] = res.astype(BF16)
    if len(dils) == 1:
        return
    n_slab = res.shape[1] // LANES
    for s in range(n_slab):
        slabs[s] = res[:, s * LANES:(s + 1) * LANES]
    for d, o_ref in zip(dils[1:], out_refs[1:]):
        for r in range(d):
            for s in range(n_slab):
                o_ref[0, r, :, s * LANES:(s + 1) * LANES] = (
                    slabs[s, pl.ds(r, rows // d, stride=d), :].astype(BF16))


def _proj_kernel(x_ref, g_ref, w_ref, *refs, first, keep_transposed, dils):
    n = len(dils)
    cb_ref, u_ref = refs[0:2]
    q_refs, k_refs, v_refs = refs[2:2 + n], refs[2 + n:2 + 2 * n], refs[2 + 2 * n:2 + 3 * n]
    k32_ref, v32_ref, mq_ref, z_ref = refs[2 + 3 * n:6 + 3 * n]
    slabs = refs[6 + 3 * n:] or (None, None, None)
    x = x_ref[0]
    r = lax.rsqrt(jnp.mean(x * x, axis=-1, keepdims=True) + RMS_EPS)
    h = (x * r * g_ref[...]).astype(BF16)

    def seg(lo, hi):
        return jnp.dot(h, w_ref[:, lo:hi], preferred_element_type=F32)

    cb_ref[0] = seg(_C_CB, _C_CC).astype(BF16)
    cch = seg(_C_CC, _C_Q)
    u_ref[0] = cch[:, :CONV_W] * cch[:, CONV_W:]
    _fold_store(seg(_C_Q, _C_K) * Q_SCALE, q_refs, slabs[0], dils)
    k = seg(_C_K, _C_V)
    _fold_store(k, k_refs, slabs[1], dils)
    v = seg(_C_V, _C_MQ)
    _fold_store(v, v_refs, slabs[2], dils)
    if keep_transposed:
        @pl.when(pl.program_id(1) >= first)
        def _():
            k32_ref[0] = k.T
            v32_ref[0] = v.T
    else:
        k32_ref[0] = k
        v32_ref[0] = v
    mq_ref[0] = (seg(_C_MQ, _C_Z) * Q_SCALE).astype(BF16)
    z_ref[0] = seg(_C_Z, _C_END).astype(BF16)


def _project(x, g, w_bf16, n_keep, keep_transposed, dils):
    B, T, D = x.shape
    tm = ROW_TILE
    nt = T // tm
    first = (T - n_keep) // tm
    n = len(dils)

    def row(width):
        return pl.BlockSpec((1, tm, width), lambda b, i: (b, i, 0))

    folded_specs = [pl.BlockSpec((1, d, tm // d, ATTN_W), lambda b, i: (b, 0, i, 0)) for d in dils]
    folded_shapes = [jax.ShapeDtypeStruct((B, d, T // d, ATTN_W), BF16) for d in dils]

    if keep_transposed:
        keep = pl.BlockSpec((1, ATTN_W, tm), lambda b, i: (b, 0, jnp.maximum(i - first, 0)))
        keep_shape = (B, ATTN_W, n_keep)
    else:
        keep = pl.BlockSpec((1, tm, ATTN_W), lambda b, i: (b, jnp.maximum(i - first, 0), 0))
        keep_shape = (B, n_keep, ATTN_W)
    out_shape = (
        [jax.ShapeDtypeStruct((B, T, CONV_W), BF16),
         jax.ShapeDtypeStruct((B, T, CONV_W), F32)]
        + folded_shapes * 3
        + [jax.ShapeDtypeStruct(keep_shape, F32),
           jax.ShapeDtypeStruct(keep_shape, F32),
           jax.ShapeDtypeStruct((B, T, MEM_W), BF16),
           jax.ShapeDtypeStruct((B, T, D_MIX), BF16)])
    out_specs = ([row(CONV_W), row(CONV_W)] + folded_specs * 3
                 + [keep, keep, row(MEM_W), row(D_MIX)])
    scratch = [pltpu.VMEM((ATTN_W // LANES, tm, LANES), F32)] * 3 if n > 1 else []
    res = pl.pallas_call(
        functools.partial(_proj_kernel, first=first, keep_transposed=keep_transposed, dils=dils),
        grid=(B, nt),
        in_specs=[row(D),
                  pl.BlockSpec((1, D), lambda b, i: (0, 0)),
                  pl.BlockSpec(w_bf16.shape, lambda b, i: (0, 0))],
        out_specs=tuple(out_specs),
        out_shape=tuple(out_shape),
        scratch_shapes=scratch,
        compiler_params=_params(2),
        name="proj",
    )(x, g.reshape(1, D), w_bf16)
    cb, u = res[0:2]
    q, k, v = res[2:2 + n], res[2 + n:2 + 2 * n], res[2 + 2 * n:2 + 3 * n]
    k32, v32, mq, z = res[2 + 3 * n:]
    return cb, u, q, k, v, k32, v32, mq, z


def _memkv_kernel(x_ref, g_ref, w_ref, k32_ref, v32_ref, k_ref, v_ref):
    x = x_ref[...]
    r = lax.rsqrt(jnp.mean(x * x, axis=-1, keepdims=True) + RMS_EPS)
    h = (x * r * g_ref[...]).astype(BF16)
    kv = jnp.dot(h, w_ref[...], preferred_element_type=F32)
    k_ref[...] = kv[:, :MEM_W].astype(BF16)
    v_ref[...] = kv[:, MEM_W:].astype(BF16)
    n_mem = k32_ref.shape[2]
    for b in range(k32_ref.shape[0]):
        kvt = kv[b * n_mem:(b + 1) * n_mem].T
        k32_ref[b] = kvt[:MEM_W]
        v32_ref[b] = kvt[MEM_W:]


def _mem_kv(mem, g, w_bf16):
    B, M, D = mem.shape
    rows = B * M
    k32, v32, k, v = pl.pallas_call(
        _memkv_kernel,
        out_shape=(jax.ShapeDtypeStruct((B, MEM_W, M), F32),
                   jax.ShapeDtypeStruct((B, MEM_W, M), F32),
                   jax.ShapeDtypeStruct((rows, MEM_W), BF16),
                   jax.ShapeDtypeStruct((rows, MEM_W), BF16)),
        compiler_params=pltpu.CompilerParams(vmem_limit_bytes=VMEM_LIMIT),
        name="memkv",
    )(mem.reshape(rows, D), g.reshape(1, D), w_bf16)
    return k32, v32, k.reshape(B, M, MEM_W), v.reshape(B, M, MEM_W)


def _band_bias_table():
    qi = np.arange(SPAN)[:, None] + SPAN
    ki = np.arange(2 * SPAN)[None, :]
    dist = qi - ki
    valid = (dist >= 0) & (dist <= SPAN)
    slopes = _slopes()
    tabs = []
    for _, dil in DILATED_CONFIGS:
        for first in (False, True):
            ok = valid & (ki >= SPAN) if first else valid
            for h in range(ATTN_HEADS):
                bias = -(slopes[h] * dil) * dist
                tabs.append(np.where(ok, bias, -np.inf))
    return np.stack(tabs).astype(np.float32)


def _pair_attention(qb, kb, vb, bias_lo, bias_hi):
    lane = lax.broadcasted_iota(jnp.int32, qb.shape, 1)
    lo = lane < HEAD_DIM
    zero = jnp.zeros_like(qb)
    outs, lses = [], []
    for sel, bias in ((lo, bias_lo), (~lo, bias_hi)):
        qm = jnp.where(sel, qb, zero)
        s = lax.dot_general(qm, kb, (((1,), (1,)), ((), ())), preferred_element_type=F32)
        if bias is not None:
            s = s + bias
        m = jnp.max(s, axis=-1, keepdims=True)
        p = jnp.exp(s - m)
        l = jnp.sum(p, axis=-1, keepdims=True)
        o = jnp.dot(p.astype(BF16), vb, preferred_element_type=F32)
        outs.append(o / l)
        lses.append(m + jnp.log(l))
    o = jnp.where(lo, outs[0], outs[1])
    lse = jnp.where(lo, lses[0], lses[1])
    return o, lse


def _attn_kernel(*refs, steps):
    ins = refs[:15]
    tab_ref = refs[15]
    outs = refs[16:22]
    kcat, vcat = refs[22], refs[23]
    s = pl.program_id(1)
    nblk = ATTN_TILE // SPAN
    for c in range(3):
        _, kp, ko, vp, vo = ins[5 * c:5 * c + 5]
        kcat[c, 0:SPAN] = kp[0, 0]
        kcat[c, SPAN:] = ko[0, 0]
        vcat[c, 0:SPAN] = vp[0, 0]
        vcat[c, SPAN:] = vo[0, 0]

    def body(j, carry):
        row0 = pl.multiple_of(j * SPAN, SPAN)
        for c, (_, dil) in enumerate(DILATED_CONFIGS):
            q_ref = ins[5 * c]
            o_ref, l_ref = outs[2 * c], outs[2 * c + 1]
            units_per_residue = steps // dil
            first = jnp.logical_and(s % units_per_residue == 0, j == 0).astype(jnp.int32)
            base = (c * 2 + first) * ATTN_HEADS
            for p in range(ATTN_HEADS // 2):
                lanes = slice(p * LANES, (p + 1) * LANES)
                qb = q_ref[0, 0, pl.ds(row0, SPAN), lanes]
                kb = kcat[c, pl.ds(row0, 2 * SPAN), lanes]
                vb = vcat[c, pl.ds(row0, 2 * SPAN), lanes]
                o, lse = _pair_attention(qb, kb, vb, tab_ref[base + 2 * p], tab_ref[base + 2 * p + 1])
                o_ref[0, 0, pl.ds(row0, SPAN), lanes] = o.astype(BF16)
                l_ref[0, 0, pl.ds(row0, SPAN), lanes] = lse
        return carry

    lax.fori_loop(0, nblk, body, 0)


def _prompt_attention(q, k, v):
    B, _, T, W = q[0].shape
    steps = T // ATTN_TILE
    prev_per_tile = ATTN_TILE // SPAN
    args, in_specs, out_specs, out_shape = [], [], [], []
    for c, (_, dil) in enumerate(DILATED_CONFIGS):
        L = T // dil
        upr = L // ATTN_TILE

        def own(b, s, upr=upr):
            return (b, s // upr, s % upr, 0)

        def prev(b, s, upr=upr):
            return (b, s // upr, jnp.maximum((s % upr) * prev_per_tile - 1, 0), 0)

        shape = (B, dil, L, W)
        assert q[c].shape == shape
        args += [q[c], k[c], k[c], v[c], v[c]]
        in_specs += [pl.BlockSpec((1, 1, ATTN_TILE, W), own),
                     pl.BlockSpec((1, 1, SPAN, W), prev),
                     pl.BlockSpec((1, 1, ATTN_TILE, W), own),
                     pl.BlockSpec((1, 1, SPAN, W), prev),
                     pl.BlockSpec((1, 1, ATTN_TILE, W), own)]
        out_specs += [pl.BlockSpec((1, 1, ATTN_TILE, W), own),
                      pl.BlockSpec((1, 1, ATTN_TILE, W), own)]
        out_shape += [jax.ShapeDtypeStruct(shape, BF16), jax.ShapeDtypeStruct(shape, F32)]
    tab = jnp.asarray(_band_bias_table())
    args.append(tab)
    in_specs.append(pl.BlockSpec(tab.shape, lambda b, s: (0, 0, 0)))
    res = pl.pallas_call(
        functools.partial(_attn_kernel, steps=steps),
        grid=(B, steps),
        in_specs=in_specs,
        out_specs=tuple(out_specs),
        out_shape=tuple(out_shape),
        scratch_shapes=[pltpu.VMEM((3, ATTN_TILE + SPAN, W), BF16),
                        pltpu.VMEM((3, ATTN_TILE + SPAN, W), BF16)],
        compiler_params=_params(2),
        name="prompt_attn",
    )(*args)
    return list(res)


def _silu(z):
    return z / (1.0 + jnp.exp(-z))


def _gate_project_norm(x, mix_ref, wout_ref, gf_ref):
    y = x + jnp.dot(mix_ref[...], wout_ref[...], preferred_element_type=F32)
    r = lax.rsqrt(jnp.mean(y * y, axis=-1, keepdims=True) + RMS_EPS)
    return y * r * gf_ref[...]


def _unfold(src_ref, slabs, d):
    rows = slabs.shape[1]
    for r in range(d):
        for s in range(slabs.shape[0]):
            slabs[s, pl.ds(r, rows // d, stride=d), :] = (
                src_ref[0, r, :, s * LANES:(s + 1) * LANES].astype(F32))


def _finish_kernel(x_ref, cb_ref, u_ref, up_ref, o1_ref, l1_ref, o4_ref, l4_ref, o16_ref, l16_ref,
                   mq_ref, z_ref, mk_ref, mv_ref, cw_ref, wout_ref, gf_ref, y_ref, ucat, mix,
                   o4s, l4s, o16s, l16s):
    i = pl.program_id(1)
    tm = ROW_TILE
    ucat[0:SUBLANES] = jnp.where(i == 0, 0.0, up_ref[0])
    ucat[SUBLANES:] = u_ref[0]
    cw = cw_ref[...]
    conv = (cw[0:1] * ucat[pl.ds(SUBLANES - 2, tm)]
            + cw[1:2] * ucat[pl.ds(SUBLANES - 1, tm)]
            + cw[2:3] * ucat[pl.ds(SUBLANES, tm)])
    conv_y = cb_ref[0].astype(F32) * conv
    z = z_ref[0].astype(F32)
    mix[:, 0:CONV_W] = (conv_y * _silu(z[:, 0:CONV_W])).astype(BF16)
    dil4, dil16 = DILATED_CONFIGS[1][1], DILATED_CONFIGS[2][1]
    _unfold(o4_ref, o4s, dil4)
    _unfold(l4_ref, l4s, dil4)
    _unfold(o16_ref, o16s, dil16)
    _unfold(l16_ref, l16s, dil16)
    lo, hi = CONV_W, CONV_W + ATTN_W
    for s in range(ATTN_W // LANES):
        lanes = slice(s * LANES, (s + 1) * LANES)
        l1, l4, l16 = l1_ref[0, 0, :, lanes], l4s[s], l16s[s]
        mx = jnp.maximum(jnp.maximum(l1, l4), l16)
        e1, e4, e16 = jnp.exp(l1 - mx), jnp.exp(l4 - mx), jnp.exp(l16 - mx)
        den = e1 + e4 + e16
        attn = (e1 * o1_ref[0, 0, :, lanes].astype(F32) + e4 * o4s[s] + e16 * o16s[s]) / den
        a = lo + s * LANES
        mix[:, a:a + LANES] = (attn * _silu(z[:, a:a + LANES])).astype(BF16)
    for p in range(MEM_HEADS // 2):
        lanes = slice(p * LANES, (p + 1) * LANES)
        o, _ = _pair_attention(mq_ref[0, :, lanes], mk_ref[0, :, lanes], mv_ref[0, :, lanes],
                               None, None)
        a = hi + p * LANES
        mix[:, a:a + LANES] = (o * _silu(z[:, a:a + LANES])).astype(BF16)
    y_ref[0] = _gate_project_norm(x_ref[0], mix, wout_ref, gf_ref)


def _prompt_finish(x, cb, u, attn, mq, z, mk, mv, conv_w, w_out_bf16, g_final):
    B, T, D = x.shape
    tm = ROW_TILE

    def row(width):
        return pl.BlockSpec((1, tm, width), lambda b, i: (b, i, 0))

    def const(shape):
        return pl.BlockSpec(shape, lambda b, i: (0,) * len(shape))

    prev8 = pl.BlockSpec((1, SUBLANES, CONV_W),
                         lambda b, i: (b, jnp.maximum(i * (tm // SUBLANES) - 1, 0), 0))
    memspec = pl.BlockSpec((1,) + mk.shape[1:], lambda b, i: (b, 0, 0))
    in_specs = [row(D), row(CONV_W), row(CONV_W), prev8]
    for _, dil in DILATED_CONFIGS:
        folded = pl.BlockSpec((1, dil, tm // dil, ATTN_W), lambda b, i: (b, 0, i, 0))
        in_specs += [folded, folded]
    in_specs += [row(MEM_W), row(D_MIX), memspec, memspec,
                 const(conv_w.shape), const(w_out_bf16.shape), const((1, D))]
    slabs = pltpu.VMEM((ATTN_W // LANES, tm, LANES), F32)
    return pl.pallas_call(
        _finish_kernel,
        grid=(B, T // tm),
        in_specs=in_specs,
        out_specs=row(D),
        out_shape=jax.ShapeDtypeStruct((B, T, D), F32),
        scratch_shapes=[pltpu.VMEM((tm + SUBLANES, CONV_W), F32),
                        pltpu.VMEM((tm, D_MIX), BF16), slabs, slabs, slabs, slabs],
        compiler_params=_params(2),
        name="prompt_finish",
    )(x, cb, u, u, *attn, mq, z, mk, mv, conv_w, w_out_bf16, g_final.reshape(1, D))


def _sample_tables(S, n_buf):
    j = np.arange(S)[:, None]
    t = np.arange(n_buf)[None, :]
    dist = n_buf + j - t
    count = np.zeros(dist.shape)
    for win, dil in DILATED_CONFIGS:
        count += (dist % dil == 0) & (dist <= win)
    bias = np.where(count > 0, -_slopes()[:, None, None] * dist, -np.inf)
    return count.astype(np.float32), bias.astype(np.float32)


def _sample_attn_kernel(q_ref, kn_ref, vn_ref, kt_ref, vt_ref, mq_ref, mkt_ref, mvt_ref,
                        cnt_ref, bias_ref, o_ref, mo_ref):
    S = q_ref.shape[1]
    nt = (((1,), (1,)), ((), ()))
    slopes = _slopes()
    n_cfg = float(len(DILATED_CONFIGS))
    rowj = lax.broadcasted_iota(jnp.int32, (S, 1), 0)
    q, kn, vn = q_ref[0], kn_ref[0], vn_ref[0]
    cnt = cnt_ref[...]
    outs = []
    for h in range(ATTN_HEADS):
        lanes = slice(h * HEAD_DIM, (h + 1) * HEAD_DIM)
        qh, knh, vnh = q[:, lanes], kn[:, lanes], vn[:, lanes]
        s = jnp.dot(qh.astype(BF16), kt_ref[0, h].astype(BF16),
                    preferred_element_type=F32) + bias_ref[h]
        m = jnp.max(s, axis=-1, keepdims=True)
        s_new = []
        for jp in range(S):
            d = rowj - jp
            sj = jnp.sum(qh * knh[jp:jp + 1], axis=-1, keepdims=True)
            sj = jnp.where(d >= 0, sj - float(slopes[h]) * d.astype(F32), -jnp.inf)
            s_new.append((sj, jnp.where(d == 0, n_cfg, 1.0)))
            m = jnp.maximum(m, sj)
        p = cnt * jnp.exp(s - m)
        l = jnp.sum(p, axis=-1, keepdims=True)
        acc = lax.dot_general(p.astype(BF16), vt_ref[0, h].astype(BF16), nt,
                              preferred_element_type=F32)
        for jp in range(S):
            sj, mult = s_new[jp]
            w = mult * jnp.exp(sj - m)
            l = l + w
            acc = acc + w * vnh[jp:jp + 1]
        outs.append(acc / l)
    o_ref[0] = jnp.concatenate(outs, axis=-1)

    mq = mq_ref[0]
    outs = []
    for h in range(MEM_HEADS):
        lanes = slice(h * HEAD_DIM, (h + 1) * HEAD_DIM)
        s = jnp.dot(mq[:, lanes].astype(BF16), mkt_ref[0, h].astype(BF16),
                    preferred_element_type=F32)
        p = jnp.exp(s - jnp.max(s, axis=-1, keepdims=True))
        l = jnp.sum(p, axis=-1, keepdims=True)
        o = lax.dot_general(p.astype(BF16), mvt_ref[0, h].astype(BF16), nt,
                            preferred_element_type=F32)
        outs.append(o / l)
    mo_ref[0] = jnp.concatenate(outs, axis=-1)


def _sample_attention(q, k_new, v_new, cache_kt, cache_vt, mq, mem_kt, mem_vt):
    DB, S, W = q.shape
    n_buf = cache_kt.shape[-1]
    assert n_buf >= DILATED_CONFIGS[-1][0] and S <= DILATED_CONFIGS[1][1], \
        "new tokens reach each other through dilation 1 only; the buffer covers every window"
    cnt, bias = (jnp.asarray(t) for t in _sample_tables(S, n_buf))

    def tok(width):
        return pl.BlockSpec((1, S, width), lambda b: (b, 0, 0))

    def per_batch(a):
        return pl.BlockSpec((1,) + a.shape[1:], lambda b: (b,) + (0,) * (a.ndim - 1))

    def const(a):
        return pl.BlockSpec(a.shape, lambda b: (0,) * a.ndim)

    return pl.pallas_call(
        _sample_attn_kernel,
        grid=(DB,),
        in_specs=[tok(W), tok(W), tok(W), per_batch(cache_kt), per_batch(cache_vt),
                  tok(MEM_W), per_batch(mem_kt), per_batch(mem_vt), const(cnt), const(bias)],
        out_specs=(tok(W), tok(MEM_W)),
        out_shape=(jax.ShapeDtypeStruct((DB, S, W), F32),
                   jax.ShapeDtypeStruct((DB, S, MEM_W), F32)),
        compiler_params=_params(1),
        name="sample_attn",
    )(q, k_new, v_new, cache_kt, cache_vt, mq, mem_kt, mem_vt, cnt, bias)


def _sample_finish_kernel(x_ref, cb_ref, u0_ref, u1_ref, u2_ref, attn_ref, mem_ref, z_ref,
                          cw_ref, wout_ref, gf_ref, y_ref, mix):
    cw = cw_ref[...]
    conv = cw[0:1] * u0_ref[...] + cw[1:2] * u1_ref[...] + cw[2:3] * u2_ref[...]
    z = z_ref[...].astype(F32)
    lo, hi = CONV_W, CONV_W + ATTN_W
    mix[:, 0:lo] = (cb_ref[...].astype(F32) * conv * _silu(z[:, 0:lo])).astype(BF16)
    mix[:, lo:hi] = (attn_ref[...] * _silu(z[:, lo:hi])).astype(BF16)
    mix[:, hi:] = (mem_ref[...] * _silu(z[:, hi:])).astype(BF16)
    y_ref[...] = _gate_project_norm(x_ref[...], mix, wout_ref, gf_ref)


def _sample_finish(x, cb, u_taps, attn, mem, z, conv_w, w_out_bf16, g_final):
    rows, D = x.shape
    return pl.pallas_call(
        _sample_finish_kernel,
        out_shape=jax.ShapeDtypeStruct((rows, D), F32),
        scratch_shapes=[pltpu.VMEM((rows, D_MIX), BF16)],
        compiler_params=pltpu.CompilerParams(vmem_limit_bytes=VMEM_LIMIT),
        name="sample_finish",
    )(x, cb, *u_taps, attn, mem, z, conv_w, w_out_bf16, g_final.reshape(1, D))


def kernel(x_prompt, x_sample, mem_prompt, cache_win_k, cache_win_v, cache_conv, cache_mem_k,
           cache_mem_v, g_in, w_in, conv_w, g_mem, w_mem_kv, w_out, g_final):
    assert g_in.shape[0] == 1, "the final RMSNorm is fused into the single layer's tail"
    B, T, D = x_prompt.shape
    DB, S, _ = x_sample.shape
    n_buf = cache_win_k.shape[2]
    n_keep = min(DILATED_CONFIGS[-1][0], T)
    w_in_b = w_in[0].astype(BF16)
    w_out_b = w_out[0].astype(BF16)
    dils = tuple(d for _, d in DILATED_CONFIGS)
    cb, u, q, k, v, kt32, vt32, mq, z = _project(x_prompt, g_in[0], w_in_b, n_keep, True, dils)
    mkt32, mvt32, mk, mv = _mem_kv(mem_prompt, g_mem[0], w_mem_kv[0].astype(BF16))
    attn = _prompt_attention(q, k, v)
    y_prompt = _prompt_finish(x_prompt, cb, u, attn, mq, z, mk, mv, conv_w[0], w_out_b, g_final)

    def heads_last(t, heads):
        return jnp.transpose(t.reshape(t.shape[0], heads, HEAD_DIM, -1), (0, 3, 1, 2))[None]

    def heads_first(c):
        return jnp.transpose(c, (0, 2, 3, 1))

    p_win_k = heads_last(kt32, ATTN_HEADS)
    p_win_v = heads_last(vt32, ATTN_HEADS)
    p_conv = u[None, :, -(CONV_WIDTH - 1):]
    p_mem_k = heads_last(mkt32, MEM_HEADS)
    p_mem_v = heads_last(mvt32, MEM_HEADS)
    rows = DB * S
    cb, u, q, k, v, k32, v32, mq, z = _project(x_sample.reshape(1, rows, D), g_in[0], w_in_b,
                                               rows, False, (1,))
    k32 = k32.reshape(DB, S, ATTN_W)
    v32 = v32.reshape(DB, S, ATTN_W)
    attn_s, mem_s = _sample_attention(
        q[0].reshape(DB, S, ATTN_W).astype(F32), k32, v32,
        heads_first(cache_win_k[0]), heads_first(cache_win_v[0]),
        mq.reshape(DB, S, MEM_W).astype(F32),
        heads_first(cache_mem_k[0]), heads_first(cache_mem_v[0]))
    full = jnp.concatenate([cache_conv[0], u.reshape(DB, S, CONV_W)], axis=1)
    taps = [full[:, t:t + S].reshape(rows, CONV_W) for t in range(CONV_WIDTH)]
    y_sample = _sample_finish(x_sample.reshape(rows, D), cb.reshape(rows, CONV_W), taps,
                              attn_s.reshape(rows, ATTN_W), mem_s.reshape(rows, MEM_W),
                              z.reshape(rows, D_MIX), conv_w[0], w_out_b, g_final).reshape(DB, S, D)
    s_win_k = k32.reshape(1, DB, S, ATTN_HEADS, HEAD_DIM)
    s_win_v = v32.reshape(1, DB, S, ATTN_HEADS, HEAD_DIM)
    s_conv = full[None, :, -(CONV_WIDTH - 1):]
    return (y_prompt, y_sample, p_win_k, p_win_v, p_conv, p_mem_k, p_mem_v,
            s_win_k, s_win_v, s_conv)
```

```python
import functools

import numpy as np
import jax
import jax.numpy as jnp
from jax import lax
from jax.experimental import pallas as pl
from jax.experimental.pallas import tpu as pltpu

F32 = jnp.float32
BF16 = jnp.bfloat16

HEAD_DIM = 64
ATTN_HEADS = 8
MEM_HEADS = 4
CONV_W = 256
ATTN_W = 512
MEM_W = 256
D_MIX = 1024
CONV_WIDTH = 3
DILATED_CONFIGS = ((128, 1), (512, 4), (2048, 16))
SPAN = 128
ALIBI_MAX = 8.0
RMS_EPS = 1e-6
Q_SCALE = HEAD_DIM ** -0.5

LANES = 128
SUBLANES = 8
VMEM_LIMIT = 56 * 1024 * 1024

_C_CB, _C_CC, _C_Q, _C_K, _C_V, _C_MQ, _C_Z, _C_END = 0, 256, 768, 1280, 1792, 2304, 2560, 3584

ROW_TILE = 512
ATTN_TILE = 512


def _slopes():
    h = np.arange(ATTN_HEADS, dtype=np.float64) + 1.0
    return np.exp2(-ALIBI_MAX * h / ATTN_HEADS)


def _params(n_axes):
    return pltpu.CompilerParams(
        dimension_semantics=("arbitrary",) * n_axes,
        vmem_limit_bytes=VMEM_LIMIT)


def _fold_store(res, out_refs, slabs, dils):
    rows = res.shape[0]
    out_refs[0][0, 0] = res.astype(BF16)
    if len(dils) == 1:
        return
    n_slab = res.shape[1] // LANES
    src, prev_d = slabs[0], 1
    for s in range(n_slab):
        src[s] = res[:, s * LANES:(s + 1) * LANES]
    for idx, (d, o_ref) in enumerate(zip(dils[1:], out_refs[1:])):
        step, n = d // prev_d, rows // d
        dst = slabs[idx + 1] if idx + 2 < len(dils) else None
        for rp in range(prev_d):
            for q in range(step):
                r = q * prev_d + rp
                for s in range(n_slab):
                    val = src[s, pl.ds(rp * (rows // prev_d) + q, n, stride=step), :]
                    o_ref[0, r, :, s * LANES:(s + 1) * LANES] = val.astype(BF16)
                    if dst is not None:
                        dst[s, pl.ds(r * n, n), :] = val
        src, prev_d = dst, d


def _proj_kernel(x_ref, g_ref, w_ref, wkvt_ref, *refs, first, keep_transposed, dils):
    n = len(dils)
    cb_ref, u_ref = refs[0:2]
    q_refs, k_refs, v_refs = refs[2:2 + n], refs[2 + n:2 + 2 * n], refs[2 + 2 * n:2 + 3 * n]
    k32_ref, v32_ref, mq_ref, z_ref = refs[2 + 3 * n:6 + 3 * n]
    scratch = refs[6 + 3 * n:]
    slabs = [scratch[a * (n - 1):(a + 1) * (n - 1)] for a in range(3)]
    x = x_ref[0]
    r = lax.rsqrt(jnp.mean(x * x, axis=-1, keepdims=True) + RMS_EPS)
    h = (x * r * g_ref[...]).astype(BF16)

    def seg(lo, hi):
        return jnp.dot(h, w_ref[:, lo:hi], preferred_element_type=F32)

    cb_ref[0] = seg(_C_CB, _C_CC).astype(BF16)
    cch = seg(_C_CC, _C_Q)
    u_ref[0] = cch[:, :CONV_W] * cch[:, CONV_W:]
    _fold_store(seg(_C_Q, _C_K) * Q_SCALE, q_refs, slabs[0], dils)
    k = seg(_C_K, _C_V)
    _fold_store(k, k_refs, slabs[1], dils)
    v = seg(_C_V, _C_MQ)
    _fold_store(v, v_refs, slabs[2], dils)
    mq_ref[0] = (seg(_C_MQ, _C_Z) * Q_SCALE).astype(BF16)
    z_ref[0] = seg(_C_Z, _C_END).astype(BF16)
    if keep_transposed:
        @pl.when(pl.program_id(1) >= first)
        def _():
            nt = (((1,), (1,)), ((), ()))
            k32_ref[0] = lax.dot_general(wkvt_ref[0:ATTN_W], h, nt, preferred_element_type=F32)
            v32_ref[0] = lax.dot_general(wkvt_ref[ATTN_W:], h, nt, preferred_element_type=F32)
    else:
        k32_ref[0] = k
        v32_ref[0] = v


def _project(x, g, w_bf16, wkvt_bf16, n_keep, keep_transposed, dils):
    B, T, D = x.shape
    tm = ROW_TILE
    nt = T // tm
    first = (T - n_keep) // tm
    n = len(dils)

    def row(width):
        return pl.BlockSpec((1, tm, width), lambda b, i: (b, i, 0))

    folded_specs = [pl.BlockSpec((1, d, tm // d, ATTN_W), lambda b, i: (b, 0, i, 0)) for d in dils]
    folded_shapes = [jax.ShapeDtypeStruct((B, d, T // d, ATTN_W), BF16) for d in dils]

    if keep_transposed:
        keep = pl.BlockSpec((1, ATTN_W, tm), lambda b, i: (b, 0, jnp.maximum(i - first, 0)))
        keep_shape = (B, ATTN_W, n_keep)
    else:
        keep = pl.BlockSpec((1, tm, ATTN_W), lambda b, i: (b, jnp.maximum(i - first, 0), 0))
        keep_shape = (B, n_keep, ATTN_W)
    out_shape = (
        [jax.ShapeDtypeStruct((B, T, CONV_W), BF16),
         jax.ShapeDtypeStruct((B, T, CONV_W), F32)]
        + folded_shapes * 3
        + [jax.ShapeDtypeStruct(keep_shape, F32),
           jax.ShapeDtypeStruct(keep_shape, F32),
           jax.ShapeDtypeStruct((B, T, MEM_W), BF16),
           jax.ShapeDtypeStruct((B, T, D_MIX), BF16)])
    out_specs = ([row(CONV_W), row(CONV_W)] + folded_specs * 3
                 + [keep, keep, row(MEM_W), row(D_MIX)])
    scratch = [pltpu.VMEM((ATTN_W // LANES, tm, LANES), F32)] * (3 * (n - 1))
    res = pl.pallas_call(
        functools.partial(_proj_kernel, first=first, keep_transposed=keep_transposed, dils=dils),
        grid=(B, nt),
        in_specs=[row(D),
                  pl.BlockSpec((1, D), lambda b, i: (0, 0)),
                  pl.BlockSpec(w_bf16.shape, lambda b, i: (0, 0)),
                  pl.BlockSpec(wkvt_bf16.shape, lambda b, i: (0, 0))],
        out_specs=tuple(out_specs),
        out_shape=tuple(out_shape),
        scratch_shapes=scratch,
        compiler_params=_params(2),
        name="proj",
    )(x, g.reshape(1, D), w_bf16, wkvt_bf16)
    cb, u = res[0:2]
    q, k, v = res[2:2 + n], res[2 + n:2 + 2 * n], res[2 + 2 * n:2 + 3 * n]
    k32, v32, mq, z = res[2 + 3 * n:]
    return cb, u, q, k, v, k32, v32, mq, z


def _memkv_kernel(x_ref, g_ref, w_ref, k32_ref, v32_ref, k_ref, v_ref):
    x = x_ref[...]
    r = lax.rsqrt(jnp.mean(x * x, axis=-1, keepdims=True) + RMS_EPS)
    h = (x * r * g_ref[...]).astype(BF16)
    kv = jnp.dot(h, w_ref[...], preferred_element_type=F32)
    k_ref[...] = kv[:, :MEM_W].astype(BF16)
    v_ref[...] = kv[:, MEM_W:].astype(BF16)
    n_mem = k32_ref.shape[2]
    for b in range(k32_ref.shape[0]):
        kvt = kv[b * n_mem:(b + 1) * n_mem].T
        k32_ref[b] = kvt[:MEM_W]
        v32_ref[b] = kvt[MEM_W:]


def _mem_kv(mem, g, w_bf16):
    B, M, D = mem.shape
    rows = B * M
    k32, v32, k, v = pl.pallas_call(
        _memkv_kernel,
        out_shape=(jax.ShapeDtypeStruct((B, MEM_W, M), F32),
                   jax.ShapeDtypeStruct((B, MEM_W, M), F32),
                   jax.ShapeDtypeStruct((rows, MEM_W), BF16),
                   jax.ShapeDtypeStruct((rows, MEM_W), BF16)),
        compiler_params=pltpu.CompilerParams(vmem_limit_bytes=VMEM_LIMIT),
        name="memkv",
    )(mem.reshape(rows, D), g.reshape(1, D), w_bf16)
    return k32, v32, k.reshape(B, M, MEM_W), v.reshape(B, M, MEM_W)


def _band_bias_table():
    qi = np.arange(SPAN)[:, None] + SPAN
    ki = np.arange(2 * SPAN)[None, :]
    dist = qi - ki
    valid = (dist >= 0) & (dist <= SPAN)
    slopes = _slopes()
    tabs = []
    for _, dil in DILATED_CONFIGS:
        for first in (False, True):
            ok = valid & (ki >= SPAN) if first else valid
            for h in range(ATTN_HEADS):
                bias = -(slopes[h] * dil) * dist
                tabs.append(np.where(ok, bias, -np.inf))
    return np.stack(tabs).astype(np.float32)


def _pair_attention(qb, kb, vb, bias_lo, bias_hi):
    lane = lax.broadcasted_iota(jnp.int32, qb.shape, 1)
    lo = lane < HEAD_DIM
    zero = jnp.zeros_like(qb)
    outs, lses = [], []
    for sel, bias in ((lo, bias_lo), (~lo, bias_hi)):
        qm = jnp.where(sel, qb, zero)
        s = lax.dot_general(qm, kb, (((1,), (1,)), ((), ())), preferred_element_type=F32)
        if bias is not None:
            s = s + bias
        m = jnp.max(s, axis=-1, keepdims=True)
        p = jnp.exp(s - m)
        l = jnp.sum(p, axis=-1, keepdims=True)
        o = jnp.dot(p.astype(BF16), vb, preferred_element_type=F32)
        outs.append(o / l)
        lses.append(m + jnp.log(l))
    o = jnp.where(lo, outs[0], outs[1])
    lse = jnp.where(lo, lses[0], lses[1])
    return o, lse


def _attn_kernel(*refs, steps):
    ins = refs[:15]
    tab_ref = refs[15]
    outs = refs[16:22]
    kcat, vcat = refs[22], refs[23]
    s = pl.program_id(1)
    nblk = ATTN_TILE // SPAN
    for c in range(3):
        _, kp, ko, vp, vo = ins[5 * c:5 * c + 5]
        kcat[c, 0:SPAN] = kp[0, 0]
        kcat[c, SPAN:] = ko[0, 0]
        vcat[c, 0:SPAN] = vp[0, 0]
        vcat[c, SPAN:] = vo[0, 0]

    def body(j, carry):
        row0 = pl.multiple_of(j * SPAN, SPAN)
        for c, (_, dil) in enumerate(DILATED_CONFIGS):
            q_ref = ins[5 * c]
            o_ref, l_ref = outs[2 * c], outs[2 * c + 1]
            units_per_residue = steps // dil
            first = jnp.logical_and(s % units_per_residue == 0, j == 0).astype(jnp.int32)
            base = (c * 2 + first) * ATTN_HEADS
            for p in range(ATTN_HEADS // 2):
                lanes = slice(p * LANES, (p + 1) * LANES)
                qb = q_ref[0, 0, pl.ds(row0, SPAN), lanes]
                kb = kcat[c, pl.ds(row0, 2 * SPAN), lanes]
                vb = vcat[c, pl.ds(row0, 2 * SPAN), lanes]
                o, lse = _pair_attention(qb, kb, vb, tab_ref[base + 2 * p], tab_ref[base + 2 * p + 1])
                o_ref[0, 0, pl.ds(row0, SPAN), lanes] = o.astype(BF16)
                l_ref[0, 0, pl.ds(row0, SPAN), lanes] = lse
        return carry

    lax.fori_loop(0, nblk, body, 0)


def _prompt_attention(q, k, v):
    B, _, T, W = q[0].shape
    steps = T // ATTN_TILE
    prev_per_tile = ATTN_TILE // SPAN
    args, in_specs, out_specs, out_shape = [], [], [], []
    for c, (_, dil) in enumerate(DILATED_CONFIGS):
        L = T // dil
        upr = L // ATTN_TILE

        def own(b, s, upr=upr):
            return (b, s // upr, s % upr, 0)

        def prev(b, s, upr=upr):
            return (b, s // upr, jnp.maximum((s % upr) * prev_per_tile - 1, 0), 0)

        shape = (B, dil, L, W)
        assert q[c].shape == shape
        args += [q[c], k[c], k[c], v[c], v[c]]
        in_specs += [pl.BlockSpec((1, 1, ATTN_TILE, W), own),
                     pl.BlockSpec((1, 1, SPAN, W), prev),
                     pl.BlockSpec((1, 1, ATTN_TILE, W), own),
                     pl.BlockSpec((1, 1, SPAN, W), prev),
                     pl.BlockSpec((1, 1, ATTN_TILE, W), own)]
        out_specs += [pl.BlockSpec((1, 1, ATTN_TILE, W), own),
                      pl.BlockSpec((1, 1, ATTN_TILE, W), own)]
        out_shape += [jax.ShapeDtypeStruct(shape, BF16), jax.ShapeDtypeStruct(shape, F32)]
    tab = jnp.asarray(_band_bias_table())
    args.append(tab)
    in_specs.append(pl.BlockSpec(tab.shape, lambda b, s: (0, 0, 0)))
    res = pl.pallas_call(
        functools.partial(_attn_kernel, steps=steps),
        grid=(B, steps),
        in_specs=in_specs,
        out_specs=tuple(out_specs),
        out_shape=tuple(out_shape),
        scratch_shapes=[pltpu.VMEM((3, ATTN_TILE + SPAN, W), BF16),
                        pltpu.VMEM((3, ATTN_TILE + SPAN, W), BF16)],
        compiler_params=_params(2),
        name="prompt_attn",
    )(*args)
    return list(res)


def _silu(z):
    return z / (1.0 + jnp.exp(-z))


def _gate_project_norm(x, mix_ref, wout_ref, gf_ref):
    y = x + jnp.dot(mix_ref[...], wout_ref[...], preferred_element_type=F32)
    r = lax.rsqrt(jnp.mean(y * y, axis=-1, keepdims=True) + RMS_EPS)
    return y * r * gf_ref[...]


def _unfold(src_ref, slabs, d, tmp=None, step=4):
    n_slab, rows = slabs.shape[0], slabs.shape[1]

    def piece(r, s):
        return src_ref[0, r, :, s * LANES:(s + 1) * LANES].astype(F32)

    if d <= step:
        for r in range(d):
            for s in range(n_slab):
                slabs[s, pl.ds(r, rows // d, stride=d), :] = piece(r, s)
        return
    prev_d, n = d // step, rows // d
    per = rows // prev_d
    for rp in range(prev_d):
        for q in range(step):
            for s in range(n_slab):
                tmp[s, pl.ds(rp * per + q, n, stride=step), :] = piece(q * prev_d + rp, s)
    for rp in range(prev_d):
        for s in range(n_slab):
            slabs[s, pl.ds(rp, per, stride=prev_d), :] = tmp[s, pl.ds(rp * per, per), :]


def _finish_kernel(x_ref, cb_ref, u_ref, up_ref, o1_ref, l1_ref, o4_ref, l4_ref, o16_ref, l16_ref,
                   mq_ref, z_ref, mk_ref, mv_ref, cw_ref, wout_ref, gf_ref, y_ref, ucat, mix,
                   o4s, l4s, o16s, l16s, otmp, ltmp):
    i = pl.program_id(1)
    tm = ROW_TILE
    ucat[0:SUBLANES] = jnp.where(i == 0, 0.0, up_ref[0])
    ucat[SUBLANES:] = u_ref[0]
    cw = cw_ref[...]
    conv = (cw[0:1] * ucat[pl.ds(SUBLANES - 2, tm)]
            + cw[1:2] * ucat[pl.ds(SUBLANES - 1, tm)]
            + cw[2:3] * ucat[pl.ds(SUBLANES, tm)])
    conv_y = cb_ref[0].astype(F32) * conv
    z = z_ref[0].astype(F32)
    mix[:, 0:CONV_W] = (conv_y * _silu(z[:, 0:CONV_W])).astype(BF16)
    dil4, dil16 = DILATED_CONFIGS[1][1], DILATED_CONFIGS[2][1]
    _unfold(o4_ref, o4s, dil4)
    _unfold(l4_ref, l4s, dil4)
    _unfold(o16_ref, o16s, dil16, otmp)
    _unfold(l16_ref, l16s, dil16, ltmp)
    lo, hi = CONV_W, CONV_W + ATTN_W
    for s in range(ATTN_W // LANES):
        lanes = slice(s * LANES, (s + 1) * LANES)
        l1, l4, l16 = l1_ref[0, 0, :, lanes], l4s[s], l16s[s]
        mx = jnp.maximum(jnp.maximum(l1, l4), l16)
        e1, e4, e16 = jnp.exp(l1 - mx), jnp.exp(l4 - mx), jnp.exp(l16 - mx)
        den = e1 + e4 + e16
        attn = (e1 * o1_ref[0, 0, :, lanes].astype(F32) + e4 * o4s[s] + e16 * o16s[s]) / den
        a = lo + s * LANES
        mix[:, a:a + LANES] = (attn * _silu(z[:, a:a + LANES])).astype(BF16)
    for p in range(MEM_HEADS // 2):
        lanes = slice(p * LANES, (p + 1) * LANES)
        o, _ = _pair_attention(mq_ref[0, :, lanes], mk_ref[0, :, lanes], mv_ref[0, :, lanes],
                               None, None)
        a = hi + p * LANES
        mix[:, a:a + LANES] = (o * _silu(z[:, a:a + LANES])).astype(BF16)
    y_ref[0] = _gate_project_norm(x_ref[0], mix, wout_ref, gf_ref)


def _prompt_finish(x, cb, u, attn, mq, z, mk, mv, conv_w, w_out_bf16, g_final):
    B, T, D = x.shape
    tm = ROW_TILE

    def row(width):
        return pl.BlockSpec((1, tm, width), lambda b, i: (b, i, 0))

    def const(shape):
        return pl.BlockSpec(shape, lambda b, i: (0,) * len(shape))

    prev8 = pl.BlockSpec((1, SUBLANES, CONV_W),
                         lambda b, i: (b, jnp.maximum(i * (tm // SUBLANES) - 1, 0), 0))
    memspec = pl.BlockSpec((1,) + mk.shape[1:], lambda b, i: (b, 0, 0))
    in_specs = [row(D), row(CONV_W), row(CONV_W), prev8]
    for _, dil in DILATED_CONFIGS:
        folded = pl.BlockSpec((1, dil, tm // dil, ATTN_W), lambda b, i: (b, 0, i, 0))
        in_specs += [folded, folded]
    in_specs += [row(MEM_W), row(D_MIX), memspec, memspec,
                 const(conv_w.shape), const(w_out_bf16.shape), const((1, D))]
    slabs = pltpu.VMEM((ATTN_W // LANES, tm, LANES), F32)
    return pl.pallas_call(
        _finish_kernel,
        grid=(B, T // tm),
        in_specs=in_specs,
        out_specs=row(D),
        out_shape=jax.ShapeDtypeStruct((B, T, D), F32),
        scratch_shapes=[pltpu.VMEM((tm + SUBLANES, CONV_W), F32),
                        pltpu.VMEM((tm, D_MIX), BF16)] + [slabs] * 6,
        compiler_params=_params(2),
        name="prompt_finish",
    )(x, cb, u, u, *attn, mq, z, mk, mv, conv_w, w_out_bf16, g_final.reshape(1, D))


def _sample_tables(S, n_buf):
    j = np.arange(S)[:, None]
    t = np.arange(n_buf)[None, :]
    dist = n_buf + j - t
    count = np.zeros(dist.shape)
    for win, dil in DILATED_CONFIGS:
        count += (dist % dil == 0) & (dist <= win)
    bias = np.where(count > 0, -_slopes()[:, None, None] * dist, -np.inf)
    return count.astype(np.float32), bias.astype(np.float32)


def _sample_attn_kernel(q_ref, kn_ref, vn_ref, kt_ref, vt_ref, mq_ref, mkt_ref, mvt_ref,
                        cnt_ref, bias_ref, o_ref, mo_ref):
    S = q_ref.shape[1]
    nt = (((1,), (1,)), ((), ()))
    slopes = _slopes()
    n_cfg = float(len(DILATED_CONFIGS))
    rowj = lax.broadcasted_iota(jnp.int32, (S, 1), 0)
    q, kn, vn = q_ref[0], kn_ref[0], vn_ref[0]
    mq = mq_ref[0]
    cnt = cnt_ref[...]

    def head(a, h):
        return a[:, h * HEAD_DIM:(h + 1) * HEAD_DIM]

    scores = [jnp.dot(head(q, h).astype(BF16), kt_ref[0, h].astype(BF16),
                      preferred_element_type=F32) + bias_ref[h] for h in range(ATTN_HEADS)]
    mscores = [jnp.dot(head(mq, h).astype(BF16), mkt_ref[0, h].astype(BF16),
                       preferred_element_type=F32) for h in range(MEM_HEADS)]
    probs, stats = [], []
    for h in range(ATTN_HEADS):
        qh, knh = head(q, h), head(kn, h)
        m = jnp.max(scores[h], axis=-1, keepdims=True)
        s_new = []
        for jp in range(S):
            d = rowj - jp
            sj = jnp.sum(qh * knh[jp:jp + 1], axis=-1, keepdims=True)
            sj = jnp.where(d >= 0, sj - float(slopes[h]) * d.astype(F32), -jnp.inf)
            s_new.append(sj)
            m = jnp.maximum(m, sj)
        p = cnt * jnp.exp(scores[h] - m)
        l = jnp.sum(p, axis=-1, keepdims=True)
        w_new = []
        for jp in range(S):
            w = jnp.where(rowj == jp, n_cfg, 1.0) * jnp.exp(s_new[jp] - m)
            l = l + w
            w_new.append(w)
        probs.append(p.astype(BF16))
        stats.append((l, w_new))
    mprobs, mden = [], []
    for h in range(MEM_HEADS):
        p = jnp.exp(mscores[h] - jnp.max(mscores[h], axis=-1, keepdims=True))
        mden.append(jnp.sum(p, axis=-1, keepdims=True))
        mprobs.append(p.astype(BF16))
    accs = [lax.dot_general(probs[h], vt_ref[0, h].astype(BF16), nt,
                            preferred_element_type=F32) for h in range(ATTN_HEADS)]
    maccs = [lax.dot_general(mprobs[h], mvt_ref[0, h].astype(BF16), nt,
                             preferred_element_type=F32) for h in range(MEM_HEADS)]
    outs = []
    for h in range(ATTN_HEADS):
        l, w_new = stats[h]
        acc, vnh = accs[h], head(vn, h)
        for jp in range(S):
            acc = acc + w_new[jp] * vnh[jp:jp + 1]
        outs.append(acc / l)
    o_ref[0] = jnp.concatenate(outs, axis=-1)
    mo_ref[0] = jnp.concatenate([maccs[h] / mden[h] for h in range(MEM_HEADS)], axis=-1)


def _sample_attention(q, k_new, v_new, cache_kt, cache_vt, mq, mem_kt, mem_vt):
    DB, S, W = q.shape
    n_buf = cache_kt.shape[-1]
    assert n_buf >= DILATED_CONFIGS[-1][0] and S <= DILATED_CONFIGS[1][1], \
        "new tokens reach each other through dilation 1 only; the buffer covers every window"
    cnt, bias = (jnp.asarray(t) for t in _sample_tables(S, n_buf))

    def tok(width):
        return pl.BlockSpec((1, S, width), lambda b: (b, 0, 0))

    def per_batch(a):
        return pl.BlockSpec((1,) + a.shape[1:], lambda b: (b,) + (0,) * (a.ndim - 1))

    def const(a):
        return pl.BlockSpec(a.shape, lambda b: (0,) * a.ndim)

    return pl.pallas_call(
        _sample_attn_kernel,
        grid=(DB,),
        in_specs=[tok(W), tok(W), tok(W), per_batch(cache_kt), per_batch(cache_vt),
                  tok(MEM_W), per_batch(mem_kt), per_batch(mem_vt), const(cnt), const(bias)],
        out_specs=(tok(W), tok(MEM_W)),
        out_shape=(jax.ShapeDtypeStruct((DB, S, W), F32),
                   jax.ShapeDtypeStruct((DB, S, MEM_W), F32)),
        compiler_params=_params(1),
        name="sample_attn",
    )(q, k_new, v_new, cache_kt, cache_vt, mq, mem_kt, mem_vt, cnt, bias)


def _sample_finish_kernel(x_ref, cb_ref, u0_ref, u1_ref, u2_ref, attn_ref, mem_ref, z_ref,
                          cw_ref, wout_ref, gf_ref, y_ref, mix):
    cw = cw_ref[...]
    conv = cw[0:1] * u0_ref[...] + cw[1:2] * u1_ref[...] + cw[2:3] * u2_ref[...]
    z = z_ref[...].astype(F32)
    lo, hi = CONV_W, CONV_W + ATTN_W
    mix[:, 0:lo] = (cb_ref[...].astype(F32) * conv * _silu(z[:, 0:lo])).astype(BF16)
    mix[:, lo:hi] = (attn_ref[...] * _silu(z[:, lo:hi])).astype(BF16)
    mix[:, hi:] = (mem_ref[...] * _silu(z[:, hi:])).astype(BF16)
    y_ref[...] = _gate_project_norm(x_ref[...], mix, wout_ref, gf_ref)


def _sample_finish(x, cb, u_taps, attn, mem, z, conv_w, w_out_bf16, g_final):
    rows, D = x.shape
    return pl.pallas_call(
        _sample_finish_kernel,
        out_shape=jax.ShapeDtypeStruct((rows, D), F32),
        scratch_shapes=[pltpu.VMEM((rows, D_MIX), BF16)],
        compiler_params=pltpu.CompilerParams(vmem_limit_bytes=VMEM_LIMIT),
        name="sample_finish",
    )(x, cb, *u_taps, attn, mem, z, conv_w, w_out_bf16, g_final.reshape(1, D))


def kernel(x_prompt, x_sample, mem_prompt, cache_win_k, cache_win_v, cache_conv, cache_mem_k,
           cache_mem_v, g_in, w_in, conv_w, g_mem, w_mem_kv, w_out, g_final):
    assert g_in.shape[0] == 1, "the final RMSNorm is fused into the single layer's tail"
    B, T, D = x_prompt.shape
    DB, S, _ = x_sample.shape
    n_buf = cache_win_k.shape[2]
    n_keep = min(DILATED_CONFIGS[-1][0], T)
    w_in_b = w_in[0].astype(BF16)
    w_out_b = w_out[0].astype(BF16)
    dils = tuple(d for _, d in DILATED_CONFIGS)
    wkvt_b = w_in[0][:, _C_K:_C_MQ].T.astype(BF16)
    cb, u, q, k, v, kt32, vt32, mq, z = _project(x_prompt, g_in[0], w_in_b, wkvt_b, n_keep, True,
                                                 dils)
    mkt32, mvt32, mk, mv = _mem_kv(mem_prompt, g_mem[0], w_mem_kv[0].astype(BF16))
    attn = _prompt_attention(q, k, v)
    y_prompt = _prompt_finish(x_prompt, cb, u, attn, mq, z, mk, mv, conv_w[0], w_out_b, g_final)

    def heads_last(t, heads):
        return jnp.transpose(t.reshape(t.shape[0], heads, HEAD_DIM, -1), (0, 3, 1, 2))[None]

    def heads_first(c):
        return jnp.transpose(c, (0, 2, 3, 1))

    p_win_k = heads_last(kt32, ATTN_HEADS)
    p_win_v = heads_last(vt32, ATTN_HEADS)
    p_conv = u[None, :, -(CONV_WIDTH - 1):]
    p_mem_k = heads_last(mkt32, MEM_HEADS)
    p_mem_v = heads_last(mvt32, MEM_HEADS)
    rows = DB * S
    cb, u, q, k, v, k32, v32, mq, z = _project(x_sample.reshape(1, rows, D), g_in[0], w_in_b,
                                               wkvt_b, rows, False, (1,))
    k32 = k32.reshape(DB, S, ATTN_W)
    v32 = v32.reshape(DB, S, ATTN_W)
    attn_s, mem_s = _sample_attention(
        q[0].reshape(DB, S, ATTN_W).astype(F32), k32, v32,
        heads_first(cache_win_k[0]), heads_first(cache_win_v[0]),
        mq.reshape(DB, S, MEM_W).astype(F32),
        heads_first(cache_mem_k[0]), heads_first(cache_mem_v[0]))
    full = jnp.concatenate([cache_conv[0], u.reshape(DB, S, CONV_W)], axis=1)
    taps = [full[:, t:t + S].reshape(rows, CONV_W) for t in range(CONV_WIDTH)]
    y_sample = _sample_finish(x_sample.reshape(rows, D), cb.reshape(rows, CONV_W), taps,
                              attn_s.reshape(rows, ATTN_W), mem_s.reshape(rows, MEM_W),
                              z.reshape(rows, D_MIX), conv_w[0], w_out_b, g_final).reshape(DB, S, D)
    s_win_k = k32.reshape(1, DB, S, ATTN_HEADS, HEAD_DIM)
    s_win_v = v32.reshape(1, DB, S, ATTN_HEADS, HEAD_DIM)
    s_conv = full[None, :, -(CONV_WIDTH - 1):]
    return (y_prompt, y_sample, p_win_k, p_win_v, p_conv, p_mem_k, p_mem_v,
            s_win_k, s_win_v, s_conv)
```

```python
import functools

import numpy as np
import jax
import jax.numpy as jnp
from jax import lax
from jax.experimental import pallas as pl
from jax.experimental.pallas import tpu as pltpu

F32 = jnp.float32
BF16 = jnp.bfloat16

HEAD_DIM = 64
ATTN_HEADS = 8
MEM_HEADS = 4
CONV_W = 256
ATTN_W = 512
MEM_W = 256
D_MIX = 1024
CONV_WIDTH = 3
DILATED_CONFIGS = ((128, 1), (512, 4), (2048, 16))
SPAN = 128
ALIBI_MAX = 8.0
RMS_EPS = 1e-6
Q_SCALE = HEAD_DIM ** -0.5
LOG2E = float(np.log2(np.e))
LN2 = float(np.log(2.0))

LANES = 128
SUBLANES = 8
VMEM_LIMIT = 56 * 1024 * 1024

_C_CB, _C_CC, _C_Q, _C_K, _C_V, _C_MQ, _C_Z, _C_END = 0, 256, 768, 1280, 1792, 2304, 2560, 3584

ROW_TILE = 512
ATTN_TILE = 512


def _slopes():
    h = np.arange(ATTN_HEADS, dtype=np.float64) + 1.0
    return np.exp2(-ALIBI_MAX * h / ATTN_HEADS)


def _params(n_axes):
    return pltpu.CompilerParams(
        dimension_semantics=("arbitrary",) * n_axes,
        vmem_limit_bytes=VMEM_LIMIT)


def _fold_store(res, out_refs, slabs, dils):
    rows = res.shape[0]
    out_refs[0][0, 0] = res.astype(BF16)
    if len(dils) == 1:
        return
    n_slab = res.shape[1] // LANES
    src, prev_d = slabs[0], 1
    for s in range(n_slab):
        src[s] = res[:, s * LANES:(s + 1) * LANES]
    for idx, (d, o_ref) in enumerate(zip(dils[1:], out_refs[1:])):
        step, n = d // prev_d, rows // d
        dst = slabs[idx + 1] if idx + 2 < len(dils) else None
        for rp in range(prev_d):
            for q in range(step):
                r = q * prev_d + rp
                for s in range(n_slab):
                    val = src[s, pl.ds(rp * (rows // prev_d) + q, n, stride=step), :]
                    o_ref[0, r, :, s * LANES:(s + 1) * LANES] = val.astype(BF16)
                    if dst is not None:
                        dst[s, pl.ds(r * n, n), :] = val
        src, prev_d = dst, d


def _proj_kernel(x_ref, g_ref, w_ref, wkvt_ref, *refs, first, keep_transposed, dils, q_scale):
    n = len(dils)
    cb_ref, u_ref = refs[0:2]
    q_refs, k_refs, v_refs = refs[2:2 + n], refs[2 + n:2 + 2 * n], refs[2 + 2 * n:2 + 3 * n]
    k32_ref, v32_ref, mq_ref, z_ref = refs[2 + 3 * n:6 + 3 * n]
    scratch = refs[6 + 3 * n:]
    slabs = [scratch[a * (n - 1):(a + 1) * (n - 1)] for a in range(3)]
    x = x_ref[0]
    r = lax.rsqrt(jnp.mean(x * x, axis=-1, keepdims=True) + RMS_EPS)
    h = (x * r * g_ref[...]).astype(BF16)

    def seg(lo, hi):
        return jnp.dot(h, w_ref[:, lo:hi], preferred_element_type=F32)

    cb_ref[0] = seg(_C_CB, _C_CC).astype(BF16)
    cch = seg(_C_CC, _C_Q)
    u_ref[0] = cch[:, :CONV_W] * cch[:, CONV_W:]
    _fold_store(seg(_C_Q, _C_K) * q_scale, q_refs, slabs[0], dils)
    k = seg(_C_K, _C_V)
    _fold_store(k, k_refs, slabs[1], dils)
    v = seg(_C_V, _C_MQ)
    _fold_store(v, v_refs, slabs[2], dils)
    mq_ref[0] = (seg(_C_MQ, _C_Z) * q_scale).astype(BF16)
    z_ref[0] = seg(_C_Z, _C_END).astype(BF16)
    if keep_transposed:
        @pl.when(pl.program_id(1) >= first)
        def _():
            nt = (((1,), (1,)), ((), ()))
            k32_ref[0] = lax.dot_general(wkvt_ref[0:ATTN_W], h, nt, preferred_element_type=F32)
            v32_ref[0] = lax.dot_general(wkvt_ref[ATTN_W:], h, nt, preferred_element_type=F32)
    else:
        k32_ref[0] = k
        v32_ref[0] = v


def _project(x, g, w_bf16, wkvt_bf16, n_keep, keep_transposed, dils, q_scale):
    B, T, D = x.shape
    tm = ROW_TILE
    nt = T // tm
    first = (T - n_keep) // tm
    n = len(dils)

    def row(width):
        return pl.BlockSpec((1, tm, width), lambda b, i: (b, i, 0))

    folded_specs = [pl.BlockSpec((1, d, tm // d, ATTN_W), lambda b, i: (b, 0, i, 0)) for d in dils]
    folded_shapes = [jax.ShapeDtypeStruct((B, d, T // d, ATTN_W), BF16) for d in dils]

    if keep_transposed:
        keep = pl.BlockSpec((1, ATTN_W, tm), lambda b, i: (b, 0, jnp.maximum(i - first, 0)))
        keep_shape = (B, ATTN_W, n_keep)
    else:
        keep = pl.BlockSpec((1, tm, ATTN_W), lambda b, i: (b, jnp.maximum(i - first, 0), 0))
        keep_shape = (B, n_keep, ATTN_W)
    out_shape = (
        [jax.ShapeDtypeStruct((B, T, CONV_W), BF16),
         jax.ShapeDtypeStruct((B, T, CONV_W), F32)]
        + folded_shapes * 3
        + [jax.ShapeDtypeStruct(keep_shape, F32),
           jax.ShapeDtypeStruct(keep_shape, F32),
           jax.ShapeDtypeStruct((B, T, MEM_W), BF16),
           jax.ShapeDtypeStruct((B, T, D_MIX), BF16)])
    out_specs = ([row(CONV_W), row(CONV_W)] + folded_specs * 3
                 + [keep, keep, row(MEM_W), row(D_MIX)])
    scratch = [pltpu.VMEM((ATTN_W // LANES, tm, LANES), F32)] * (3 * (n - 1))
    res = pl.pallas_call(
        functools.partial(_proj_kernel, first=first, keep_transposed=keep_transposed, dils=dils,
                          q_scale=q_scale),
        grid=(B, nt),
        in_specs=[row(D),
                  pl.BlockSpec((1, D), lambda b, i: (0, 0)),
                  pl.BlockSpec(w_bf16.shape, lambda b, i: (0, 0)),
                  pl.BlockSpec(wkvt_bf16.shape, lambda b, i: (0, 0))],
        out_specs=tuple(out_specs),
        out_shape=tuple(out_shape),
        scratch_shapes=scratch,
        compiler_params=_params(2),
        name="proj",
    )(x, g.reshape(1, D), w_bf16, wkvt_bf16)
    cb, u = res[0:2]
    q, k, v = res[2:2 + n], res[2 + n:2 + 2 * n], res[2 + 2 * n:2 + 3 * n]
    k32, v32, mq, z = res[2 + 3 * n:]
    return cb, u, q, k, v, k32, v32, mq, z


def _memkv_kernel(x_ref, g_ref, w_ref, k32_ref, v32_ref, k_ref, v_ref):
    x = x_ref[...]
    r = lax.rsqrt(jnp.mean(x * x, axis=-1, keepdims=True) + RMS_EPS)
    h = (x * r * g_ref[...]).astype(BF16)
    kv = jnp.dot(h, w_ref[...], preferred_element_type=F32)
    k_ref[...] = kv[:, :MEM_W].astype(BF16)
    v_ref[...] = kv[:, MEM_W:].astype(BF16)
    n_mem = k32_ref.shape[2]
    for b in range(k32_ref.shape[0]):
        kvt = kv[b * n_mem:(b + 1) * n_mem].T
        k32_ref[b] = kvt[:MEM_W]
        v32_ref[b] = kvt[MEM_W:]


def _mem_kv(mem, g, w_bf16):
    B, M, D = mem.shape
    rows = B * M
    k32, v32, k, v = pl.pallas_call(
        _memkv_kernel,
        out_shape=(jax.ShapeDtypeStruct((B, MEM_W, M), F32),
                   jax.ShapeDtypeStruct((B, MEM_W, M), F32),
                   jax.ShapeDtypeStruct((rows, MEM_W), BF16),
                   jax.ShapeDtypeStruct((rows, MEM_W), BF16)),
        compiler_params=pltpu.CompilerParams(vmem_limit_bytes=VMEM_LIMIT),
        name="memkv",
    )(mem.reshape(rows, D), g.reshape(1, D), w_bf16)
    return k32, v32, k.reshape(B, M, MEM_W), v.reshape(B, M, MEM_W)


def _band_bias_table():
    qi = np.arange(SPAN)[:, None] + SPAN
    ki = np.arange(2 * SPAN)[None, :]
    dist = qi - ki
    valid = (dist >= 0) & (dist <= SPAN)
    slopes = _slopes()
    tabs = []
    for _, dil in DILATED_CONFIGS:
        for first in (False, True):
            ok = valid & (ki >= SPAN) if first else valid
            for h in range(ATTN_HEADS):
                bias = -(slopes[h] * dil) * dist * LOG2E
                tabs.append(np.where(ok, bias, -np.inf))
    return np.stack(tabs).astype(np.float32)


def _pair_attention(qb, kb, vb, bias_lo, bias_hi):
    lane = lax.broadcasted_iota(jnp.int32, qb.shape, 1)
    lo = lane < HEAD_DIM
    zero = jnp.zeros_like(qb)
    accs, ms, ls = [], [], []
    for sel, bias in ((lo, bias_lo), (~lo, bias_hi)):
        qm = jnp.where(sel, qb, zero)
        s = lax.dot_general(qm, kb, (((1,), (1,)), ((), ())), preferred_element_type=F32)
        if bias is not None:
            s = s + bias
        m = jnp.max(s, axis=-1, keepdims=True)
        p = jnp.exp2(s - m)
        ms.append(m)
        ls.append(jnp.sum(p, axis=-1, keepdims=True))
        accs.append(jnp.dot(p.astype(BF16), vb, preferred_element_type=F32))
    l = jnp.where(lo, ls[0], ls[1])
    o = jnp.where(lo, accs[0], accs[1]) / l
    lse = jnp.where(lo, ms[0], ms[1]) * LN2 + jnp.log(l)
    return o, lse


def _attn_kernel(*refs, steps):
    ins = refs[:15]
    tab_ref = refs[15]
    outs = refs[16:22]
    kcat, vcat = refs[22], refs[23]
    s = pl.program_id(1)
    nblk = ATTN_TILE // SPAN
    for c in range(3):
        _, kp, ko, vp, vo = ins[5 * c:5 * c + 5]
        kcat[c, 0:SPAN] = kp[0, 0]
        kcat[c, SPAN:] = ko[0, 0]
        vcat[c, 0:SPAN] = vp[0, 0]
        vcat[c, SPAN:] = vo[0, 0]

    def body(j, carry):
        row0 = pl.multiple_of(j * SPAN, SPAN)
        for c, (_, dil) in enumerate(DILATED_CONFIGS):
            q_ref = ins[5 * c]
            o_ref, l_ref = outs[2 * c], outs[2 * c + 1]
            units_per_residue = steps // dil
            first = jnp.logical_and(s % units_per_residue == 0, j == 0).astype(jnp.int32)
            base = (c * 2 + first) * ATTN_HEADS
            for p in range(ATTN_HEADS // 2):
                lanes = slice(p * LANES, (p + 1) * LANES)
                qb = q_ref[0, 0, pl.ds(row0, SPAN), lanes]
                kb = kcat[c, pl.ds(row0, 2 * SPAN), lanes]
                vb = vcat[c, pl.ds(row0, 2 * SPAN), lanes]
                o, lse = _pair_attention(qb, kb, vb, tab_ref[base + 2 * p], tab_ref[base + 2 * p + 1])
                o_ref[0, 0, pl.ds(row0, SPAN), lanes] = o.astype(BF16)
                l_ref[0, 0, pl.ds(row0, SPAN), lanes] = lse
        return carry

    lax.fori_loop(0, nblk, body, 0, unroll=True)


def _prompt_attention(q, k, v):
    B, _, T, W = q[0].shape
    steps = T // ATTN_TILE
    prev_per_tile = ATTN_TILE // SPAN
    args, in_specs, out_specs, out_shape = [], [], [], []
    for c, (_, dil) in enumerate(DILATED_CONFIGS):
        L = T // dil
        upr = L // ATTN_TILE

        def own(b, s, upr=upr):
            return (b, s // upr, s % upr, 0)

        def prev(b, s, upr=upr):
            return (b, s // upr, jnp.maximum((s % upr) * prev_per_tile - 1, 0), 0)

        shape = (B, dil, L, W)
        assert q[c].shape == shape
        args += [q[c], k[c], k[c], v[c], v[c]]
        in_specs += [pl.BlockSpec((1, 1, ATTN_TILE, W), own),
                     pl.BlockSpec((1, 1, SPAN, W), prev),
                     pl.BlockSpec((1, 1, ATTN_TILE, W), own),
                     pl.BlockSpec((1, 1, SPAN, W), prev),
                     pl.BlockSpec((1, 1, ATTN_TILE, W), own)]
        out_specs += [pl.BlockSpec((1, 1, ATTN_TILE, W), own),
                      pl.BlockSpec((1, 1, ATTN_TILE, W), own)]
        out_shape += [jax.ShapeDtypeStruct(shape, BF16), jax.ShapeDtypeStruct(shape, F32)]
    tab = jnp.asarray(_band_bias_table())
    args.append(tab)
    in_specs.append(pl.BlockSpec(tab.shape, lambda b, s: (0, 0, 0)))
    res = pl.pallas_call(
        functools.partial(_attn_kernel, steps=steps),
        grid=(B, steps),
        in_specs=in_specs,
        out_specs=tuple(out_specs),
        out_shape=tuple(out_shape),
        scratch_shapes=[pltpu.VMEM((3, ATTN_TILE + SPAN, W), BF16),
                        pltpu.VMEM((3, ATTN_TILE + SPAN, W), BF16)],
        compiler_params=_params(2),
        name="prompt_attn",
    )(*args)
    return list(res)


def _silu(z):
    return z / (1.0 + jnp.exp(-z))


def _gate_project_norm(x, mix_ref, wout_ref, gf_ref):
    y = x + jnp.dot(mix_ref[...], wout_ref[...], preferred_element_type=F32)
    r = lax.rsqrt(jnp.mean(y * y, axis=-1, keepdims=True) + RMS_EPS)
    return y * r * gf_ref[...]


def _unfold(src_ref, slabs, d, tmp=None, step=4):
    n_slab, rows = slabs.shape[0], slabs.shape[1]

    def piece(r, s):
        return src_ref[0, r, :, s * LANES:(s + 1) * LANES].astype(F32)

    if d <= step:
        for r in range(d):
            for s in range(n_slab):
                slabs[s, pl.ds(r, rows // d, stride=d), :] = piece(r, s)
        return
    prev_d, n = d // step, rows // d
    per = rows // prev_d
    for rp in range(prev_d):
        for q in range(step):
            for s in range(n_slab):
                tmp[s, pl.ds(rp * per + q, n, stride=step), :] = piece(q * prev_d + rp, s)
    for rp in range(prev_d):
        for s in range(n_slab):
            slabs[s, pl.ds(rp, per, stride=prev_d), :] = tmp[s, pl.ds(rp * per, per), :]


def _finish_kernel(x_ref, cb_ref, u_ref, up_ref, o1_ref, l1_ref, o4_ref, l4_ref, o16_ref, l16_ref,
                   mq_ref, z_ref, mk_ref, mv_ref, cw_ref, wout_ref, gf_ref, y_ref, ucat, mix,
                   o4s, l4s, o16s, l16s, otmp, ltmp):
    i = pl.program_id(1)
    tm = ROW_TILE
    ucat[0:SUBLANES] = jnp.where(i == 0, 0.0, up_ref[0])
    ucat[SUBLANES:] = u_ref[0]
    cw = cw_ref[...]
    conv = (cw[0:1] * ucat[pl.ds(SUBLANES - 2, tm)]
            + cw[1:2] * ucat[pl.ds(SUBLANES - 1, tm)]
            + cw[2:3] * ucat[pl.ds(SUBLANES, tm)])
    conv_y = cb_ref[0].astype(F32) * conv
    z = z_ref[0].astype(F32)
    mix[:, 0:CONV_W] = (conv_y * _silu(z[:, 0:CONV_W])).astype(BF16)
    dil4, dil16 = DILATED_CONFIGS[1][1], DILATED_CONFIGS[2][1]
    _unfold(o4_ref, o4s, dil4)
    _unfold(l4_ref, l4s, dil4)
    _unfold(o16_ref, o16s, dil16, otmp)
    _unfold(l16_ref, l16s, dil16, ltmp)
    lo, hi = CONV_W, CONV_W + ATTN_W
    for s in range(ATTN_W // LANES):
        lanes = slice(s * LANES, (s + 1) * LANES)
        l1, l4, l16 = l1_ref[0, 0, :, lanes], l4s[s], l16s[s]
        mx = jnp.maximum(jnp.maximum(l1, l4), l16)
        e1, e4, e16 = jnp.exp(l1 - mx), jnp.exp(l4 - mx), jnp.exp(l16 - mx)
        den = e1 + e4 + e16
        attn = (e1 * o1_ref[0, 0, :, lanes].astype(F32) + e4 * o4s[s] + e16 * o16s[s]) / den
        a = lo + s * LANES
        mix[:, a:a + LANES] = (attn * _silu(z[:, a:a + LANES])).astype(BF16)
    for p in range(MEM_HEADS // 2):
        lanes = slice(p * LANES, (p + 1) * LANES)
        o, _ = _pair_attention(mq_ref[0, :, lanes], mk_ref[0, :, lanes], mv_ref[0, :, lanes],
                               None, None)
        a = hi + p * LANES
        mix[:, a:a + LANES] = (o * _silu(z[:, a:a + LANES])).astype(BF16)
    y_ref[0] = _gate_project_norm(x_ref[0], mix, wout_ref, gf_ref)


def _prompt_finish(x, cb, u, attn, mq, z, mk, mv, conv_w, w_out_bf16, g_final):
    B, T, D = x.shape
    tm = ROW_TILE

    def row(width):
        return pl.BlockSpec((1, tm, width), lambda b, i: (b, i, 0))

    def const(shape):
        return pl.BlockSpec(shape, lambda b, i: (0,) * len(shape))

    prev8 = pl.BlockSpec((1, SUBLANES, CONV_W),
                         lambda b, i: (b, jnp.maximum(i * (tm // SUBLANES) - 1, 0), 0))
    memspec = pl.BlockSpec((1,) + mk.shape[1:], lambda b, i: (b, 0, 0))
    in_specs = [row(D), row(CONV_W), row(CONV_W), prev8]
    for _, dil in DILATED_CONFIGS:
        folded = pl.BlockSpec((1, dil, tm // dil, ATTN_W), lambda b, i: (b, 0, i, 0))
        in_specs += [folded, folded]
    in_specs += [row(MEM_W), row(D_MIX), memspec, memspec,
                 const(conv_w.shape), const(w_out_bf16.shape), const((1, D))]
    slabs = pltpu.VMEM((ATTN_W // LANES, tm, LANES), F32)
    return pl.pallas_call(
        _finish_kernel,
        grid=(B, T // tm),
        in_specs=in_specs,
        out_specs=row(D),
        out_shape=jax.ShapeDtypeStruct((B, T, D), F32),
        scratch_shapes=[pltpu.VMEM((tm + SUBLANES, CONV_W), F32),
                        pltpu.VMEM((tm, D_MIX), BF16)] + [slabs] * 6,
        compiler_params=_params(2),
        name="prompt_finish",
    )(x, cb, u, u, *attn, mq, z, mk, mv, conv_w, w_out_bf16, g_final.reshape(1, D))


def _sample_tables(S, n_buf):
    j = np.arange(S)[:, None]
    t = np.arange(n_buf)[None, :]
    dist = n_buf + j - t
    count = np.zeros(dist.shape)
    for win, dil in DILATED_CONFIGS:
        count += (dist % dil == 0) & (dist <= win)
    bias = np.where(count > 0, -_slopes()[:, None, None] * dist, -np.inf)
    return count.astype(np.float32), bias.astype(np.float32)


def _sample_attn_kernel(q_ref, kn_ref, vn_ref, kt_ref, vt_ref, mq_ref, mkt_ref, mvt_ref,
                        cnt_ref, bias_ref, o_ref, mo_ref):
    S = q_ref.shape[1]
    nt = (((1,), (1,)), ((), ()))
    slopes = _slopes()
    n_cfg = float(len(DILATED_CONFIGS))
    rowj = lax.broadcasted_iota(jnp.int32, (S, 1), 0)
    q, kn, vn = q_ref[0], kn_ref[0], vn_ref[0]
    mq = mq_ref[0]
    cnt = cnt_ref[...]

    def head(a, h):
        return a[:, h * HEAD_DIM:(h + 1) * HEAD_DIM]

    scores = [jnp.dot(head(q, h).astype(BF16), kt_ref[0, h].astype(BF16),
                      preferred_element_type=F32) + bias_ref[h] for h in range(ATTN_HEADS)]
    mscores = [jnp.dot(head(mq, h).astype(BF16), mkt_ref[0, h].astype(BF16),
                       preferred_element_type=F32) for h in range(MEM_HEADS)]
    probs, stats = [], []
    for h in range(ATTN_HEADS):
        qh, knh = head(q, h), head(kn, h)
        m = jnp.max(scores[h], axis=-1, keepdims=True)
        s_new = []
        for jp in range(S):
            d = rowj - jp
            sj = jnp.sum(qh * knh[jp:jp + 1], axis=-1, keepdims=True)
            sj = jnp.where(d >= 0, sj - float(slopes[h]) * d.astype(F32), -jnp.inf)
            s_new.append(sj)
            m = jnp.maximum(m, sj)
        p = cnt * jnp.exp(scores[h] - m)
        l = jnp.sum(p, axis=-1, keepdims=True)
        w_new = []
        for jp in range(S):
            w = jnp.where(rowj == jp, n_cfg, 1.0) * jnp.exp(s_new[jp] - m)
            l = l + w
            w_new.append(w)
        probs.append(p.astype(BF16))
        stats.append((l, w_new))
    mprobs, mden = [], []
    for h in range(MEM_HEADS):
        p = jnp.exp(mscores[h] - jnp.max(mscores[h], axis=-1, keepdims=True))
        mden.append(jnp.sum(p, axis=-1, keepdims=True))
        mprobs.append(p.astype(BF16))
    accs = [lax.dot_general(probs[h], vt_ref[0, h].astype(BF16), nt,
                            preferred_element_type=F32) for h in range(ATTN_HEADS)]
    maccs = [lax.dot_general(mprobs[h], mvt_ref[0, h].astype(BF16), nt,
                             preferred_element_type=F32) for h in range(MEM_HEADS)]
    outs = []
    for h in range(ATTN_HEADS):
        l, w_new = stats[h]
        acc, vnh = accs[h], head(vn, h)
        for jp in range(S):
            acc = acc + w_new[jp] * vnh[jp:jp + 1]
        outs.append(acc / l)
    o_ref[0] = jnp.concatenate(outs, axis=-1)
    mo_ref[0] = jnp.concatenate([maccs[h] / mden[h] for h in range(MEM_HEADS)], axis=-1)


def _sample_attention(q, k_new, v_new, cache_kt, cache_vt, mq, mem_kt, mem_vt):
    DB, S, W = q.shape
    n_buf = cache_kt.shape[-1]
    assert n_buf >= DILATED_CONFIGS[-1][0] and S <= DILATED_CONFIGS[1][1], \
        "new tokens reach each other through dilation 1 only; the buffer covers every window"
    cnt, bias = (jnp.asarray(t) for t in _sample_tables(S, n_buf))

    def tok(width):
        return pl.BlockSpec((1, S, width), lambda b: (b, 0, 0))

    def per_batch(a):
        return pl.BlockSpec((1,) + a.shape[1:], lambda b: (b,) + (0,) * (a.ndim - 1))

    def const(a):
        return pl.BlockSpec(a.shape, lambda b: (0,) * a.ndim)

    return pl.pallas_call(
        _sample_attn_kernel,
        grid=(DB,),
        in_specs=[tok(W), tok(W), tok(W), per_batch(cache_kt), per_batch(cache_vt),
                  tok(MEM_W), per_batch(mem_kt), per_batch(mem_vt), const(cnt), const(bias)],
        out_specs=(tok(W), tok(MEM_W)),
        out_shape=(jax.ShapeDtypeStruct((DB, S, W), F32),
                   jax.ShapeDtypeStruct((DB, S, MEM_W), F32)),
        compiler_params=_params(1),
        name="sample_attn",
    )(q, k_new, v_new, cache_kt, cache_vt, mq, mem_kt, mem_vt, cnt, bias)


def _sample_finish_kernel(x_ref, cb_ref, u0_ref, u1_ref, u2_ref, attn_ref, mem_ref, z_ref,
                          cw_ref, wout_ref, gf_ref, y_ref, mix):
    cw = cw_ref[...]
    conv = cw[0:1] * u0_ref[...] + cw[1:2] * u1_ref[...] + cw[2:3] * u2_ref[...]
    z = z_ref[...].astype(F32)
    lo, hi = CONV_W, CONV_W + ATTN_W
    mix[:, 0:lo] = (cb_ref[...].astype(F32) * conv * _silu(z[:, 0:lo])).astype(BF16)
    mix[:, lo:hi] = (attn_ref[...] * _silu(z[:, lo:hi])).astype(BF16)
    mix[:, hi:] = (mem_ref[...] * _silu(z[:, hi:])).astype(BF16)
    y_ref[...] = _gate_project_norm(x_ref[...], mix, wout_ref, gf_ref)


def _sample_finish(x, cb, u_taps, attn, mem, z, conv_w, w_out_bf16, g_final):
    rows, D = x.shape
    return pl.pallas_call(
        _sample_finish_kernel,
        out_shape=jax.ShapeDtypeStruct((rows, D), F32),
        scratch_shapes=[pltpu.VMEM((rows, D_MIX), BF16)],
        compiler_params=pltpu.CompilerParams(vmem_limit_bytes=VMEM_LIMIT),
        name="sample_finish",
    )(x, cb, *u_taps, attn, mem, z, conv_w, w_out_bf16, g_final.reshape(1, D))


def kernel(x_prompt, x_sample, mem_prompt, cache_win_k, cache_win_v, cache_conv, cache_mem_k,
           cache_mem_v, g_in, w_in, conv_w, g_mem, w_mem_kv, w_out, g_final):
    assert g_in.shape[0] == 1, "the final RMSNorm is fused into the single layer's tail"
    B, T, D = x_prompt.shape
    DB, S, _ = x_sample.shape
    n_buf = cache_win_k.shape[2]
    n_keep = min(DILATED_CONFIGS[-1][0], T)
    w_in_b = w_in[0].astype(BF16)
    w_out_b = w_out[0].astype(BF16)
    dils = tuple(d for _, d in DILATED_CONFIGS)
    wkvt_b = w_in[0][:, _C_K:_C_MQ].T.astype(BF16)
    cb, u, q, k, v, kt32, vt32, mq, z = _project(x_prompt, g_in[0], w_in_b, wkvt_b, n_keep, True,
                                                 dils, Q_SCALE * LOG2E)
    mkt32, mvt32, mk, mv = _mem_kv(mem_prompt, g_mem[0], w_mem_kv[0].astype(BF16))
    attn = _prompt_attention(q, k, v)
    y_prompt = _prompt_finish(x_prompt, cb, u, attn, mq, z, mk, mv, conv_w[0], w_out_b, g_final)

    def heads_last(t, heads):
        return jnp.transpose(t.reshape(t.shape[0], heads, HEAD_DIM, -1), (0, 3, 1, 2))[None]

    def heads_first(c):
        return jnp.transpose(c, (0, 2, 3, 1))

    p_win_k = heads_last(kt32, ATTN_HEADS)
    p_win_v = heads_last(vt32, ATTN_HEADS)
    p_conv = u[None, :, -(CONV_WIDTH - 1):]
    p_mem_k = heads_last(mkt32, MEM_HEADS)
    p_mem_v = heads_last(mvt32, MEM_HEADS)
    rows = DB * S
    cb, u, q, k, v, k32, v32, mq, z = _project(x_sample.reshape(1, rows, D), g_in[0], w_in_b,
                                               wkvt_b, rows, False, (1,), Q_SCALE)
    k32 = k32.reshape(DB, S, ATTN_W)
    v32 = v32.reshape(DB, S, ATTN_W)
    attn_s, mem_s = _sample_attention(
        q[0].reshape(DB, S, ATTN_W).astype(F32), k32, v32,
        heads_first(cache_win_k[0]), heads_first(cache_win_v[0]),
        mq.reshape(DB, S, MEM_W).astype(F32),
        heads_first(cache_mem_k[0]), heads_first(cache_mem_v[0]))
    full = jnp.concatenate([cache_conv[0], u.reshape(DB, S, CONV_W)], axis=1)
    taps = [full[:, t:t + S].reshape(rows, CONV_W) for t in range(CONV_WIDTH)]
    y_sample = _sample_finish(x_sample.reshape(rows, D), cb.reshape(rows, CONV_W), taps,
                              attn_s.reshape(rows, ATTN_W), mem_s.reshape(rows, MEM_W),
                              z.reshape(rows, D_MIX), conv_w[0], w_out_b, g_final).reshape(DB, S, D)
    s_win_k = k32.reshape(1, DB, S, ATTN_HEADS, HEAD_DIM)
    s_win_v = v32.reshape(1, DB, S, ATTN_HEADS, HEAD_DIM)
    s_conv = full[None, :, -(CONV_WIDTH - 1):]
    return (y_prompt, y_sample, p_win_k, p_win_v, p_conv, p_mem_k, p_mem_v,
            s_win_k, s_win_v, s_conv)
```

```python
import functools

import numpy as np
import jax
import jax.numpy as jnp
from jax import lax
from jax.experimental import pallas as pl
from jax.experimental.pallas import tpu as pltpu

F32 = jnp.float32
BF16 = jnp.bfloat16

HEAD_DIM = 64
ATTN_HEADS = 8
MEM_HEADS = 4
CONV_W = 256
ATTN_W = 512
MEM_W = 256
D_MIX = 1024
CONV_WIDTH = 3
DILATED_CONFIGS = ((128, 1), (512, 4), (2048, 16))
SPAN = 128
ALIBI_MAX = 8.0
RMS_EPS = 1e-6
Q_SCALE = HEAD_DIM ** -0.5
LOG2E = float(np.log2(np.e))
LN2 = float(np.log(2.0))

LANES = 128
SUBLANES = 8
VMEM_LIMIT = 56 * 1024 * 1024

_C_CB, _C_CC, _C_Q, _C_K, _C_V, _C_MQ, _C_Z, _C_END = 0, 256, 768, 1280, 1792, 2304, 2560, 3584

ROW_TILE = 512
ATTN_TILE = 512


def _slopes():
    h = np.arange(ATTN_HEADS, dtype=np.float64) + 1.0
    return np.exp2(-ALIBI_MAX * h / ATTN_HEADS)


def _params(n_axes):
    return pltpu.CompilerParams(
        dimension_semantics=("arbitrary",) * n_axes,
        vmem_limit_bytes=VMEM_LIMIT)


def _fold_store(res, out_refs, slabs, dils):
    rows = res.shape[0]
    out_refs[0][0, 0] = res.astype(BF16)
    if len(dils) == 1:
        return
    n_slab = res.shape[1] // LANES
    src, prev_d = slabs[0], 1
    for s in range(n_slab):
        src[s] = res[:, s * LANES:(s + 1) * LANES]
    for idx, (d, o_ref) in enumerate(zip(dils[1:], out_refs[1:])):
        step, n = d // prev_d, rows // d
        dst = slabs[idx + 1] if idx + 2 < len(dils) else None
        for rp in range(prev_d):
            for q in range(step):
                r = q * prev_d + rp
                for s in range(n_slab):
                    val = src[s, pl.ds(rp * (rows // prev_d) + q, n, stride=step), :]
                    o_ref[0, r, :, s * LANES:(s + 1) * LANES] = val.astype(BF16)
                    if dst is not None:
                        dst[s, pl.ds(r * n, n), :] = val
        src, prev_d = dst, d


def _proj_kernel(x_ref, g_ref, w_ref, wkvt_ref, *refs, first, keep_transposed, dils, q_scale):
    n = len(dils)
    cb_ref, u_ref = refs[0:2]
    q_refs, k_refs, v_refs = refs[2:2 + n], refs[2 + n:2 + 2 * n], refs[2 + 2 * n:2 + 3 * n]
    k32_ref, v32_ref, mq_ref, z_ref = refs[2 + 3 * n:6 + 3 * n]
    scratch = refs[6 + 3 * n:]
    slabs = [scratch[a * (n - 1):(a + 1) * (n - 1)] for a in range(3)]
    x = x_ref[0]
    r = lax.rsqrt(jnp.mean(x * x, axis=-1, keepdims=True) + RMS_EPS)
    h = (x * r * g_ref[...]).astype(BF16)

    def seg(lo, hi):
        return jnp.dot(h, w_ref[:, lo:hi], preferred_element_type=F32)

    cb_ref[0] = seg(_C_CB, _C_CC).astype(BF16)
    cch = seg(_C_CC, _C_Q)
    u_ref[0] = cch[:, :CONV_W] * cch[:, CONV_W:]
    _fold_store(seg(_C_Q, _C_K) * q_scale, q_refs, slabs[0], dils)
    k = seg(_C_K, _C_V)
    _fold_store(k, k_refs, slabs[1], dils)
    v = seg(_C_V, _C_MQ)
    _fold_store(v, v_refs, slabs[2], dils)
    mq_ref[0] = (seg(_C_MQ, _C_Z) * q_scale).astype(BF16)
    z_ref[0] = seg(_C_Z, _C_END).astype(BF16)
    if keep_transposed:
        @pl.when(pl.program_id(1) >= first)
        def _():
            nt = (((1,), (1,)), ((), ()))
            k32_ref[0] = lax.dot_general(wkvt_ref[0:ATTN_W], h, nt, preferred_element_type=F32)
            v32_ref[0] = lax.dot_general(wkvt_ref[ATTN_W:], h, nt, preferred_element_type=F32)
    else:
        k32_ref[0] = k
        v32_ref[0] = v


def _project(x, g, w_bf16, wkvt_bf16, n_keep, keep_transposed, dils, q_scale):
    B, T, D = x.shape
    tm = ROW_TILE
    nt = T // tm
    first = (T - n_keep) // tm
    n = len(dils)

    def row(width):
        return pl.BlockSpec((1, tm, width), lambda b, i: (b, i, 0))

    folded_specs = [pl.BlockSpec((1, d, tm // d, ATTN_W), lambda b, i: (b, 0, i, 0)) for d in dils]
    folded_shapes = [jax.ShapeDtypeStruct((B, d, T // d, ATTN_W), BF16) for d in dils]

    if keep_transposed:
        keep = pl.BlockSpec((1, ATTN_W, tm), lambda b, i: (b, 0, jnp.maximum(i - first, 0)))
        keep_shape = (B, ATTN_W, n_keep)
    else:
        keep = pl.BlockSpec((1, tm, ATTN_W), lambda b, i: (b, jnp.maximum(i - first, 0), 0))
        keep_shape = (B, n_keep, ATTN_W)
    out_shape = (
        [jax.ShapeDtypeStruct((B, T, CONV_W), BF16),
         jax.ShapeDtypeStruct((B, T, CONV_W), F32)]
        + folded_shapes * 3
        + [jax.ShapeDtypeStruct(keep_shape, F32),
           jax.ShapeDtypeStruct(keep_shape, F32),
           jax.ShapeDtypeStruct((B, T, MEM_W), BF16),
           jax.ShapeDtypeStruct((B, T, D_MIX), BF16)])
    out_specs = ([row(CONV_W), row(CONV_W)] + folded_specs * 3
                 + [keep, keep, row(MEM_W), row(D_MIX)])
    scratch = [pltpu.VMEM((ATTN_W // LANES, tm, LANES), F32)] * (3 * (n - 1))
    res = pl.pallas_call(
        functools.partial(_proj_kernel, first=first, keep_transposed=keep_transposed, dils=dils,
                          q_scale=q_scale),
        grid=(B, nt),
        in_specs=[row(D),
                  pl.BlockSpec((1, D), lambda b, i: (0, 0)),
                  pl.BlockSpec(w_bf16.shape, lambda b, i: (0, 0)),
                  pl.BlockSpec(wkvt_bf16.shape, lambda b, i: (0, 0))],
        out_specs=tuple(out_specs),
        out_shape=tuple(out_shape),
        scratch_shapes=scratch,
        compiler_params=_params(2),
        name="proj",
    )(x, g.reshape(1, D), w_bf16, wkvt_bf16)
    cb, u = res[0:2]
    q, k, v = res[2:2 + n], res[2 + n:2 + 2 * n], res[2 + 2 * n:2 + 3 * n]
    k32, v32, mq, z = res[2 + 3 * n:]
    return cb, u, q, k, v, k32, v32, mq, z


def _memkv_kernel(x_ref, g_ref, w_ref, k32_ref, v32_ref, k_ref, v_ref):
    x = x_ref[...]
    r = lax.rsqrt(jnp.mean(x * x, axis=-1, keepdims=True) + RMS_EPS)
    h = (x * r * g_ref[...]).astype(BF16)
    kv = jnp.dot(h, w_ref[...], preferred_element_type=F32)
    k_ref[...] = kv[:, :MEM_W].astype(BF16)
    v_ref[...] = kv[:, MEM_W:].astype(BF16)
    n_mem = k32_ref.shape[2]
    for b in range(k32_ref.shape[0]):
        kvt = kv[b * n_mem:(b + 1) * n_mem].T
        k32_ref[b] = kvt[:MEM_W]
        v32_ref[b] = kvt[MEM_W:]


def _mem_kv(mem, g, w_bf16):
    B, M, D = mem.shape
    rows = B * M
    k32, v32, k, v = pl.pallas_call(
        _memkv_kernel,
        out_shape=(jax.ShapeDtypeStruct((B, MEM_W, M), F32),
                   jax.ShapeDtypeStruct((B, MEM_W, M), F32),
                   jax.ShapeDtypeStruct((rows, MEM_W), BF16),
                   jax.ShapeDtypeStruct((rows, MEM_W), BF16)),
        compiler_params=pltpu.CompilerParams(vmem_limit_bytes=VMEM_LIMIT),
        name="memkv",
    )(mem.reshape(rows, D), g.reshape(1, D), w_bf16)
    return k32, v32, k.reshape(B, M, MEM_W), v.reshape(B, M, MEM_W)


def _band_bias_table():
    qi = np.arange(SPAN)[:, None] + SPAN
    ki = np.arange(2 * SPAN)[None, :]
    dist = qi - ki
    valid = (dist >= 0) & (dist <= SPAN)
    slopes = _slopes()
    tabs = []
    for _, dil in DILATED_CONFIGS:
        for first in (False, True):
            ok = valid & (ki >= SPAN) if first else valid
            for h in range(ATTN_HEADS):
                bias = -(slopes[h] * dil) * dist * LOG2E
                tabs.append(np.where(ok, bias, -np.inf))
    return np.stack(tabs).astype(np.float32)


def _pair_attention(qb, kb, vb, bias_lo, bias_hi):
    lane = lax.broadcasted_iota(jnp.int32, qb.shape, 1)
    lo = lane < HEAD_DIM
    zero = jnp.zeros_like(qb)
    accs, ms, ls = [], [], []
    for sel, bias in ((lo, bias_lo), (~lo, bias_hi)):
        qm = jnp.where(sel, qb, zero)
        s = lax.dot_general(qm, kb, (((1,), (1,)), ((), ())), preferred_element_type=F32)
        if bias is not None:
            s = s + bias
        m = jnp.max(s, axis=-1, keepdims=True)
        p = jnp.exp2(s - m)
        ms.append(m)
        ls.append(jnp.sum(p, axis=-1, keepdims=True))
        accs.append(jnp.dot(p.astype(BF16), vb, preferred_element_type=F32))
    l = jnp.where(lo, ls[0], ls[1])
    o = jnp.where(lo, accs[0], accs[1]) / l
    lse = jnp.where(lo, ms[0], ms[1]) * LN2 + jnp.log(l)
    return o, lse


def _attn_kernel(*refs, steps):
    ins = refs[:15]
    tab_ref = refs[15]
    sample_ins = refs[16:26]
    outs = refs[26:32]
    sample_outs = refs[32:34]
    kcat, vcat = refs[34], refs[35]
    s, j = pl.program_id(1), pl.program_id(2)

    @pl.when(j == 0)
    def _():
        for c in range(3):
            _, kp, ko, vp, vo = ins[5 * c:5 * c + 5]
            kcat[c, 0:SPAN] = kp[0, 0]
            kcat[c, SPAN:] = ko[0, 0]
            vcat[c, 0:SPAN] = vp[0, 0]
            vcat[c, SPAN:] = vo[0, 0]

    for _ in _sample_attn_body(*sample_ins, *sample_outs):
        pass
    row0 = pl.multiple_of(j * SPAN, SPAN)
    for c, (_, dil) in enumerate(DILATED_CONFIGS):
        q_ref = ins[5 * c]
        o_ref, l_ref = outs[2 * c], outs[2 * c + 1]
        units_per_residue = steps // dil
        first = jnp.logical_and(s % units_per_residue == 0, j == 0).astype(jnp.int32)
        base = (c * 2 + first) * ATTN_HEADS
        for p in range(ATTN_HEADS // 2):
            lanes = slice(p * LANES, (p + 1) * LANES)
            qb = q_ref[0, 0, pl.ds(row0, SPAN), lanes]
            kb = kcat[c, pl.ds(row0, 2 * SPAN), lanes]
            vb = vcat[c, pl.ds(row0, 2 * SPAN), lanes]
            o, lse = _pair_attention(qb, kb, vb, tab_ref[base + 2 * p], tab_ref[base + 2 * p + 1])
            o_ref[0, 0, pl.ds(row0, SPAN), lanes] = o.astype(BF16)
            l_ref[0, 0, pl.ds(row0, SPAN), lanes] = lse


def _attention(q, k, v, sample):
    B, _, T, W = q[0].shape
    steps = T // ATTN_TILE
    prev_per_tile = ATTN_TILE // SPAN
    sq, skn, svn, ckt, cvt, smq, mkt, mvt = sample
    DB, S, _ = sq.shape
    n_buf = ckt.shape[-1]
    assert DB == B * steps * prev_per_tile, "one sample batch row rides on each prompt query block"
    assert n_buf >= DILATED_CONFIGS[-1][0] and S <= DILATED_CONFIGS[1][1], \
        "new tokens reach each other through dilation 1 only; the buffer covers every window"
    args, in_specs, out_specs, out_shape = [], [], [], []
    for c, (_, dil) in enumerate(DILATED_CONFIGS):
        L = T // dil
        upr = L // ATTN_TILE

        def own(b, s, i, upr=upr):
            return (b, s // upr, s % upr, 0)

        def prev(b, s, i, upr=upr):
            return (b, s // upr, jnp.maximum((s % upr) * prev_per_tile - 1, 0), 0)

        shape = (B, dil, L, W)
        assert q[c].shape == shape
        args += [q[c], k[c], k[c], v[c], v[c]]
        in_specs += [pl.BlockSpec((1, 1, ATTN_TILE, W), own),
                     pl.BlockSpec((1, 1, SPAN, W), prev),
                     pl.BlockSpec((1, 1, ATTN_TILE, W), own),
                     pl.BlockSpec((1, 1, SPAN, W), prev),
                     pl.BlockSpec((1, 1, ATTN_TILE, W), own)]
        out_specs += [pl.BlockSpec((1, 1, ATTN_TILE, W), own),
                      pl.BlockSpec((1, 1, ATTN_TILE, W), own)]
        out_shape += [jax.ShapeDtypeStruct(shape, BF16), jax.ShapeDtypeStruct(shape, F32)]
    tab = jnp.asarray(_band_bias_table())
    args.append(tab)
    in_specs.append(pl.BlockSpec(tab.shape, lambda b, s, i: (0, 0, 0),
                                 pipeline_mode=pl.Buffered(1)))

    def row_of(b, s, i):
        return (b * steps + s) * prev_per_tile + i

    def per_batch(a):
        return pl.BlockSpec((1,) + a.shape[1:],
                            lambda b, s, i: (row_of(b, s, i),) + (0,) * (a.ndim - 1))

    def const(a):
        return pl.BlockSpec(a.shape, lambda b, s, i: (0,) * a.ndim)

    cnt, bias = (jnp.asarray(t) for t in _sample_tables(S, n_buf))
    sample_args = [sq, skn, svn, ckt, cvt, smq, mkt, mvt]
    args += sample_args + [cnt, bias]
    in_specs += [per_batch(a) for a in sample_args] + [const(cnt), const(bias)]
    out_specs += [per_batch(sq), per_batch(smq)]
    out_shape += [jax.ShapeDtypeStruct(sq.shape, F32), jax.ShapeDtypeStruct(smq.shape, F32)]
    res = pl.pallas_call(
        functools.partial(_attn_kernel, steps=steps),
        grid=(B, steps, prev_per_tile),
        in_specs=in_specs,
        out_specs=tuple(out_specs),
        out_shape=tuple(out_shape),
        scratch_shapes=[pltpu.VMEM((3, ATTN_TILE + SPAN, W), BF16),
                        pltpu.VMEM((3, ATTN_TILE + SPAN, W), BF16)],
        compiler_params=_params(3),
        name="attn",
    )(*args)
    return list(res[:6]), res[6], res[7]


def _silu(z):
    return z / (1.0 + jnp.exp(-z))


def _gate_project_norm(x, mix_ref, wout_ref, gf_ref):
    y = x + jnp.dot(mix_ref[...], wout_ref[...], preferred_element_type=F32)
    r = lax.rsqrt(jnp.mean(y * y, axis=-1, keepdims=True) + RMS_EPS)
    return y * r * gf_ref[...]


def _unfold(src_ref, slabs, d, tmp=None, step=4):
    n_slab, rows = slabs.shape[0], slabs.shape[1]

    def piece(r, s):
        return src_ref[0, r, :, s * LANES:(s + 1) * LANES].astype(F32)

    if d <= step:
        for r in range(d):
            for s in range(n_slab):
                slabs[s, pl.ds(r, rows // d, stride=d), :] = piece(r, s)
        return
    prev_d, n = d // step, rows // d
    per = rows // prev_d
    for rp in range(prev_d):
        for q in range(step):
            for s in range(n_slab):
                tmp[s, pl.ds(rp * per + q, n, stride=step), :] = piece(q * prev_d + rp, s)
    for rp in range(prev_d):
        for s in range(n_slab):
            slabs[s, pl.ds(rp, per, stride=prev_d), :] = tmp[s, pl.ds(rp * per, per), :]


def _finish_kernel(x_ref, cb_ref, u_ref, up_ref, o1_ref, l1_ref, o4_ref, l4_ref, o16_ref, l16_ref,
                   mq_ref, z_ref, mk_ref, mv_ref, cw_ref, wout_ref, gf_ref, y_ref, ucat, mix,
                   o4s, l4s, o16s, l16s, otmp, ltmp):
    i = pl.program_id(1)
    tm = ROW_TILE
    ucat[0:SUBLANES] = jnp.where(i == 0, 0.0, up_ref[0])
    ucat[SUBLANES:] = u_ref[0]
    cw = cw_ref[...]
    conv = (cw[0:1] * ucat[pl.ds(SUBLANES - 2, tm)]
            + cw[1:2] * ucat[pl.ds(SUBLANES - 1, tm)]
            + cw[2:3] * ucat[pl.ds(SUBLANES, tm)])
    conv_y = cb_ref[0].astype(F32) * conv
    z = z_ref[0].astype(F32)
    mix[:, 0:CONV_W] = (conv_y * _silu(z[:, 0:CONV_W])).astype(BF16)
    dil4, dil16 = DILATED_CONFIGS[1][1], DILATED_CONFIGS[2][1]
    _unfold(o4_ref, o4s, dil4)
    _unfold(l4_ref, l4s, dil4)
    _unfold(o16_ref, o16s, dil16, otmp)
    _unfold(l16_ref, l16s, dil16, ltmp)
    lo, hi = CONV_W, CONV_W + ATTN_W
    for s in range(ATTN_W // LANES):
        lanes = slice(s * LANES, (s + 1) * LANES)
        l1, l4, l16 = l1_ref[0, 0, :, lanes], l4s[s], l16s[s]
        mx = jnp.maximum(jnp.maximum(l1, l4), l16)
        e1, e4, e16 = jnp.exp(l1 - mx), jnp.exp(l4 - mx), jnp.exp(l16 - mx)
        den = e1 + e4 + e16
        attn = (e1 * o1_ref[0, 0, :, lanes].astype(F32) + e4 * o4s[s] + e16 * o16s[s]) / den
        a = lo + s * LANES
        mix[:, a:a + LANES] = (attn * _silu(z[:, a:a + LANES])).astype(BF16)
    for p in range(MEM_HEADS // 2):
        lanes = slice(p * LANES, (p + 1) * LANES)
        o, _ = _pair_attention(mq_ref[0, :, lanes], mk_ref[0, :, lanes], mv_ref[0, :, lanes],
                               None, None)
        a = hi + p * LANES
        mix[:, a:a + LANES] = (o * _silu(z[:, a:a + LANES])).astype(BF16)
    y_ref[0] = _gate_project_norm(x_ref[0], mix, wout_ref, gf_ref)


def _prompt_finish(x, cb, u, attn, mq, z, mk, mv, conv_w, w_out_bf16, g_final):
    B, T, D = x.shape
    tm = ROW_TILE

    def row(width):
        return pl.BlockSpec((1, tm, width), lambda b, i: (b, i, 0))

    def const(shape):
        return pl.BlockSpec(shape, lambda b, i: (0,) * len(shape))

    prev8 = pl.BlockSpec((1, SUBLANES, CONV_W),
                         lambda b, i: (b, jnp.maximum(i * (tm // SUBLANES) - 1, 0), 0))
    memspec = pl.BlockSpec((1,) + mk.shape[1:], lambda b, i: (b, 0, 0))
    in_specs = [row(D), row(CONV_W), row(CONV_W), prev8]
    for _, dil in DILATED_CONFIGS:
        folded = pl.BlockSpec((1, dil, tm // dil, ATTN_W), lambda b, i: (b, 0, i, 0))
        in_specs += [folded, folded]
    in_specs += [row(MEM_W), row(D_MIX), memspec, memspec,
                 const(conv_w.shape), const(w_out_bf16.shape), const((1, D))]
    slabs = pltpu.VMEM((ATTN_W // LANES, tm, LANES), F32)
    return pl.pallas_call(
        _finish_kernel,
        grid=(B, T // tm),
        in_specs=in_specs,
        out_specs=row(D),
        out_shape=jax.ShapeDtypeStruct((B, T, D), F32),
        scratch_shapes=[pltpu.VMEM((tm + SUBLANES, CONV_W), F32),
                        pltpu.VMEM((tm, D_MIX), BF16)] + [slabs] * 6,
        compiler_params=_params(2),
        name="prompt_finish",
    )(x, cb, u, u, *attn, mq, z, mk, mv, conv_w, w_out_bf16, g_final.reshape(1, D))


def _sample_tables(S, n_buf):
    j = np.arange(S)[:, None]
    t = np.arange(n_buf)[None, :]
    dist = n_buf + j - t
    count = np.zeros(dist.shape)
    for win, dil in DILATED_CONFIGS:
        count += (dist % dil == 0) & (dist <= win)
    bias = np.where(count > 0, -_slopes()[:, None, None] * dist, -np.inf)
    return count.astype(np.float32), bias.astype(np.float32)


def _sample_attn_body(q_ref, kn_ref, vn_ref, kt_ref, vt_ref, mq_ref, mkt_ref, mvt_ref,
                      cnt_ref, bias_ref, o_ref, mo_ref):
    S = q_ref.shape[1]
    nt = (((1,), (1,)), ((), ()))
    slopes = _slopes()
    n_cfg = float(len(DILATED_CONFIGS))
    rowj = lax.broadcasted_iota(jnp.int32, (S, 1), 0)
    q, kn, vn = q_ref[0], kn_ref[0], vn_ref[0]
    mq = mq_ref[0]
    cnt = cnt_ref[...]

    def head(a, h):
        return a[:, h * HEAD_DIM:(h + 1) * HEAD_DIM]

    scores, mscores = [], []
    for h in range(ATTN_HEADS):
        scores.append(jnp.dot(head(q, h).astype(BF16), kt_ref[0, h].astype(BF16),
                              preferred_element_type=F32) + bias_ref[h])
        yield
    for h in range(MEM_HEADS):
        mscores.append(jnp.dot(head(mq, h).astype(BF16), mkt_ref[0, h].astype(BF16),
                               preferred_element_type=F32))
        yield
    probs, stats = [], []
    for h in range(ATTN_HEADS):
        qh, knh = head(q, h), head(kn, h)
        m = jnp.max(scores[h], axis=-1, keepdims=True)
        s_new = []
        for jp in range(S):
            d = rowj - jp
            sj = jnp.sum(qh * knh[jp:jp + 1], axis=-1, keepdims=True)
            sj = jnp.where(d >= 0, sj - float(slopes[h]) * d.astype(F32), -jnp.inf)
            s_new.append(sj)
            m = jnp.maximum(m, sj)
        p = cnt * jnp.exp(scores[h] - m)
        l = jnp.sum(p, axis=-1, keepdims=True)
        w_new = []
        for jp in range(S):
            w = jnp.where(rowj == jp, n_cfg, 1.0) * jnp.exp(s_new[jp] - m)
            l = l + w
            w_new.append(w)
        probs.append(p.astype(BF16))
        stats.append((l, w_new))
        yield
    mprobs, mden = [], []
    for h in range(MEM_HEADS):
        p = jnp.exp(mscores[h] - jnp.max(mscores[h], axis=-1, keepdims=True))
        mden.append(jnp.sum(p, axis=-1, keepdims=True))
        mprobs.append(p.astype(BF16))
        yield
    accs, maccs = [], []
    for h in range(ATTN_HEADS):
        accs.append(lax.dot_general(probs[h], vt_ref[0, h].astype(BF16), nt,
                                    preferred_element_type=F32))
        yield
    for h in range(MEM_HEADS):
        maccs.append(lax.dot_general(mprobs[h], mvt_ref[0, h].astype(BF16), nt,
                                     preferred_element_type=F32))
        yield
    outs = []
    for h in range(ATTN_HEADS):
        l, w_new = stats[h]
        acc, vnh = accs[h], head(vn, h)
        for jp in range(S):
            acc = acc + w_new[jp] * vnh[jp:jp + 1]
        outs.append(acc / l)
    o_ref[0] = jnp.concatenate(outs, axis=-1)
    mo_ref[0] = jnp.concatenate([maccs[h] / mden[h] for h in range(MEM_HEADS)], axis=-1)


def _sample_finish_kernel(x_ref, cb_ref, u0_ref, u1_ref, u2_ref, attn_ref, mem_ref, z_ref,
                          cw_ref, wout_ref, gf_ref, y_ref, mix):
    cw = cw_ref[...]
    conv = cw[0:1] * u0_ref[...] + cw[1:2] * u1_ref[...] + cw[2:3] * u2_ref[...]
    z = z_ref[...].astype(F32)
    lo, hi = CONV_W, CONV_W + ATTN_W
    mix[:, 0:lo] = (cb_ref[...].astype(F32) * conv * _silu(z[:, 0:lo])).astype(BF16)
    mix[:, lo:hi] = (attn_ref[...] * _silu(z[:, lo:hi])).astype(BF16)
    mix[:, hi:] = (mem_ref[...] * _silu(z[:, hi:])).astype(BF16)
    y_ref[...] = _gate_project_norm(x_ref[...], mix, wout_ref, gf_ref)


def _sample_finish(x, cb, u_taps, attn, mem, z, conv_w, w_out_bf16, g_final):
    rows, D = x.shape
    return pl.pallas_call(
        _sample_finish_kernel,
        out_shape=jax.ShapeDtypeStruct((rows, D), F32),
        scratch_shapes=[pltpu.VMEM((rows, D_MIX), BF16)],
        compiler_params=pltpu.CompilerParams(vmem_limit_bytes=VMEM_LIMIT),
        name="sample_finish",
    )(x, cb, *u_taps, attn, mem, z, conv_w, w_out_bf16, g_final.reshape(1, D))


def kernel(x_prompt, x_sample, mem_prompt, cache_win_k, cache_win_v, cache_conv, cache_mem_k,
           cache_mem_v, g_in, w_in, conv_w, g_mem, w_mem_kv, w_out, g_final):
    assert g_in.shape[0] == 1, "the final RMSNorm is fused into the single layer's tail"
    B, T, D = x_prompt.shape
    DB, S, _ = x_sample.shape
    n_buf = cache_win_k.shape[2]
    n_keep = min(DILATED_CONFIGS[-1][0], T)
    w_in_b = w_in[0].astype(BF16)
    w_out_b = w_out[0].astype(BF16)
    dils = tuple(d for _, d in DILATED_CONFIGS)
    wkvt_b = w_in[0][:, _C_K:_C_MQ].T.astype(BF16)
    cb, u, q, k, v, kt32, vt32, mq, z = _project(x_prompt, g_in[0], w_in_b, wkvt_b, n_keep, True,
                                                 dils, Q_SCALE * LOG2E)
    mkt32, mvt32, mk, mv = _mem_kv(mem_prompt, g_mem[0], w_mem_kv[0].astype(BF16))

    def heads_last(t, heads):
        return jnp.transpose(t.reshape(t.shape[0], heads, HEAD_DIM, -1), (0, 3, 1, 2))[None]

    def heads_first(c):
        return jnp.transpose(c, (0, 2, 3, 1))

    p_win_k = heads_last(kt32, ATTN_HEADS)
    p_win_v = heads_last(vt32, ATTN_HEADS)
    p_conv = u[None, :, -(CONV_WIDTH - 1):]
    p_mem_k = heads_last(mkt32, MEM_HEADS)
    p_mem_v = heads_last(mvt32, MEM_HEADS)
    rows = DB * S
    cb_s, u_s, q_s, _, _, k32, v32, mq_s, z_s = _project(
        x_sample.reshape(1, rows, D), g_in[0], w_in_b, wkvt_b, rows, False, (1,), Q_SCALE)
    k32 = k32.reshape(DB, S, ATTN_W)
    v32 = v32.reshape(DB, S, ATTN_W)
    sample = (q_s[0].reshape(DB, S, ATTN_W).astype(F32), k32, v32,
              heads_first(cache_win_k[0]), heads_first(cache_win_v[0]),
              mq_s.reshape(DB, S, MEM_W).astype(F32),
              heads_first(cache_mem_k[0]), heads_first(cache_mem_v[0]))
    attn, attn_s, mem_s = _attention(q, k, v, sample)
    y_prompt = _prompt_finish(x_prompt, cb, u, attn, mq, z, mk, mv, conv_w[0], w_out_b, g_final)
    full = jnp.concatenate([cache_conv[0], u_s.reshape(DB, S, CONV_W)], axis=1)
    taps = [full[:, t:t + S].reshape(rows, CONV_W) for t in range(CONV_WIDTH)]
    y_sample = _sample_finish(x_sample.reshape(rows, D), cb_s.reshape(rows, CONV_W), taps,
                              attn_s.reshape(rows, ATTN_W), mem_s.reshape(rows, MEM_W),
                              z_s.reshape(rows, D_MIX), conv_w[0], w_out_b,
                              g_final).reshape(DB, S, D)
    s_win_k = k32.reshape(1, DB, S, ATTN_HEADS, HEAD_DIM)
    s_win_v = v32.reshape(1, DB, S, ATTN_HEADS, HEAD_DIM)
    s_conv = full[None, :, -(CONV_WIDTH - 1):]
    return (y_prompt, y_sample, p_win_k, p_win_v, p_conv, p_mem_k, p_mem_v,
            s_win_k, s_win_v, s_conv)
```

```python
import functools

import numpy as np
import jax
import jax.numpy as jnp
from jax import lax
from jax.experimental import pallas as pl
from jax.experimental.pallas import tpu as pltpu

F32 = jnp.float32
BF16 = jnp.bfloat16

HEAD_DIM = 64
ATTN_HEADS = 8
MEM_HEADS = 4
CONV_W = 256
ATTN_W = 512
MEM_W = 256
D_MIX = 1024
CONV_WIDTH = 3
DILATED_CONFIGS = ((128, 1), (512, 4), (2048, 16))
SPAN = 128
ALIBI_MAX = 8.0
RMS_EPS = 1e-6
Q_SCALE = HEAD_DIM ** -0.5
LOG2E = float(np.log2(np.e))

LANES = 128
SUBLANES = 8
VMEM_LIMIT = 56 * 1024 * 1024

_C_CB, _C_CC, _C_Q, _C_K, _C_V, _C_MQ, _C_Z, _C_END = 0, 256, 768, 1280, 1792, 2304, 2560, 3584

ROW_TILE = 512
ATTN_TILE = 512
SAMPLE_ROWS = 2


def _slopes():
    h = np.arange(ATTN_HEADS, dtype=np.float64) + 1.0
    return np.exp2(-ALIBI_MAX * h / ATTN_HEADS)


def _params(n_axes):
    return pltpu.CompilerParams(
        dimension_semantics=("arbitrary",) * n_axes,
        vmem_limit_bytes=VMEM_LIMIT)


def _fold_store(res, out_refs, slabs, dils):
    rows = res.shape[0]
    out_refs[0][0, 0] = res.astype(BF16)
    if len(dils) == 1:
        return
    n_slab = res.shape[1] // LANES
    src, prev_d = slabs[0], 1
    for s in range(n_slab):
        src[s] = res[:, s * LANES:(s + 1) * LANES]
    for idx, (d, o_ref) in enumerate(zip(dils[1:], out_refs[1:])):
        step, n = d // prev_d, rows // d
        dst = slabs[idx + 1] if idx + 2 < len(dils) else None
        for rp in range(prev_d):
            for q in range(step):
                r = q * prev_d + rp
                for s in range(n_slab):
                    val = src[s, pl.ds(rp * (rows // prev_d) + q, n, stride=step), :]
                    o_ref[0, r, :, s * LANES:(s + 1) * LANES] = val.astype(BF16)
                    if dst is not None:
                        dst[s, pl.ds(r * n, n), :] = val
        src, prev_d = dst, d


def _proj_kernel(x_ref, g_ref, w_ref, wkvt_ref, *refs, first, keep_transposed, dils, q_scale):
    n = len(dils)
    cb_ref, u_ref = refs[0:2]
    q_refs, k_refs, v_refs = refs[2:2 + n], refs[2 + n:2 + 2 * n], refs[2 + 2 * n:2 + 3 * n]
    k32_ref, v32_ref, mq_ref, z_ref = refs[2 + 3 * n:6 + 3 * n]
    scratch = refs[6 + 3 * n:]
    slabs = [scratch[a * (n - 1):(a + 1) * (n - 1)] for a in range(3)]
    x = x_ref[0]
    r = lax.rsqrt(jnp.mean(x * x, axis=-1, keepdims=True) + RMS_EPS)
    h = (x * r * g_ref[...]).astype(BF16)

    def seg(lo, hi):
        return jnp.dot(h, w_ref[:, lo:hi], preferred_element_type=F32)

    cb_ref[0] = seg(_C_CB, _C_CC).astype(BF16)
    cch = seg(_C_CC, _C_Q)
    u_ref[0] = cch[:, :CONV_W] * cch[:, CONV_W:]
    _fold_store(seg(_C_Q, _C_K) * q_scale, q_refs, slabs[0], dils)
    k = seg(_C_K, _C_V)
    _fold_store(k, k_refs, slabs[1], dils)
    v = seg(_C_V, _C_MQ)
    _fold_store(v, v_refs, slabs[2], dils)
    mq_ref[0] = (seg(_C_MQ, _C_Z) * q_scale).astype(BF16)
    z_ref[0] = seg(_C_Z, _C_END).astype(BF16)
    if keep_transposed:
        @pl.when(pl.program_id(1) >= first)
        def _():
            nt = (((1,), (1,)), ((), ()))
            k32_ref[0] = lax.dot_general(wkvt_ref[0:ATTN_W], h, nt, preferred_element_type=F32)
            v32_ref[0] = lax.dot_general(wkvt_ref[ATTN_W:], h, nt, preferred_element_type=F32)
    else:
        k32_ref[0] = k
        v32_ref[0] = v


def _project(x, g, w_bf16, wkvt_bf16, n_keep, keep_transposed, dils, q_scale):
    B, T, D = x.shape
    tm = ROW_TILE
    nt = T // tm
    first = (T - n_keep) // tm
    n = len(dils)

    def row(width):
        return pl.BlockSpec((1, tm, width), lambda b, i: (b, i, 0))

    folded_specs = [pl.BlockSpec((1, d, tm // d, ATTN_W), lambda b, i: (b, 0, i, 0)) for d in dils]
    folded_shapes = [jax.ShapeDtypeStruct((B, d, T // d, ATTN_W), BF16) for d in dils]

    if keep_transposed:
        keep = pl.BlockSpec((1, ATTN_W, tm), lambda b, i: (b, 0, jnp.maximum(i - first, 0)))
        keep_shape = (B, ATTN_W, n_keep)
    else:
        keep = pl.BlockSpec((1, tm, ATTN_W), lambda b, i: (b, jnp.maximum(i - first, 0), 0))
        keep_shape = (B, n_keep, ATTN_W)
    out_shape = (
        [jax.ShapeDtypeStruct((B, T, CONV_W), BF16),
         jax.ShapeDtypeStruct((B, T, CONV_W), F32)]
        + folded_shapes * 3
        + [jax.ShapeDtypeStruct(keep_shape, F32),
           jax.ShapeDtypeStruct(keep_shape, F32),
           jax.ShapeDtypeStruct((B, T, MEM_W), BF16),
           jax.ShapeDtypeStruct((B, T, D_MIX), BF16)])
    out_specs = ([row(CONV_W), row(CONV_W)] + folded_specs * 3
                 + [keep, keep, row(MEM_W), row(D_MIX)])
    scratch = [pltpu.VMEM((ATTN_W // LANES, tm, LANES), F32)] * (3 * (n - 1))
    res = pl.pallas_call(
        functools.partial(_proj_kernel, first=first, keep_transposed=keep_transposed, dils=dils,
                          q_scale=q_scale),
        grid=(B, nt),
        in_specs=[row(D),
                  pl.BlockSpec((1, D), lambda b, i: (0, 0)),
                  pl.BlockSpec(w_bf16.shape, lambda b, i: (0, 0)),
                  pl.BlockSpec(wkvt_bf16.shape, lambda b, i: (0, 0))],
        out_specs=tuple(out_specs),
        out_shape=tuple(out_shape),
        scratch_shapes=scratch,
        compiler_params=_params(2),
        name="proj",
    )(x, g.reshape(1, D), w_bf16, wkvt_bf16)
    cb, u = res[0:2]
    q, k, v = res[2:2 + n], res[2 + n:2 + 2 * n], res[2 + 2 * n:2 + 3 * n]
    k32, v32, mq, z = res[2 + 3 * n:]
    return cb, u, q, k, v, k32, v32, mq, z


def _memkv_kernel(x_ref, g_ref, w_ref, k32_ref, v32_ref, k_ref, v_ref):
    x = x_ref[...]
    r = lax.rsqrt(jnp.mean(x * x, axis=-1, keepdims=True) + RMS_EPS)
    h = (x * r * g_ref[...]).astype(BF16)
    kv = jnp.dot(h, w_ref[...], preferred_element_type=F32)
    k_ref[...] = kv[:, :MEM_W].astype(BF16)
    v_ref[...] = kv[:, MEM_W:].astype(BF16)
    n_mem = k32_ref.shape[2]
    for b in range(k32_ref.shape[0]):
        kvt = kv[b * n_mem:(b + 1) * n_mem].T
        k32_ref[b] = kvt[:MEM_W]
        v32_ref[b] = kvt[MEM_W:]


def _mem_kv(mem, g, w_bf16):
    B, M, D = mem.shape
    rows = B * M
    k32, v32, k, v = pl.pallas_call(
        _memkv_kernel,
        out_shape=(jax.ShapeDtypeStruct((B, MEM_W, M), F32),
                   jax.ShapeDtypeStruct((B, MEM_W, M), F32),
                   jax.ShapeDtypeStruct((rows, MEM_W), BF16),
                   jax.ShapeDtypeStruct((rows, MEM_W), BF16)),
        compiler_params=pltpu.CompilerParams(vmem_limit_bytes=VMEM_LIMIT),
        name="memkv",
    )(mem.reshape(rows, D), g.reshape(1, D), w_bf16)
    return k32, v32, k.reshape(B, M, MEM_W), v.reshape(B, M, MEM_W)


def _band_bias_table():
    qi = np.arange(SPAN)[:, None] + SPAN
    ki = np.arange(2 * SPAN)[None, :]
    dist = qi - ki
    valid = (dist >= 0) & (dist <= SPAN)
    slopes = _slopes()
    tabs = []
    for _, dil in DILATED_CONFIGS:
        for first in (False, True):
            ok = valid & (ki >= SPAN) if first else valid
            for h in range(ATTN_HEADS):
                bias = -(slopes[h] * dil) * dist * LOG2E
                tabs.append(np.where(ok, bias, -np.inf))
    return np.stack(tabs).astype(np.float32)


def _pair_attention(qb, kb, vb, bias_lo, bias_hi):
    lane = lax.broadcasted_iota(jnp.int32, qb.shape, 1)
    lo = lane < HEAD_DIM
    zero = jnp.zeros_like(qb)
    accs, ms, ls = [], [], []
    for sel, bias in ((lo, bias_lo), (~lo, bias_hi)):
        qm = jnp.where(sel, qb, zero)
        s = lax.dot_general(qm, kb, (((1,), (1,)), ((), ())), preferred_element_type=F32)
        if bias is not None:
            s = s + bias
        m = jnp.max(s, axis=-1, keepdims=True)
        p = jnp.exp2(s - m)
        ms.append(m)
        ls.append(jnp.sum(p, axis=-1, keepdims=True))
        accs.append(jnp.dot(p.astype(BF16), vb, preferred_element_type=F32))
    l = jnp.where(lo, ls[0], ls[1])
    o = jnp.where(lo, accs[0], accs[1]) / l
    lse = jnp.where(lo, ms[0], ms[1]) + jnp.log2(l)
    return o, lse


def _attn_kernel(*refs, steps):
    ins = refs[:15]
    tab_ref = refs[15]
    outs = refs[16:22]
    kcat, vcat = refs[22], refs[23]
    s = pl.program_id(1)
    nblk = ATTN_TILE // SPAN
    for c in range(3):
        _, kp, ko, vp, vo = ins[5 * c:5 * c + 5]
        kcat[c, 0:SPAN] = kp[0, 0]
        kcat[c, SPAN:] = ko[0, 0]
        vcat[c, 0:SPAN] = vp[0, 0]
        vcat[c, SPAN:] = vo[0, 0]

    def body(j, carry):
        row0 = pl.multiple_of(j * SPAN, SPAN)
        for c, (_, dil) in enumerate(DILATED_CONFIGS):
            q_ref = ins[5 * c]
            o_ref, l_ref = outs[2 * c], outs[2 * c + 1]
            units_per_residue = steps // dil
            first = jnp.logical_and(s % units_per_residue == 0, j == 0).astype(jnp.int32)
            base = (c * 2 + first) * ATTN_HEADS
            for p in range(ATTN_HEADS // 2):
                lanes = slice(p * LANES, (p + 1) * LANES)
                qb = q_ref[0, 0, pl.ds(row0, SPAN), lanes]
                kb = kcat[c, pl.ds(row0, 2 * SPAN), lanes]
                vb = vcat[c, pl.ds(row0, 2 * SPAN), lanes]
                o, lse = _pair_attention(qb, kb, vb, tab_ref[base + 2 * p], tab_ref[base + 2 * p + 1])
                o_ref[0, 0, pl.ds(row0, SPAN), lanes] = o.astype(BF16)
                l_ref[0, 0, pl.ds(row0, SPAN), lanes] = lse
        return carry

    lax.fori_loop(0, nblk, body, 0, unroll=True)


def _prompt_attention(q, k, v):
    B, _, T, W = q[0].shape
    steps = T // ATTN_TILE
    prev_per_tile = ATTN_TILE // SPAN
    args, in_specs, out_specs, out_shape = [], [], [], []
    for c, (_, dil) in enumerate(DILATED_CONFIGS):
        L = T // dil
        upr = L // ATTN_TILE

        def own(b, s, upr=upr):
            return (b, s // upr, s % upr, 0)

        def prev(b, s, upr=upr):
            return (b, s // upr, jnp.maximum((s % upr) * prev_per_tile - 1, 0), 0)

        shape = (B, dil, L, W)
        assert q[c].shape == shape
        args += [q[c], k[c], k[c], v[c], v[c]]
        in_specs += [pl.BlockSpec((1, 1, ATTN_TILE, W), own),
                     pl.BlockSpec((1, 1, SPAN, W), prev),
                     pl.BlockSpec((1, 1, ATTN_TILE, W), own),
                     pl.BlockSpec((1, 1, SPAN, W), prev),
                     pl.BlockSpec((1, 1, ATTN_TILE, W), own)]
        out_specs += [pl.BlockSpec((1, 1, ATTN_TILE, W), own),
                      pl.BlockSpec((1, 1, ATTN_TILE, W), own)]
        out_shape += [jax.ShapeDtypeStruct(shape, BF16), jax.ShapeDtypeStruct(shape, F32)]
    tab = jnp.asarray(_band_bias_table())
    args.append(tab)
    in_specs.append(pl.BlockSpec(tab.shape, lambda b, s: (0, 0, 0)))
    res = pl.pallas_call(
        functools.partial(_attn_kernel, steps=steps),
        grid=(B, steps),
        in_specs=in_specs,
        out_specs=tuple(out_specs),
        out_shape=tuple(out_shape),
        scratch_shapes=[pltpu.VMEM((3, ATTN_TILE + SPAN, W), BF16),
                        pltpu.VMEM((3, ATTN_TILE + SPAN, W), BF16)],
        compiler_params=_params(2),
        name="prompt_attn",
    )(*args)
    return list(res)


def _silu(z):
    return z / (1.0 + jnp.exp(-z))


def _gate_project_norm(x, mix_ref, wout_ref, gf_ref):
    y = x + jnp.dot(mix_ref[...], wout_ref[...], preferred_element_type=F32)
    r = lax.rsqrt(jnp.mean(y * y, axis=-1, keepdims=True) + RMS_EPS)
    return y * r * gf_ref[...]


def _unfold(src_ref, slabs, d, tmp=None, step=4):
    n_slab, rows = slabs.shape[0], slabs.shape[1]

    def piece(r, s):
        return src_ref[0, r, :, s * LANES:(s + 1) * LANES].astype(F32)

    if d <= step:
        for r in range(d):
            for s in range(n_slab):
                slabs[s, pl.ds(r, rows // d, stride=d), :] = piece(r, s)
        return
    prev_d, n = d // step, rows // d
    per = rows // prev_d
    for rp in range(prev_d):
        for q in range(step):
            for s in range(n_slab):
                tmp[s, pl.ds(rp * per + q, n, stride=step), :] = piece(q * prev_d + rp, s)
    for rp in range(prev_d):
        for s in range(n_slab):
            slabs[s, pl.ds(rp, per, stride=prev_d), :] = tmp[s, pl.ds(rp * per, per), :]


def _finish_kernel(x_ref, cb_ref, u_ref, up_ref, o1_ref, l1_ref, o4_ref, l4_ref, o16_ref, l16_ref,
                   mq_ref, z_ref, mk_ref, mv_ref, cw_ref, wout_ref, gf_ref, y_ref, ucat, mix,
                   o4s, l4s, o16s, l16s, otmp, ltmp):
    i = pl.program_id(1)
    tm = ROW_TILE
    ucat[0:SUBLANES] = jnp.where(i == 0, 0.0, up_ref[0])
    ucat[SUBLANES:] = u_ref[0]
    cw = cw_ref[...]
    conv = (cw[0:1] * ucat[pl.ds(SUBLANES - 2, tm)]
            + cw[1:2] * ucat[pl.ds(SUBLANES - 1, tm)]
            + cw[2:3] * ucat[pl.ds(SUBLANES, tm)])
    conv_y = cb_ref[0].astype(F32) * conv
    z = z_ref[0].astype(F32)
    mix[:, 0:CONV_W] = (conv_y * _silu(z[:, 0:CONV_W])).astype(BF16)
    dil4, dil16 = DILATED_CONFIGS[1][1], DILATED_CONFIGS[2][1]
    _unfold(o4_ref, o4s, dil4)
    _unfold(l4_ref, l4s, dil4)
    _unfold(o16_ref, o16s, dil16, otmp)
    _unfold(l16_ref, l16s, dil16, ltmp)
    lo, hi = CONV_W, CONV_W + ATTN_W
    for s in range(ATTN_W // LANES):
        lanes = slice(s * LANES, (s + 1) * LANES)
        l1, l4, l16 = l1_ref[0, 0, :, lanes], l4s[s], l16s[s]
        mx = jnp.maximum(jnp.maximum(l1, l4), l16)
        e1, e4, e16 = jnp.exp2(l1 - mx), jnp.exp2(l4 - mx), jnp.exp2(l16 - mx)
        den = e1 + e4 + e16
        attn = (e1 * o1_ref[0, 0, :, lanes].astype(F32) + e4 * o4s[s] + e16 * o16s[s]) / den
        a = lo + s * LANES
        mix[:, a:a + LANES] = (attn * _silu(z[:, a:a + LANES])).astype(BF16)
    for p in range(MEM_HEADS // 2):
        lanes = slice(p * LANES, (p + 1) * LANES)
        o, _ = _pair_attention(mq_ref[0, :, lanes], mk_ref[0, :, lanes], mv_ref[0, :, lanes],
                               None, None)
        a = hi + p * LANES
        mix[:, a:a + LANES] = (o * _silu(z[:, a:a + LANES])).astype(BF16)
    y_ref[0] = _gate_project_norm(x_ref[0], mix, wout_ref, gf_ref)


def _prompt_finish(x, cb, u, attn, mq, z, mk, mv, conv_w, w_out_bf16, g_final):
    B, T, D = x.shape
    tm = ROW_TILE

    def row(width):
        return pl.BlockSpec((1, tm, width), lambda b, i: (b, i, 0))

    def const(shape):
        return pl.BlockSpec(shape, lambda b, i: (0,) * len(shape))

    prev8 = pl.BlockSpec((1, SUBLANES, CONV_W),
                         lambda b, i: (b, jnp.maximum(i * (tm // SUBLANES) - 1, 0), 0))
    memspec = pl.BlockSpec((1,) + mk.shape[1:], lambda b, i: (b, 0, 0))
    in_specs = [row(D), row(CONV_W), row(CONV_W), prev8]
    for _, dil in DILATED_CONFIGS:
        folded = pl.BlockSpec((1, dil, tm // dil, ATTN_W), lambda b, i: (b, 0, i, 0))
        in_specs += [folded, folded]
    in_specs += [row(MEM_W), row(D_MIX), memspec, memspec,
                 const(conv_w.shape), const(w_out_bf16.shape), const((1, D))]
    slabs = pltpu.VMEM((ATTN_W // LANES, tm, LANES), F32)
    return pl.pallas_call(
        _finish_kernel,
        grid=(B, T // tm),
        in_specs=in_specs,
        out_specs=row(D),
        out_shape=jax.ShapeDtypeStruct((B, T, D), F32),
        scratch_shapes=[pltpu.VMEM((tm + SUBLANES, CONV_W), F32),
                        pltpu.VMEM((tm, D_MIX), BF16)] + [slabs] * 6,
        compiler_params=_params(2),
        name="prompt_finish",
    )(x, cb, u, u, *attn, mq, z, mk, mv, conv_w, w_out_bf16, g_final.reshape(1, D))


def _sample_tables(S, n_buf):
    j = np.arange(S)[:, None]
    t = np.arange(n_buf)[None, :]
    dist = n_buf + j - t
    count = np.zeros(dist.shape)
    for win, dil in DILATED_CONFIGS:
        count += (dist % dil == 0) & (dist <= win)
    bias = np.where(count > 0, -_slopes()[:, None, None] * dist, -np.inf)
    return count.astype(np.float32), bias.astype(np.float32)


def _sample_attn_kernel(q_ref, kn_ref, vn_ref, kt_ref, vt_ref, mq_ref, mkt_ref, mvt_ref,
                        cnt_ref, bias_ref, o_ref, mo_ref):
    R, S = q_ref.shape[0], q_ref.shape[1]
    nt = (((1,), (1,)), ((), ()))
    slopes = _slopes()
    n_cfg = float(len(DILATED_CONFIGS))
    rowj = lax.broadcasted_iota(jnp.int32, (S, 1), 0)
    cnt = cnt_ref[...]
    rows = range(R)
    q, kn, vn, mq = ([ref[r] for r in rows] for ref in (q_ref, kn_ref, vn_ref, mq_ref))

    def head(a, h):
        return a[:, h * HEAD_DIM:(h + 1) * HEAD_DIM]

    units = [(r, h) for r in rows for h in range(ATTN_HEADS)]
    munits = [(r, h) for r in rows for h in range(MEM_HEADS)]
    scores = [jnp.dot(head(q[r], h).astype(BF16), kt_ref[r, h].astype(BF16),
                      preferred_element_type=F32) + bias_ref[h] for r, h in units]
    mscores = [jnp.dot(head(mq[r], h).astype(BF16), mkt_ref[r, h].astype(BF16),
                       preferred_element_type=F32) for r, h in munits]
    probs, stats = [], []
    for (r, h), s in zip(units, scores):
        qh, knh = head(q[r], h), head(kn[r], h)
        m = jnp.max(s, axis=-1, keepdims=True)
        s_new = []
        for jp in range(S):
            d = rowj - jp
            sj = jnp.sum(qh * knh[jp:jp + 1], axis=-1, keepdims=True)
            sj = jnp.where(d >= 0, sj - float(slopes[h]) * d.astype(F32), -jnp.inf)
            s_new.append(sj)
            m = jnp.maximum(m, sj)
        p = cnt * jnp.exp(s - m)
        l = jnp.sum(p, axis=-1, keepdims=True)
        w_new = []
        for jp in range(S):
            w = jnp.where(rowj == jp, n_cfg, 1.0) * jnp.exp(s_new[jp] - m)
            l = l + w
            w_new.append(w)
        probs.append(p.astype(BF16))
        stats.append((l, w_new))
    mprobs, mden = [], []
    for s in mscores:
        p = jnp.exp(s - jnp.max(s, axis=-1, keepdims=True))
        mden.append(jnp.sum(p, axis=-1, keepdims=True))
        mprobs.append(p.astype(BF16))
    accs = [lax.dot_general(p, vt_ref[r, h].astype(BF16), nt, preferred_element_type=F32)
            for (r, h), p in zip(units, probs)]
    maccs = [lax.dot_general(p, mvt_ref[r, h].astype(BF16), nt, preferred_element_type=F32)
             for (r, h), p in zip(munits, mprobs)]
    outs = []
    for (r, h), acc, (l, w_new) in zip(units, accs, stats):
        vnh = head(vn[r], h)
        for jp in range(S):
            acc = acc + w_new[jp] * vnh[jp:jp + 1]
        outs.append(acc / l)
    mouts = [acc / den for acc, den in zip(maccs, mden)]
    for r in rows:
        o_ref[r] = jnp.concatenate(outs[r * ATTN_HEADS:(r + 1) * ATTN_HEADS], axis=-1)
        mo_ref[r] = jnp.concatenate(mouts[r * MEM_HEADS:(r + 1) * MEM_HEADS], axis=-1)


def _sample_attention(q, k_new, v_new, cache_kt, cache_vt, mq, mem_kt, mem_vt):
    DB, S, W = q.shape
    n_buf = cache_kt.shape[-1]
    R = SAMPLE_ROWS
    assert DB % R == 0
    assert n_buf >= DILATED_CONFIGS[-1][0] and S <= DILATED_CONFIGS[1][1], \
        "new tokens reach each other through dilation 1 only; the buffer covers every window"
    cnt, bias = (jnp.asarray(t) for t in _sample_tables(S, n_buf))

    def rows(a):
        return pl.BlockSpec((R,) + a.shape[1:], lambda b: (b,) + (0,) * (a.ndim - 1))

    def const(a):
        return pl.BlockSpec(a.shape, lambda b: (0,) * a.ndim)

    args = (q, k_new, v_new, cache_kt, cache_vt, mq, mem_kt, mem_vt)
    return pl.pallas_call(
        _sample_attn_kernel,
        grid=(DB // R,),
        in_specs=[rows(a) for a in args] + [const(cnt), const(bias)],
        out_specs=(rows(q), rows(mq)),
        out_shape=(jax.ShapeDtypeStruct((DB, S, W), F32),
                   jax.ShapeDtypeStruct((DB, S, MEM_W), F32)),
        compiler_params=_params(1),
        name="sample_attn",
    )(*args, cnt, bias)


def _sample_finish_kernel(x_ref, cb_ref, u0_ref, u1_ref, u2_ref, attn_ref, mem_ref, z_ref,
                          cw_ref, wout_ref, gf_ref, y_ref, mix):
    cw = cw_ref[...]
    conv = cw[0:1] * u0_ref[...] + cw[1:2] * u1_ref[...] + cw[2:3] * u2_ref[...]
    z = z_ref[...].astype(F32)
    lo, hi = CONV_W, CONV_W + ATTN_W
    mix[:, 0:lo] = (cb_ref[...].astype(F32) * conv * _silu(z[:, 0:lo])).astype(BF16)
    mix[:, lo:hi] = (attn_ref[...] * _silu(z[:, lo:hi])).astype(BF16)
    mix[:, hi:] = (mem_ref[...] * _silu(z[:, hi:])).astype(BF16)
    y_ref[...] = _gate_project_norm(x_ref[...], mix, wout_ref, gf_ref)


def _sample_finish(x, cb, u_taps, attn, mem, z, conv_w, w_out_bf16, g_final):
    rows, D = x.shape
    return pl.pallas_call(
        _sample_finish_kernel,
        out_shape=jax.ShapeDtypeStruct((rows, D), F32),
        scratch_shapes=[pltpu.VMEM((rows, D_MIX), BF16)],
        compiler_params=pltpu.CompilerParams(vmem_limit_bytes=VMEM_LIMIT),
        name="sample_finish",
    )(x, cb, *u_taps, attn, mem, z, conv_w, w_out_bf16, g_final.reshape(1, D))


def kernel(x_prompt, x_sample, mem_prompt, cache_win_k, cache_win_v, cache_conv, cache_mem_k,
           cache_mem_v, g_in, w_in, conv_w, g_mem, w_mem_kv, w_out, g_final):
    assert g_in.shape[0] == 1, "the final RMSNorm is fused into the single layer's tail"
    B, T, D = x_prompt.shape
    DB, S, _ = x_sample.shape
    n_keep = min(DILATED_CONFIGS[-1][0], T)
    w_in_b = w_in[0].astype(BF16)
    w_out_b = w_out[0].astype(BF16)
    dils = tuple(d for _, d in DILATED_CONFIGS)
    wkvt_b = lax.optimization_barrier(w_in[0][:, _C_K:_C_MQ]).T.astype(BF16)
    cb, u, q, k, v, kt32, vt32, mq, z = _project(x_prompt, g_in[0], w_in_b, wkvt_b, n_keep, True,
                                                 dils, Q_SCALE * LOG2E)
    mkt32, mvt32, mk, mv = _mem_kv(mem_prompt, g_mem[0], w_mem_kv[0].astype(BF16))
    attn = _prompt_attention(q, k, v)
    y_prompt = _prompt_finish(x_prompt, cb, u, attn, mq, z, mk, mv, conv_w[0], w_out_b, g_final)

    def heads_last(t, heads):
        return jnp.transpose(t.reshape(t.shape[0], heads, HEAD_DIM, -1), (0, 3, 1, 2))[None]

    def heads_first(c):
        return jnp.transpose(c, (0, 2, 3, 1))

    p_win_k = heads_last(kt32, ATTN_HEADS)
    p_win_v = heads_last(vt32, ATTN_HEADS)
    p_conv = u[None, :, -(CONV_WIDTH - 1):]
    p_mem_k = heads_last(mkt32, MEM_HEADS)
    p_mem_v = heads_last(mvt32, MEM_HEADS)
    rows = DB * S
    cb, u, q, k, v, k32, v32, mq, z = _project(x_sample.reshape(1, rows, D), g_in[0], w_in_b,
                                               wkvt_b, rows, False, (1,), Q_SCALE)
    k32 = k32.reshape(DB, S, ATTN_W)
    v32 = v32.reshape(DB, S, ATTN_W)
    attn_s, mem_s = _sample_attention(
        q[0].reshape(DB, S, ATTN_W).astype(F32), k32, v32,
        heads_first(cache_win_k[0]), heads_first(cache_win_v[0]),
        mq.reshape(DB, S, MEM_W).astype(F32),
        heads_first(cache_mem_k[0]), heads_first(cache_mem_v[0]))
    full = jnp.concatenate([cache_conv[0], u.reshape(DB, S, CONV_W)], axis=1)
    taps = [full[:, t:t + S].reshape(rows, CONV_W) for t in range(CONV_WIDTH)]
    y_sample = _sample_finish(x_sample.reshape(rows, D), cb.reshape(rows, CONV_W), taps,
                              attn_s.reshape(rows, ATTN_W), mem_s.reshape(rows, MEM_W),
                              z.reshape(rows, D_MIX), conv_w[0], w_out_b, g_final).reshape(DB, S, D)
    s_win_k = k32.reshape(1, DB, S, ATTN_HEADS, HEAD_DIM)
    s_win_v = v32.reshape(1, DB, S, ATTN_HEADS, HEAD_DIM)
    s_conv = full[None, :, -(CONV_WIDTH - 1):]
    return (y_prompt, y_sample, p_win_k, p_win_v, p_conv, p_mem_k, p_mem_v,
            s_win_k, s_win_v, s_conv)
```

```python
import functools

import numpy as np
import jax
import jax.numpy as jnp
from jax import lax
from jax.experimental import pallas as pl
from jax.experimental.pallas import tpu as pltpu

F32 = jnp.float32
BF16 = jnp.bfloat16

HEAD_DIM = 64
ATTN_HEADS = 8
MEM_HEADS = 4
CONV_W = 256
ATTN_W = 512
MEM_W = 256
D_MIX = 1024
CONV_WIDTH = 3
DILATED_CONFIGS = ((128, 1), (512, 4), (2048, 16))
SPAN = 128
ALIBI_MAX = 8.0
RMS_EPS = 1e-6
Q_SCALE = HEAD_DIM ** -0.5
LOG2E = float(np.log2(np.e))

LANES = 128
SUBLANES = 8
VMEM_LIMIT = 56 * 1024 * 1024

_C_CB, _C_CC, _C_Q, _C_K, _C_V, _C_MQ, _C_Z, _C_END = 0, 256, 768, 1280, 1792, 2304, 2560, 3584

ROW_TILE = 512
ATTN_TILE = 512
SAMPLE_ROWS = 1


def _slopes():
    h = np.arange(ATTN_HEADS, dtype=np.float64) + 1.0
    return np.exp2(-ALIBI_MAX * h / ATTN_HEADS)


def _params(n_axes):
    return pltpu.CompilerParams(
        dimension_semantics=("arbitrary",) * n_axes,
        vmem_limit_bytes=VMEM_LIMIT)


def _transpose_cast_kernel(w_ref, o_ref):
    o_ref[...] = w_ref[...].T.astype(BF16)


def _transposed_columns(w, lo, hi):
    K = w.shape[0]
    blk = 2 * LANES
    assert lo % blk == 0 and hi % blk == 0
    return pl.pallas_call(
        _transpose_cast_kernel,
        grid=((hi - lo) // blk,),
        in_specs=[pl.BlockSpec((K, blk), lambda i: (0, lo // blk + i))],
        out_specs=pl.BlockSpec((blk, K), lambda i: (i, 0)),
        out_shape=jax.ShapeDtypeStruct((hi - lo, K), BF16),
        compiler_params=_params(1),
        name="wkv_t",
    )(w)


def _fold_store(res, out_refs, slabs, dils):
    rows = res.shape[0]
    out_refs[0][0, 0] = res.astype(BF16)
    if len(dils) == 1:
        return
    n_slab = res.shape[1] // LANES
    src, prev_d = slabs[0], 1
    for s in range(n_slab):
        src[s] = res[:, s * LANES:(s + 1) * LANES]
    for idx, (d, o_ref) in enumerate(zip(dils[1:], out_refs[1:])):
        step, n = d // prev_d, rows // d
        dst = slabs[idx + 1] if idx + 2 < len(dils) else None
        for rp in range(prev_d):
            for q in range(step):
                r = q * prev_d + rp
                for s in range(n_slab):
                    val = src[s, pl.ds(rp * (rows // prev_d) + q, n, stride=step), :]
                    o_ref[0, r, :, s * LANES:(s + 1) * LANES] = val.astype(BF16)
                    if dst is not None:
                        dst[s, pl.ds(r * n, n), :] = val
        src, prev_d = dst, d


def _proj_kernel(x_ref, g_ref, w_ref, wkvt_ref, *refs, first, keep_transposed, dils, q_scale):
    n = len(dils)
    cb_ref, u_ref = refs[0:2]
    q_refs, k_refs, v_refs = refs[2:2 + n], refs[2 + n:2 + 2 * n], refs[2 + 2 * n:2 + 3 * n]
    k32_ref, v32_ref, mq_ref, z_ref = refs[2 + 3 * n:6 + 3 * n]
    scratch = refs[6 + 3 * n:]
    slabs = [scratch[a * (n - 1):(a + 1) * (n - 1)] for a in range(3)]
    x = x_ref[0]
    r = lax.rsqrt(jnp.mean(x * x, axis=-1, keepdims=True) + RMS_EPS)
    h = (x * r * g_ref[...]).astype(BF16)

    def seg(lo, hi):
        return jnp.dot(h, w_ref[:, lo:hi], preferred_element_type=F32)

    cb_ref[0] = seg(_C_CB, _C_CC).astype(BF16)
    cch = seg(_C_CC, _C_Q)
    u_ref[0] = cch[:, :CONV_W] * cch[:, CONV_W:]
    _fold_store(seg(_C_Q, _C_K) * q_scale, q_refs, slabs[0], dils)
    k = seg(_C_K, _C_V)
    _fold_store(k, k_refs, slabs[1], dils)
    v = seg(_C_V, _C_MQ)
    _fold_store(v, v_refs, slabs[2], dils)
    mq_ref[0] = (seg(_C_MQ, _C_Z) * q_scale).astype(BF16)
    z_ref[0] = seg(_C_Z, _C_END).astype(BF16)
    if keep_transposed:
        @pl.when(pl.program_id(1) >= first)
        def _():
            nt = (((1,), (1,)), ((), ()))
            k32_ref[0] = lax.dot_general(wkvt_ref[0:ATTN_W], h, nt, preferred_element_type=F32)
            v32_ref[0] = lax.dot_general(wkvt_ref[ATTN_W:], h, nt, preferred_element_type=F32)
    else:
        k32_ref[0] = k
        v32_ref[0] = v


def _project(x, g, w_bf16, wkvt_bf16, n_keep, keep_transposed, dils, q_scale):
    B, T, D = x.shape
    tm = ROW_TILE
    nt = T // tm
    first = (T - n_keep) // tm
    n = len(dils)

    def row(width):
        return pl.BlockSpec((1, tm, width), lambda b, i: (b, i, 0))

    folded_specs = [pl.BlockSpec((1, d, tm // d, ATTN_W), lambda b, i: (b, 0, i, 0)) for d in dils]
    folded_shapes = [jax.ShapeDtypeStruct((B, d, T // d, ATTN_W), BF16) for d in dils]

    if keep_transposed:
        keep = pl.BlockSpec((1, ATTN_W, tm), lambda b, i: (b, 0, jnp.maximum(i - first, 0)))
        keep_shape = (B, ATTN_W, n_keep)
    else:
        keep = pl.BlockSpec((1, tm, ATTN_W), lambda b, i: (b, jnp.maximum(i - first, 0), 0))
        keep_shape = (B, n_keep, ATTN_W)
    out_shape = (
        [jax.ShapeDtypeStruct((B, T, CONV_W), BF16),
         jax.ShapeDtypeStruct((B, T, CONV_W), F32)]
        + folded_shapes * 3
        + [jax.ShapeDtypeStruct(keep_shape, F32),
           jax.ShapeDtypeStruct(keep_shape, F32),
           jax.ShapeDtypeStruct((B, T, MEM_W), BF16),
           jax.ShapeDtypeStruct((B, T, D_MIX), BF16)])
    out_specs = ([row(CONV_W), row(CONV_W)] + folded_specs * 3
                 + [keep, keep, row(MEM_W), row(D_MIX)])
    scratch = [pltpu.VMEM((ATTN_W // LANES, tm, LANES), F32)] * (3 * (n - 1))
    res = pl.pallas_call(
        functools.partial(_proj_kernel, first=first, keep_transposed=keep_transposed, dils=dils,
                          q_scale=q_scale),
        grid=(B, nt),
        in_specs=[row(D),
                  pl.BlockSpec((1, D), lambda b, i: (0, 0)),
                  pl.BlockSpec(w_bf16.shape, lambda b, i: (0, 0)),
                  pl.BlockSpec(wkvt_bf16.shape, lambda b, i: (0, 0))],
        out_specs=tuple(out_specs),
        out_shape=tuple(out_shape),
        scratch_shapes=scratch,
        compiler_params=_params(2),
        name="proj",
    )(x, g.reshape(1, D), w_bf16, wkvt_bf16)
    cb, u = res[0:2]
    q, k, v = res[2:2 + n], res[2 + n:2 + 2 * n], res[2 + 2 * n:2 + 3 * n]
    k32, v32, mq, z = res[2 + 3 * n:]
    return cb, u, q, k, v, k32, v32, mq, z


def _memkv_kernel(x_ref, g_ref, w_ref, k32_ref, v32_ref, k_ref, v_ref):
    x = x_ref[...]
    r = lax.rsqrt(jnp.mean(x * x, axis=-1, keepdims=True) + RMS_EPS)
    h = (x * r * g_ref[...]).astype(BF16)
    kv = jnp.dot(h, w_ref[...], preferred_element_type=F32)
    k_ref[...] = kv[:, :MEM_W].astype(BF16)
    v_ref[...] = kv[:, MEM_W:].astype(BF16)
    n_mem = k32_ref.shape[2]
    for b in range(k32_ref.shape[0]):
        kvt = kv[b * n_mem:(b + 1) * n_mem].T
        k32_ref[b] = kvt[:MEM_W]
        v32_ref[b] = kvt[MEM_W:]


def _mem_kv(mem, g, w_bf16):
    B, M, D = mem.shape
    rows = B * M
    k32, v32, k, v = pl.pallas_call(
        _memkv_kernel,
        out_shape=(jax.ShapeDtypeStruct((B, MEM_W, M), F32),
                   jax.ShapeDtypeStruct((B, MEM_W, M), F32),
                   jax.ShapeDtypeStruct((rows, MEM_W), BF16),
                   jax.ShapeDtypeStruct((rows, MEM_W), BF16)),
        compiler_params=pltpu.CompilerParams(vmem_limit_bytes=VMEM_LIMIT),
        name="memkv",
    )(mem.reshape(rows, D), g.reshape(1, D), w_bf16)
    return k32, v32, k.reshape(B, M, MEM_W), v.reshape(B, M, MEM_W)


def _band_bias_table():
    qi = np.arange(SPAN)[:, None] + SPAN
    ki = np.arange(2 * SPAN)[None, :]
    dist = qi - ki
    valid = (dist >= 0) & (dist <= SPAN)
    slopes = _slopes()
    tabs = []
    for _, dil in DILATED_CONFIGS:
        for first in (False, True):
            ok = valid & (ki >= SPAN) if first else valid
            for h in range(ATTN_HEADS):
                bias = -(slopes[h] * dil) * dist * LOG2E
                tabs.append(np.where(ok, bias, -np.inf))
    return np.stack(tabs).astype(np.float32)


def _pair_attention(qb, kb, vb, bias_lo, bias_hi):
    lane = lax.broadcasted_iota(jnp.int32, qb.shape, 1)
    lo = lane < HEAD_DIM
    zero = jnp.zeros_like(qb)
    accs, ms, ls = [], [], []
    for sel, bias in ((lo, bias_lo), (~lo, bias_hi)):
        qm = jnp.where(sel, qb, zero)
        s = lax.dot_general(qm, kb, (((1,), (1,)), ((), ())), preferred_element_type=F32)
        if bias is not None:
            s = s + bias
        m = jnp.max(s, axis=-1, keepdims=True)
        p = jnp.exp2(s - m)
        ms.append(m)
        ls.append(jnp.sum(p, axis=-1, keepdims=True))
        accs.append(jnp.dot(p.astype(BF16), vb, preferred_element_type=F32))
    l = jnp.where(lo, ls[0], ls[1])
    o = jnp.where(lo, accs[0], accs[1]) / l
    lse = jnp.where(lo, ms[0], ms[1]) + jnp.log2(l)
    return o, lse


def _attn_kernel(*refs, steps):
    ins = refs[:15]
    tab_ref = refs[15]
    outs = refs[16:22]
    kcat, vcat = refs[22], refs[23]
    s = pl.program_id(1)
    nblk = ATTN_TILE // SPAN
    for c in range(3):
        _, kp, ko, vp, vo = ins[5 * c:5 * c + 5]
        kcat[c, 0:SPAN] = kp[0, 0]
        kcat[c, SPAN:] = ko[0, 0]
        vcat[c, 0:SPAN] = vp[0, 0]
        vcat[c, SPAN:] = vo[0, 0]

    def body(j, carry):
        row0 = pl.multiple_of(j * SPAN, SPAN)
        for c, (_, dil) in enumerate(DILATED_CONFIGS):
            q_ref = ins[5 * c]
            o_ref, l_ref = outs[2 * c], outs[2 * c + 1]
            units_per_residue = steps // dil
            first = jnp.logical_and(s % units_per_residue == 0, j == 0).astype(jnp.int32)
            base = (c * 2 + first) * ATTN_HEADS
            for p in range(ATTN_HEADS // 2):
                lanes = slice(p * LANES, (p + 1) * LANES)
                qb = q_ref[0, 0, pl.ds(row0, SPAN), lanes]
                kb = kcat[c, pl.ds(row0, 2 * SPAN), lanes]
                vb = vcat[c, pl.ds(row0, 2 * SPAN), lanes]
                o, lse = _pair_attention(qb, kb, vb, tab_ref[base + 2 * p], tab_ref[base + 2 * p + 1])
                o_ref[0, 0, pl.ds(row0, SPAN), lanes] = o.astype(BF16)
                l_ref[0, 0, pl.ds(row0, SPAN), lanes] = lse
        return carry

    lax.fori_loop(0, nblk, body, 0, unroll=True)


def _prompt_attention(q, k, v):
    B, _, T, W = q[0].shape
    steps = T // ATTN_TILE
    prev_per_tile = ATTN_TILE // SPAN
    args, in_specs, out_specs, out_shape = [], [], [], []
    for c, (_, dil) in enumerate(DILATED_CONFIGS):
        L = T // dil
        upr = L // ATTN_TILE

        def own(b, s, upr=upr):
            return (b, s // upr, s % upr, 0)

        def prev(b, s, upr=upr):
            return (b, s // upr, jnp.maximum((s % upr) * prev_per_tile - 1, 0), 0)

        shape = (B, dil, L, W)
        assert q[c].shape == shape
        args += [q[c], k[c], k[c], v[c], v[c]]
        in_specs += [pl.BlockSpec((1, 1, ATTN_TILE, W), own),
                     pl.BlockSpec((1, 1, SPAN, W), prev),
                     pl.BlockSpec((1, 1, ATTN_TILE, W), own),
                     pl.BlockSpec((1, 1, SPAN, W), prev),
                     pl.BlockSpec((1, 1, ATTN_TILE, W), own)]
        out_specs += [pl.BlockSpec((1, 1, ATTN_TILE, W), own),
                      pl.BlockSpec((1, 1, ATTN_TILE, W), own)]
        out_shape += [jax.ShapeDtypeStruct(shape, BF16), jax.ShapeDtypeStruct(shape, F32)]
    tab = jnp.asarray(_band_bias_table())
    args.append(tab)
    in_specs.append(pl.BlockSpec(tab.shape, lambda b, s: (0, 0, 0)))
    res = pl.pallas_call(
        functools.partial(_attn_kernel, steps=steps),
        grid=(B, steps),
        in_specs=in_specs,
        out_specs=tuple(out_specs),
        out_shape=tuple(out_shape),
        scratch_shapes=[pltpu.VMEM((3, ATTN_TILE + SPAN, W), BF16),
                        pltpu.VMEM((3, ATTN_TILE + SPAN, W), BF16)],
        compiler_params=_params(2),
        name="prompt_attn",
    )(*args)
    return list(res)


def _silu(z):
    return z / (1.0 + jnp.exp(-z))


def _gate_project_norm(x, mix_ref, wout_ref, gf_ref):
    y = x + jnp.dot(mix_ref[...], wout_ref[...], preferred_element_type=F32)
    r = lax.rsqrt(jnp.mean(y * y, axis=-1, keepdims=True) + RMS_EPS)
    return y * r * gf_ref[...]


def _unfold(src_ref, slabs, d, tmp=None, step=4):
    n_slab, rows = slabs.shape[0], slabs.shape[1]

    def piece(r, s):
        return src_ref[0, r, :, s * LANES:(s + 1) * LANES].astype(F32)

    if d <= step:
        for r in range(d):
            for s in range(n_slab):
                slabs[s, pl.ds(r, rows // d, stride=d), :] = piece(r, s)
        return
    prev_d, n = d // step, rows // d
    per = rows // prev_d
    for rp in range(prev_d):
        for q in range(step):
            for s in range(n_slab):
                tmp[s, pl.ds(rp * per + q, n, stride=step), :] = piece(q * prev_d + rp, s)
    for rp in range(prev_d):
        for s in range(n_slab):
            slabs[s, pl.ds(rp, per, stride=prev_d), :] = tmp[s, pl.ds(rp * per, per), :]


def _finish_kernel(x_ref, cb_ref, u_ref, up_ref, o1_ref, l1_ref, o4_ref, l4_ref, o16_ref, l16_ref,
                   mq_ref, z_ref, mk_ref, mv_ref, cw_ref, wout_ref, gf_ref, y_ref, ucat, mix,
                   o4s, l4s, o16s, l16s, otmp, ltmp):
    i = pl.program_id(1)
    tm = ROW_TILE
    ucat[0:SUBLANES] = jnp.where(i == 0, 0.0, up_ref[0])
    ucat[SUBLANES:] = u_ref[0]
    cw = cw_ref[...]
    conv = (cw[0:1] * ucat[pl.ds(SUBLANES - 2, tm)]
            + cw[1:2] * ucat[pl.ds(SUBLANES - 1, tm)]
            + cw[2:3] * ucat[pl.ds(SUBLANES, tm)])
    conv_y = cb_ref[0].astype(F32) * conv
    z = z_ref[0].astype(F32)
    mix[:, 0:CONV_W] = (conv_y * _silu(z[:, 0:CONV_W])).astype(BF16)
    dil4, dil16 = DILATED_CONFIGS[1][1], DILATED_CONFIGS[2][1]
    _unfold(o4_ref, o4s, dil4)
    _unfold(l4_ref, l4s, dil4)
    _unfold(o16_ref, o16s, dil16, otmp)
    _unfold(l16_ref, l16s, dil16, ltmp)
    lo, hi = CONV_W, CONV_W + ATTN_W
    for s in range(ATTN_W // LANES):
        lanes = slice(s * LANES, (s + 1) * LANES)
        l1, l4, l16 = l1_ref[0, 0, :, lanes], l4s[s], l16s[s]
        mx = jnp.maximum(jnp.maximum(l1, l4), l16)
        e1, e4, e16 = jnp.exp2(l1 - mx), jnp.exp2(l4 - mx), jnp.exp2(l16 - mx)
        den = e1 + e4 + e16
        attn = (e1 * o1_ref[0, 0, :, lanes].astype(F32) + e4 * o4s[s] + e16 * o16s[s]) / den
        a = lo + s * LANES
        mix[:, a:a + LANES] = (attn * _silu(z[:, a:a + LANES])).astype(BF16)
    for p in range(MEM_HEADS // 2):
        lanes = slice(p * LANES, (p + 1) * LANES)
        o, _ = _pair_attention(mq_ref[0, :, lanes], mk_ref[0, :, lanes], mv_ref[0, :, lanes],
                               None, None)
        a = hi + p * LANES
        mix[:, a:a + LANES] = (o * _silu(z[:, a:a + LANES])).astype(BF16)
    y_ref[0] = _gate_project_norm(x_ref[0], mix, wout_ref, gf_ref)


def _prompt_finish(x, cb, u, attn, mq, z, mk, mv, conv_w, w_out_bf16, g_final):
    B, T, D = x.shape
    tm = ROW_TILE

    def row(width):
        return pl.BlockSpec((1, tm, width), lambda b, i: (b, i, 0))

    def const(shape):
        return pl.BlockSpec(shape, lambda b, i: (0,) * len(shape))

    prev8 = pl.BlockSpec((1, SUBLANES, CONV_W),
                         lambda b, i: (b, jnp.maximum(i * (tm // SUBLANES) - 1, 0), 0))
    memspec = pl.BlockSpec((1,) + mk.shape[1:], lambda b, i: (b, 0, 0))
    in_specs = [row(D), row(CONV_W), row(CONV_W), prev8]
    for _, dil in DILATED_CONFIGS:
        folded = pl.BlockSpec((1, dil, tm // dil, ATTN_W), lambda b, i: (b, 0, i, 0))
        in_specs += [folded, folded]
    in_specs += [row(MEM_W), row(D_MIX), memspec, memspec,
                 const(conv_w.shape), const(w_out_bf16.shape), const((1, D))]
    slabs = pltpu.VMEM((ATTN_W // LANES, tm, LANES), F32)
    return pl.pallas_call(
        _finish_kernel,
        grid=(B, T // tm),
        in_specs=in_specs,
        out_specs=row(D),
        out_shape=jax.ShapeDtypeStruct((B, T, D), F32),
        scratch_shapes=[pltpu.VMEM((tm + SUBLANES, CONV_W), F32),
                        pltpu.VMEM((tm, D_MIX), BF16)] + [slabs] * 6,
        compiler_params=_params(2),
        name="prompt_finish",
    )(x, cb, u, u, *attn, mq, z, mk, mv, conv_w, w_out_bf16, g_final.reshape(1, D))


def _sample_tables(S, n_buf):
    j = np.arange(S)[:, None]
    t = np.arange(n_buf)[None, :]
    dist = n_buf + j - t
    count = np.zeros(dist.shape)
    for win, dil in DILATED_CONFIGS:
        count += (dist % dil == 0) & (dist <= win)
    bias = np.where(count > 0, -_slopes()[:, None, None] * dist, -np.inf)
    return count.astype(np.float32), bias.astype(np.float32)


def _sample_attn_kernel(q_ref, kn_ref, vn_ref, kt_ref, vt_ref, mq_ref, mkt_ref, mvt_ref,
                        cnt_ref, bias_ref, o_ref, mo_ref):
    R, S = q_ref.shape[0], q_ref.shape[1]
    nt = (((1,), (1,)), ((), ()))
    slopes = _slopes()
    n_cfg = float(len(DILATED_CONFIGS))
    rowj = lax.broadcasted_iota(jnp.int32, (S, 1), 0)
    cnt = cnt_ref[...]
    rows = range(R)
    q, kn, vn, mq = ([ref[r] for r in rows] for ref in (q_ref, kn_ref, vn_ref, mq_ref))

    def head(a, h):
        return a[:, h * HEAD_DIM:(h + 1) * HEAD_DIM]

    units = [(r, h) for r in rows for h in range(ATTN_HEADS)]
    munits = [(r, h) for r in rows for h in range(MEM_HEADS)]
    scores = [jnp.dot(head(q[r], h).astype(BF16), kt_ref[r, h].astype(BF16),
                      preferred_element_type=F32) + bias_ref[h] for r, h in units]
    mscores = [jnp.dot(head(mq[r], h).astype(BF16), mkt_ref[r, h].astype(BF16),
                       preferred_element_type=F32) for r, h in munits]
    probs, stats = [], []
    for (r, h), s in zip(units, scores):
        qh, knh = head(q[r], h), head(kn[r], h)
        m = jnp.max(s, axis=-1, keepdims=True)
        s_new = []
        for jp in range(S):
            d = rowj - jp
            sj = jnp.sum(qh * knh[jp:jp + 1], axis=-1, keepdims=True)
            sj = jnp.where(d >= 0, sj - float(slopes[h]) * d.astype(F32), -jnp.inf)
            s_new.append(sj)
            m = jnp.maximum(m, sj)
        p = cnt * jnp.exp(s - m)
        l = jnp.sum(p, axis=-1, keepdims=True)
        w_new = []
        for jp in range(S):
            w = jnp.where(rowj == jp, n_cfg, 1.0) * jnp.exp(s_new[jp] - m)
            l = l + w
            w_new.append(w)
        probs.append(p.astype(BF16))
        stats.append((l, w_new))
    mprobs, mden = [], []
    for s in mscores:
        p = jnp.exp(s - jnp.max(s, axis=-1, keepdims=True))
        mden.append(jnp.sum(p, axis=-1, keepdims=True))
        mprobs.append(p.astype(BF16))
    accs = [lax.dot_general(p, vt_ref[r, h].astype(BF16), nt, preferred_element_type=F32)
            for (r, h), p in zip(units, probs)]
    maccs = [lax.dot_general(p, mvt_ref[r, h].astype(BF16), nt, preferred_element_type=F32)
             for (r, h), p in zip(munits, mprobs)]
    outs = []
    for (r, h), acc, (l, w_new) in zip(units, accs, stats):
        vnh = head(vn[r], h)
        for jp in range(S):
            acc = acc + w_new[jp] * vnh[jp:jp + 1]
        outs.append(acc / l)
    mouts = [acc / den for acc, den in zip(maccs, mden)]
    for r in rows:
        o_ref[r] = jnp.concatenate(outs[r * ATTN_HEADS:(r + 1) * ATTN_HEADS], axis=-1)
        mo_ref[r] = jnp.concatenate(mouts[r * MEM_HEADS:(r + 1) * MEM_HEADS], axis=-1)


def _sample_attention(q, k_new, v_new, cache_kt, cache_vt, mq, mem_kt, mem_vt):
    DB, S, W = q.shape
    n_buf = cache_kt.shape[-1]
    R = SAMPLE_ROWS
    assert DB % R == 0
    assert n_buf >= DILATED_CONFIGS[-1][0] and S <= DILATED_CONFIGS[1][1], \
        "new tokens reach each other through dilation 1 only; the buffer covers every window"
    cnt, bias = (jnp.asarray(t) for t in _sample_tables(S, n_buf))

    def rows(a):
        return pl.BlockSpec((R,) + a.shape[1:], lambda b: (b,) + (0,) * (a.ndim - 1))

    def const(a):
        return pl.BlockSpec(a.shape, lambda b: (0,) * a.ndim)

    args = (q, k_new, v_new, cache_kt, cache_vt, mq, mem_kt, mem_vt)
    return pl.pallas_call(
        _sample_attn_kernel,
        grid=(DB // R,),
        in_specs=[rows(a) for a in args] + [const(cnt), const(bias)],
        out_specs=(rows(q), rows(mq)),
        out_shape=(jax.ShapeDtypeStruct((DB, S, W), F32),
                   jax.ShapeDtypeStruct((DB, S, MEM_W), F32)),
        compiler_params=_params(1),
        name="sample_attn",
    )(*args, cnt, bias)


def _sample_finish_kernel(x_ref, cb_ref, u0_ref, u1_ref, u2_ref, attn_ref, mem_ref, z_ref,
                          cw_ref, wout_ref, gf_ref, y_ref, mix):
    cw = cw_ref[...]
    conv = cw[0:1] * u0_ref[...] + cw[1:2] * u1_ref[...] + cw[2:3] * u2_ref[...]
    z = z_ref[...].astype(F32)
    lo, hi = CONV_W, CONV_W + ATTN_W
    mix[:, 0:lo] = (cb_ref[...].astype(F32) * conv * _silu(z[:, 0:lo])).astype(BF16)
    mix[:, lo:hi] = (attn_ref[...] * _silu(z[:, lo:hi])).astype(BF16)
    mix[:, hi:] = (mem_ref[...] * _silu(z[:, hi:])).astype(BF16)
    y_ref[...] = _gate_project_norm(x_ref[...], mix, wout_ref, gf_ref)


def _sample_finish(x, cb, u_taps, attn, mem, z, conv_w, w_out_bf16, g_final):
    rows, D = x.shape
    return pl.pallas_call(
        _sample_finish_kernel,
        out_shape=jax.ShapeDtypeStruct((rows, D), F32),
        scratch_shapes=[pltpu.VMEM((rows, D_MIX), BF16)],
        compiler_params=pltpu.CompilerParams(vmem_limit_bytes=VMEM_LIMIT),
        name="sample_finish",
    )(x, cb, *u_taps, attn, mem, z, conv_w, w_out_bf16, g_final.reshape(1, D))


def kernel(x_prompt, x_sample, mem_prompt, cache_win_k, cache_win_v, cache_conv, cache_mem_k,
           cache_mem_v, g_in, w_in, conv_w, g_mem, w_mem_kv, w_out, g_final):
    assert g_in.shape[0] == 1, "the final RMSNorm is fused into the single layer's tail"
    B, T, D = x_prompt.shape
    DB, S, _ = x_sample.shape
    n_keep = min(DILATED_CONFIGS[-1][0], T)
    w_in_b = w_in[0].astype(BF16)
    w_out_b = w_out[0].astype(BF16)
    dils = tuple(d for _, d in DILATED_CONFIGS)
    wkvt_b = _transposed_columns(w_in[0], _C_K, _C_MQ)
    cb, u, q, k, v, kt32, vt32, mq, z = _project(x_prompt, g_in[0], w_in_b, wkvt_b, n_keep, True,
                                                 dils, Q_SCALE * LOG2E)
    mkt32, mvt32, mk, mv = _mem_kv(mem_prompt, g_mem[0], w_mem_kv[0].astype(BF16))
    attn = _prompt_attention(q, k, v)
    y_prompt = _prompt_finish(x_prompt, cb, u, attn, mq, z, mk, mv, conv_w[0], w_out_b, g_final)

    def heads_last(t, heads):
        return jnp.transpose(t.reshape(t.shape[0], heads, HEAD_DIM, -1), (0, 3, 1, 2))[None]

    def heads_first(c):
        return jnp.transpose(c, (0, 2, 3, 1))

    p_win_k = heads_last(kt32, ATTN_HEADS)
    p_win_v = heads_last(vt32, ATTN_HEADS)
    p_conv = u[None, :, -(CONV_WIDTH - 1):]
    p_mem_k = heads_last(mkt32, MEM_HEADS)
    p_mem_v = heads_last(mvt32, MEM_HEADS)
    rows = DB * S
    cb, u, q, k, v, k32, v32, mq, z = _project(x_sample.reshape(1, rows, D), g_in[0], w_in_b,
                                               wkvt_b, rows, False, (1,), Q_SCALE)
    k32 = k32.reshape(DB, S, ATTN_W)
    v32 = v32.reshape(DB, S, ATTN_W)
    attn_s, mem_s = _sample_attention(
        q[0].reshape(DB, S, ATTN_W).astype(F32), k32, v32,
        heads_first(cache_win_k[0]), heads_first(cache_win_v[0]),
        mq.reshape(DB, S, MEM_W).astype(F32),
        heads_first(cache_mem_k[0]), heads_first(cache_mem_v[0]))
    full = jnp.concatenate([cache_conv[0], u.reshape(DB, S, CONV_W)], axis=1)
    taps = [full[:, t:t + S].reshape(rows, CONV_W) for t in range(CONV_WIDTH)]
    y_sample = _sample_finish(x_sample.reshape(rows, D), cb.reshape(rows, CONV_W), taps,
                              attn_s.reshape(rows, ATTN_W), mem_s.reshape(rows, MEM_W),
                              z.reshape(rows, D_MIX), conv_w[0], w_out_b, g_final).reshape(DB, S, D)
    s_win_k = k32.reshape(1, DB, S, ATTN_HEADS, HEAD_DIM)
    s_win_v = v32.reshape(1, DB, S, ATTN_HEADS, HEAD_DIM)
    s_conv = full[None, :, -(CONV_WIDTH - 1):]
    return (y_prompt, y_sample, p_win_k, p_win_v, p_conv, p_mem_k, p_mem_v,
            s_win_k, s_win_v, s_conv)
```

```python
import functools

import numpy as np
import jax
import jax.numpy as jnp
from jax import lax
from jax.experimental import pallas as pl
from jax.experimental.pallas import tpu as pltpu

F32 = jnp.float32
BF16 = jnp.bfloat16

HEAD_DIM = 64
ATTN_HEADS = 8
MEM_HEADS = 4
CONV_W = 256
ATTN_W = 512
MEM_W = 256
D_MIX = 1024
CONV_WIDTH = 3
DILATED_CONFIGS = ((128, 1), (512, 4), (2048, 16))
SPAN = 128
ALIBI_MAX = 8.0
RMS_EPS = 1e-6
Q_SCALE = HEAD_DIM ** -0.5
LOG2E = float(np.log2(np.e))

LANES = 128
SUBLANES = 8
VMEM_LIMIT = 56 * 1024 * 1024

_C_CB, _C_CC, _C_Q, _C_K, _C_V, _C_MQ, _C_Z, _C_END = 0, 256, 768, 1280, 1792, 2304, 2560, 3584

ROW_TILE = 512
ATTN_TILE = 512
SAMPLE_ROWS = 1


def _slopes():
    h = np.arange(ATTN_HEADS, dtype=np.float64) + 1.0
    return np.exp2(-ALIBI_MAX * h / ATTN_HEADS)


def _params(n_axes):
    return pltpu.CompilerParams(
        dimension_semantics=("arbitrary",) * n_axes,
        vmem_limit_bytes=VMEM_LIMIT)


def _transpose_cast_kernel(w_ref, o_ref):
    o_ref[...] = w_ref[...].T.astype(BF16)


def _transposed_columns(w, lo, hi):
    K = w.shape[0]
    blk = 2 * LANES
    assert lo % blk == 0 and hi % blk == 0
    return pl.pallas_call(
        _transpose_cast_kernel,
        grid=((hi - lo) // blk,),
        in_specs=[pl.BlockSpec((K, blk), lambda i: (0, lo // blk + i))],
        out_specs=pl.BlockSpec((blk, K), lambda i: (i, 0)),
        out_shape=jax.ShapeDtypeStruct((hi - lo, K), BF16),
        compiler_params=_params(1),
        name="wkv_t",
    )(w)


def _fold_store(res, out_refs, slabs, dils):
    rows = res.shape[0]
    out_refs[0][0, 0] = res.astype(BF16)
    if len(dils) == 1:
        return
    n_slab = res.shape[1] // LANES
    src, prev_d = slabs[0], 1
    for s in range(n_slab):
        src[s] = res[:, s * LANES:(s + 1) * LANES]
    for idx, (d, o_ref) in enumerate(zip(dils[1:], out_refs[1:])):
        step, n = d // prev_d, rows // d
        dst = slabs[idx + 1] if idx + 2 < len(dils) else None
        for rp in range(prev_d):
            for q in range(step):
                r = q * prev_d + rp
                for s in range(n_slab):
                    val = src[s, pl.ds(rp * (rows // prev_d) + q, n, stride=step), :]
                    o_ref[0, r, :, s * LANES:(s + 1) * LANES] = val.astype(BF16)
                    if dst is not None:
                        dst[s, pl.ds(r * n, n), :] = val
        src, prev_d = dst, d


def _proj_kernel(x_ref, g_ref, w_ref, wkvt_ref, *refs, first, keep_transposed, dils, q_scale):
    n = len(dils)
    cb_ref, u_ref = refs[0:2]
    q_refs, k_refs, v_refs = refs[2:2 + n], refs[2 + n:2 + 2 * n], refs[2 + 2 * n:2 + 3 * n]
    k32_ref, v32_ref, mq_ref, z_ref = refs[2 + 3 * n:6 + 3 * n]
    scratch = refs[6 + 3 * n:]
    slabs = [scratch[a * (n - 1):(a + 1) * (n - 1)] for a in range(3)]
    x = x_ref[0]
    r = lax.rsqrt(jnp.mean(x * x, axis=-1, keepdims=True) + RMS_EPS)
    h = (x * r * g_ref[...]).astype(BF16)

    def seg(lo, hi):
        return jnp.dot(h, w_ref[:, lo:hi], preferred_element_type=F32)

    cb_ref[0] = seg(_C_CB, _C_CC).astype(BF16)
    cch = seg(_C_CC, _C_Q)
    u_ref[0] = cch[:, :CONV_W] * cch[:, CONV_W:]
    _fold_store(seg(_C_Q, _C_K) * q_scale, q_refs, slabs[0], dils)
    k = seg(_C_K, _C_V)
    _fold_store(k, k_refs, slabs[1], dils)
    v = seg(_C_V, _C_MQ)
    _fold_store(v, v_refs, slabs[2], dils)
    mq_ref[0] = (seg(_C_MQ, _C_Z) * q_scale).astype(BF16)
    z_ref[0] = seg(_C_Z, _C_END).astype(BF16)
    if keep_transposed:
        @pl.when(pl.program_id(1) >= first)
        def _():
            nt = (((1,), (1,)), ((), ()))
            k32_ref[0] = lax.dot_general(wkvt_ref[0:ATTN_W], h, nt, preferred_element_type=F32)
            v32_ref[0] = lax.dot_general(wkvt_ref[ATTN_W:], h, nt, preferred_element_type=F32)
    else:
        k32_ref[0] = k
        v32_ref[0] = v


def _project(x, g, w_bf16, wkvt_bf16, n_keep, keep_transposed, dils, q_scale):
    B, T, D = x.shape
    tm = ROW_TILE
    nt = T // tm
    first = (T - n_keep) // tm
    n = len(dils)

    def row(width):
        return pl.BlockSpec((1, tm, width), lambda b, i: (b, i, 0))

    folded_specs = [pl.BlockSpec((1, d, tm // d, ATTN_W), lambda b, i: (b, 0, i, 0)) for d in dils]
    folded_shapes = [jax.ShapeDtypeStruct((B, d, T // d, ATTN_W), BF16) for d in dils]

    if keep_transposed:
        keep = pl.BlockSpec((1, ATTN_W, tm), lambda b, i: (b, 0, jnp.maximum(i - first, 0)))
        keep_shape = (B, ATTN_W, n_keep)
    else:
        keep = pl.BlockSpec((1, tm, ATTN_W), lambda b, i: (b, jnp.maximum(i - first, 0), 0))
        keep_shape = (B, n_keep, ATTN_W)
    out_shape = (
        [jax.ShapeDtypeStruct((B, T, CONV_W), BF16),
         jax.ShapeDtypeStruct((B, T, CONV_W), F32)]
        + folded_shapes * 3
        + [jax.ShapeDtypeStruct(keep_shape, F32),
           jax.ShapeDtypeStruct(keep_shape, F32),
           jax.ShapeDtypeStruct((B, T, MEM_W), BF16),
           jax.ShapeDtypeStruct((B, T, D_MIX), BF16)])
    out_specs = ([row(CONV_W), row(CONV_W)] + folded_specs * 3
                 + [keep, keep, row(MEM_W), row(D_MIX)])
    scratch = [pltpu.VMEM((ATTN_W // LANES, tm, LANES), F32)] * (3 * (n - 1))
    res = pl.pallas_call(
        functools.partial(_proj_kernel, first=first, keep_transposed=keep_transposed, dils=dils,
                          q_scale=q_scale),
        grid=(B, nt),
        in_specs=[row(D),
                  pl.BlockSpec((1, D), lambda b, i: (0, 0)),
                  pl.BlockSpec(w_bf16.shape, lambda b, i: (0, 0)),
                  pl.BlockSpec(wkvt_bf16.shape, lambda b, i: (0, 0))],
        out_specs=tuple(out_specs),
        out_shape=tuple(out_shape),
        scratch_shapes=scratch,
        compiler_params=_params(2),
        name="proj",
    )(x, g.reshape(1, D), w_bf16, wkvt_bf16)
    cb, u = res[0:2]
    q, k, v = res[2:2 + n], res[2 + n:2 + 2 * n], res[2 + 2 * n:2 + 3 * n]
    k32, v32, mq, z = res[2 + 3 * n:]
    return cb, u, q, k, v, k32, v32, mq, z


def _memkv_kernel(x_ref, g_ref, w_ref, k32_ref, v32_ref, k_ref, v_ref):
    x = x_ref[...]
    r = lax.rsqrt(jnp.mean(x * x, axis=-1, keepdims=True) + RMS_EPS)
    h = (x * r * g_ref[...]).astype(BF16)
    kv = jnp.dot(h, w_ref[...], preferred_element_type=F32)
    k_ref[...] = kv[:, :MEM_W].astype(BF16)
    v_ref[...] = kv[:, MEM_W:].astype(BF16)
    n_mem = k32_ref.shape[2]
    for b in range(k32_ref.shape[0]):
        kvt = kv[b * n_mem:(b + 1) * n_mem].T
        k32_ref[b] = kvt[:MEM_W]
        v32_ref[b] = kvt[MEM_W:]


def _mem_kv(mem, g, w_bf16):
    B, M, D = mem.shape
    rows = B * M
    k32, v32, k, v = pl.pallas_call(
        _memkv_kernel,
        out_shape=(jax.ShapeDtypeStruct((B, MEM_W, M), F32),
                   jax.ShapeDtypeStruct((B, MEM_W, M), F32),
                   jax.ShapeDtypeStruct((rows, MEM_W), BF16),
                   jax.ShapeDtypeStruct((rows, MEM_W), BF16)),
        compiler_params=pltpu.CompilerParams(vmem_limit_bytes=VMEM_LIMIT),
        name="memkv",
    )(mem.reshape(rows, D), g.reshape(1, D), w_bf16)
    return k32, v32, k.reshape(B, M, MEM_W), v.reshape(B, M, MEM_W)


def _band_bias_table():
    qi = np.arange(SPAN)[:, None] + SPAN
    ki = np.arange(2 * SPAN)[None, :]
    dist = qi - ki
    valid = (dist >= 0) & (dist <= SPAN)
    slopes = _slopes()
    tabs = []
    for _, dil in DILATED_CONFIGS:
        for first in (False, True):
            ok = valid & (ki >= SPAN) if first else valid
            for h in range(ATTN_HEADS):
                bias = -(slopes[h] * dil) * dist * LOG2E
                tabs.append(np.where(ok, bias, -np.inf))
    return np.stack(tabs).astype(np.float32)


def _pair_attention(qb, kb, vb, bias_lo, bias_hi):
    lane = lax.broadcasted_iota(jnp.int32, qb.shape, 1)
    lo = lane < HEAD_DIM
    zero = jnp.zeros_like(qb)
    accs, ms, ls = [], [], []
    for sel, bias in ((lo, bias_lo), (~lo, bias_hi)):
        qm = jnp.where(sel, qb, zero)
        s = lax.dot_general(qm, kb, (((1,), (1,)), ((), ())), preferred_element_type=F32)
        if bias is not None:
            s = s + bias
        m = jnp.max(s, axis=-1, keepdims=True)
        p = jnp.exp2(s - m)
        ms.append(m)
        ls.append(jnp.sum(p, axis=-1, keepdims=True))
        accs.append(jnp.dot(p.astype(BF16), vb, preferred_element_type=F32))
    o = jnp.where(lo, accs[0], accs[1]) / jnp.where(lo, ls[0], ls[1])
    return o, ms, ls


def _attn_kernel(*refs, steps):
    ins = refs[:15]
    tab_ref = refs[15]
    outs = refs[16:22]
    kcat, vcat = refs[22], refs[23]
    s = pl.program_id(1)
    nblk = ATTN_TILE // SPAN
    for c in range(3):
        _, kp, ko, vp, vo = ins[5 * c:5 * c + 5]
        kcat[c, 0:SPAN] = kp[0, 0]
        kcat[c, SPAN:] = ko[0, 0]
        vcat[c, 0:SPAN] = vp[0, 0]
        vcat[c, SPAN:] = vo[0, 0]

    def body(j, carry):
        row0 = pl.multiple_of(j * SPAN, SPAN)
        for c, (_, dil) in enumerate(DILATED_CONFIGS):
            q_ref = ins[5 * c]
            o_ref, l_ref = outs[2 * c], outs[2 * c + 1]
            units_per_residue = steps // dil
            first = jnp.logical_and(s % units_per_residue == 0, j == 0).astype(jnp.int32)
            base = (c * 2 + first) * ATTN_HEADS
            lane = lax.broadcasted_iota(jnp.int32, (SPAN, LANES), 1)
            m8 = jnp.zeros((SPAN, LANES), F32)
            l8 = jnp.ones((SPAN, LANES), F32)
            for p in range(ATTN_HEADS // 2):
                lanes = slice(p * LANES, (p + 1) * LANES)
                qb = q_ref[0, 0, pl.ds(row0, SPAN), lanes]
                kb = kcat[c, pl.ds(row0, 2 * SPAN), lanes]
                vb = vcat[c, pl.ds(row0, 2 * SPAN), lanes]
                o, ms, ls = _pair_attention(qb, kb, vb, tab_ref[base + 2 * p],
                                            tab_ref[base + 2 * p + 1])
                o_ref[0, 0, pl.ds(row0, SPAN), lanes] = o.astype(BF16)
                for half in range(2):
                    m8 = jnp.where(lane == 2 * p + half, ms[half], m8)
                    l8 = jnp.where(lane == 2 * p + half, ls[half], l8)
            l_ref[0, 0, pl.ds(row0, SPAN), :] = m8 + jnp.log2(l8)
        return carry

    lax.fori_loop(0, nblk, body, 0, unroll=True)


def _prompt_attention(q, k, v):
    B, _, T, W = q[0].shape
    steps = T // ATTN_TILE
    prev_per_tile = ATTN_TILE // SPAN
    args, in_specs, out_specs, out_shape = [], [], [], []
    for c, (_, dil) in enumerate(DILATED_CONFIGS):
        L = T // dil
        upr = L // ATTN_TILE

        def own(b, s, upr=upr):
            return (b, s // upr, s % upr, 0)

        def prev(b, s, upr=upr):
            return (b, s // upr, jnp.maximum((s % upr) * prev_per_tile - 1, 0), 0)

        shape = (B, dil, L, W)
        assert q[c].shape == shape
        args += [q[c], k[c], k[c], v[c], v[c]]
        in_specs += [pl.BlockSpec((1, 1, ATTN_TILE, W), own),
                     pl.BlockSpec((1, 1, SPAN, W), prev),
                     pl.BlockSpec((1, 1, ATTN_TILE, W), own),
                     pl.BlockSpec((1, 1, SPAN, W), prev),
                     pl.BlockSpec((1, 1, ATTN_TILE, W), own)]
        out_specs += [pl.BlockSpec((1, 1, ATTN_TILE, W), own),
                      pl.BlockSpec((1, 1, ATTN_TILE, LANES), own)]
        out_shape += [jax.ShapeDtypeStruct(shape, BF16),
                      jax.ShapeDtypeStruct((B, dil, L, LANES), F32)]
    tab = jnp.asarray(_band_bias_table())
    args.append(tab)
    in_specs.append(pl.BlockSpec(tab.shape, lambda b, s: (0, 0, 0)))
    res = pl.pallas_call(
        functools.partial(_attn_kernel, steps=steps),
        grid=(B, steps),
        in_specs=in_specs,
        out_specs=tuple(out_specs),
        out_shape=tuple(out_shape),
        scratch_shapes=[pltpu.VMEM((3, ATTN_TILE + SPAN, W), BF16),
                        pltpu.VMEM((3, ATTN_TILE + SPAN, W), BF16)],
        compiler_params=_params(2),
        name="prompt_attn",
    )(*args)
    return list(res)


def _silu(z):
    return z / (1.0 + jnp.exp(-z))


def _gate_project_norm(x, mix_ref, wout_ref, gf_ref):
    y = x + jnp.dot(mix_ref[...], wout_ref[...], preferred_element_type=F32)
    r = lax.rsqrt(jnp.mean(y * y, axis=-1, keepdims=True) + RMS_EPS)
    return y * r * gf_ref[...]


def _unfold(src_ref, slabs, d, tmp=None, step=4):
    n_slab, rows = slabs.shape[0], slabs.shape[1]

    def piece(r, s):
        return src_ref[0, r, :, s * LANES:(s + 1) * LANES].astype(F32)

    if d <= step:
        for r in range(d):
            for s in range(n_slab):
                slabs[s, pl.ds(r, rows // d, stride=d), :] = piece(r, s)
        return
    prev_d, n = d // step, rows // d
    per = rows // prev_d
    for rp in range(prev_d):
        for q in range(step):
            for s in range(n_slab):
                tmp[s, pl.ds(rp * per + q, n, stride=step), :] = piece(q * prev_d + rp, s)
    for rp in range(prev_d):
        for s in range(n_slab):
            slabs[s, pl.ds(rp, per, stride=prev_d), :] = tmp[s, pl.ds(rp * per, per), :]


def _head_expander():
    e = np.zeros((2 * LANES, ATTN_W), np.float32)
    for h in range(ATTN_HEADS):
        e[h, h * HEAD_DIM:(h + 1) * HEAD_DIM] = 1.0
        e[LANES + h, h * HEAD_DIM:(h + 1) * HEAD_DIM] = 1.0
    return e


def _finish_kernel(x_ref, cb_ref, u_ref, up_ref, o1_ref, l1_ref, o4_ref, l4_ref, o16_ref, l16_ref,
                   mq_ref, z_ref, mk_ref, mv_ref, cw_ref, wout_ref, gf_ref, ex_ref, y_ref,
                   ucat, mix, o4s, l4s, o16s, l16s, otmp, ltmp):
    i = pl.program_id(1)
    tm = ROW_TILE
    ucat[0:SUBLANES] = jnp.where(i == 0, 0.0, up_ref[0])
    ucat[SUBLANES:] = u_ref[0]
    cw = cw_ref[...]
    conv = (cw[0:1] * ucat[pl.ds(SUBLANES - 2, tm)]
            + cw[1:2] * ucat[pl.ds(SUBLANES - 1, tm)]
            + cw[2:3] * ucat[pl.ds(SUBLANES, tm)])
    conv_y = cb_ref[0].astype(F32) * conv
    z = z_ref[0].astype(F32)
    mix[:, 0:CONV_W] = (conv_y * _silu(z[:, 0:CONV_W])).astype(BF16)
    dil4, dil16 = DILATED_CONFIGS[1][1], DILATED_CONFIGS[2][1]
    _unfold(o4_ref, o4s, dil4)
    _unfold(l4_ref, l4s, dil4)
    _unfold(o16_ref, o16s, dil16, otmp)
    _unfold(l16_ref, l16s, dil16, ltmp)
    lo, hi = CONV_W, CONV_W + ATTN_W
    l1, l4, l16 = l1_ref[0, 0], l4s[0], l16s[0]
    mx = jnp.maximum(jnp.maximum(l1, l4), l16)
    e1, e4, e16 = jnp.exp2(l1 - mx), jnp.exp2(l4 - mx), jnp.exp2(l16 - mx)
    den = e1 + e4 + e16

    def spread(w):
        w_hi = w.astype(BF16)
        w_lo = (w - w_hi.astype(F32)).astype(BF16)
        return jnp.dot(jnp.concatenate([w_hi, w_lo], axis=1), ex_ref[...],
                       preferred_element_type=F32)

    a1, a4, a16 = spread(e1 / den), spread(e4 / den), spread(e16 / den)
    for s in range(ATTN_W // LANES):
        lanes = slice(s * LANES, (s + 1) * LANES)
        attn = (a1[:, lanes] * o1_ref[0, 0, :, lanes].astype(F32) + a4[:, lanes] * o4s[s]
                + a16[:, lanes] * o16s[s])
        a = lo + s * LANES
        mix[:, a:a + LANES] = (attn * _silu(z[:, a:a + LANES])).astype(BF16)
    for p in range(MEM_HEADS // 2):
        lanes = slice(p * LANES, (p + 1) * LANES)
        o, _, _ = _pair_attention(mq_ref[0, :, lanes], mk_ref[0, :, lanes], mv_ref[0, :, lanes],
                                  None, None)
        a = hi + p * LANES
        mix[:, a:a + LANES] = (o * _silu(z[:, a:a + LANES])).astype(BF16)
    y_ref[0] = _gate_project_norm(x_ref[0], mix, wout_ref, gf_ref)


def _prompt_finish(x, cb, u, attn, mq, z, mk, mv, conv_w, w_out_bf16, g_final):
    B, T, D = x.shape
    tm = ROW_TILE

    def row(width):
        return pl.BlockSpec((1, tm, width), lambda b, i: (b, i, 0))

    def const(shape):
        return pl.BlockSpec(shape, lambda b, i: (0,) * len(shape))

    prev8 = pl.BlockSpec((1, SUBLANES, CONV_W),
                         lambda b, i: (b, jnp.maximum(i * (tm // SUBLANES) - 1, 0), 0))
    memspec = pl.BlockSpec((1,) + mk.shape[1:], lambda b, i: (b, 0, 0))
    in_specs = [row(D), row(CONV_W), row(CONV_W), prev8]
    for _, dil in DILATED_CONFIGS:
        in_specs += [pl.BlockSpec((1, dil, tm // dil, ATTN_W), lambda b, i: (b, 0, i, 0)),
                     pl.BlockSpec((1, dil, tm // dil, LANES), lambda b, i: (b, 0, i, 0))]
    expander = jnp.asarray(_head_expander(), BF16)
    in_specs += [row(MEM_W), row(D_MIX), memspec, memspec,
                 const(conv_w.shape), const(w_out_bf16.shape), const((1, D)),
                 const(expander.shape)]
    o_slabs = pltpu.VMEM((ATTN_W // LANES, tm, LANES), F32)
    l_slabs = pltpu.VMEM((1, tm, LANES), F32)
    return pl.pallas_call(
        _finish_kernel,
        grid=(B, T // tm),
        in_specs=in_specs,
        out_specs=row(D),
        out_shape=jax.ShapeDtypeStruct((B, T, D), F32),
        scratch_shapes=[pltpu.VMEM((tm + SUBLANES, CONV_W), F32),
                        pltpu.VMEM((tm, D_MIX), BF16),
                        o_slabs, l_slabs, o_slabs, l_slabs, o_slabs, l_slabs],
        compiler_params=_params(2),
        name="prompt_finish",
    )(x, cb, u, u, *attn, mq, z, mk, mv, conv_w, w_out_bf16, g_final.reshape(1, D), expander)


def _sample_tables(S, n_buf):
    j = np.arange(S)[:, None]
    t = np.arange(n_buf)[None, :]
    dist = n_buf + j - t
    count = np.zeros(dist.shape)
    for win, dil in DILATED_CONFIGS:
        count += (dist % dil == 0) & (dist <= win)
    bias = np.where(count > 0, -_slopes()[:, None, None] * dist, -np.inf)
    return count.astype(np.float32), bias.astype(np.float32)


def _sample_attn_kernel(q_ref, kn_ref, vn_ref, kt_ref, vt_ref, mq_ref, mkt_ref, mvt_ref,
                        cnt_ref, bias_ref, o_ref, mo_ref):
    R, S = q_ref.shape[0], q_ref.shape[1]
    nt = (((1,), (1,)), ((), ()))
    slopes = _slopes()
    n_cfg = float(len(DILATED_CONFIGS))
    rowj = lax.broadcasted_iota(jnp.int32, (S, 1), 0)
    cnt = cnt_ref[...]
    rows = range(R)
    q, kn, vn, mq = ([ref[r] for r in rows] for ref in (q_ref, kn_ref, vn_ref, mq_ref))

    def head(a, h):
        return a[:, h * HEAD_DIM:(h + 1) * HEAD_DIM]

    units = [(r, h) for r in rows for h in range(ATTN_HEADS)]
    munits = [(r, h) for r in rows for h in range(MEM_HEADS)]
    scores = [jnp.dot(head(q[r], h).astype(BF16), kt_ref[r, h].astype(BF16),
                      preferred_element_type=F32) + bias_ref[h] for r, h in units]
    mscores = [jnp.dot(head(mq[r], h).astype(BF16), mkt_ref[r, h].astype(BF16),
                       preferred_element_type=F32) for r, h in munits]
    probs, stats = [], []
    for (r, h), s in zip(units, scores):
        qh, knh = head(q[r], h), head(kn[r], h)
        m = jnp.max(s, axis=-1, keepdims=True)
        s_new = []
        for jp in range(S):
            d = rowj - jp
            sj = jnp.sum(qh * knh[jp:jp + 1], axis=-1, keepdims=True)
            sj = jnp.where(d >= 0, sj - float(slopes[h]) * d.astype(F32), -jnp.inf)
            s_new.append(sj)
            m = jnp.maximum(m, sj)
        p = cnt * jnp.exp(s - m)
        l = jnp.sum(p, axis=-1, keepdims=True)
        w_new = []
        for jp in range(S):
            w = jnp.where(rowj == jp, n_cfg, 1.0) * jnp.exp(s_new[jp] - m)
            l = l + w
            w_new.append(w)
        probs.append(p.astype(BF16))
        stats.append((l, w_new))
    mprobs, mden = [], []
    for s in mscores:
        p = jnp.exp(s - jnp.max(s, axis=-1, keepdims=True))
        mden.append(jnp.sum(p, axis=-1, keepdims=True))
        mprobs.append(p.astype(BF16))
    accs = [lax.dot_general(p, vt_ref[r, h].astype(BF16), nt, preferred_element_type=F32)
            for (r, h), p in zip(units, probs)]
    maccs = [lax.dot_general(p, mvt_ref[r, h].astype(BF16), nt, preferred_element_type=F32)
             for (r, h), p in zip(munits, mprobs)]
    outs = []
    for (r, h), acc, (l, w_new) in zip(units, accs, stats):
        vnh = head(vn[r], h)
        for jp in range(S):
            acc = acc + w_new[jp] * vnh[jp:jp + 1]
        outs.append(acc / l)
    mouts = [acc / den for acc, den in zip(maccs, mden)]
    for r in rows:
        o_ref[r] = jnp.concatenate(outs[r * ATTN_HEADS:(r + 1) * ATTN_HEADS], axis=-1)
        mo_ref[r] = jnp.concatenate(mouts[r * MEM_HEADS:(r + 1) * MEM_HEADS], axis=-1)


def _sample_attention(q, k_new, v_new, cache_kt, cache_vt, mq, mem_kt, mem_vt):
    DB, S, W = q.shape
    n_buf = cache_kt.shape[-1]
    R = SAMPLE_ROWS
    assert DB % R == 0
    assert n_buf >= DILATED_CONFIGS[-1][0] and S <= DILATED_CONFIGS[1][1], \
        "new tokens reach each other through dilation 1 only; the buffer covers every window"
    cnt, bias = (jnp.asarray(t) for t in _sample_tables(S, n_buf))

    def rows(a):
        return pl.BlockSpec((R,) + a.shape[1:], lambda b: (b,) + (0,) * (a.ndim - 1))

    def const(a):
        return pl.BlockSpec(a.shape, lambda b: (0,) * a.ndim)

    args = (q, k_new, v_new, cache_kt, cache_vt, mq, mem_kt, mem_vt)
    return pl.pallas_call(
        _sample_attn_kernel,
        grid=(DB // R,),
        in_specs=[rows(a) for a in args] + [const(cnt), const(bias)],
        out_specs=(rows(q), rows(mq)),
        out_shape=(jax.ShapeDtypeStruct((DB, S, W), F32),
                   jax.ShapeDtypeStruct((DB, S, MEM_W), F32)),
        compiler_params=_params(1),
        name="sample_attn",
    )(*args, cnt, bias)


def _sample_finish_kernel(x_ref, cb_ref, u0_ref, u1_ref, u2_ref, attn_ref, mem_ref, z_ref,
                          cw_ref, wout_ref, gf_ref, y_ref, mix):
    cw = cw_ref[...]
    conv = cw[0:1] * u0_ref[...] + cw[1:2] * u1_ref[...] + cw[2:3] * u2_ref[...]
    z = z_ref[...].astype(F32)
    lo, hi = CONV_W, CONV_W + ATTN_W
    mix[:, 0:lo] = (cb_ref[...].astype(F32) * conv * _silu(z[:, 0:lo])).astype(BF16)
    mix[:, lo:hi] = (attn_ref[...] * _silu(z[:, lo:hi])).astype(BF16)
    mix[:, hi:] = (mem_ref[...] * _silu(z[:, hi:])).astype(BF16)
    y_ref[...] = _gate_project_norm(x_ref[...], mix, wout_ref, gf_ref)


def _sample_finish(x, cb, u_taps, attn, mem, z, conv_w, w_out_bf16, g_final):
    rows, D = x.shape
    return pl.pallas_call(
        _sample_finish_kernel,
        out_shape=jax.ShapeDtypeStruct((rows, D), F32),
        scratch_shapes=[pltpu.VMEM((rows, D_MIX), BF16)],
        compiler_params=pltpu.CompilerParams(vmem_limit_bytes=VMEM_LIMIT),
        name="sample_finish",
    )(x, cb, *u_taps, attn, mem, z, conv_w, w_out_bf16, g_final.reshape(1, D))


def kernel(x_prompt, x_sample, mem_prompt, cache_win_k, cache_win_v, cache_conv, cache_mem_k,
           cache_mem_v, g_in, w_in, conv_w, g_mem, w_mem_kv, w_out, g_final):
    assert g_in.shape[0] == 1, "the final RMSNorm is fused into the single layer's tail"
    B, T, D = x_prompt.shape
    DB, S, _ = x_sample.shape
    n_keep = min(DILATED_CONFIGS[-1][0], T)
    w_in_b = w_in[0].astype(BF16)
    w_out_b = w_out[0].astype(BF16)
    dils = tuple(d for _, d in DILATED_CONFIGS)
    wkvt_b = _transposed_columns(w_in[0], _C_K, _C_MQ)
    cb, u, q, k, v, kt32, vt32, mq, z = _project(x_prompt, g_in[0], w_in_b, wkvt_b, n_keep, True,
                                                 dils, Q_SCALE * LOG2E)
    mkt32, mvt32, mk, mv = _mem_kv(mem_prompt, g_mem[0], w_mem_kv[0].astype(BF16))
    attn = _prompt_attention(q, k, v)
    y_prompt = _prompt_finish(x_prompt, cb, u, attn, mq, z, mk, mv, conv_w[0], w_out_b, g_final)

    def heads_last(t, heads):
        return jnp.transpose(t.reshape(t.shape[0], heads, HEAD_DIM, -1), (0, 3, 1, 2))[None]

    def heads_first(c):
        return jnp.transpose(c, (0, 2, 3, 1))

    p_win_k = heads_last(kt32, ATTN_HEADS)
    p_win_v = heads_last(vt32, ATTN_HEADS)
    p_conv = u[None, :, -(CONV_WIDTH - 1):]
    p_mem_k = heads_last(mkt32, MEM_HEADS)
    p_mem_v = heads_last(mvt32, MEM_HEADS)
    rows = DB * S
    cb, u, q, k, v, k32, v32, mq, z = _project(x_sample.reshape(1, rows, D), g_in[0], w_in_b,
                                               wkvt_b, rows, False, (1,), Q_SCALE)
    k32 = k32.reshape(DB, S, ATTN_W)
    v32 = v32.reshape(DB, S, ATTN_W)
    attn_s, mem_s = _sample_attention(
        q[0].reshape(DB, S, ATTN_W).astype(F32), k32, v32,
        heads_first(cache_win_k[0]), heads_first(cache_win_v[0]),
        mq.reshape(DB, S, MEM_W).astype(F32),
        heads_first(cache_mem_k[0]), heads_first(cache_mem_v[0]))
    full = jnp.concatenate([cache_conv[0], u.reshape(DB, S, CONV_W)], axis=1)
    taps = [full[:, t:t + S].reshape(rows, CONV_W) for t in range(CONV_WIDTH)]
    y_sample = _sample_finish(x_sample.reshape(rows, D), cb.reshape(rows, CONV_W), taps,
                              attn_s.reshape(rows, ATTN_W), mem_s.reshape(rows, MEM_W),
                              z.reshape(rows, D_MIX), conv_w[0], w_out_b, g_final).reshape(DB, S, D)
    s_win_k = k32.reshape(1, DB, S, ATTN_HEADS, HEAD_DIM)
    s_win_v = v32.reshape(1, DB, S, ATTN_HEADS, HEAD_DIM)
    s_conv = full[None, :, -(CONV_WIDTH - 1):]
    return (y_prompt, y_sample, p_win_k, p_win_v, p_conv, p_mem_k, p_mem_v,
            s_win_k, s_win_v, s_conv)
```

```python
import functools

import numpy as np
import jax
import jax.numpy as jnp
from jax import lax
from jax.experimental import pallas as pl
from jax.experimental.pallas import tpu as pltpu

F32 = jnp.float32
BF16 = jnp.bfloat16

HEAD_DIM = 64
ATTN_HEADS = 8
MEM_HEADS = 4
CONV_W = 256
ATTN_W = 512
MEM_W = 256
D_MIX = 1024
CONV_WIDTH = 3
DILATED_CONFIGS = ((128, 1), (512, 4), (2048, 16))
SPAN = 128
ALIBI_MAX = 8.0
RMS_EPS = 1e-6
Q_SCALE = HEAD_DIM ** -0.5
LOG2E = float(np.log2(np.e))

LANES = 128
SUBLANES = 8
VMEM_LIMIT = 56 * 1024 * 1024

_C_CB, _C_CC, _C_Q, _C_K, _C_V, _C_MQ, _C_Z, _C_END = 0, 256, 768, 1280, 1792, 2304, 2560, 3584

ROW_TILE = 512
ATTN_TILE = 512
SAMPLE_ROWS = 1


def _slopes():
    h = np.arange(ATTN_HEADS, dtype=np.float64) + 1.0
    return np.exp2(-ALIBI_MAX * h / ATTN_HEADS)


def _params(n_axes):
    return pltpu.CompilerParams(
        dimension_semantics=("arbitrary",) * n_axes,
        vmem_limit_bytes=VMEM_LIMIT)


def _transpose_cast_kernel(w_ref, o_ref):
    o_ref[...] = w_ref[...].T.astype(BF16)


def _transposed_columns(w, lo, hi):
    K = w.shape[0]
    blk = 2 * LANES
    assert lo % blk == 0 and hi % blk == 0
    return pl.pallas_call(
        _transpose_cast_kernel,
        grid=((hi - lo) // blk,),
        in_specs=[pl.BlockSpec((K, blk), lambda i: (0, lo // blk + i))],
        out_specs=pl.BlockSpec((blk, K), lambda i: (i, 0)),
        out_shape=jax.ShapeDtypeStruct((hi - lo, K), BF16),
        compiler_params=_params(1),
        name="wkv_t",
    )(w)


def _fold_store(res, out_refs, slabs, dils):
    rows = res.shape[0]
    out_refs[0][0, 0] = res.astype(BF16)
    if len(dils) == 1:
        return
    n_slab = res.shape[1] // LANES
    src, prev_d = slabs[0], 1
    for s in range(n_slab):
        src[s] = res[:, s * LANES:(s + 1) * LANES]
    for idx, (d, o_ref) in enumerate(zip(dils[1:], out_refs[1:])):
        step, n = d // prev_d, rows // d
        dst = slabs[idx + 1] if idx + 2 < len(dils) else None
        for rp in range(prev_d):
            for q in range(step):
                r = q * prev_d + rp
                for s in range(n_slab):
                    val = src[s, pl.ds(rp * (rows // prev_d) + q, n, stride=step), :]
                    o_ref[0, r, :, s * LANES:(s + 1) * LANES] = val.astype(BF16)
                    if dst is not None:
                        dst[s, pl.ds(r * n, n), :] = val
        src, prev_d = dst, d


def _proj_kernel(x_ref, g_ref, w_ref, wkvt_ref, *refs, first, keep_transposed, dils, q_scale):
    n = len(dils)
    cb_ref, u_ref = refs[0:2]
    q_refs, k_refs, v_refs = refs[2:2 + n], refs[2 + n:2 + 2 * n], refs[2 + 2 * n:2 + 3 * n]
    k32_ref, v32_ref, mq_ref, z_ref = refs[2 + 3 * n:6 + 3 * n]
    scratch = refs[6 + 3 * n:]
    slabs = [scratch[a * (n - 1):(a + 1) * (n - 1)] for a in range(3)]
    x = x_ref[0]
    r = lax.rsqrt(jnp.mean(x * x, axis=-1, keepdims=True) + RMS_EPS)
    h = (x * r * g_ref[...]).astype(BF16)

    def seg(lo, hi):
        return jnp.dot(h, w_ref[:, lo:hi], preferred_element_type=F32)

    cb_ref[0] = seg(_C_CB, _C_CC).astype(BF16)
    cch = seg(_C_CC, _C_Q)
    u_ref[0] = cch[:, :CONV_W] * cch[:, CONV_W:]
    _fold_store(seg(_C_Q, _C_K) * q_scale, q_refs, slabs[0], dils)
    k = seg(_C_K, _C_V)
    _fold_store(k, k_refs, slabs[1], dils)
    v = seg(_C_V, _C_MQ)
    _fold_store(v, v_refs, slabs[2], dils)
    mq_ref[0] = (seg(_C_MQ, _C_Z) * q_scale).astype(BF16)
    z_ref[0] = seg(_C_Z, _C_END).astype(BF16)
    if keep_transposed:
        @pl.when(pl.program_id(1) >= first)
        def _():
            nt = (((1,), (1,)), ((), ()))
            k32_ref[0] = lax.dot_general(wkvt_ref[0:ATTN_W], h, nt, preferred_element_type=F32)
            v32_ref[0] = lax.dot_general(wkvt_ref[ATTN_W:], h, nt, preferred_element_type=F32)
    else:
        k32_ref[0] = k
        v32_ref[0] = v


def _project(x, g, w_bf16, wkvt_bf16, n_keep, keep_transposed, dils, q_scale):
    B, T, D = x.shape
    tm = ROW_TILE
    nt = T // tm
    first = (T - n_keep) // tm
    n = len(dils)

    def row(width):
        return pl.BlockSpec((1, tm, width), lambda b, i: (b, i, 0))

    folded_specs = [pl.BlockSpec((1, d, tm // d, ATTN_W), lambda b, i: (b, 0, i, 0)) for d in dils]
    folded_shapes = [jax.ShapeDtypeStruct((B, d, T // d, ATTN_W), BF16) for d in dils]

    if keep_transposed:
        keep = pl.BlockSpec((1, ATTN_W, tm), lambda b, i: (b, 0, jnp.maximum(i - first, 0)))
        keep_shape = (B, ATTN_W, n_keep)
    else:
        keep = pl.BlockSpec((1, tm, ATTN_W), lambda b, i: (b, jnp.maximum(i - first, 0), 0))
        keep_shape = (B, n_keep, ATTN_W)
    out_shape = (
        [jax.ShapeDtypeStruct((B, T, CONV_W), BF16),
         jax.ShapeDtypeStruct((B, T, CONV_W), F32)]
        + folded_shapes * 3
        + [jax.ShapeDtypeStruct(keep_shape, F32),
           jax.ShapeDtypeStruct(keep_shape, F32),
           jax.ShapeDtypeStruct((B, T, MEM_W), BF16),
           jax.ShapeDtypeStruct((B, T, D_MIX), BF16)])
    out_specs = ([row(CONV_W), row(CONV_W)] + folded_specs * 3
                 + [keep, keep, row(MEM_W), row(D_MIX)])
    scratch = [pltpu.VMEM((ATTN_W // LANES, tm, LANES), F32)] * (3 * (n - 1))
    res = pl.pallas_call(
        functools.partial(_proj_kernel, first=first, keep_transposed=keep_transposed, dils=dils,
                          q_scale=q_scale),
        grid=(B, nt),
        in_specs=[row(D),
                  pl.BlockSpec((1, D), lambda b, i: (0, 0)),
                  pl.BlockSpec(w_bf16.shape, lambda b, i: (0, 0)),
                  pl.BlockSpec(wkvt_bf16.shape, lambda b, i: (0, 0))],
        out_specs=tuple(out_specs),
        out_shape=tuple(out_shape),
        scratch_shapes=scratch,
        compiler_params=_params(2),
        name="proj",
    )(x, g.reshape(1, D), w_bf16, wkvt_bf16)
    cb, u = res[0:2]
    q, k, v = res[2:2 + n], res[2 + n:2 + 2 * n], res[2 + 2 * n:2 + 3 * n]
    k32, v32, mq, z = res[2 + 3 * n:]
    return cb, u, q, k, v, k32, v32, mq, z


def _memkv_kernel(x_ref, g_ref, w_ref, k32_ref, v32_ref, k_ref, v_ref):
    x = x_ref[...]
    r = lax.rsqrt(jnp.mean(x * x, axis=-1, keepdims=True) + RMS_EPS)
    h = (x * r * g_ref[...]).astype(BF16)
    kv = jnp.dot(h, w_ref[...], preferred_element_type=F32)
    k_ref[...] = kv[:, :MEM_W].astype(BF16)
    v_ref[...] = kv[:, MEM_W:].astype(BF16)
    n_mem = k32_ref.shape[2]
    for b in range(k32_ref.shape[0]):
        kvt = kv[b * n_mem:(b + 1) * n_mem].T
        k32_ref[b] = kvt[:MEM_W]
        v32_ref[b] = kvt[MEM_W:]


def _mem_kv(mem, g, w_bf16):
    B, M, D = mem.shape
    rows = B * M
    k32, v32, k, v = pl.pallas_call(
        _memkv_kernel,
        out_shape=(jax.ShapeDtypeStruct((B, MEM_W, M), F32),
                   jax.ShapeDtypeStruct((B, MEM_W, M), F32),
                   jax.ShapeDtypeStruct((rows, MEM_W), BF16),
                   jax.ShapeDtypeStruct((rows, MEM_W), BF16)),
        compiler_params=pltpu.CompilerParams(vmem_limit_bytes=VMEM_LIMIT),
        name="memkv",
    )(mem.reshape(rows, D), g.reshape(1, D), w_bf16)
    return k32, v32, k.reshape(B, M, MEM_W), v.reshape(B, M, MEM_W)


def _band_bias_table():
    qi = np.arange(SPAN)[:, None] + SPAN
    ki = np.arange(2 * SPAN)[None, :]
    dist = qi - ki
    valid = (dist >= 0) & (dist <= SPAN)
    slopes = _slopes()
    tabs = []
    for _, dil in DILATED_CONFIGS:
        for first in (False, True):
            ok = valid & (ki >= SPAN) if first else valid
            for h in range(ATTN_HEADS):
                bias = -(slopes[h] * dil) * dist * LOG2E
                tabs.append(np.where(ok, bias, -np.inf))
    return np.stack(tabs).astype(np.float32)


def _pair_attention(qb, kb, vb, bias_lo, bias_hi):
    lane = lax.broadcasted_iota(jnp.int32, qb.shape, 1)
    lo = lane < HEAD_DIM
    zero = jnp.zeros_like(qb)
    v_ones = jnp.concatenate([vb, jnp.ones_like(vb)], axis=1)
    accs, ms, ls = [], [], []
    for sel, bias in ((lo, bias_lo), (~lo, bias_hi)):
        qm = jnp.where(sel, qb, zero)
        s = lax.dot_general(qm, kb, (((1,), (1,)), ((), ())), preferred_element_type=F32)
        if bias is not None:
            s = s + bias
        m = jnp.max(s, axis=-1, keepdims=True)
        p = jnp.exp2(s - m)
        acc = jnp.dot(p.astype(BF16), v_ones, preferred_element_type=F32)
        ms.append(m)
        accs.append(acc[:, :LANES])
        ls.append(acc[:, LANES:])
    o = jnp.where(lo, accs[0], accs[1]) / jnp.where(lo, ls[0], ls[1])
    return o, ms, ls


def _attn_kernel(*refs, steps):
    ins = refs[:15]
    tab_ref = refs[15]
    outs = refs[16:22]
    kcat, vcat = refs[22], refs[23]
    s = pl.program_id(1)
    nblk = ATTN_TILE // SPAN
    for c in range(3):
        _, kp, ko, vp, vo = ins[5 * c:5 * c + 5]
        kcat[c, 0:SPAN] = kp[0, 0]
        kcat[c, SPAN:] = ko[0, 0]
        vcat[c, 0:SPAN] = vp[0, 0]
        vcat[c, SPAN:] = vo[0, 0]

    def body(j, carry):
        row0 = pl.multiple_of(j * SPAN, SPAN)
        for c, (_, dil) in enumerate(DILATED_CONFIGS):
            q_ref = ins[5 * c]
            o_ref, l_ref = outs[2 * c], outs[2 * c + 1]
            units_per_residue = steps // dil
            first = jnp.logical_and(s % units_per_residue == 0, j == 0).astype(jnp.int32)
            base = (c * 2 + first) * ATTN_HEADS
            lane = lax.broadcasted_iota(jnp.int32, (SPAN, LANES), 1)
            m8 = jnp.zeros((SPAN, LANES), F32)
            l8 = jnp.ones((SPAN, LANES), F32)
            for p in range(ATTN_HEADS // 2):
                lanes = slice(p * LANES, (p + 1) * LANES)
                qb = q_ref[0, 0, pl.ds(row0, SPAN), lanes]
                kb = kcat[c, pl.ds(row0, 2 * SPAN), lanes]
                vb = vcat[c, pl.ds(row0, 2 * SPAN), lanes]
                o, ms, ls = _pair_attention(qb, kb, vb, tab_ref[base + 2 * p],
                                            tab_ref[base + 2 * p + 1])
                o_ref[0, 0, pl.ds(row0, SPAN), lanes] = o.astype(BF16)
                for half in range(2):
                    m8 = jnp.where(lane == 2 * p + half, ms[half], m8)
                    l8 = jnp.where(lane == 2 * p + half, ls[half], l8)
            l_ref[0, 0, pl.ds(row0, SPAN), :] = m8 + jnp.log2(l8)
        return carry

    lax.fori_loop(0, nblk, body, 0, unroll=True)


def _prompt_attention(q, k, v):
    B, _, T, W = q[0].shape
    steps = T // ATTN_TILE
    prev_per_tile = ATTN_TILE // SPAN
    args, in_specs, out_specs, out_shape = [], [], [], []
    for c, (_, dil) in enumerate(DILATED_CONFIGS):
        L = T // dil
        upr = L // ATTN_TILE

        def own(b, s, upr=upr):
            return (b, s // upr, s % upr, 0)

        def prev(b, s, upr=upr):
            return (b, s // upr, jnp.maximum((s % upr) * prev_per_tile - 1, 0), 0)

        shape = (B, dil, L, W)
        assert q[c].shape == shape
        args += [q[c], k[c], k[c], v[c], v[c]]
        in_specs += [pl.BlockSpec((1, 1, ATTN_TILE, W), own),
                     pl.BlockSpec((1, 1, SPAN, W), prev),
                     pl.BlockSpec((1, 1, ATTN_TILE, W), own),
                     pl.BlockSpec((1, 1, SPAN, W), prev),
                     pl.BlockSpec((1, 1, ATTN_TILE, W), own)]
        out_specs += [pl.BlockSpec((1, 1, ATTN_TILE, W), own),
                      pl.BlockSpec((1, 1, ATTN_TILE, LANES), own)]
        out_shape += [jax.ShapeDtypeStruct(shape, BF16),
                      jax.ShapeDtypeStruct((B, dil, L, LANES), F32)]
    tab = jnp.asarray(_band_bias_table())
    args.append(tab)
    in_specs.append(pl.BlockSpec(tab.shape, lambda b, s: (0, 0, 0)))
    res = pl.pallas_call(
        functools.partial(_attn_kernel, steps=steps),
        grid=(B, steps),
        in_specs=in_specs,
        out_specs=tuple(out_specs),
        out_shape=tuple(out_shape),
        scratch_shapes=[pltpu.VMEM((3, ATTN_TILE + SPAN, W), BF16)] * 2,
        compiler_params=_params(2),
        name="prompt_attn",
    )(*args)
    return list(res)


def _silu(z):
    return z / (1.0 + jnp.exp(-z))


def _gate_project_norm(x, mix_ref, wout_ref, gf_ref):
    y = x + jnp.dot(mix_ref[...], wout_ref[...], preferred_element_type=F32)
    r = lax.rsqrt(jnp.mean(y * y, axis=-1, keepdims=True) + RMS_EPS)
    return y * r * gf_ref[...]


def _unfold(src_ref, slabs, d, tmp=None, step=4):
    n_slab, rows = slabs.shape[0], slabs.shape[1]

    def piece(r, s):
        return src_ref[0, r, :, s * LANES:(s + 1) * LANES].astype(F32)

    if d <= step:
        for r in range(d):
            for s in range(n_slab):
                slabs[s, pl.ds(r, rows // d, stride=d), :] = piece(r, s)
        return
    prev_d, n = d // step, rows // d
    per = rows // prev_d
    for rp in range(prev_d):
        for q in range(step):
            for s in range(n_slab):
                tmp[s, pl.ds(rp * per + q, n, stride=step), :] = piece(q * prev_d + rp, s)
    for rp in range(prev_d):
        for s in range(n_slab):
            slabs[s, pl.ds(rp, per, stride=prev_d), :] = tmp[s, pl.ds(rp * per, per), :]


def _head_expander():
    e = np.zeros((2 * LANES, ATTN_W), np.float32)
    for h in range(ATTN_HEADS):
        e[h, h * HEAD_DIM:(h + 1) * HEAD_DIM] = 1.0
        e[LANES + h, h * HEAD_DIM:(h + 1) * HEAD_DIM] = 1.0
    return e


def _finish_kernel(x_ref, cb_ref, u_ref, up_ref, o1_ref, l1_ref, o4_ref, l4_ref, o16_ref, l16_ref,
                   mq_ref, z_ref, mk_ref, mv_ref, cw_ref, wout_ref, gf_ref, ex_ref, y_ref,
                   ucat, mix, o4s, l4s, o16s, l16s, otmp, ltmp):
    i = pl.program_id(1)
    tm = ROW_TILE
    ucat[0:SUBLANES] = jnp.where(i == 0, 0.0, up_ref[0])
    ucat[SUBLANES:] = u_ref[0]
    cw = cw_ref[...]
    conv = (cw[0:1] * ucat[pl.ds(SUBLANES - 2, tm)]
            + cw[1:2] * ucat[pl.ds(SUBLANES - 1, tm)]
            + cw[2:3] * ucat[pl.ds(SUBLANES, tm)])
    conv_y = cb_ref[0].astype(F32) * conv
    z = z_ref[0].astype(F32)
    mix[:, 0:CONV_W] = (conv_y * _silu(z[:, 0:CONV_W])).astype(BF16)
    dil4, dil16 = DILATED_CONFIGS[1][1], DILATED_CONFIGS[2][1]
    _unfold(o4_ref, o4s, dil4)
    _unfold(l4_ref, l4s, dil4)
    _unfold(o16_ref, o16s, dil16, otmp)
    _unfold(l16_ref, l16s, dil16, ltmp)
    lo, hi = CONV_W, CONV_W + ATTN_W
    l1, l4, l16 = l1_ref[0, 0], l4s[0], l16s[0]
    mx = jnp.maximum(jnp.maximum(l1, l4), l16)
    e1, e4, e16 = jnp.exp2(l1 - mx), jnp.exp2(l4 - mx), jnp.exp2(l16 - mx)
    den = e1 + e4 + e16

    def spread(w):
        w_hi = w.astype(BF16)
        w_lo = (w - w_hi.astype(F32)).astype(BF16)
        return jnp.dot(jnp.concatenate([w_hi, w_lo], axis=1), ex_ref[...],
                       preferred_element_type=F32)

    a1, a4, a16 = spread(e1 / den), spread(e4 / den), spread(e16 / den)
    for s in range(ATTN_W // LANES):
        lanes = slice(s * LANES, (s + 1) * LANES)
        attn = (a1[:, lanes] * o1_ref[0, 0, :, lanes].astype(F32) + a4[:, lanes] * o4s[s]
                + a16[:, lanes] * o16s[s])
        a = lo + s * LANES
        mix[:, a:a + LANES] = (attn * _silu(z[:, a:a + LANES])).astype(BF16)
    for p in range(MEM_HEADS // 2):
        lanes = slice(p * LANES, (p + 1) * LANES)
        o, _, _ = _pair_attention(mq_ref[0, :, lanes], mk_ref[0, :, lanes], mv_ref[0, :, lanes],
                                  None, None)
        a = hi + p * LANES
        mix[:, a:a + LANES] = (o * _silu(z[:, a:a + LANES])).astype(BF16)
    y_ref[0] = _gate_project_norm(x_ref[0], mix, wout_ref, gf_ref)


def _prompt_finish(x, cb, u, attn, mq, z, mk, mv, conv_w, w_out_bf16, g_final):
    B, T, D = x.shape
    tm = ROW_TILE

    def row(width):
        return pl.BlockSpec((1, tm, width), lambda b, i: (b, i, 0))

    def const(shape):
        return pl.BlockSpec(shape, lambda b, i: (0,) * len(shape))

    prev8 = pl.BlockSpec((1, SUBLANES, CONV_W),
                         lambda b, i: (b, jnp.maximum(i * (tm // SUBLANES) - 1, 0), 0))
    memspec = pl.BlockSpec((1,) + mk.shape[1:], lambda b, i: (b, 0, 0))
    in_specs = [row(D), row(CONV_W), row(CONV_W), prev8]
    for _, dil in DILATED_CONFIGS:
        in_specs += [pl.BlockSpec((1, dil, tm // dil, ATTN_W), lambda b, i: (b, 0, i, 0)),
                     pl.BlockSpec((1, dil, tm // dil, LANES), lambda b, i: (b, 0, i, 0))]
    expander = jnp.asarray(_head_expander(), BF16)
    in_specs += [row(MEM_W), row(D_MIX), memspec, memspec,
                 const(conv_w.shape), const(w_out_bf16.shape), const((1, D)),
                 const(expander.shape)]
    o_slabs = pltpu.VMEM((ATTN_W // LANES, tm, LANES), F32)
    l_slabs = pltpu.VMEM((1, tm, LANES), F32)
    return pl.pallas_call(
        _finish_kernel,
        grid=(B, T // tm),
        in_specs=in_specs,
        out_specs=row(D),
        out_shape=jax.ShapeDtypeStruct((B, T, D), F32),
        scratch_shapes=[pltpu.VMEM((tm + SUBLANES, CONV_W), F32),
                        pltpu.VMEM((tm, D_MIX), BF16),
                        o_slabs, l_slabs, o_slabs, l_slabs, o_slabs, l_slabs],
        compiler_params=_params(2),
        name="prompt_finish",
    )(x, cb, u, u, *attn, mq, z, mk, mv, conv_w, w_out_bf16, g_final.reshape(1, D), expander)


def _sample_tables(S, n_buf):
    j = np.arange(S)[:, None]
    t = np.arange(n_buf)[None, :]
    dist = n_buf + j - t
    count = np.zeros(dist.shape)
    for win, dil in DILATED_CONFIGS:
        count += (dist % dil == 0) & (dist <= win)
    bias = np.where(count > 0, -_slopes()[:, None, None] * dist, -np.inf)
    return count.astype(np.float32), bias.astype(np.float32)


def _sample_attn_kernel(q_ref, kn_ref, vn_ref, kt_ref, vt_ref, mq_ref, mkt_ref, mvt_ref,
                        cnt_ref, bias_ref, o_ref, mo_ref):
    R, S = q_ref.shape[0], q_ref.shape[1]
    nt = (((1,), (1,)), ((), ()))
    slopes = _slopes()
    n_cfg = float(len(DILATED_CONFIGS))
    rowj = lax.broadcasted_iota(jnp.int32, (S, 1), 0)
    cnt = cnt_ref[...]
    rows = range(R)
    q, kn, vn, mq = ([ref[r] for r in rows] for ref in (q_ref, kn_ref, vn_ref, mq_ref))

    def head(a, h):
        return a[:, h * HEAD_DIM:(h + 1) * HEAD_DIM]

    units = [(r, h) for r in rows for h in range(ATTN_HEADS)]
    munits = [(r, h) for r in rows for h in range(MEM_HEADS)]
    scores = [jnp.dot(head(q[r], h).astype(BF16), kt_ref[r, h].astype(BF16),
                      preferred_element_type=F32) + bias_ref[h] for r, h in units]
    mscores = [jnp.dot(head(mq[r], h).astype(BF16), mkt_ref[r, h].astype(BF16),
                       preferred_element_type=F32) for r, h in munits]
    probs, stats = [], []
    for (r, h), s in zip(units, scores):
        qh, knh = head(q[r], h), head(kn[r], h)
        m = jnp.max(s, axis=-1, keepdims=True)
        s_new = []
        for jp in range(S):
            d = rowj - jp
            sj = jnp.sum(qh * knh[jp:jp + 1], axis=-1, keepdims=True)
            sj = jnp.where(d >= 0, sj - float(slopes[h]) * d.astype(F32), -jnp.inf)
            s_new.append(sj)
            m = jnp.maximum(m, sj)
        p = cnt * jnp.exp(s - m)
        l = jnp.sum(p, axis=-1, keepdims=True)
        w_new = []
        for jp in range(S):
            w = jnp.where(rowj == jp, n_cfg, 1.0) * jnp.exp(s_new[jp] - m)
            l = l + w
            w_new.append(w)
        probs.append(p.astype(BF16))
        stats.append((l, w_new))
    mprobs, mden = [], []
    for s in mscores:
        p = jnp.exp(s - jnp.max(s, axis=-1, keepdims=True))
        mden.append(jnp.sum(p, axis=-1, keepdims=True))
        mprobs.append(p.astype(BF16))
    accs = [lax.dot_general(p, vt_ref[r, h].astype(BF16), nt, preferred_element_type=F32)
            for (r, h), p in zip(units, probs)]
    maccs = [lax.dot_general(p, mvt_ref[r, h].astype(BF16), nt, preferred_element_type=F32)
             for (r, h), p in zip(munits, mprobs)]
    outs = []
    for (r, h), acc, (l, w_new) in zip(units, accs, stats):
        vnh = head(vn[r], h)
        for jp in range(S):
            acc = acc + w_new[jp] * vnh[jp:jp + 1]
        outs.append(acc / l)
    mouts = [acc / den for acc, den in zip(maccs, mden)]
    for r in rows:
        o_ref[r] = jnp.concatenate(outs[r * ATTN_HEADS:(r + 1) * ATTN_HEADS], axis=-1)
        mo_ref[r] = jnp.concatenate(mouts[r * MEM_HEADS:(r + 1) * MEM_HEADS], axis=-1)


def _sample_attention(q, k_new, v_new, cache_kt, cache_vt, mq, mem_kt, mem_vt):
    DB, S, W = q.shape
    n_buf = cache_kt.shape[-1]
    R = SAMPLE_ROWS
    assert DB % R == 0
    assert n_buf >= DILATED_CONFIGS[-1][0] and S <= DILATED_CONFIGS[1][1], \
        "new tokens reach each other through dilation 1 only; the buffer covers every window"
    cnt, bias = (jnp.asarray(t) for t in _sample_tables(S, n_buf))

    def rows(a):
        return pl.BlockSpec((R,) + a.shape[1:], lambda b: (b,) + (0,) * (a.ndim - 1))

    def const(a):
        return pl.BlockSpec(a.shape, lambda b: (0,) * a.ndim)

    args = (q, k_new, v_new, cache_kt, cache_vt, mq, mem_kt, mem_vt)
    return pl.pallas_call(
        _sample_attn_kernel,
        grid=(DB // R,),
        in_specs=[rows(a) for a in args] + [const(cnt), const(bias)],
        out_specs=(rows(q), rows(mq)),
        out_shape=(jax.ShapeDtypeStruct((DB, S, W), F32),
                   jax.ShapeDtypeStruct((DB, S, MEM_W), F32)),
        compiler_params=_params(1),
        name="sample_attn",
    )(*args, cnt, bias)


def _sample_finish_kernel(x_ref, cb_ref, u0_ref, u1_ref, u2_ref, attn_ref, mem_ref, z_ref,
                          cw_ref, wout_ref, gf_ref, y_ref, mix):
    cw = cw_ref[...]
    conv = cw[0:1] * u0_ref[...] + cw[1:2] * u1_ref[...] + cw[2:3] * u2_ref[...]
    z = z_ref[...].astype(F32)
    lo, hi = CONV_W, CONV_W + ATTN_W
    mix[:, 0:lo] = (cb_ref[...].astype(F32) * conv * _silu(z[:, 0:lo])).astype(BF16)
    mix[:, lo:hi] = (attn_ref[...] * _silu(z[:, lo:hi])).astype(BF16)
    mix[:, hi:] = (mem_ref[...] * _silu(z[:, hi:])).astype(BF16)
    y_ref[...] = _gate_project_norm(x_ref[...], mix, wout_ref, gf_ref)


def _sample_finish(x, cb, u_taps, attn, mem, z, conv_w, w_out_bf16, g_final):
    rows, D = x.shape
    return pl.pallas_call(
        _sample_finish_kernel,
        out_shape=jax.ShapeDtypeStruct((rows, D), F32),
        scratch_shapes=[pltpu.VMEM((rows, D_MIX), BF16)],
        compiler_params=pltpu.CompilerParams(vmem_limit_bytes=VMEM_LIMIT),
        name="sample_finish",
    )(x, cb, *u_taps, attn, mem, z, conv_w, w_out_bf16, g_final.reshape(1, D))


def kernel(x_prompt, x_sample, mem_prompt, cache_win_k, cache_win_v, cache_conv, cache_mem_k,
           cache_mem_v, g_in, w_in, conv_w, g_mem, w_mem_kv, w_out, g_final):
    assert g_in.shape[0] == 1, "the final RMSNorm is fused into the single layer's tail"
    B, T, D = x_prompt.shape
    DB, S, _ = x_sample.shape
    n_keep = min(DILATED_CONFIGS[-1][0], T)
    w_in_b = w_in[0].astype(BF16)
    w_out_b = w_out[0].astype(BF16)
    dils = tuple(d for _, d in DILATED_CONFIGS)
    wkvt_b = _transposed_columns(w_in[0], _C_K, _C_MQ)
    cb, u, q, k, v, kt32, vt32, mq, z = _project(x_prompt, g_in[0], w_in_b, wkvt_b, n_keep, True,
                                                 dils, Q_SCALE * LOG2E)
    mkt32, mvt32, mk, mv = _mem_kv(mem_prompt, g_mem[0], w_mem_kv[0].astype(BF16))
    attn = _prompt_attention(q, k, v)
    y_prompt = _prompt_finish(x_prompt, cb, u, attn, mq, z, mk, mv, conv_w[0], w_out_b, g_final)

    def heads_last(t, heads):
        return jnp.transpose(t.reshape(t.shape[0], heads, HEAD_DIM, -1), (0, 3, 1, 2))[None]

    def heads_first(c):
        return jnp.transpose(c, (0, 2, 3, 1))

    p_win_k = heads_last(kt32, ATTN_HEADS)
    p_win_v = heads_last(vt32, ATTN_HEADS)
    p_conv = u[None, :, -(CONV_WIDTH - 1):]
    p_mem_k = heads_last(mkt32, MEM_HEADS)
    p_mem_v = heads_last(mvt32, MEM_HEADS)
    rows = DB * S
    cb, u, q, k, v, k32, v32, mq, z = _project(x_sample.reshape(1, rows, D), g_in[0], w_in_b,
                                               wkvt_b, rows, False, (1,), Q_SCALE)
    k32 = k32.reshape(DB, S, ATTN_W)
    v32 = v32.reshape(DB, S, ATTN_W)
    attn_s, mem_s = _sample_attention(
        q[0].reshape(DB, S, ATTN_W).astype(F32), k32, v32,
        heads_first(cache_win_k[0]), heads_first(cache_win_v[0]),
        mq.reshape(DB, S, MEM_W).astype(F32),
        heads_first(cache_mem_k[0]), heads_first(cache_mem_v[0]))
    full = jnp.concatenate([cache_conv[0], u.reshape(DB, S, CONV_W)], axis=1)
    taps = [full[:, t:t + S].reshape(rows, CONV_W) for t in range(CONV_WIDTH)]
    y_sample = _sample_finish(x_sample.reshape(rows, D), cb.reshape(rows, CONV_W), taps,
                              attn_s.reshape(rows, ATTN_W), mem_s.reshape(rows, MEM_W),
                              z.reshape(rows, D_MIX), conv_w[0], w_out_b, g_final).reshape(DB, S, D)
    s_win_k = k32.reshape(1, DB, S, ATTN_HEADS, HEAD_DIM)
    s_win_v = v32.reshape(1, DB, S, ATTN_HEADS, HEAD_DIM)
    s_conv = full[None, :, -(CONV_WIDTH - 1):]
    return (y_prompt, y_sample, p_win_k, p_win_v, p_conv, p_mem_k, p_mem_v,
            s_win_k, s_win_v, s_conv)
```

```python
import functools

import numpy as np
import jax
import jax.numpy as jnp
from jax import lax
from jax.experimental import pallas as pl
from jax.experimental.pallas import tpu as pltpu

F32 = jnp.float32
BF16 = jnp.bfloat16

HEAD_DIM = 64
ATTN_HEADS = 8
MEM_HEADS = 4
CONV_W = 256
ATTN_W = 512
MEM_W = 256
D_MIX = 1024
CONV_WIDTH = 3
DILATED_CONFIGS = ((128, 1), (512, 4), (2048, 16))
SPAN = 128
ALIBI_MAX = 8.0
RMS_EPS = 1e-6
Q_SCALE = HEAD_DIM ** -0.5
LOG2E = float(np.log2(np.e))

LANES = 128
SUBLANES = 8
VMEM_LIMIT = 56 * 1024 * 1024

_C_CB, _C_CC, _C_Q, _C_K, _C_V, _C_MQ, _C_Z, _C_END = 0, 256, 768, 1280, 1792, 2304, 2560, 3584

ROW_TILE = 512
ATTN_TILE = 512
SAMPLE_ROWS = 1


def _slopes():
    h = np.arange(ATTN_HEADS, dtype=np.float64) + 1.0
    return np.exp2(-ALIBI_MAX * h / ATTN_HEADS)


def _params(n_axes):
    return pltpu.CompilerParams(
        dimension_semantics=("arbitrary",) * n_axes,
        vmem_limit_bytes=VMEM_LIMIT)


def _transpose_cast_kernel(w_ref, o_ref):
    o_ref[...] = w_ref[...].T.astype(BF16)


def _transposed_columns(w, lo, hi):
    K = w.shape[0]
    blk = 2 * LANES
    assert lo % blk == 0 and hi % blk == 0
    return pl.pallas_call(
        _transpose_cast_kernel,
        grid=((hi - lo) // blk,),
        in_specs=[pl.BlockSpec((K, blk), lambda i: (0, lo // blk + i))],
        out_specs=pl.BlockSpec((blk, K), lambda i: (i, 0)),
        out_shape=jax.ShapeDtypeStruct((hi - lo, K), BF16),
        compiler_params=_params(1),
        name="wkv_t",
    )(w)


def _fold_store(res, out_refs, slabs, dils):
    rows = res.shape[0]
    out_refs[0][0, 0] = res.astype(BF16)
    if len(dils) == 1:
        return
    n_slab = res.shape[1] // LANES
    src, prev_d = slabs[0], 1
    for s in range(n_slab):
        src[s] = res[:, s * LANES:(s + 1) * LANES]
    for idx, (d, o_ref) in enumerate(zip(dils[1:], out_refs[1:])):
        step, n = d // prev_d, rows // d
        dst = slabs[idx + 1] if idx + 2 < len(dils) else None
        for rp in range(prev_d):
            for q in range(step):
                r = q * prev_d + rp
                for s in range(n_slab):
                    val = src[s, pl.ds(rp * (rows // prev_d) + q, n, stride=step), :]
                    o_ref[0, r, :, s * LANES:(s + 1) * LANES] = val.astype(BF16)
                    if dst is not None:
                        dst[s, pl.ds(r * n, n), :] = val
        src, prev_d = dst, d


def _proj_kernel(x_ref, g_ref, w_ref, wkvt_ref, *refs, first, keep_transposed, dils, q_scale):
    n = len(dils)
    cb_ref, u_ref = refs[0:2]
    q_refs, k_refs, v_refs = refs[2:2 + n], refs[2 + n:2 + 2 * n], refs[2 + 2 * n:2 + 3 * n]
    k32_ref, v32_ref, mq_ref, z_ref = refs[2 + 3 * n:6 + 3 * n]
    scratch = refs[6 + 3 * n:]
    slabs = [scratch[a * (n - 1):(a + 1) * (n - 1)] for a in range(3)]
    x = x_ref[0]
    r = lax.rsqrt(jnp.mean(x * x, axis=-1, keepdims=True) + RMS_EPS)
    h = (x * r * g_ref[...]).astype(BF16)

    def seg(lo, hi):
        return jnp.dot(h, w_ref[:, lo:hi], preferred_element_type=F32)

    cb_ref[0] = seg(_C_CB, _C_CC).astype(BF16)
    cch = seg(_C_CC, _C_Q)
    u_ref[0] = cch[:, :CONV_W] * cch[:, CONV_W:]
    _fold_store(seg(_C_Q, _C_K) * q_scale, q_refs, slabs[0], dils)
    k = seg(_C_K, _C_V)
    _fold_store(k, k_refs, slabs[1], dils)
    v = seg(_C_V, _C_MQ)
    _fold_store(v, v_refs, slabs[2], dils)
    mq_ref[0] = (seg(_C_MQ, _C_Z) * q_scale).astype(BF16)
    z_ref[0] = seg(_C_Z, _C_END).astype(BF16)
    if keep_transposed:
        @pl.when(pl.program_id(1) >= first)
        def _():
            nt = (((1,), (1,)), ((), ()))
            k32_ref[0] = lax.dot_general(wkvt_ref[0:ATTN_W], h, nt, preferred_element_type=F32)
            v32_ref[0] = lax.dot_general(wkvt_ref[ATTN_W:], h, nt, preferred_element_type=F32)
    else:
        k32_ref[0] = k
        v32_ref[0] = v


def _project(x, g, w_bf16, wkvt_bf16, n_keep, keep_transposed, dils, q_scale):
    B, T, D = x.shape
    tm = ROW_TILE
    nt = T // tm
    first = (T - n_keep) // tm
    n = len(dils)

    def row(width):
        return pl.BlockSpec((1, tm, width), lambda b, i: (b, i, 0))

    folded_specs = [pl.BlockSpec((1, d, tm // d, ATTN_W), lambda b, i: (b, 0, i, 0)) for d in dils]
    folded_shapes = [jax.ShapeDtypeStruct((B, d, T // d, ATTN_W), BF16) for d in dils]

    if keep_transposed:
        keep = pl.BlockSpec((1, ATTN_W, tm), lambda b, i: (b, 0, jnp.maximum(i - first, 0)))
        keep_shape = (B, ATTN_W, n_keep)
    else:
        keep = pl.BlockSpec((1, tm, ATTN_W), lambda b, i: (b, jnp.maximum(i - first, 0), 0))
        keep_shape = (B, n_keep, ATTN_W)
    out_shape = (
        [jax.ShapeDtypeStruct((B, T, CONV_W), BF16),
         jax.ShapeDtypeStruct((B, T, CONV_W), F32)]
        + folded_shapes * 3
        + [jax.ShapeDtypeStruct(keep_shape, F32),
           jax.ShapeDtypeStruct(keep_shape, F32),
           jax.ShapeDtypeStruct((B, T, MEM_W), BF16),
           jax.ShapeDtypeStruct((B, T, D_MIX), BF16)])
    out_specs = ([row(CONV_W), row(CONV_W)] + folded_specs * 3
                 + [keep, keep, row(MEM_W), row(D_MIX)])
    scratch = [pltpu.VMEM((ATTN_W // LANES, tm, LANES), F32)] * (3 * (n - 1))
    res = pl.pallas_call(
        functools.partial(_proj_kernel, first=first, keep_transposed=keep_transposed, dils=dils,
                          q_scale=q_scale),
        grid=(B, nt),
        in_specs=[row(D),
                  pl.BlockSpec((1, D), lambda b, i: (0, 0)),
                  pl.BlockSpec(w_bf16.shape, lambda b, i: (0, 0)),
                  pl.BlockSpec(wkvt_bf16.shape, lambda b, i: (0, 0))],
        out_specs=tuple(out_specs),
        out_shape=tuple(out_shape),
        scratch_shapes=scratch,
        compiler_params=_params(2),
        name="proj",
    )(x, g.reshape(1, D), w_bf16, wkvt_bf16)
    cb, u = res[0:2]
    q, k, v = res[2:2 + n], res[2 + n:2 + 2 * n], res[2 + 2 * n:2 + 3 * n]
    k32, v32, mq, z = res[2 + 3 * n:]
    return cb, u, q, k, v, k32, v32, mq, z


def _memkv_kernel(x_ref, g_ref, w_ref, k32_ref, v32_ref, k_ref, v_ref):
    x = x_ref[...]
    r = lax.rsqrt(jnp.mean(x * x, axis=-1, keepdims=True) + RMS_EPS)
    h = (x * r * g_ref[...]).astype(BF16)
    kv = jnp.dot(h, w_ref[...], preferred_element_type=F32)
    k_ref[...] = kv[:, :MEM_W].astype(BF16)
    v_ref[...] = kv[:, MEM_W:].astype(BF16)
    n_mem = k32_ref.shape[2]
    for b in range(k32_ref.shape[0]):
        kvt = kv[b * n_mem:(b + 1) * n_mem].T
        k32_ref[b] = kvt[:MEM_W]
        v32_ref[b] = kvt[MEM_W:]


def _mem_kv(mem, g, w_bf16):
    B, M, D = mem.shape
    rows = B * M
    k32, v32, k, v = pl.pallas_call(
        _memkv_kernel,
        out_shape=(jax.ShapeDtypeStruct((B, MEM_W, M), F32),
                   jax.ShapeDtypeStruct((B, MEM_W, M), F32),
                   jax.ShapeDtypeStruct((rows, MEM_W), BF16),
                   jax.ShapeDtypeStruct((rows, MEM_W), BF16)),
        compiler_params=pltpu.CompilerParams(vmem_limit_bytes=VMEM_LIMIT),
        name="memkv",
    )(mem.reshape(rows, D), g.reshape(1, D), w_bf16)
    return k32, v32, k.reshape(B, M, MEM_W), v.reshape(B, M, MEM_W)


def _band_bias_table():
    qi = np.arange(SPAN)[:, None] + SPAN
    ki = np.arange(2 * SPAN)[None, :]
    dist = qi - ki
    valid = (dist >= 0) & (dist <= SPAN)
    slopes = _slopes()
    tabs = []
    for _, dil in DILATED_CONFIGS:
        for first in (False, True):
            ok = valid & (ki >= SPAN) if first else valid
            for h in range(ATTN_HEADS):
                bias = -(slopes[h] * dil) * dist * LOG2E
                tabs.append(np.where(ok, bias, -np.inf))
    return np.stack(tabs).astype(np.float32)


def _pair_attention(qb, kb, vb, bias_lo, bias_hi):
    lane = lax.broadcasted_iota(jnp.int32, qb.shape, 1)
    lo = lane < HEAD_DIM
    zero = jnp.zeros_like(qb)
    v_ones = jnp.concatenate([vb, jnp.ones_like(vb)], axis=1)
    accs, ms, ls = [], [], []
    for sel, bias in ((lo, bias_lo), (~lo, bias_hi)):
        qm = jnp.where(sel, qb, zero)
        s = lax.dot_general(qm, kb, (((1,), (1,)), ((), ())), preferred_element_type=F32)
        if bias is not None:
            s = s + bias
        m = jnp.max(s, axis=-1, keepdims=True)
        p = jnp.exp2(s - m)
        acc = jnp.dot(p.astype(BF16), v_ones, preferred_element_type=F32)
        ms.append(m)
        accs.append(acc[:, :LANES])
        ls.append(acc[:, LANES:])
    o = jnp.where(lo, accs[0], accs[1]) / jnp.where(lo, ls[0], ls[1])
    return o, ms, ls


def _attn_kernel(*refs, steps):
    ins = refs[:15]
    tab_ref = refs[15]
    outs = refs[16:22]
    kcat, vcat = refs[22], refs[23]
    s = pl.program_id(1)
    nblk = ATTN_TILE // SPAN
    for c in range(3):
        _, kp, ko, vp, vo = ins[5 * c:5 * c + 5]
        kcat[c, 0:SPAN] = kp[0, 0]
        kcat[c, SPAN:] = ko[0, 0]
        vcat[c, 0:SPAN] = vp[0, 0]
        vcat[c, SPAN:] = vo[0, 0]

    def body(j, carry):
        row0 = pl.multiple_of(j * SPAN, SPAN)
        for c, (_, dil) in enumerate(DILATED_CONFIGS):
            q_ref = ins[5 * c]
            o_ref, l_ref = outs[2 * c], outs[2 * c + 1]
            units_per_residue = steps // dil
            first = jnp.logical_and(s % units_per_residue == 0, j == 0).astype(jnp.int32)
            base = (c * 2 + first) * ATTN_HEADS
            lane = lax.broadcasted_iota(jnp.int32, (SPAN, LANES), 1)
            m8 = jnp.zeros((SPAN, LANES), F32)
            l8 = jnp.ones((SPAN, LANES), F32)
            for p in range(ATTN_HEADS // 2):
                lanes = slice(p * LANES, (p + 1) * LANES)
                qb = q_ref[0, 0, pl.ds(row0, SPAN), lanes]
                kb = kcat[c, pl.ds(row0, 2 * SPAN), lanes]
                vb = vcat[c, pl.ds(row0, 2 * SPAN), lanes]
                o, ms, ls = _pair_attention(qb, kb, vb, tab_ref[base + 2 * p],
                                            tab_ref[base + 2 * p + 1])
                o_ref[0, 0, pl.ds(row0, SPAN), lanes] = o.astype(BF16)
                for half in range(2):
                    m8 = jnp.where(lane == 2 * p + half, ms[half], m8)
                    l8 = jnp.where(lane == 2 * p + half, ls[half], l8)
            l_ref[0, 0, pl.ds(row0, SPAN), :] = m8 + jnp.log2(l8)
        return carry

    lax.fori_loop(0, nblk, body, 0, unroll=True)


def _prompt_attention(q, k, v):
    B, _, T, W = q[0].shape
    steps = T // ATTN_TILE
    prev_per_tile = ATTN_TILE // SPAN
    args, in_specs, out_specs, out_shape = [], [], [], []
    for c, (_, dil) in enumerate(DILATED_CONFIGS):
        L = T // dil
        upr = L // ATTN_TILE

        def own(b, s, upr=upr):
            return (b, s // upr, s % upr, 0)

        def prev(b, s, upr=upr):
            return (b, s // upr, jnp.maximum((s % upr) * prev_per_tile - 1, 0), 0)

        shape = (B, dil, L, W)
        assert q[c].shape == shape
        args += [q[c], k[c], k[c], v[c], v[c]]
        in_specs += [pl.BlockSpec((1, 1, ATTN_TILE, W), own),
                     pl.BlockSpec((1, 1, SPAN, W), prev),
                     pl.BlockSpec((1, 1, ATTN_TILE, W), own),
                     pl.BlockSpec((1, 1, SPAN, W), prev),
                     pl.BlockSpec((1, 1, ATTN_TILE, W), own)]
        out_specs += [pl.BlockSpec((1, 1, ATTN_TILE, W), own),
                      pl.BlockSpec((1, 1, ATTN_TILE, LANES), own)]
        out_shape += [jax.ShapeDtypeStruct(shape, BF16),
                      jax.ShapeDtypeStruct((B, dil, L, LANES), F32)]
    tab = jnp.asarray(_band_bias_table())
    args.append(tab)
    in_specs.append(pl.BlockSpec(tab.shape, lambda b, s: (0, 0, 0)))
    res = pl.pallas_call(
        functools.partial(_attn_kernel, steps=steps),
        grid=(B, steps),
        in_specs=in_specs,
        out_specs=tuple(out_specs),
        out_shape=tuple(out_shape),
        scratch_shapes=[pltpu.VMEM((3, ATTN_TILE + SPAN, W), BF16)] * 2,
        compiler_params=_params(2),
        name="prompt_attn",
    )(*args)
    return list(res)


def _silu(z):
    return z / (1.0 + jnp.exp(-z))


def _gate_project_norm(x, mix_ref, wout_ref, gf_ref):
    y = x + jnp.dot(mix_ref[...], wout_ref[...], preferred_element_type=F32)
    r = lax.rsqrt(jnp.mean(y * y, axis=-1, keepdims=True) + RMS_EPS)
    return y * r * gf_ref[...]


def _unfold(src_ref, slabs, d, tmp=None, step=4):
    n_slab, rows = slabs.shape[0], slabs.shape[1]

    def piece(r, s):
        return src_ref[0, r, :, s * LANES:(s + 1) * LANES].astype(F32)

    if d <= step:
        for r in range(d):
            for s in range(n_slab):
                slabs[s, pl.ds(r, rows // d, stride=d), :] = piece(r, s)
        return
    prev_d, n = d // step, rows // d
    per = rows // prev_d
    for rp in range(prev_d):
        for q in range(step):
            for s in range(n_slab):
                tmp[s, pl.ds(rp * per + q, n, stride=step), :] = piece(q * prev_d + rp, s)
    for rp in range(prev_d):
        for s in range(n_slab):
            slabs[s, pl.ds(rp, per, stride=prev_d), :] = tmp[s, pl.ds(rp * per, per), :]


def _head_expander():
    e = np.zeros((2 * LANES, ATTN_W), np.float32)
    for h in range(ATTN_HEADS):
        e[h, h * HEAD_DIM:(h + 1) * HEAD_DIM] = 1.0
        e[LANES + h, h * HEAD_DIM:(h + 1) * HEAD_DIM] = 1.0
    return e


def _finish_kernel(x_ref, cb_ref, u_ref, up_ref, o1_ref, l1_ref, o4_ref, l4_ref, o16_ref, l16_ref,
                   mem_ref, z_ref, cw_ref, wout_ref, gf_ref, ex_ref, y_ref,
                   ucat, mix, o4s, l4s, o16s, l16s, otmp, ltmp):
    i = pl.program_id(1)
    tm = ROW_TILE
    ucat[0:SUBLANES] = jnp.where(i == 0, 0.0, up_ref[0])
    ucat[SUBLANES:] = u_ref[0]
    cw = cw_ref[...]
    conv = (cw[0:1] * ucat[pl.ds(SUBLANES - 2, tm)]
            + cw[1:2] * ucat[pl.ds(SUBLANES - 1, tm)]
            + cw[2:3] * ucat[pl.ds(SUBLANES, tm)])
    conv_y = cb_ref[0].astype(F32) * conv
    z = z_ref[0].astype(F32)
    mix[:, 0:CONV_W] = (conv_y * _silu(z[:, 0:CONV_W])).astype(BF16)
    dil4, dil16 = DILATED_CONFIGS[1][1], DILATED_CONFIGS[2][1]
    _unfold(o4_ref, o4s, dil4)
    _unfold(l4_ref, l4s, dil4)
    _unfold(o16_ref, o16s, dil16, otmp)
    _unfold(l16_ref, l16s, dil16, ltmp)
    lo, hi = CONV_W, CONV_W + ATTN_W
    l1, l4, l16 = l1_ref[0, 0], l4s[0], l16s[0]
    mx = jnp.maximum(jnp.maximum(l1, l4), l16)
    e1, e4, e16 = jnp.exp2(l1 - mx), jnp.exp2(l4 - mx), jnp.exp2(l16 - mx)
    den = e1 + e4 + e16

    def spread(w):
        w_hi = w.astype(BF16)
        w_lo = (w - w_hi.astype(F32)).astype(BF16)
        return jnp.dot(jnp.concatenate([w_hi, w_lo], axis=1), ex_ref[...],
                       preferred_element_type=F32)

    a1, a4, a16 = spread(e1 / den), spread(e4 / den), spread(e16 / den)
    for s in range(ATTN_W // LANES):
        lanes = slice(s * LANES, (s + 1) * LANES)
        attn = (a1[:, lanes] * o1_ref[0, 0, :, lanes].astype(F32) + a4[:, lanes] * o4s[s]
                + a16[:, lanes] * o16s[s])
        a = lo + s * LANES
        mix[:, a:a + LANES] = (attn * _silu(z[:, a:a + LANES])).astype(BF16)
    mix[:, hi:] = (mem_ref[0].astype(F32) * _silu(z[:, hi:])).astype(BF16)
    y_ref[0] = _gate_project_norm(x_ref[0], mix, wout_ref, gf_ref)


def _prompt_finish(x, cb, u, attn, mem_o, z, conv_w, w_out_bf16, g_final):
    B, T, D = x.shape
    tm = ROW_TILE

    def row(width):
        return pl.BlockSpec((1, tm, width), lambda b, i: (b, i, 0))

    def const(shape):
        return pl.BlockSpec(shape, lambda b, i: (0,) * len(shape))

    prev8 = pl.BlockSpec((1, SUBLANES, CONV_W),
                         lambda b, i: (b, jnp.maximum(i * (tm // SUBLANES) - 1, 0), 0))
    in_specs = [row(D), row(CONV_W), row(CONV_W), prev8]
    for _, dil in DILATED_CONFIGS:
        in_specs += [pl.BlockSpec((1, dil, tm // dil, ATTN_W), lambda b, i: (b, 0, i, 0)),
                     pl.BlockSpec((1, dil, tm // dil, LANES), lambda b, i: (b, 0, i, 0))]
    expander = jnp.asarray(_head_expander(), BF16)
    in_specs += [row(MEM_W), row(D_MIX),
                 const(conv_w.shape), const(w_out_bf16.shape), const((1, D)),
                 const(expander.shape)]
    o_slabs = pltpu.VMEM((ATTN_W // LANES, tm, LANES), F32)
    l_slabs = pltpu.VMEM((1, tm, LANES), F32)
    return pl.pallas_call(
        _finish_kernel,
        grid=(B, T // tm),
        in_specs=in_specs,
        out_specs=row(D),
        out_shape=jax.ShapeDtypeStruct((B, T, D), F32),
        scratch_shapes=[pltpu.VMEM((tm + SUBLANES, CONV_W), F32),
                        pltpu.VMEM((tm, D_MIX), BF16),
                        o_slabs, l_slabs, o_slabs, l_slabs, o_slabs, l_slabs],
        compiler_params=_params(2),
        name="prompt_finish",
    )(x, cb, u, u, *attn, mem_o, z, conv_w, w_out_bf16, g_final.reshape(1, D), expander)


def _sample_tables(S, n_buf):
    j = np.arange(S)[:, None]
    t = np.arange(n_buf)[None, :]
    dist = n_buf + j - t
    count = np.zeros(dist.shape)
    for win, dil in DILATED_CONFIGS:
        count += (dist % dil == 0) & (dist <= win)
    bias = np.where(count > 0, -_slopes()[:, None, None] * dist, -np.inf)
    return count.astype(np.float32), bias.astype(np.float32)


def _sample_attn_kernel(q_ref, kn_ref, vn_ref, kt_ref, vt_ref, mq_ref, mkt_ref, mvt_ref,
                        cnt_ref, bias_ref, pmq_ref, pmk_ref, pmv_ref, o_ref, mo_ref, pmo_ref):
    R, S = q_ref.shape[0], q_ref.shape[1]
    nt = (((1,), (1,)), ((), ()))
    slopes = _slopes()
    n_cfg = float(len(DILATED_CONFIGS))
    rowj = lax.broadcasted_iota(jnp.int32, (S, 1), 0)
    cnt = cnt_ref[...]
    rows = range(R)
    q, kn, vn, mq = ([ref[r] for r in rows] for ref in (q_ref, kn_ref, vn_ref, mq_ref))

    def head(a, h):
        return a[:, h * HEAD_DIM:(h + 1) * HEAD_DIM]

    units = [(r, h) for r in rows for h in range(ATTN_HEADS)]
    munits = [(r, h) for r in rows for h in range(MEM_HEADS)]
    scores = [jnp.dot(head(q[r], h).astype(BF16), kt_ref[r, h].astype(BF16),
                      preferred_element_type=F32) + bias_ref[h] for r, h in units]
    mscores = [jnp.dot(head(mq[r], h).astype(BF16), mkt_ref[r, h].astype(BF16),
                       preferred_element_type=F32) for r, h in munits]
    probs, stats = [], []
    for (r, h), s in zip(units, scores):
        qh, knh = head(q[r], h), head(kn[r], h)
        m = jnp.max(s, axis=-1, keepdims=True)
        s_new = []
        for jp in range(S):
            d = rowj - jp
            sj = jnp.sum(qh * knh[jp:jp + 1], axis=-1, keepdims=True)
            sj = jnp.where(d >= 0, sj - float(slopes[h]) * d.astype(F32), -jnp.inf)
            s_new.append(sj)
            m = jnp.maximum(m, sj)
        p = cnt * jnp.exp(s - m)
        l = jnp.sum(p, axis=-1, keepdims=True)
        w_new = []
        for jp in range(S):
            w = jnp.where(rowj == jp, n_cfg, 1.0) * jnp.exp(s_new[jp] - m)
            l = l + w
            w_new.append(w)
        probs.append(p.astype(BF16))
        stats.append((l, w_new))
    mprobs, mden = [], []
    for s in mscores:
        p = jnp.exp(s - jnp.max(s, axis=-1, keepdims=True))
        mden.append(jnp.sum(p, axis=-1, keepdims=True))
        mprobs.append(p.astype(BF16))
    accs = [lax.dot_general(p, vt_ref[r, h].astype(BF16), nt, preferred_element_type=F32)
            for (r, h), p in zip(units, probs)]
    maccs = [lax.dot_general(p, mvt_ref[r, h].astype(BF16), nt, preferred_element_type=F32)
             for (r, h), p in zip(munits, mprobs)]
    outs = []
    for (r, h), acc, (l, w_new) in zip(units, accs, stats):
        vnh = head(vn[r], h)
        for jp in range(S):
            acc = acc + w_new[jp] * vnh[jp:jp + 1]
        outs.append(acc / l)
    mouts = [acc / den for acc, den in zip(maccs, mden)]
    for r in rows:
        o_ref[r] = jnp.concatenate(outs[r * ATTN_HEADS:(r + 1) * ATTN_HEADS], axis=-1)
        mo_ref[r] = jnp.concatenate(mouts[r * MEM_HEADS:(r + 1) * MEM_HEADS], axis=-1)

    for p in range(MEM_HEADS // 2):
        lanes = slice(p * LANES, (p + 1) * LANES)
        o, _, _ = _pair_attention(pmq_ref[0, :, lanes], pmk_ref[0, :, lanes],
                                  pmv_ref[0, :, lanes], None, None)
        pmo_ref[0, :, lanes] = o.astype(BF16)


def _sample_attention(q, k_new, v_new, cache_kt, cache_vt, mq, mem_kt, mem_vt,
                      prompt_mq, prompt_mk, prompt_mv):
    DB, S, W = q.shape
    n_buf = cache_kt.shape[-1]
    R = SAMPLE_ROWS
    assert DB % R == 0
    steps = DB // R
    B, T, _ = prompt_mq.shape
    assert T % (steps // B) == 0 and steps % B == 0
    blk = T // (steps // B)
    pmq = prompt_mq.reshape(steps, blk, MEM_W)
    assert n_buf >= DILATED_CONFIGS[-1][0] and S <= DILATED_CONFIGS[1][1], \
        "new tokens reach each other through dilation 1 only; the buffer covers every window"
    cnt, bias = (jnp.asarray(t) for t in _sample_tables(S, n_buf))

    def rows(a):
        return pl.BlockSpec((R,) + a.shape[1:], lambda b: (b,) + (0,) * (a.ndim - 1))

    def const(a):
        return pl.BlockSpec(a.shape, lambda b: (0,) * a.ndim)

    steps_per_batch = steps // B
    pm_block = pl.BlockSpec((1, blk, MEM_W), lambda b: (b, 0, 0))
    pm_kv = pl.BlockSpec((1,) + prompt_mk.shape[1:], lambda b: (b // steps_per_batch, 0, 0))
    args = (q, k_new, v_new, cache_kt, cache_vt, mq, mem_kt, mem_vt)
    attn, mem, pmo = pl.pallas_call(
        _sample_attn_kernel,
        grid=(steps,),
        in_specs=([rows(a) for a in args] + [const(cnt), const(bias)]
                  + [pm_block, pm_kv, pm_kv]),
        out_specs=(rows(q), rows(mq), pm_block),
        out_shape=(jax.ShapeDtypeStruct((DB, S, W), F32),
                   jax.ShapeDtypeStruct((DB, S, MEM_W), F32),
                   jax.ShapeDtypeStruct(pmq.shape, BF16)),
        compiler_params=_params(1),
        name="sample_attn",
    )(*args, cnt, bias, pmq, prompt_mk, prompt_mv)
    return attn, mem, pmo.reshape(B, T, MEM_W)


def _sample_finish_kernel(x_ref, cb_ref, u0_ref, u1_ref, u2_ref, attn_ref, mem_ref, z_ref,
                          cw_ref, wout_ref, gf_ref, y_ref, mix):
    cw = cw_ref[...]
    conv = cw[0:1] * u0_ref[...] + cw[1:2] * u1_ref[...] + cw[2:3] * u2_ref[...]
    z = z_ref[...].astype(F32)
    lo, hi = CONV_W, CONV_W + ATTN_W
    mix[:, 0:lo] = (cb_ref[...].astype(F32) * conv * _silu(z[:, 0:lo])).astype(BF16)
    mix[:, lo:hi] = (attn_ref[...] * _silu(z[:, lo:hi])).astype(BF16)
    mix[:, hi:] = (mem_ref[...] * _silu(z[:, hi:])).astype(BF16)
    y_ref[...] = _gate_project_norm(x_ref[...], mix, wout_ref, gf_ref)


def _sample_finish(x, cb, u_taps, attn, mem, z, conv_w, w_out_bf16, g_final):
    rows, D = x.shape
    return pl.pallas_call(
        _sample_finish_kernel,
        out_shape=jax.ShapeDtypeStruct((rows, D), F32),
        scratch_shapes=[pltpu.VMEM((rows, D_MIX), BF16)],
        compiler_params=pltpu.CompilerParams(vmem_limit_bytes=VMEM_LIMIT),
        name="sample_finish",
    )(x, cb, *u_taps, attn, mem, z, conv_w, w_out_bf16, g_final.reshape(1, D))


def kernel(x_prompt, x_sample, mem_prompt, cache_win_k, cache_win_v, cache_conv, cache_mem_k,
           cache_mem_v, g_in, w_in, conv_w, g_mem, w_mem_kv, w_out, g_final):
    assert g_in.shape[0] == 1, "the final RMSNorm is fused into the single layer's tail"
    B, T, D = x_prompt.shape
    DB, S, _ = x_sample.shape
    n_keep = min(DILATED_CONFIGS[-1][0], T)
    w_in_b = w_in[0].astype(BF16)
    w_out_b = w_out[0].astype(BF16)
    dils = tuple(d for _, d in DILATED_CONFIGS)
    wkvt_b = _transposed_columns(w_in[0], _C_K, _C_MQ)
    cb, u, q, k, v, kt32, vt32, mq, z = _project(x_prompt, g_in[0], w_in_b, wkvt_b, n_keep, True,
                                                 dils, Q_SCALE * LOG2E)
    mkt32, mvt32, mk, mv = _mem_kv(mem_prompt, g_mem[0], w_mem_kv[0].astype(BF16))
    attn = _prompt_attention(q, k, v)

    def heads_last(t, heads):
        return jnp.transpose(t.reshape(t.shape[0], heads, HEAD_DIM, -1), (0, 3, 1, 2))[None]

    def heads_first(c):
        return jnp.transpose(c, (0, 2, 3, 1))

    p_win_k = heads_last(kt32, ATTN_HEADS)
    p_win_v = heads_last(vt32, ATTN_HEADS)
    p_conv = u[None, :, -(CONV_WIDTH - 1):]
    p_mem_k = heads_last(mkt32, MEM_HEADS)
    p_mem_v = heads_last(mvt32, MEM_HEADS)
    rows = DB * S
    cb_s, u_s, q_s, _, _, k32, v32, mq_s, z_s = _project(
        x_sample.reshape(1, rows, D), g_in[0], w_in_b, wkvt_b, rows, False, (1,), Q_SCALE)
    k32 = k32.reshape(DB, S, ATTN_W)
    v32 = v32.reshape(DB, S, ATTN_W)
    attn_s, mem_s, mem_p = _sample_attention(
        q_s[0].reshape(DB, S, ATTN_W).astype(F32), k32, v32,
        heads_first(cache_win_k[0]), heads_first(cache_win_v[0]),
        mq_s.reshape(DB, S, MEM_W).astype(F32),
        heads_first(cache_mem_k[0]), heads_first(cache_mem_v[0]),
        mq, mk, mv)
    y_prompt = _prompt_finish(x_prompt, cb, u, attn, mem_p, z, conv_w[0], w_out_b, g_final)
    full = jnp.concatenate([cache_conv[0], u_s.reshape(DB, S, CONV_W)], axis=1)
    taps = [full[:, t:t + S].reshape(rows, CONV_W) for t in range(CONV_WIDTH)]
    y_sample = _sample_finish(x_sample.reshape(rows, D), cb_s.reshape(rows, CONV_W), taps,
                              attn_s.reshape(rows, ATTN_W), mem_s.reshape(rows, MEM_W),
                              z_s.reshape(rows, D_MIX), conv_w[0], w_out_b,
                              g_final).reshape(DB, S, D)
    s_win_k = k32.reshape(1, DB, S, ATTN_HEADS, HEAD_DIM)
    s_win_v = v32.reshape(1, DB, S, ATTN_HEADS, HEAD_DIM)
    s_conv = full[None, :, -(CONV_WIDTH - 1):]
    return (y_prompt, y_sample, p_win_k, p_win_v, p_conv, p_mem_k, p_mem_v,
            s_win_k, s_win_v, s_conv)
```

```python
import functools

import numpy as np
import jax
import jax.numpy as jnp
from jax import lax
from jax.experimental import pallas as pl
from jax.experimental.pallas import tpu as pltpu

F32 = jnp.float32
BF16 = jnp.bfloat16

HEAD_DIM = 64
ATTN_HEADS = 8
MEM_HEADS = 4
CONV_W = 256
ATTN_W = 512
MEM_W = 256
D_MIX = 1024
CONV_WIDTH = 3
DILATED_CONFIGS = ((128, 1), (512, 4), (2048, 16))
SPAN = 128
ALIBI_MAX = 8.0
RMS_EPS = 1e-6
Q_SCALE = HEAD_DIM ** -0.5
LOG2E = float(np.log2(np.e))

LANES = 128
SUBLANES = 8
VMEM_LIMIT = 56 * 1024 * 1024

_C_CB, _C_CC, _C_Q, _C_K, _C_V, _C_MQ, _C_Z, _C_END = 0, 256, 768, 1280, 1792, 2304, 2560, 3584

ROW_TILE = 512
ATTN_TILE = 512
SAMPLE_ROWS = 1


def _slopes():
    h = np.arange(ATTN_HEADS, dtype=np.float64) + 1.0
    return np.exp2(-ALIBI_MAX * h / ATTN_HEADS)


def _params(n_axes):
    return pltpu.CompilerParams(
        dimension_semantics=("arbitrary",) * n_axes,
        vmem_limit_bytes=VMEM_LIMIT)


def _transpose_cast_kernel(w_ref, o_ref):
    o_ref[...] = w_ref[...].T.astype(BF16)


def _transposed_columns(w, lo, hi):
    K = w.shape[0]
    blk = 2 * LANES
    assert lo % blk == 0 and hi % blk == 0
    return pl.pallas_call(
        _transpose_cast_kernel,
        grid=((hi - lo) // blk,),
        in_specs=[pl.BlockSpec((K, blk), lambda i: (0, lo // blk + i))],
        out_specs=pl.BlockSpec((blk, K), lambda i: (i, 0)),
        out_shape=jax.ShapeDtypeStruct((hi - lo, K), BF16),
        compiler_params=_params(1),
        name="wkv_t",
    )(w)


def _fold_store(res, out_refs, slabs, dils):
    rows = res.shape[0]
    out_refs[0][0, 0] = res.astype(BF16)
    if len(dils) == 1:
        return
    n_slab = res.shape[1] // LANES
    src, prev_d = slabs[0], 1
    for s in range(n_slab):
        src[s] = res[:, s * LANES:(s + 1) * LANES]
    for idx, (d, o_ref) in enumerate(zip(dils[1:], out_refs[1:])):
        step, n = d // prev_d, rows // d
        dst = slabs[idx + 1] if idx + 2 < len(dils) else None
        for rp in range(prev_d):
            for q in range(step):
                r = q * prev_d + rp
                for s in range(n_slab):
                    val = src[s, pl.ds(rp * (rows // prev_d) + q, n, stride=step), :]
                    o_ref[0, r, :, s * LANES:(s + 1) * LANES] = val.astype(BF16)
                    if dst is not None:
                        dst[s, pl.ds(r * n, n), :] = val
        src, prev_d = dst, d


def _proj_kernel(x_ref, g_ref, w_ref, wkvt_ref, *refs, first, keep_transposed, dils, q_scale):
    n = len(dils)
    cb_ref, u_ref = refs[0:2]
    q_refs, k_refs, v_refs = refs[2:2 + n], refs[2 + n:2 + 2 * n], refs[2 + 2 * n:2 + 3 * n]
    k32_ref, v32_ref, mq_ref, z_ref = refs[2 + 3 * n:6 + 3 * n]
    scratch = refs[6 + 3 * n:]
    slabs = [scratch[a * (n - 1):(a + 1) * (n - 1)] for a in range(3)]
    x = x_ref[0]
    r = lax.rsqrt(jnp.mean(x * x, axis=-1, keepdims=True) + RMS_EPS)
    h = (x * r * g_ref[...]).astype(BF16)

    def seg(lo, hi):
        return jnp.dot(h, w_ref[:, lo:hi], preferred_element_type=F32)

    cb_ref[0] = seg(_C_CB, _C_CC).astype(BF16)
    cch = seg(_C_CC, _C_Q)
    u_ref[0] = cch[:, :CONV_W] * cch[:, CONV_W:]
    _fold_store(seg(_C_Q, _C_K) * q_scale, q_refs, slabs[0], dils)
    k = seg(_C_K, _C_V)
    _fold_store(k, k_refs, slabs[1], dils)
    v = seg(_C_V, _C_MQ)
    _fold_store(v, v_refs, slabs[2], dils)
    mq_ref[0] = (seg(_C_MQ, _C_Z) * q_scale).astype(BF16)
    z_ref[0] = seg(_C_Z, _C_END).astype(BF16)
    if keep_transposed:
        @pl.when(pl.program_id(1) >= first)
        def _():
            nt = (((1,), (1,)), ((), ()))
            k32_ref[0] = lax.dot_general(wkvt_ref[0:ATTN_W], h, nt, preferred_element_type=F32)
            v32_ref[0] = lax.dot_general(wkvt_ref[ATTN_W:], h, nt, preferred_element_type=F32)
    else:
        k32_ref[0] = k
        v32_ref[0] = v


def _project(x, g, w_bf16, wkvt_bf16, n_keep, keep_transposed, dils, q_scale):
    B, T, D = x.shape
    tm = ROW_TILE
    nt = T // tm
    first = (T - n_keep) // tm
    n = len(dils)

    def row(width):
        return pl.BlockSpec((1, tm, width), lambda b, i: (b, i, 0))

    folded_specs = [pl.BlockSpec((1, d, tm // d, ATTN_W), lambda b, i: (b, 0, i, 0)) for d in dils]
    folded_shapes = [jax.ShapeDtypeStruct((B, d, T // d, ATTN_W), BF16) for d in dils]

    if keep_transposed:
        keep = pl.BlockSpec((1, ATTN_W, tm), lambda b, i: (b, 0, jnp.maximum(i - first, 0)))
        keep_shape = (B, ATTN_W, n_keep)
    else:
        keep = pl.BlockSpec((1, tm, ATTN_W), lambda b, i: (b, jnp.maximum(i - first, 0), 0))
        keep_shape = (B, n_keep, ATTN_W)
    out_shape = (
        [jax.ShapeDtypeStruct((B, T, CONV_W), BF16),
         jax.ShapeDtypeStruct((B, T, CONV_W), F32)]
        + folded_shapes * 3
        + [jax.ShapeDtypeStruct(keep_shape, F32),
           jax.ShapeDtypeStruct(keep_shape, F32),
           jax.ShapeDtypeStruct((B, T, MEM_W), BF16),
           jax.ShapeDtypeStruct((B, T, D_MIX), BF16)])
    out_specs = ([row(CONV_W), row(CONV_W)] + folded_specs * 3
                 + [keep, keep, row(MEM_W), row(D_MIX)])
    scratch = [pltpu.VMEM((ATTN_W // LANES, tm, LANES), F32)] * (3 * (n - 1))
    res = pl.pallas_call(
        functools.partial(_proj_kernel, first=first, keep_transposed=keep_transposed, dils=dils,
                          q_scale=q_scale),
        grid=(B, nt),
        in_specs=[row(D),
                  pl.BlockSpec((1, D), lambda b, i: (0, 0)),
                  pl.BlockSpec(w_bf16.shape, lambda b, i: (0, 0)),
                  pl.BlockSpec(wkvt_bf16.shape, lambda b, i: (0, 0))],
        out_specs=tuple(out_specs),
        out_shape=tuple(out_shape),
        scratch_shapes=scratch,
        compiler_params=_params(2),
        name="proj",
    )(x, g.reshape(1, D), w_bf16, wkvt_bf16)
    cb, u = res[0:2]
    q, k, v = res[2:2 + n], res[2 + n:2 + 2 * n], res[2 + 2 * n:2 + 3 * n]
    k32, v32, mq, z = res[2 + 3 * n:]
    return cb, u, q, k, v, k32, v32, mq, z


def _memkv_kernel(x_ref, g_ref, w_ref, k32_ref, v32_ref, k_ref, v_ref):
    x = x_ref[...]
    r = lax.rsqrt(jnp.mean(x * x, axis=-1, keepdims=True) + RMS_EPS)
    h = (x * r * g_ref[...]).astype(BF16)
    kv = jnp.dot(h, w_ref[...], preferred_element_type=F32)
    k_ref[...] = kv[:, :MEM_W].astype(BF16)
    v_ref[...] = kv[:, MEM_W:].astype(BF16)
    n_mem = k32_ref.shape[2]
    for b in range(k32_ref.shape[0]):
        kvt = kv[b * n_mem:(b + 1) * n_mem].T
        k32_ref[b] = kvt[:MEM_W]
        v32_ref[b] = kvt[MEM_W:]


def _mem_kv(mem, g, w_bf16):
    B, M, D = mem.shape
    rows = B * M
    k32, v32, k, v = pl.pallas_call(
        _memkv_kernel,
        out_shape=(jax.ShapeDtypeStruct((B, MEM_W, M), F32),
                   jax.ShapeDtypeStruct((B, MEM_W, M), F32),
                   jax.ShapeDtypeStruct((rows, MEM_W), BF16),
                   jax.ShapeDtypeStruct((rows, MEM_W), BF16)),
        compiler_params=pltpu.CompilerParams(vmem_limit_bytes=VMEM_LIMIT),
        name="memkv",
    )(mem.reshape(rows, D), g.reshape(1, D), w_bf16)
    return k32, v32, k.reshape(B, M, MEM_W), v.reshape(B, M, MEM_W)


def _band_bias_table():
    qi = np.arange(SPAN)[:, None] + SPAN
    ki = np.arange(2 * SPAN)[None, :]
    dist = qi - ki
    valid = (dist >= 0) & (dist <= SPAN)
    slopes = _slopes()
    tabs = []
    for _, dil in DILATED_CONFIGS:
        for first in (False, True):
            ok = valid & (ki >= SPAN) if first else valid
            for h in range(ATTN_HEADS):
                bias = -(slopes[h] * dil) * dist * LOG2E
                tabs.append(np.where(ok, bias, -np.inf))
    return np.stack(tabs).astype(np.float32)


def _pair_attention(qb, kb, vb, bias_lo, bias_hi):
    lane = lax.broadcasted_iota(jnp.int32, qb.shape, 1)
    lo = lane < HEAD_DIM
    zero = jnp.zeros_like(qb)
    v_ones = jnp.concatenate([vb, jnp.ones_like(vb)], axis=1)
    accs, ms, ls = [], [], []
    for sel, bias in ((lo, bias_lo), (~lo, bias_hi)):
        qm = jnp.where(sel, qb, zero)
        s = lax.dot_general(qm, kb, (((1,), (1,)), ((), ())), preferred_element_type=F32)
        if bias is not None:
            s = s + bias
        m = jnp.max(s, axis=-1, keepdims=True)
        p = jnp.exp2(s - m)
        acc = jnp.dot(p.astype(BF16), v_ones, preferred_element_type=F32)
        ms.append(m)
        accs.append(acc[:, :LANES])
        ls.append(acc[:, LANES:])
    o = jnp.where(lo, accs[0], accs[1]) / jnp.where(lo, ls[0], ls[1])
    return o, ms, ls


def _attn_kernel(*refs, steps):
    ins = refs[:15]
    tab_ref = refs[15]
    outs = refs[16:22]
    kcat, vcat = refs[22], refs[23]
    s = pl.program_id(1)
    nblk = ATTN_TILE // SPAN
    for c in range(3):
        _, kp, ko, vp, vo = ins[5 * c:5 * c + 5]
        kcat[c, 0:SPAN] = kp[0, 0]
        kcat[c, SPAN:] = ko[0, 0]
        vcat[c, 0:SPAN] = vp[0, 0]
        vcat[c, SPAN:] = vo[0, 0]

    def body(j, carry):
        row0 = pl.multiple_of(j * SPAN, SPAN)
        for c, (_, dil) in enumerate(DILATED_CONFIGS):
            q_ref = ins[5 * c]
            o_ref, l_ref = outs[2 * c], outs[2 * c + 1]
            units_per_residue = steps // dil
            first = jnp.logical_and(s % units_per_residue == 0, j == 0).astype(jnp.int32)
            base = (c * 2 + first) * ATTN_HEADS
            lane = lax.broadcasted_iota(jnp.int32, (SPAN, LANES), 1)
            m8 = jnp.zeros((SPAN, LANES), F32)
            l8 = jnp.ones((SPAN, LANES), F32)
            for p in range(ATTN_HEADS // 2):
                lanes = slice(p * LANES, (p + 1) * LANES)
                qb = q_ref[0, 0, pl.ds(row0, SPAN), lanes]
                kb = kcat[c, pl.ds(row0, 2 * SPAN), lanes]
                vb = vcat[c, pl.ds(row0, 2 * SPAN), lanes]
                o, ms, ls = _pair_attention(qb, kb, vb, tab_ref[base + 2 * p],
                                            tab_ref[base + 2 * p + 1])
                o_ref[0, 0, pl.ds(row0, SPAN), lanes] = o.astype(BF16)
                for half in range(2):
                    m8 = jnp.where(lane == 2 * p + half, ms[half], m8)
                    l8 = jnp.where(lane == 2 * p + half, ls[half], l8)
            l_ref[0, 0, pl.ds(row0, SPAN), :] = m8 + jnp.log2(l8)
        return carry

    lax.fori_loop(0, nblk, body, 0, unroll=True)


def _prompt_attention(q, k, v):
    B, _, T, W = q[0].shape
    steps = T // ATTN_TILE
    prev_per_tile = ATTN_TILE // SPAN
    args, in_specs, out_specs, out_shape = [], [], [], []
    for c, (_, dil) in enumerate(DILATED_CONFIGS):
        L = T // dil
        upr = L // ATTN_TILE

        def own(b, s, upr=upr):
            return (b, s // upr, s % upr, 0)

        def prev(b, s, upr=upr):
            return (b, s // upr, jnp.maximum((s % upr) * prev_per_tile - 1, 0), 0)

        shape = (B, dil, L, W)
        assert q[c].shape == shape
        args += [q[c], k[c], k[c], v[c], v[c]]
        in_specs += [pl.BlockSpec((1, 1, ATTN_TILE, W), own),
                     pl.BlockSpec((1, 1, SPAN, W), prev),
                     pl.BlockSpec((1, 1, ATTN_TILE, W), own),
                     pl.BlockSpec((1, 1, SPAN, W), prev),
                     pl.BlockSpec((1, 1, ATTN_TILE, W), own)]
        out_specs += [pl.BlockSpec((1, 1, ATTN_TILE, W), own),
                      pl.BlockSpec((1, 1, ATTN_TILE, LANES), own)]
        out_shape += [jax.ShapeDtypeStruct(shape, BF16),
                      jax.ShapeDtypeStruct((B, dil, L, LANES), F32)]
    tab = jnp.asarray(_band_bias_table())
    args.append(tab)
    in_specs.append(pl.BlockSpec(tab.shape, lambda b, s: (0, 0, 0)))
    res = pl.pallas_call(
        functools.partial(_attn_kernel, steps=steps),
        grid=(B, steps),
        in_specs=in_specs,
        out_specs=tuple(out_specs),
        out_shape=tuple(out_shape),
        scratch_shapes=[pltpu.VMEM((3, ATTN_TILE + SPAN, W), BF16)] * 2,
        compiler_params=_params(2),
        name="prompt_attn",
    )(*args)
    return list(res)


def _silu(z):
    return z / (1.0 + jnp.exp(-z))


def _gate_project_norm(x, mix_ref, wout_ref, gf_ref):
    y = x + jnp.dot(mix_ref[...], wout_ref[...], preferred_element_type=F32)
    r = lax.rsqrt(jnp.mean(y * y, axis=-1, keepdims=True) + RMS_EPS)
    return y * r * gf_ref[...]


def _unfold(src_ref, slabs, d, tmp=None, step=4):
    n_slab, rows = slabs.shape[0], slabs.shape[1]

    def piece(r, s):
        return src_ref[0, r, :, s * LANES:(s + 1) * LANES].astype(F32)

    if d <= step:
        for r in range(d):
            for s in range(n_slab):
                slabs[s, pl.ds(r, rows // d, stride=d), :] = piece(r, s)
        return
    prev_d, n = d // step, rows // d
    per = rows // prev_d
    for rp in range(prev_d):
        for q in range(step):
            for s in range(n_slab):
                tmp[s, pl.ds(rp * per + q, n, stride=step), :] = piece(q * prev_d + rp, s)
    for rp in range(prev_d):
        for s in range(n_slab):
            slabs[s, pl.ds(rp, per, stride=prev_d), :] = tmp[s, pl.ds(rp * per, per), :]


def _head_expander():
    e = np.zeros((2 * LANES, ATTN_W), np.float32)
    for h in range(ATTN_HEADS):
        e[h, h * HEAD_DIM:(h + 1) * HEAD_DIM] = 1.0
        e[LANES + h, h * HEAD_DIM:(h + 1) * HEAD_DIM] = 1.0
    return e


def _finish_kernel(x_ref, cb_ref, u_ref, up_ref, o1_ref, l1_ref, o4_ref, l4_ref, o16_ref, l16_ref,
                   mq_ref, z_ref, mk_ref, mv_ref, cw_ref, wout_ref, gf_ref, ex_ref, y_ref,
                   ucat, mix, o4s, l4s, o16s, l16s, otmp, ltmp):
    i = pl.program_id(1)
    tm = ROW_TILE
    ucat[0:SUBLANES] = jnp.where(i == 0, 0.0, up_ref[0])
    ucat[SUBLANES:] = u_ref[0]
    cw = cw_ref[...]
    conv = (cw[0:1] * ucat[pl.ds(SUBLANES - 2, tm)]
            + cw[1:2] * ucat[pl.ds(SUBLANES - 1, tm)]
            + cw[2:3] * ucat[pl.ds(SUBLANES, tm)])
    conv_y = cb_ref[0].astype(F32) * conv
    z = z_ref[0].astype(F32)
    mix[:, 0:CONV_W] = (conv_y * _silu(z[:, 0:CONV_W])).astype(BF16)
    dil4, dil16 = DILATED_CONFIGS[1][1], DILATED_CONFIGS[2][1]
    _unfold(o4_ref, o4s, dil4)
    _unfold(l4_ref, l4s, dil4)
    _unfold(o16_ref, o16s, dil16, otmp)
    _unfold(l16_ref, l16s, dil16, ltmp)
    lo, hi = CONV_W, CONV_W + ATTN_W
    l1, l4, l16 = l1_ref[0, 0], l4s[0], l16s[0]
    mx = jnp.maximum(jnp.maximum(l1, l4), l16)
    e1, e4, e16 = jnp.exp2(l1 - mx), jnp.exp2(l4 - mx), jnp.exp2(l16 - mx)
    den = e1 + e4 + e16

    def spread(w):
        w_hi = w.astype(BF16)
        w_lo = (w - w_hi.astype(F32)).astype(BF16)
        return jnp.dot(jnp.concatenate([w_hi, w_lo], axis=1), ex_ref[...],
                       preferred_element_type=F32)

    a1, a4, a16 = spread(e1 / den), spread(e4 / den), spread(e16 / den)
    for s in range(ATTN_W // LANES):
        lanes = slice(s * LANES, (s + 1) * LANES)
        attn = (a1[:, lanes] * o1_ref[0, 0, :, lanes].astype(F32) + a4[:, lanes] * o4s[s]
                + a16[:, lanes] * o16s[s])
        a = lo + s * LANES
        mix[:, a:a + LANES] = (attn * _silu(z[:, a:a + LANES])).astype(BF16)
    for p in range(MEM_HEADS // 2):
        lanes = slice(p * LANES, (p + 1) * LANES)
        o, _, _ = _pair_attention(mq_ref[0, :, lanes], mk_ref[0, :, lanes], mv_ref[0, :, lanes],
                                  None, None)
        a = hi + p * LANES
        mix[:, a:a + LANES] = (o * _silu(z[:, a:a + LANES])).astype(BF16)
    y_ref[0] = _gate_project_norm(x_ref[0], mix, wout_ref, gf_ref)


def _prompt_finish(x, cb, u, attn, mq, z, mk, mv, conv_w, w_out_bf16, g_final):
    B, T, D = x.shape
    tm = ROW_TILE

    def row(width):
        return pl.BlockSpec((1, tm, width), lambda b, i: (b, i, 0))

    def const(shape):
        return pl.BlockSpec(shape, lambda b, i: (0,) * len(shape))

    prev8 = pl.BlockSpec((1, SUBLANES, CONV_W),
                         lambda b, i: (b, jnp.maximum(i * (tm // SUBLANES) - 1, 0), 0))
    memspec = pl.BlockSpec((1,) + mk.shape[1:], lambda b, i: (b, 0, 0))
    in_specs = [row(D), row(CONV_W), row(CONV_W), prev8]
    for _, dil in DILATED_CONFIGS:
        in_specs += [pl.BlockSpec((1, dil, tm // dil, ATTN_W), lambda b, i: (b, 0, i, 0)),
                     pl.BlockSpec((1, dil, tm // dil, LANES), lambda b, i: (b, 0, i, 0))]
    expander = jnp.asarray(_head_expander(), BF16)
    in_specs += [row(MEM_W), row(D_MIX), memspec, memspec,
                 const(conv_w.shape), const(w_out_bf16.shape), const((1, D)),
                 const(expander.shape)]
    o_slabs = pltpu.VMEM((ATTN_W // LANES, tm, LANES), F32)
    l_slabs = pltpu.VMEM((1, tm, LANES), F32)
    return pl.pallas_call(
        _finish_kernel,
        grid=(B, T // tm),
        in_specs=in_specs,
        out_specs=row(D),
        out_shape=jax.ShapeDtypeStruct((B, T, D), F32),
        scratch_shapes=[pltpu.VMEM((tm + SUBLANES, CONV_W), F32),
                        pltpu.VMEM((tm, D_MIX), BF16),
                        o_slabs, l_slabs, o_slabs, l_slabs, o_slabs, l_slabs],
        compiler_params=_params(2),
        name="prompt_finish",
    )(x, cb, u, u, *attn, mq, z, mk, mv, conv_w, w_out_bf16, g_final.reshape(1, D), expander)


def _sample_tables(S, n_buf):
    j = np.arange(S)[:, None]
    t = np.arange(n_buf)[None, :]
    dist = n_buf + j - t
    count = np.zeros(dist.shape)
    for win, dil in DILATED_CONFIGS:
        count += (dist % dil == 0) & (dist <= win)
    bias = np.where(count > 0, -_slopes()[:, None, None] * dist, -np.inf)
    bias = bias.reshape(ATTN_HEADS // 2, 2 * S, n_buf)
    count = np.concatenate([count, count], axis=0)
    return count.astype(np.float32), bias.astype(np.float32)


def _stack_pair(a, p):
    ap = a[:, p * LANES:(p + 1) * LANES]
    lo = lax.broadcasted_iota(jnp.int32, ap.shape, 1) < HEAD_DIM
    zero = jnp.zeros_like(ap)
    return jnp.concatenate([jnp.where(lo, ap, zero), jnp.where(lo, zero, ap)], axis=0)


def _unstack_pair(o, S):
    lo = lax.broadcasted_iota(jnp.int32, (S, LANES), 1) < HEAD_DIM
    return jnp.where(lo, o[:S], o[S:])


def _sample_attn_kernel(q_ref, kn_ref, vn_ref, kt_ref, vt_ref, mq_ref, mkt_ref, mvt_ref,
                        cnt_ref, bias_ref, o_ref, mo_ref):
    R, S = q_ref.shape[0], q_ref.shape[1]
    nt = (((1,), (1,)), ((), ()))
    slopes = _slopes()
    n_cfg = float(len(DILATED_CONFIGS))
    row = lax.broadcasted_iota(jnp.int32, (2 * S, 1), 0)
    upper = row >= S
    rowj = jnp.where(upper, row - S, row)
    cnt = cnt_ref[...]
    rows = range(R)

    def lane_group(ref, r, p):
        return ref[r][:, p * LANES:(p + 1) * LANES]

    def pair_rows(ref, r, p):
        return ref[r, 2 * p:2 * p + 2].reshape(2 * HEAD_DIM, ref.shape[-1]).astype(BF16)

    units = [(r, p) for r in rows for p in range(ATTN_HEADS // 2)]
    munits = [(r, p) for r in rows for p in range(MEM_HEADS // 2)]
    qs = [_stack_pair(q_ref[r], p) for r, p in units]
    scores = [jnp.dot(q8.astype(BF16), pair_rows(kt_ref, r, p),
                      preferred_element_type=F32) + bias_ref[p] for (r, p), q8 in zip(units, qs)]
    mscores = [jnp.dot(_stack_pair(mq_ref[r], p).astype(BF16), pair_rows(mkt_ref, r, p),
                       preferred_element_type=F32) for r, p in munits]
    probs, stats = [], []
    for (r, p), q8, s in zip(units, qs, scores):
        knp = lane_group(kn_ref, r, p)
        slope = jnp.where(upper, float(slopes[2 * p + 1]), float(slopes[2 * p]))
        m = jnp.max(s, axis=-1, keepdims=True)
        s_new = []
        for jp in range(S):
            d = rowj - jp
            sj = jnp.sum(q8 * knp[jp:jp + 1], axis=-1, keepdims=True)
            sj = jnp.where(d >= 0, sj - slope * d.astype(F32), -jnp.inf)
            s_new.append(sj)
            m = jnp.maximum(m, sj)
        pr = cnt * jnp.exp(s - m)
        l = jnp.sum(pr, axis=-1, keepdims=True)
        w_new = []
        for jp in range(S):
            w = jnp.where(rowj == jp, n_cfg, 1.0) * jnp.exp(s_new[jp] - m)
            l = l + w
            w_new.append(w)
        probs.append(pr.astype(BF16))
        stats.append((l, w_new))
    mprobs, mden = [], []
    for s in mscores:
        pr = jnp.exp(s - jnp.max(s, axis=-1, keepdims=True))
        mden.append(jnp.sum(pr, axis=-1, keepdims=True))
        mprobs.append(pr.astype(BF16))
    accs = [lax.dot_general(pr, pair_rows(vt_ref, r, p), nt, preferred_element_type=F32)
            for (r, p), pr in zip(units, probs)]
    maccs = [lax.dot_general(pr, pair_rows(mvt_ref, r, p), nt, preferred_element_type=F32)
             for (r, p), pr in zip(munits, mprobs)]
    outs = []
    for (r, p), acc, (l, w_new) in zip(units, accs, stats):
        vnp = lane_group(vn_ref, r, p)
        for jp in range(S):
            acc = acc + w_new[jp] * vnp[jp:jp + 1]
        outs.append(_unstack_pair(acc / l, S))
    mouts = [_unstack_pair(acc / den, S) for acc, den in zip(maccs, mden)]
    n_p, n_mp = ATTN_HEADS // 2, MEM_HEADS // 2
    for r in rows:
        o_ref[r] = jnp.concatenate(outs[r * n_p:(r + 1) * n_p], axis=-1)
        mo_ref[r] = jnp.concatenate(mouts[r * n_mp:(r + 1) * n_mp], axis=-1)


def _sample_attention(q, k_new, v_new, cache_kt, cache_vt, mq, mem_kt, mem_vt):
    DB, S, W = q.shape
    n_buf = cache_kt.shape[-1]
    R = SAMPLE_ROWS
    assert DB % R == 0
    assert n_buf >= DILATED_CONFIGS[-1][0] and S <= DILATED_CONFIGS[1][1], \
        "new tokens reach each other through dilation 1 only; the buffer covers every window"
    cnt, bias = (jnp.asarray(t) for t in _sample_tables(S, n_buf))

    def rows(a):
        return pl.BlockSpec((R,) + a.shape[1:], lambda b: (b,) + (0,) * (a.ndim - 1))

    def const(a):
        return pl.BlockSpec(a.shape, lambda b: (0,) * a.ndim)

    args = (q, k_new, v_new, cache_kt, cache_vt, mq, mem_kt, mem_vt)
    return pl.pallas_call(
        _sample_attn_kernel,
        grid=(DB // R,),
        in_specs=[rows(a) for a in args] + [const(cnt), const(bias)],
        out_specs=(rows(q), rows(mq)),
        out_shape=(jax.ShapeDtypeStruct((DB, S, W), F32),
                   jax.ShapeDtypeStruct((DB, S, MEM_W), F32)),
        compiler_params=_params(1),
        name="sample_attn",
    )(*args, cnt, bias)


def _sample_finish_kernel(x_ref, cb_ref, u0_ref, u1_ref, u2_ref, attn_ref, mem_ref, z_ref,
                          cw_ref, wout_ref, gf_ref, y_ref, mix):
    cw = cw_ref[...]
    conv = cw[0:1] * u0_ref[...] + cw[1:2] * u1_ref[...] + cw[2:3] * u2_ref[...]
    z = z_ref[...].astype(F32)
    lo, hi = CONV_W, CONV_W + ATTN_W
    mix[:, 0:lo] = (cb_ref[...].astype(F32) * conv * _silu(z[:, 0:lo])).astype(BF16)
    mix[:, lo:hi] = (attn_ref[...] * _silu(z[:, lo:hi])).astype(BF16)
    mix[:, hi:] = (mem_ref[...] * _silu(z[:, hi:])).astype(BF16)
    y_ref[...] = _gate_project_norm(x_ref[...], mix, wout_ref, gf_ref)


def _sample_finish(x, cb, u_taps, attn, mem, z, conv_w, w_out_bf16, g_final):
    rows, D = x.shape
    return pl.pallas_call(
        _sample_finish_kernel,
        out_shape=jax.ShapeDtypeStruct((rows, D), F32),
        scratch_shapes=[pltpu.VMEM((rows, D_MIX), BF16)],
        compiler_params=pltpu.CompilerParams(vmem_limit_bytes=VMEM_LIMIT),
        name="sample_finish",
    )(x, cb, *u_taps, attn, mem, z, conv_w, w_out_bf16, g_final.reshape(1, D))


def kernel(x_prompt, x_sample, mem_prompt, cache_win_k, cache_win_v, cache_conv, cache_mem_k,
           cache_mem_v, g_in, w_in, conv_w, g_mem, w_mem_kv, w_out, g_final):
    assert g_in.shape[0] == 1, "the final RMSNorm is fused into the single layer's tail"
    B, T, D = x_prompt.shape
    DB, S, _ = x_sample.shape
    n_keep = min(DILATED_CONFIGS[-1][0], T)
    w_in_b = w_in[0].astype(BF16)
    w_out_b = w_out[0].astype(BF16)
    dils = tuple(d for _, d in DILATED_CONFIGS)
    wkvt_b = _transposed_columns(w_in[0], _C_K, _C_MQ)
    cb, u, q, k, v, kt32, vt32, mq, z = _project(x_prompt, g_in[0], w_in_b, wkvt_b, n_keep, True,
                                                 dils, Q_SCALE * LOG2E)
    mkt32, mvt32, mk, mv = _mem_kv(mem_prompt, g_mem[0], w_mem_kv[0].astype(BF16))
    attn = _prompt_attention(q, k, v)

    def heads_last(t, heads):
        return jnp.transpose(t.reshape(t.shape[0], heads, HEAD_DIM, -1), (0, 3, 1, 2))[None]

    def heads_first(c):
        return jnp.transpose(c, (0, 2, 3, 1))

    p_win_k = heads_last(kt32, ATTN_HEADS)
    p_win_v = heads_last(vt32, ATTN_HEADS)
    p_conv = u[None, :, -(CONV_WIDTH - 1):]
    p_mem_k = heads_last(mkt32, MEM_HEADS)
    p_mem_v = heads_last(mvt32, MEM_HEADS)
    rows = DB * S
    cb_s, u_s, q_s, _, _, k32, v32, mq_s, z_s = _project(
        x_sample.reshape(1, rows, D), g_in[0], w_in_b, wkvt_b, rows, False, (1,), Q_SCALE)
    k32 = k32.reshape(DB, S, ATTN_W)
    v32 = v32.reshape(DB, S, ATTN_W)
    attn_s, mem_s = _sample_attention(
        q_s[0].reshape(DB, S, ATTN_W).astype(F32), k32, v32,
        heads_first(cache_win_k[0]), heads_first(cache_win_v[0]),
        mq_s.reshape(DB, S, MEM_W).astype(F32),
        heads_first(cache_mem_k[0]), heads_first(cache_mem_v[0]))
    y_prompt = _prompt_finish(x_prompt, cb, u, attn, mq, z, mk, mv, conv_w[0], w_out_b, g_final)
    full = jnp.concatenate([cache_conv[0], u_s.reshape(DB, S, CONV_W)], axis=1)
    taps = [full[:, t:t + S].reshape(rows, CONV_W) for t in range(CONV_WIDTH)]
    y_sample = _sample_finish(x_sample.reshape(rows, D), cb_s.reshape(rows, CONV_W), taps,
                              attn_s.reshape(rows, ATTN_W), mem_s.reshape(rows, MEM_W),
                              z_s.reshape(rows, D_MIX), conv_w[0], w_out_b,
                              g_final).reshape(DB, S, D)
    s_win_k = k32.reshape(1, DB, S, ATTN_HEADS, HEAD_DIM)
    s_win_v = v32.reshape(1, DB, S, ATTN_HEADS, HEAD_DIM)
    s_conv = full[None, :, -(CONV_WIDTH - 1):]
    return (y_prompt, y_sample, p_win_k, p_win_v, p_conv, p_mem_k, p_mem_v,
            s_win_k, s_win_v, s_conv)
```

```python
import functools

import numpy as np
import jax
import jax.numpy as jnp
from jax import lax
from jax.experimental import pallas as pl
from jax.experimental.pallas import tpu as pltpu

F32 = jnp.float32
BF16 = jnp.bfloat16

HEAD_DIM = 64
ATTN_HEADS = 8
MEM_HEADS = 4
CONV_W = 256
ATTN_W = 512
MEM_W = 256
D_MIX = 1024
CONV_WIDTH = 3
DILATED_CONFIGS = ((128, 1), (512, 4), (2048, 16))
SPAN = 128
ALIBI_MAX = 8.0
RMS_EPS = 1e-6
Q_SCALE = HEAD_DIM ** -0.5
LOG2E = float(np.log2(np.e))

LANES = 128
SUBLANES = 8
VMEM_LIMIT = 56 * 1024 * 1024

_C_CB, _C_CC, _C_Q, _C_K, _C_V, _C_MQ, _C_Z, _C_END = 0, 256, 768, 1280, 1792, 2304, 2560, 3584

ROW_TILE = 512
ATTN_TILE = 512
SAMPLE_ROWS = 1


def _slopes():
    h = np.arange(ATTN_HEADS, dtype=np.float64) + 1.0
    return np.exp2(-ALIBI_MAX * h / ATTN_HEADS)


def _params(n_axes):
    return pltpu.CompilerParams(
        dimension_semantics=("arbitrary",) * n_axes,
        vmem_limit_bytes=VMEM_LIMIT)


def _transpose_cast_kernel(w_ref, o_ref):
    o_ref[...] = w_ref[...].T.astype(BF16)


def _transposed_columns(w, lo, hi):
    K = w.shape[0]
    blk = 2 * LANES
    assert lo % blk == 0 and hi % blk == 0
    return pl.pallas_call(
        _transpose_cast_kernel,
        grid=((hi - lo) // blk,),
        in_specs=[pl.BlockSpec((K, blk), lambda i: (0, lo // blk + i))],
        out_specs=pl.BlockSpec((blk, K), lambda i: (i, 0)),
        out_shape=jax.ShapeDtypeStruct((hi - lo, K), BF16),
        compiler_params=_params(1),
        name="wkv_t",
    )(w)


def _fold_store(res, out_refs, slabs, dils):
    rows = res.shape[0]
    out_refs[0][0, 0] = res.astype(BF16)
    if len(dils) == 1:
        return
    n_slab = res.shape[1] // LANES
    src, prev_d = slabs[0], 1
    for s in range(n_slab):
        src[s] = res[:, s * LANES:(s + 1) * LANES]
    for idx, (d, o_ref) in enumerate(zip(dils[1:], out_refs[1:])):
        step, n = d // prev_d, rows // d
        dst = slabs[idx + 1] if idx + 2 < len(dils) else None
        for rp in range(prev_d):
            for q in range(step):
                r = q * prev_d + rp
                for s in range(n_slab):
                    val = src[s, pl.ds(rp * (rows // prev_d) + q, n, stride=step), :]
                    o_ref[0, r, :, s * LANES:(s + 1) * LANES] = val.astype(BF16)
                    if dst is not None:
                        dst[s, pl.ds(r * n, n), :] = val
        src, prev_d = dst, d


def _proj_kernel(x_ref, g_ref, w_ref, wkvt_ref, *refs, first, keep_transposed, dils, q_scale):
    n = len(dils)
    cb_ref, u_ref = refs[0:2]
    q_refs, k_refs, v_refs = refs[2:2 + n], refs[2 + n:2 + 2 * n], refs[2 + 2 * n:2 + 3 * n]
    k32_ref, v32_ref, mq_ref, z_ref = refs[2 + 3 * n:6 + 3 * n]
    scratch = refs[6 + 3 * n:]
    slabs = [scratch[a * (n - 1):(a + 1) * (n - 1)] for a in range(3)]
    x = x_ref[0]
    r = lax.rsqrt(jnp.mean(x * x, axis=-1, keepdims=True) + RMS_EPS)
    h = (x * r * g_ref[...]).astype(BF16)

    def seg(lo, hi):
        return jnp.dot(h, w_ref[:, lo:hi], preferred_element_type=F32)

    cb_ref[0] = seg(_C_CB, _C_CC).astype(BF16)
    cch = seg(_C_CC, _C_Q)
    u_ref[0] = cch[:, :CONV_W] * cch[:, CONV_W:]
    _fold_store(seg(_C_Q, _C_K) * q_scale, q_refs, slabs[0], dils)
    k = seg(_C_K, _C_V)
    _fold_store(k, k_refs, slabs[1], dils)
    v = seg(_C_V, _C_MQ)
    _fold_store(v, v_refs, slabs[2], dils)
    mq_ref[0] = (seg(_C_MQ, _C_Z) * q_scale).astype(BF16)
    z_ref[0] = seg(_C_Z, _C_END).astype(BF16)
    if keep_transposed:
        @pl.when(pl.program_id(1) >= first)
        def _():
            nt = (((1,), (1,)), ((), ()))
            k32_ref[0] = lax.dot_general(wkvt_ref[0:ATTN_W], h, nt, preferred_element_type=F32)
            v32_ref[0] = lax.dot_general(wkvt_ref[ATTN_W:], h, nt, preferred_element_type=F32)
    else:
        k32_ref[0] = k
        v32_ref[0] = v


def _project(x, g, w_bf16, wkvt_bf16, n_keep, keep_transposed, dils, q_scale):
    B, T, D = x.shape
    tm = ROW_TILE
    nt = T // tm
    first = (T - n_keep) // tm
    n = len(dils)

    def row(width):
        return pl.BlockSpec((1, tm, width), lambda b, i: (b, i, 0))

    folded_specs = [pl.BlockSpec((1, d, tm // d, ATTN_W), lambda b, i: (b, 0, i, 0)) for d in dils]
    folded_shapes = [jax.ShapeDtypeStruct((B, d, T // d, ATTN_W), BF16) for d in dils]

    if keep_transposed:
        keep = pl.BlockSpec((1, ATTN_W, tm), lambda b, i: (b, 0, jnp.maximum(i - first, 0)))
        keep_shape = (B, ATTN_W, n_keep)
    else:
        keep = pl.BlockSpec((1, tm, ATTN_W), lambda b, i: (b, jnp.maximum(i - first, 0), 0))
        keep_shape = (B, n_keep, ATTN_W)
    out_shape = (
        [jax.ShapeDtypeStruct((B, T, CONV_W), BF16),
         jax.ShapeDtypeStruct((B, T, CONV_W), F32)]
        + folded_shapes * 3
        + [jax.ShapeDtypeStruct(keep_shape, F32),
           jax.ShapeDtypeStruct(keep_shape, F32),
           jax.ShapeDtypeStruct((B, T, MEM_W), BF16),
           jax.ShapeDtypeStruct((B, T, D_MIX), BF16)])
    out_specs = ([row(CONV_W), row(CONV_W)] + folded_specs * 3
                 + [keep, keep, row(MEM_W), row(D_MIX)])
    scratch = [pltpu.VMEM((ATTN_W // LANES, tm, LANES), F32)] * (3 * (n - 1))
    res = pl.pallas_call(
        functools.partial(_proj_kernel, first=first, keep_transposed=keep_transposed, dils=dils,
                          q_scale=q_scale),
        grid=(B, nt),
        in_specs=[row(D),
                  pl.BlockSpec((1, D), lambda b, i: (0, 0)),
                  pl.BlockSpec(w_bf16.shape, lambda b, i: (0, 0)),
                  pl.BlockSpec(wkvt_bf16.shape, lambda b, i: (0, 0))],
        out_specs=tuple(out_specs),
        out_shape=tuple(out_shape),
        scratch_shapes=scratch,
        compiler_params=_params(2),
        name="proj",
    )(x, g.reshape(1, D), w_bf16, wkvt_bf16)
    cb, u = res[0:2]
    q, k, v = res[2:2 + n], res[2 + n:2 + 2 * n], res[2 + 2 * n:2 + 3 * n]
    k32, v32, mq, z = res[2 + 3 * n:]
    return cb, u, q, k, v, k32, v32, mq, z


def _memkv_kernel(x_ref, g_ref, w_ref, k32_ref, v32_ref, k_ref, v_ref):
    x = x_ref[...]
    r = lax.rsqrt(jnp.mean(x * x, axis=-1, keepdims=True) + RMS_EPS)
    h = (x * r * g_ref[...]).astype(BF16)
    kv = jnp.dot(h, w_ref[...], preferred_element_type=F32)
    k_ref[...] = kv[:, :MEM_W].astype(BF16)
    v_ref[...] = kv[:, MEM_W:].astype(BF16)
    n_mem = k32_ref.shape[2]
    for b in range(k32_ref.shape[0]):
        kvt = kv[b * n_mem:(b + 1) * n_mem].T
        k32_ref[b] = kvt[:MEM_W]
        v32_ref[b] = kvt[MEM_W:]


def _mem_kv(mem, g, w_bf16):
    B, M, D = mem.shape
    rows = B * M
    k32, v32, k, v = pl.pallas_call(
        _memkv_kernel,
        out_shape=(jax.ShapeDtypeStruct((B, MEM_W, M), F32),
                   jax.ShapeDtypeStruct((B, MEM_W, M), F32),
                   jax.ShapeDtypeStruct((rows, MEM_W), BF16),
                   jax.ShapeDtypeStruct((rows, MEM_W), BF16)),
        compiler_params=pltpu.CompilerParams(vmem_limit_bytes=VMEM_LIMIT),
        name="memkv",
    )(mem.reshape(rows, D), g.reshape(1, D), w_bf16)
    return k32, v32, k.reshape(B, M, MEM_W), v.reshape(B, M, MEM_W)


def _band_bias_table():
    qi = np.arange(SPAN)[:, None] + SPAN
    ki = np.arange(2 * SPAN)[None, :]
    dist = qi - ki
    valid = (dist >= 0) & (dist <= SPAN)
    slopes = _slopes()
    tabs = []
    for _, dil in DILATED_CONFIGS:
        for first in (False, True):
            ok = valid & (ki >= SPAN) if first else valid
            for h in range(ATTN_HEADS):
                bias = -(slopes[h] * dil) * dist * LOG2E
                tabs.append(np.where(ok, bias, -np.inf))
    return np.stack(tabs).astype(np.float32)


def _pair_attention(qb, kb, vb, bias_lo, bias_hi):
    lane = lax.broadcasted_iota(jnp.int32, qb.shape, 1)
    lo = lane < HEAD_DIM
    zero = jnp.zeros_like(qb)
    v_ones = jnp.concatenate([vb, jnp.ones_like(vb)], axis=1)
    accs, ms, ls = [], [], []
    for sel, bias in ((lo, bias_lo), (~lo, bias_hi)):
        qm = jnp.where(sel, qb, zero)
        s = lax.dot_general(qm, kb, (((1,), (1,)), ((), ())), preferred_element_type=F32)
        if bias is not None:
            s = s + bias
        m = jnp.max(s, axis=-1, keepdims=True)
        p = jnp.exp2(s - m)
        acc = jnp.dot(p.astype(BF16), v_ones, preferred_element_type=F32)
        ms.append(m)
        accs.append(acc[:, :LANES])
        ls.append(acc[:, LANES:])
    o = jnp.where(lo, accs[0], accs[1]) / jnp.where(lo, ls[0], ls[1])
    return o, ms, ls


def _attn_kernel(*refs, steps):
    ins = refs[:15]
    tab_ref = refs[15]
    outs = refs[16:22]
    kcat, vcat = refs[22], refs[23]
    s = pl.program_id(1)
    nblk = ATTN_TILE // SPAN
    for c in range(3):
        _, kp, ko, vp, vo = ins[5 * c:5 * c + 5]
        kcat[c, 0:SPAN] = kp[0, 0]
        kcat[c, SPAN:] = ko[0, 0, 0:SPAN]
        vcat[c, 0:SPAN] = vp[0, 0]
        vcat[c, SPAN:] = vo[0, 0, 0:SPAN]

    for j in range(nblk):
        rows = slice(j * SPAN, (j + 1) * SPAN)
        keys = slice((j - 1) * SPAN, (j + 1) * SPAN)
        for c, (_, dil) in enumerate(DILATED_CONFIGS):
            q_ref, _, ko, _, vo = ins[5 * c:5 * c + 5]
            o_ref, l_ref = outs[2 * c], outs[2 * c + 1]
            units_per_residue = steps // dil
            if j == 0:
                base = (c * 2 + (s % units_per_residue == 0).astype(jnp.int32)) * ATTN_HEADS
            else:
                base = c * 2 * ATTN_HEADS
            lane = lax.broadcasted_iota(jnp.int32, (SPAN, LANES), 1)
            m8 = jnp.zeros((SPAN, LANES), F32)
            l8 = jnp.ones((SPAN, LANES), F32)
            for p in range(ATTN_HEADS // 2):
                lanes = slice(p * LANES, (p + 1) * LANES)
                qb = q_ref[0, 0, rows, lanes]
                if j == 0:
                    kb, vb = kcat[c, :, lanes], vcat[c, :, lanes]
                else:
                    kb, vb = ko[0, 0, keys, lanes], vo[0, 0, keys, lanes]
                o, ms, ls = _pair_attention(qb, kb, vb, tab_ref[base + 2 * p],
                                            tab_ref[base + 2 * p + 1])
                o_ref[0, 0, rows, lanes] = o.astype(BF16)
                for half in range(2):
                    m8 = jnp.where(lane == 2 * p + half, ms[half], m8)
                    l8 = jnp.where(lane == 2 * p + half, ls[half], l8)
            l_ref[0, 0, rows, :] = m8 + jnp.log2(l8)


def _prompt_attention(q, k, v):
    B, _, T, W = q[0].shape
    steps = T // ATTN_TILE
    prev_per_tile = ATTN_TILE // SPAN
    args, in_specs, out_specs, out_shape = [], [], [], []
    for c, (_, dil) in enumerate(DILATED_CONFIGS):
        L = T // dil
        upr = L // ATTN_TILE

        def own(b, s, upr=upr):
            return (b, s // upr, s % upr, 0)

        def prev(b, s, upr=upr):
            return (b, s // upr, jnp.maximum((s % upr) * prev_per_tile - 1, 0), 0)

        shape = (B, dil, L, W)
        assert q[c].shape == shape
        args += [q[c], k[c], k[c], v[c], v[c]]
        in_specs += [pl.BlockSpec((1, 1, ATTN_TILE, W), own),
                     pl.BlockSpec((1, 1, SPAN, W), prev),
                     pl.BlockSpec((1, 1, ATTN_TILE, W), own),
                     pl.BlockSpec((1, 1, SPAN, W), prev),
                     pl.BlockSpec((1, 1, ATTN_TILE, W), own)]
        out_specs += [pl.BlockSpec((1, 1, ATTN_TILE, W), own),
                      pl.BlockSpec((1, 1, ATTN_TILE, LANES), own)]
        out_shape += [jax.ShapeDtypeStruct(shape, BF16),
                      jax.ShapeDtypeStruct((B, dil, L, LANES), F32)]
    tab = jnp.asarray(_band_bias_table())
    args.append(tab)
    in_specs.append(pl.BlockSpec(tab.shape, lambda b, s: (0, 0, 0)))
    res = pl.pallas_call(
        functools.partial(_attn_kernel, steps=steps),
        grid=(B, steps),
        in_specs=in_specs,
        out_specs=tuple(out_specs),
        out_shape=tuple(out_shape),
        scratch_shapes=[pltpu.VMEM((3, 2 * SPAN, W), BF16)] * 2,
        compiler_params=_params(2),
        name="prompt_attn",
    )(*args)
    return list(res)


def _silu(z):
    return z / (1.0 + jnp.exp(-z))


def _gate_project_norm(x, mix_ref, wout_ref, gf_ref):
    y = x + jnp.dot(mix_ref[...], wout_ref[...], preferred_element_type=F32)
    r = lax.rsqrt(jnp.mean(y * y, axis=-1, keepdims=True) + RMS_EPS)
    return y * r * gf_ref[...]


def _unfold(src_ref, slabs, d, tmp=None, step=4):
    n_slab, rows = slabs.shape[0], slabs.shape[1]

    def piece(r, s):
        return src_ref[0, r, :, s * LANES:(s + 1) * LANES].astype(F32)

    if d <= step:
        for r in range(d):
            for s in range(n_slab):
                slabs[s, pl.ds(r, rows // d, stride=d), :] = piece(r, s)
        return
    prev_d, n = d // step, rows // d
    per = rows // prev_d
    for rp in range(prev_d):
        for q in range(step):
            for s in range(n_slab):
                tmp[s, pl.ds(rp * per + q, n, stride=step), :] = piece(q * prev_d + rp, s)
    for rp in range(prev_d):
        for s in range(n_slab):
            slabs[s, pl.ds(rp, per, stride=prev_d), :] = tmp[s, pl.ds(rp * per, per), :]


def _head_expander():
    e = np.zeros((2 * LANES, ATTN_W), np.float32)
    for h in range(ATTN_HEADS):
        e[h, h * HEAD_DIM:(h + 1) * HEAD_DIM] = 1.0
        e[LANES + h, h * HEAD_DIM:(h + 1) * HEAD_DIM] = 1.0
    return e


def _finish_kernel(x_ref, cb_ref, u_ref, up_ref, o1_ref, l1_ref, o4_ref, l4_ref, o16_ref, l16_ref,
                   mq_ref, z_ref, mk_ref, mv_ref, cw_ref, wout_ref, gf_ref, ex_ref, y_ref,
                   ucat, mix, o4s, l4s, o16s, l16s, otmp, ltmp):
    i = pl.program_id(1)
    tm = ROW_TILE
    ucat[0:SUBLANES] = jnp.where(i == 0, 0.0, up_ref[0])
    ucat[SUBLANES:] = u_ref[0]
    cw = cw_ref[...]
    conv = (cw[0:1] * ucat[pl.ds(SUBLANES - 2, tm)]
            + cw[1:2] * ucat[pl.ds(SUBLANES - 1, tm)]
            + cw[2:3] * ucat[pl.ds(SUBLANES, tm)])
    conv_y = cb_ref[0].astype(F32) * conv
    z = z_ref[0].astype(F32)
    mix[:, 0:CONV_W] = (conv_y * _silu(z[:, 0:CONV_W])).astype(BF16)
    dil4, dil16 = DILATED_CONFIGS[1][1], DILATED_CONFIGS[2][1]
    _unfold(o4_ref, o4s, dil4)
    _unfold(l4_ref, l4s, dil4)
    _unfold(o16_ref, o16s, dil16, otmp)
    _unfold(l16_ref, l16s, dil16, ltmp)
    lo, hi = CONV_W, CONV_W + ATTN_W
    l1, l4, l16 = l1_ref[0, 0], l4s[0], l16s[0]
    mx = jnp.maximum(jnp.maximum(l1, l4), l16)
    e1, e4, e16 = jnp.exp2(l1 - mx), jnp.exp2(l4 - mx), jnp.exp2(l16 - mx)
    den = e1 + e4 + e16

    def spread(w):
        w_hi = w.astype(BF16)
        w_lo = (w - w_hi.astype(F32)).astype(BF16)
        return jnp.dot(jnp.concatenate([w_hi, w_lo], axis=1), ex_ref[...],
                       preferred_element_type=F32)

    a1, a4, a16 = spread(e1 / den), spread(e4 / den), spread(e16 / den)
    for s in range(ATTN_W // LANES):
        lanes = slice(s * LANES, (s + 1) * LANES)
        attn = (a1[:, lanes] * o1_ref[0, 0, :, lanes].astype(F32) + a4[:, lanes] * o4s[s]
                + a16[:, lanes] * o16s[s])
        a = lo + s * LANES
        mix[:, a:a + LANES] = (attn * _silu(z[:, a:a + LANES])).astype(BF16)
    for p in range(MEM_HEADS // 2):
        lanes = slice(p * LANES, (p + 1) * LANES)
        o, _, _ = _pair_attention(mq_ref[0, :, lanes], mk_ref[0, :, lanes], mv_ref[0, :, lanes],
                                  None, None)
        a = hi + p * LANES
        mix[:, a:a + LANES] = (o * _silu(z[:, a:a + LANES])).astype(BF16)
    y_ref[0] = _gate_project_norm(x_ref[0], mix, wout_ref, gf_ref)


def _prompt_finish(x, cb, u, attn, mq, z, mk, mv, conv_w, w_out_bf16, g_final):
    B, T, D = x.shape
    tm = ROW_TILE

    def row(width):
        return pl.BlockSpec((1, tm, width), lambda b, i: (b, i, 0))

    def const(shape):
        return pl.BlockSpec(shape, lambda b, i: (0,) * len(shape))

    prev8 = pl.BlockSpec((1, SUBLANES, CONV_W),
                         lambda b, i: (b, jnp.maximum(i * (tm // SUBLANES) - 1, 0), 0))
    memspec = pl.BlockSpec((1,) + mk.shape[1:], lambda b, i: (b, 0, 0))
    in_specs = [row(D), row(CONV_W), row(CONV_W), prev8]
    for _, dil in DILATED_CONFIGS:
        in_specs += [pl.BlockSpec((1, dil, tm // dil, ATTN_W), lambda b, i: (b, 0, i, 0)),
                     pl.BlockSpec((1, dil, tm // dil, LANES), lambda b, i: (b, 0, i, 0))]
    expander = jnp.asarray(_head_expander(), BF16)
    in_specs += [row(MEM_W), row(D_MIX), memspec, memspec,
                 const(conv_w.shape), const(w_out_bf16.shape), const((1, D)),
                 const(expander.shape)]
    o_slabs = pltpu.VMEM((ATTN_W // LANES, tm, LANES), F32)
    l_slabs = pltpu.VMEM((1, tm, LANES), F32)
    return pl.pallas_call(
        _finish_kernel,
        grid=(B, T // tm),
        in_specs=in_specs,
        out_specs=row(D),
        out_shape=jax.ShapeDtypeStruct((B, T, D), F32),
        scratch_shapes=[pltpu.VMEM((tm + SUBLANES, CONV_W), F32),
                        pltpu.VMEM((tm, D_MIX), BF16),
                        o_slabs, l_slabs, o_slabs, l_slabs, o_slabs, l_slabs],
        compiler_params=_params(2),
        name="prompt_finish",
    )(x, cb, u, u, *attn, mq, z, mk, mv, conv_w, w_out_bf16, g_final.reshape(1, D), expander)


def _sample_tables(S, n_buf):
    j = np.arange(S)[:, None]
    t = np.arange(n_buf)[None, :]
    dist = n_buf + j - t
    count = np.zeros(dist.shape)
    for win, dil in DILATED_CONFIGS:
        count += (dist % dil == 0) & (dist <= win)
    bias = np.where(count > 0, -_slopes()[:, None, None] * dist, -np.inf)
    bias = bias.reshape(ATTN_HEADS // 2, 2 * S, n_buf)
    count = np.concatenate([count, count], axis=0)
    return count.astype(np.float32), bias.astype(np.float32)


def _stack_pair(a, p):
    ap = a[:, p * LANES:(p + 1) * LANES]
    lo = lax.broadcasted_iota(jnp.int32, ap.shape, 1) < HEAD_DIM
    zero = jnp.zeros_like(ap)
    return jnp.concatenate([jnp.where(lo, ap, zero), jnp.where(lo, zero, ap)], axis=0)


def _unstack_pair(o, S):
    lo = lax.broadcasted_iota(jnp.int32, (S, LANES), 1) < HEAD_DIM
    return jnp.where(lo, o[:S], o[S:])


def _sample_attn_kernel(q_ref, kn_ref, vn_ref, kt_ref, vt_ref, mq_ref, mkt_ref, mvt_ref,
                        cnt_ref, bias_ref, o_ref, mo_ref):
    R, S = q_ref.shape[0], q_ref.shape[1]
    nt = (((1,), (1,)), ((), ()))
    slopes = _slopes()
    n_cfg = float(len(DILATED_CONFIGS))
    row = lax.broadcasted_iota(jnp.int32, (2 * S, 1), 0)
    upper = row >= S
    rowj = jnp.where(upper, row - S, row)
    cnt = cnt_ref[...]
    rows = range(R)

    def lane_group(ref, r, p):
        return ref[r][:, p * LANES:(p + 1) * LANES]

    def pair_rows(ref, r, p):
        return ref[r, 2 * p:2 * p + 2].reshape(2 * HEAD_DIM, ref.shape[-1]).astype(BF16)

    units = [(r, p) for r in rows for p in range(ATTN_HEADS // 2)]
    munits = [(r, p) for r in rows for p in range(MEM_HEADS // 2)]
    qs = [_stack_pair(q_ref[r], p) for r, p in units]
    scores = [jnp.dot(q8.astype(BF16), pair_rows(kt_ref, r, p),
                      preferred_element_type=F32) + bias_ref[p] for (r, p), q8 in zip(units, qs)]
    mscores = [jnp.dot(_stack_pair(mq_ref[r], p).astype(BF16), pair_rows(mkt_ref, r, p),
                       preferred_element_type=F32) for r, p in munits]
    probs, stats = [], []
    for (r, p), q8, s in zip(units, qs, scores):
        knp = lane_group(kn_ref, r, p)
        slope = jnp.where(upper, float(slopes[2 * p + 1]), float(slopes[2 * p]))
        m = jnp.max(s, axis=-1, keepdims=True)
        s_new = []
        for jp in range(S):
            d = rowj - jp
            sj = jnp.sum(q8 * knp[jp:jp + 1], axis=-1, keepdims=True)
            sj = jnp.where(d >= 0, sj - slope * d.astype(F32), -jnp.inf)
            s_new.append(sj)
            m = jnp.maximum(m, sj)
        pr = cnt * jnp.exp(s - m)
        l = jnp.sum(pr, axis=-1, keepdims=True)
        w_new = []
        for jp in range(S):
            w = jnp.where(rowj == jp, n_cfg, 1.0) * jnp.exp(s_new[jp] - m)
            l = l + w
            w_new.append(w)
        probs.append(pr.astype(BF16))
        stats.append((l, w_new))
    mprobs, mden = [], []
    for s in mscores:
        pr = jnp.exp(s - jnp.max(s, axis=-1, keepdims=True))
        mden.append(jnp.sum(pr, axis=-1, keepdims=True))
        mprobs.append(pr.astype(BF16))
    accs = [lax.dot_general(pr, pair_rows(vt_ref, r, p), nt, preferred_element_type=F32)
            for (r, p), pr in zip(units, probs)]
    maccs = [lax.dot_general(pr, pair_rows(mvt_ref, r, p), nt, preferred_element_type=F32)
             for (r, p), pr in zip(munits, mprobs)]
    outs = []
    for (r, p), acc, (l, w_new) in zip(units, accs, stats):
        vnp = lane_group(vn_ref, r, p)
        for jp in range(S):
            acc = acc + w_new[jp] * vnp[jp:jp + 1]
        outs.append(_unstack_pair(acc / l, S))
    mouts = [_unstack_pair(acc / den, S) for acc, den in zip(maccs, mden)]
    n_p, n_mp = ATTN_HEADS // 2, MEM_HEADS // 2
    for r in rows:
        o_ref[r] = jnp.concatenate(outs[r * n_p:(r + 1) * n_p], axis=-1)
        mo_ref[r] = jnp.concatenate(mouts[r * n_mp:(r + 1) * n_mp], axis=-1)


def _sample_attention(q, k_new, v_new, cache_kt, cache_vt, mq, mem_kt, mem_vt):
    DB, S, W = q.shape
    n_buf = cache_kt.shape[-1]
    R = SAMPLE_ROWS
    assert DB % R == 0
    assert n_buf >= DILATED_CONFIGS[-1][0] and S <= DILATED_CONFIGS[1][1], \
        "new tokens reach each other through dilation 1 only; the buffer covers every window"
    cnt, bias = (jnp.asarray(t) for t in _sample_tables(S, n_buf))

    def rows(a):
        return pl.BlockSpec((R,) + a.shape[1:], lambda b: (b,) + (0,) * (a.ndim - 1))

    def const(a):
        return pl.BlockSpec(a.shape, lambda b: (0,) * a.ndim)

    args = (q, k_new, v_new, cache_kt, cache_vt, mq, mem_kt, mem_vt)
    return pl.pallas_call(
        _sample_attn_kernel,
        grid=(DB // R,),
        in_specs=[rows(a) for a in args] + [const(cnt), const(bias)],
        out_specs=(rows(q), rows(mq)),
        out_shape=(jax.ShapeDtypeStruct((DB, S, W), F32),
                   jax.ShapeDtypeStruct((DB, S, MEM_W), F32)),
        compiler_params=_params(1),
        name="sample_attn",
    )(*args, cnt, bias)


def _sample_finish_kernel(x_ref, cb_ref, u0_ref, u1_ref, u2_ref, attn_ref, mem_ref, z_ref,
                          cw_ref, wout_ref, gf_ref, y_ref, mix):
    cw = cw_ref[...]
    conv = cw[0:1] * u0_ref[...] + cw[1:2] * u1_ref[...] + cw[2:3] * u2_ref[...]
    z = z_ref[...].astype(F32)
    lo, hi = CONV_W, CONV_W + ATTN_W
    mix[:, 0:lo] = (cb_ref[...].astype(F32) * conv * _silu(z[:, 0:lo])).astype(BF16)
    mix[:, lo:hi] = (attn_ref[...] * _silu(z[:, lo:hi])).astype(BF16)
    mix[:, hi:] = (mem_ref[...] * _silu(z[:, hi:])).astype(BF16)
    y_ref[...] = _gate_project_norm(x_ref[...], mix, wout_ref, gf_ref)


def _sample_finish(x, cb, u_taps, attn, mem, z, conv_w, w_out_bf16, g_final):
    rows, D = x.shape
    return pl.pallas_call(
        _sample_finish_kernel,
        out_shape=jax.ShapeDtypeStruct((rows, D), F32),
        scratch_shapes=[pltpu.VMEM((rows, D_MIX), BF16)],
        compiler_params=pltpu.CompilerParams(vmem_limit_bytes=VMEM_LIMIT),
        name="sample_finish",
    )(x, cb, *u_taps, attn, mem, z, conv_w, w_out_bf16, g_final.reshape(1, D))


def kernel(x_prompt, x_sample, mem_prompt, cache_win_k, cache_win_v, cache_conv, cache_mem_k,
           cache_mem_v, g_in, w_in, conv_w, g_mem, w_mem_kv, w_out, g_final):
    assert g_in.shape[0] == 1, "the final RMSNorm is fused into the single layer's tail"
    B, T, D = x_prompt.shape
    DB, S, _ = x_sample.shape
    n_keep = min(DILATED_CONFIGS[-1][0], T)
    w_in_b = w_in[0].astype(BF16)
    w_out_b = w_out[0].astype(BF16)
    dils = tuple(d for _, d in DILATED_CONFIGS)
    wkvt_b = _transposed_columns(w_in[0], _C_K, _C_MQ)
    cb, u, q, k, v, kt32, vt32, mq, z = _project(x_prompt, g_in[0], w_in_b, wkvt_b, n_keep, True,
                                                 dils, Q_SCALE * LOG2E)
    mkt32, mvt32, mk, mv = _mem_kv(mem_prompt, g_mem[0], w_mem_kv[0].astype(BF16))
    attn = _prompt_attention(q, k, v)

    def heads_last(t, heads):
        return jnp.transpose(t.reshape(t.shape[0], heads, HEAD_DIM, -1), (0, 3, 1, 2))[None]

    def heads_first(c):
        return jnp.transpose(c, (0, 2, 3, 1))

    p_win_k = heads_last(kt32, ATTN_HEADS)
    p_win_v = heads_last(vt32, ATTN_HEADS)
    p_conv = u[None, :, -(CONV_WIDTH - 1):]
    p_mem_k = heads_last(mkt32, MEM_HEADS)
    p_mem_v = heads_last(mvt32, MEM_HEADS)
    rows = DB * S
    cb_s, u_s, q_s, _, _, k32, v32, mq_s, z_s = _project(
        x_sample.reshape(1, rows, D), g_in[0], w_in_b, wkvt_b, rows, False, (1,), Q_SCALE)
    k32 = k32.reshape(DB, S, ATTN_W)
    v32 = v32.reshape(DB, S, ATTN_W)
    attn_s, mem_s = _sample_attention(
        q_s[0].reshape(DB, S, ATTN_W).astype(F32), k32, v32,
        heads_first(cache_win_k[0]), heads_first(cache_win_v[0]),
        mq_s.reshape(DB, S, MEM_W).astype(F32),
        heads_first(cache_mem_k[0]), heads_first(cache_mem_v[0]))
    y_prompt = _prompt_finish(x_prompt, cb, u, attn, mq, z, mk, mv, conv_w[0], w_out_b, g_final)
    full = jnp.concatenate([cache_conv[0], u_s.reshape(DB, S, CONV_W)], axis=1)
    taps = [full[:, t:t + S].reshape(rows, CONV_W) for t in range(CONV_WIDTH)]
    y_sample = _sample_finish(x_sample.reshape(rows, D), cb_s.reshape(rows, CONV_W), taps,
                              attn_s.reshape(rows, ATTN_W), mem_s.reshape(rows, MEM_W),
                              z_s.reshape(rows, D_MIX), conv_w[0], w_out_b,
                              g_final).reshape(DB, S, D)
    s_win_k = k32.reshape(1, DB, S, ATTN_HEADS, HEAD_DIM)
    s_win_v = v32.reshape(1, DB, S, ATTN_HEADS, HEAD_DIM)
    s_conv = full[None, :, -(CONV_WIDTH - 1):]
    return (y_prompt, y_sample, p_win_k, p_win_v, p_conv, p_mem_k, p_mem_v,
            s_win_k, s_win_v, s_conv)
```

```python
import functools

import numpy as np
import jax
import jax.numpy as jnp
from jax import lax
from jax.experimental import pallas as pl
from jax.experimental.pallas import tpu as pltpu

F32 = jnp.float32
BF16 = jnp.bfloat16

HEAD_DIM = 64
ATTN_HEADS = 8
MEM_HEADS = 4
CONV_W = 256
ATTN_W = 512
MEM_W = 256
D_MIX = 1024
CONV_WIDTH = 3
DILATED_CONFIGS = ((128, 1), (512, 4), (2048, 16))
SPAN = 128
ALIBI_MAX = 8.0
RMS_EPS = 1e-6
Q_SCALE = HEAD_DIM ** -0.5
LOG2E = float(np.log2(np.e))

LANES = 128
SUBLANES = 8
VMEM_LIMIT = 56 * 1024 * 1024

_C_CB, _C_CC, _C_Q, _C_K, _C_V, _C_MQ, _C_Z, _C_END = 0, 256, 768, 1280, 1792, 2304, 2560, 3584

ROW_TILE = 512
ATTN_TILE = 512


def _slopes():
    h = np.arange(ATTN_HEADS, dtype=np.float64) + 1.0
    return np.exp2(-ALIBI_MAX * h / ATTN_HEADS)


def _params(n_axes):
    return pltpu.CompilerParams(
        dimension_semantics=("arbitrary",) * n_axes,
        vmem_limit_bytes=VMEM_LIMIT)


def _transpose_cast_kernel(w_ref, o_ref):
    o_ref[...] = w_ref[...].T.astype(BF16)


def _transposed_columns(w, lo, hi):
    K = w.shape[0]
    blk = 2 * LANES
    assert lo % blk == 0 and hi % blk == 0
    return pl.pallas_call(
        _transpose_cast_kernel,
        grid=((hi - lo) // blk,),
        in_specs=[pl.BlockSpec((K, blk), lambda i: (0, lo // blk + i))],
        out_specs=pl.BlockSpec((blk, K), lambda i: (i, 0)),
        out_shape=jax.ShapeDtypeStruct((hi - lo, K), BF16),
        compiler_params=_params(1),
        name="wkv_t",
    )(w)


def _fold_store(res, out_refs, slabs, dils):
    rows = res.shape[0]
    out_refs[0][0, 0] = res.astype(BF16)
    if len(dils) == 1:
        return
    n_slab = res.shape[1] // LANES
    src, prev_d = slabs[0], 1
    for s in range(n_slab):
        src[s] = res[:, s * LANES:(s + 1) * LANES]
    for idx, (d, o_ref) in enumerate(zip(dils[1:], out_refs[1:])):
        step, n = d // prev_d, rows // d
        dst = slabs[idx + 1] if idx + 2 < len(dils) else None
        for rp in range(prev_d):
            for q in range(step):
                r = q * prev_d + rp
                for s in range(n_slab):
                    val = src[s, pl.ds(rp * (rows // prev_d) + q, n, stride=step), :]
                    o_ref[0, r, :, s * LANES:(s + 1) * LANES] = val.astype(BF16)
                    if dst is not None:
                        dst[s, pl.ds(r * n, n), :] = val
        src, prev_d = dst, d


def _proj_kernel(x_ref, g_ref, w_ref, wkvt_ref, *refs, first, keep_transposed, dils, q_scale):
    n = len(dils)
    cb_ref, u_ref = refs[0:2]
    q_refs, k_refs, v_refs = refs[2:2 + n], refs[2 + n:2 + 2 * n], refs[2 + 2 * n:2 + 3 * n]
    k32_ref, v32_ref, mq_ref, z_ref = refs[2 + 3 * n:6 + 3 * n]
    scratch = refs[6 + 3 * n:]
    slabs = [scratch[a * (n - 1):(a + 1) * (n - 1)] for a in range(3)]
    x = x_ref[0]
    r = lax.rsqrt(jnp.mean(x * x, axis=-1, keepdims=True) + RMS_EPS)
    h = (x * r * g_ref[...]).astype(BF16)

    def seg(lo, hi):
        return jnp.dot(h, w_ref[:, lo:hi], preferred_element_type=F32)

    cb_ref[0] = seg(_C_CB, _C_CC).astype(BF16)
    cch = seg(_C_CC, _C_Q)
    u_ref[0] = cch[:, :CONV_W] * cch[:, CONV_W:]
    _fold_store(seg(_C_Q, _C_K) * q_scale, q_refs, slabs[0], dils)
    k = seg(_C_K, _C_V)
    _fold_store(k, k_refs, slabs[1], dils)
    v = seg(_C_V, _C_MQ)
    _fold_store(v, v_refs, slabs[2], dils)
    mq_ref[0] = (seg(_C_MQ, _C_Z) * q_scale).astype(BF16)
    z_ref[0] = seg(_C_Z, _C_END).astype(BF16)
    if keep_transposed:
        @pl.when(pl.program_id(1) >= first)
        def _():
            nt = (((1,), (1,)), ((), ()))
            k32_ref[0] = lax.dot_general(wkvt_ref[0:ATTN_W], h, nt, preferred_element_type=F32)
            v32_ref[0] = lax.dot_general(wkvt_ref[ATTN_W:], h, nt, preferred_element_type=F32)
    else:
        k32_ref[0] = k
        v32_ref[0] = v


def _project(x, g, w_bf16, wkvt_bf16, n_keep, keep_transposed, dils, q_scale):
    B, T, D = x.shape
    tm = ROW_TILE
    nt = T // tm
    first = (T - n_keep) // tm
    n = len(dils)

    def row(width):
        return pl.BlockSpec((1, tm, width), lambda b, i: (b, i, 0))

    folded_specs = [pl.BlockSpec((1, d, tm // d, ATTN_W), lambda b, i: (b, 0, i, 0)) for d in dils]
    folded_shapes = [jax.ShapeDtypeStruct((B, d, T // d, ATTN_W), BF16) for d in dils]

    if keep_transposed:
        keep = pl.BlockSpec((1, ATTN_W, tm), lambda b, i: (b, 0, jnp.maximum(i - first, 0)))
        keep_shape = (B, ATTN_W, n_keep)
    else:
        keep = pl.BlockSpec((1, tm, ATTN_W), lambda b, i: (b, jnp.maximum(i - first, 0), 0))
        keep_shape = (B, n_keep, ATTN_W)
    out_shape = (
        [jax.ShapeDtypeStruct((B, T, CONV_W), BF16),
         jax.ShapeDtypeStruct((B, T, CONV_W), F32)]
        + folded_shapes * 3
        + [jax.ShapeDtypeStruct(keep_shape, F32),
           jax.ShapeDtypeStruct(keep_shape, F32),
           jax.ShapeDtypeStruct((B, T, MEM_W), BF16),
           jax.ShapeDtypeStruct((B, T, D_MIX), BF16)])
    out_specs = ([row(CONV_W), row(CONV_W)] + folded_specs * 3
                 + [keep, keep, row(MEM_W), row(D_MIX)])
    scratch = [pltpu.VMEM((ATTN_W // LANES, tm, LANES), F32)] * (3 * (n - 1))
    res = pl.pallas_call(
        functools.partial(_proj_kernel, first=first, keep_transposed=keep_transposed, dils=dils,
                          q_scale=q_scale),
        grid=(B, nt),
        in_specs=[row(D),
                  pl.BlockSpec((1, D), lambda b, i: (0, 0)),
                  pl.BlockSpec(w_bf16.shape, lambda b, i: (0, 0)),
                  pl.BlockSpec(wkvt_bf16.shape, lambda b, i: (0, 0))],
        out_specs=tuple(out_specs),
        out_shape=tuple(out_shape),
        scratch_shapes=scratch,
        compiler_params=_params(2),
        name="proj",
    )(x, g.reshape(1, D), w_bf16, wkvt_bf16)
    cb, u = res[0:2]
    q, k, v = res[2:2 + n], res[2 + n:2 + 2 * n], res[2 + 2 * n:2 + 3 * n]
    k32, v32, mq, z = res[2 + 3 * n:]
    return cb, u, q, k, v, k32, v32, mq, z


def _memkv_kernel(x_ref, g_ref, w_ref, k32_ref, v32_ref, k_ref, v_ref):
    x = x_ref[...]
    r = lax.rsqrt(jnp.mean(x * x, axis=-1, keepdims=True) + RMS_EPS)
    h = (x * r * g_ref[...]).astype(BF16)
    kv = jnp.dot(h, w_ref[...], preferred_element_type=F32)
    k_ref[...] = kv[:, :MEM_W].astype(BF16)
    v_ref[...] = kv[:, MEM_W:].astype(BF16)
    n_mem = k32_ref.shape[2]
    for b in range(k32_ref.shape[0]):
        kvt = kv[b * n_mem:(b + 1) * n_mem].T
        k32_ref[b] = kvt[:MEM_W]
        v32_ref[b] = kvt[MEM_W:]


def _mem_kv(mem, g, w_bf16):
    B, M, D = mem.shape
    rows = B * M
    k32, v32, k, v = pl.pallas_call(
        _memkv_kernel,
        out_shape=(jax.ShapeDtypeStruct((B, MEM_W, M), F32),
                   jax.ShapeDtypeStruct((B, MEM_W, M), F32),
                   jax.ShapeDtypeStruct((rows, MEM_W), BF16),
                   jax.ShapeDtypeStruct((rows, MEM_W), BF16)),
        compiler_params=pltpu.CompilerParams(vmem_limit_bytes=VMEM_LIMIT),
        name="memkv",
    )(mem.reshape(rows, D), g.reshape(1, D), w_bf16)
    return k32, v32, k.reshape(B, M, MEM_W), v.reshape(B, M, MEM_W)


def _band_bias_table():
    qi = np.arange(SPAN)[:, None] + SPAN
    ki = np.arange(2 * SPAN)[None, :]
    dist = qi - ki
    valid = (dist >= 0) & (dist <= SPAN)
    slopes = _slopes()
    tabs = []
    for _, dil in DILATED_CONFIGS:
        for first in (False, True):
            ok = valid & (ki >= SPAN) if first else valid
            for h in range(ATTN_HEADS):
                bias = -(slopes[h] * dil) * dist * LOG2E
                tabs.append(np.where(ok, bias, -np.inf))
    return np.stack(tabs).astype(np.float32)


def _pair_attention(qb, kb, vb, bias_lo, bias_hi):
    lane = lax.broadcasted_iota(jnp.int32, qb.shape, 1)
    lo = lane < HEAD_DIM
    zero = jnp.zeros_like(qb)
    v_ones = jnp.concatenate([vb, jnp.ones_like(vb)], axis=1)
    accs, ms, ls = [], [], []
    for sel, bias in ((lo, bias_lo), (~lo, bias_hi)):
        qm = jnp.where(sel, qb, zero)
        s = lax.dot_general(qm, kb, (((1,), (1,)), ((), ())), preferred_element_type=F32)
        if bias is not None:
            s = s + bias
        m = jnp.max(s, axis=-1, keepdims=True)
        p = jnp.exp2(s - m)
        acc = jnp.dot(p.astype(BF16), v_ones, preferred_element_type=F32)
        ms.append(m)
        accs.append(acc[:, :LANES])
        ls.append(acc[:, LANES:])
    o = jnp.where(lo, accs[0], accs[1]) / jnp.where(lo, ls[0], ls[1])
    return o, ms, ls


def _attn_kernel(*refs, steps):
    ins = refs[:15]
    tab_ref = refs[15]
    sample_ins = refs[16:16 + N_SAMPLE_IN]
    outs = refs[16 + N_SAMPLE_IN:22 + N_SAMPLE_IN]
    sample_outs = refs[22 + N_SAMPLE_IN:24 + N_SAMPLE_IN]
    kcat, vcat = refs[24 + N_SAMPLE_IN:]
    s = pl.program_id(1)
    nblk = ATTN_TILE // SPAN
    _sample_attn_kernel(*sample_ins, *sample_outs)
    for c in range(3):
        _, kp, ko, vp, vo = ins[5 * c:5 * c + 5]
        kcat[c, 0:SPAN] = kp[0, 0]
        kcat[c, SPAN:] = ko[0, 0, 0:SPAN]
        vcat[c, 0:SPAN] = vp[0, 0]
        vcat[c, SPAN:] = vo[0, 0, 0:SPAN]

    for j in range(nblk):
        rows = slice(j * SPAN, (j + 1) * SPAN)
        keys = slice((j - 1) * SPAN, (j + 1) * SPAN)
        for c, (_, dil) in enumerate(DILATED_CONFIGS):
            q_ref, _, ko, _, vo = ins[5 * c:5 * c + 5]
            o_ref, l_ref = outs[2 * c], outs[2 * c + 1]
            units_per_residue = steps // dil
            if j == 0:
                base = (c * 2 + (s % units_per_residue == 0).astype(jnp.int32)) * ATTN_HEADS
            else:
                base = c * 2 * ATTN_HEADS
            lane = lax.broadcasted_iota(jnp.int32, (SPAN, LANES), 1)
            m8 = jnp.zeros((SPAN, LANES), F32)
            l8 = jnp.ones((SPAN, LANES), F32)
            for p in range(ATTN_HEADS // 2):
                lanes = slice(p * LANES, (p + 1) * LANES)
                qb = q_ref[0, 0, rows, lanes]
                if j == 0:
                    kb, vb = kcat[c, :, lanes], vcat[c, :, lanes]
                else:
                    kb, vb = ko[0, 0, keys, lanes], vo[0, 0, keys, lanes]
                o, ms, ls = _pair_attention(qb, kb, vb, tab_ref[base + 2 * p],
                                            tab_ref[base + 2 * p + 1])
                o_ref[0, 0, rows, lanes] = o.astype(BF16)
                for half in range(2):
                    m8 = jnp.where(lane == 2 * p + half, ms[half], m8)
                    l8 = jnp.where(lane == 2 * p + half, ls[half], l8)
            l_ref[0, 0, rows, :] = m8 + jnp.log2(l8)


def _prompt_attention(q, k, v, sample, sample_base):
    B, _, T, W = q[0].shape
    steps = T // ATTN_TILE
    prev_per_tile = ATTN_TILE // SPAN
    args, in_specs, out_specs, out_shape = [], [], [], []
    for c, (_, dil) in enumerate(DILATED_CONFIGS):
        L = T // dil
        upr = L // ATTN_TILE

        def own(b, s, upr=upr):
            return (b, s // upr, s % upr, 0)

        def prev(b, s, upr=upr):
            return (b, s // upr, jnp.maximum((s % upr) * prev_per_tile - 1, 0), 0)

        shape = (B, dil, L, W)
        assert q[c].shape == shape
        args += [q[c], k[c], k[c], v[c], v[c]]
        in_specs += [pl.BlockSpec((1, 1, ATTN_TILE, W), own),
                     pl.BlockSpec((1, 1, SPAN, W), prev),
                     pl.BlockSpec((1, 1, ATTN_TILE, W), own),
                     pl.BlockSpec((1, 1, SPAN, W), prev),
                     pl.BlockSpec((1, 1, ATTN_TILE, W), own)]
        out_specs += [pl.BlockSpec((1, 1, ATTN_TILE, W), own),
                      pl.BlockSpec((1, 1, ATTN_TILE, LANES), own)]
        out_shape += [jax.ShapeDtypeStruct(shape, BF16),
                      jax.ShapeDtypeStruct((B, dil, L, LANES), F32)]
    tab = jnp.asarray(_band_bias_table())
    args.append(tab)
    in_specs.append(pl.BlockSpec(tab.shape, lambda b, s: (0, 0, 0)))
    s_args, s_in, s_out, s_shape = _sample_row_specs(sample, sample_base, B * steps,
                                                     lambda b, s: b * steps + s)
    res = pl.pallas_call(
        functools.partial(_attn_kernel, steps=steps),
        grid=(B, steps),
        in_specs=in_specs + s_in,
        out_specs=tuple(out_specs + s_out),
        out_shape=tuple(out_shape + s_shape),
        scratch_shapes=[pltpu.VMEM((3, 2 * SPAN, W), BF16)] * 2,
        compiler_params=_params(2),
        name="prompt_attn",
    )(*args, *s_args)
    return list(res[:6]), res[6], res[7]


def _silu(z):
    return z / (1.0 + jnp.exp(-z))


def _gate_project_norm(x, mix_ref, wout_ref, gf_ref):
    y = x + jnp.dot(mix_ref[...], wout_ref[...], preferred_element_type=F32)
    r = lax.rsqrt(jnp.mean(y * y, axis=-1, keepdims=True) + RMS_EPS)
    return y * r * gf_ref[...]


def _unfold(src_ref, slabs, d, tmp=None, step=4):
    n_slab, rows = slabs.shape[0], slabs.shape[1]

    def piece(r, s):
        return src_ref[0, r, :, s * LANES:(s + 1) * LANES].astype(F32)

    if d <= step:
        for r in range(d):
            for s in range(n_slab):
                slabs[s, pl.ds(r, rows // d, stride=d), :] = piece(r, s)
        return
    prev_d, n = d // step, rows // d
    per = rows // prev_d
    for rp in range(prev_d):
        for q in range(step):
            for s in range(n_slab):
                tmp[s, pl.ds(rp * per + q, n, stride=step), :] = piece(q * prev_d + rp, s)
    for rp in range(prev_d):
        for s in range(n_slab):
            slabs[s, pl.ds(rp, per, stride=prev_d), :] = tmp[s, pl.ds(rp * per, per), :]


def _head_expander():
    e = np.zeros((2 * LANES, ATTN_W), np.float32)
    for h in range(ATTN_HEADS):
        e[h, h * HEAD_DIM:(h + 1) * HEAD_DIM] = 1.0
        e[LANES + h, h * HEAD_DIM:(h + 1) * HEAD_DIM] = 1.0
    return e


N_FINISH_IN = 18


def _finish_kernel(*refs):
    (x_ref, cb_ref, u_ref, up_ref, o1_ref, l1_ref, o4_ref, l4_ref, o16_ref, l16_ref,
     mq_ref, z_ref, mk_ref, mv_ref, cw_ref, wout_ref, gf_ref, ex_ref) = refs[:N_FINISH_IN]
    sample_ins = refs[N_FINISH_IN:N_FINISH_IN + N_SAMPLE_IN]
    y_ref = refs[N_FINISH_IN + N_SAMPLE_IN]
    sample_outs = refs[N_FINISH_IN + N_SAMPLE_IN + 1:N_FINISH_IN + N_SAMPLE_IN + 3]
    ucat, mix, o4s, l4s, o16s, l16s, otmp, ltmp = refs[N_FINISH_IN + N_SAMPLE_IN + 3:]
    _sample_attn_kernel(*sample_ins, *sample_outs)
    i = pl.program_id(1)
    tm = ROW_TILE
    ucat[0:SUBLANES] = jnp.where(i == 0, 0.0, up_ref[0])
    ucat[SUBLANES:] = u_ref[0]
    cw = cw_ref[...]
    conv = (cw[0:1] * ucat[pl.ds(SUBLANES - 2, tm)]
            + cw[1:2] * ucat[pl.ds(SUBLANES - 1, tm)]
            + cw[2:3] * ucat[pl.ds(SUBLANES, tm)])
    conv_y = cb_ref[0].astype(F32) * conv
    z = z_ref[0].astype(F32)
    mix[:, 0:CONV_W] = (conv_y * _silu(z[:, 0:CONV_W])).astype(BF16)
    dil4, dil16 = DILATED_CONFIGS[1][1], DILATED_CONFIGS[2][1]
    _unfold(o4_ref, o4s, dil4)
    _unfold(l4_ref, l4s, dil4)
    _unfold(o16_ref, o16s, dil16, otmp)
    _unfold(l16_ref, l16s, dil16, ltmp)
    lo, hi = CONV_W, CONV_W + ATTN_W
    l1, l4, l16 = l1_ref[0, 0], l4s[0], l16s[0]
    mx = jnp.maximum(jnp.maximum(l1, l4), l16)
    e1, e4, e16 = jnp.exp2(l1 - mx), jnp.exp2(l4 - mx), jnp.exp2(l16 - mx)
    den = e1 + e4 + e16

    def spread(w):
        w_hi = w.astype(BF16)
        w_lo = (w - w_hi.astype(F32)).astype(BF16)
        return jnp.dot(jnp.concatenate([w_hi, w_lo], axis=1), ex_ref[...],
                       preferred_element_type=F32)

    a1, a4, a16 = spread(e1 / den), spread(e4 / den), spread(e16 / den)
    for s in range(ATTN_W // LANES):
        lanes = slice(s * LANES, (s + 1) * LANES)
        attn = (a1[:, lanes] * o1_ref[0, 0, :, lanes].astype(F32) + a4[:, lanes] * o4s[s]
                + a16[:, lanes] * o16s[s])
        a = lo + s * LANES
        mix[:, a:a + LANES] = (attn * _silu(z[:, a:a + LANES])).astype(BF16)
    for p in range(MEM_HEADS // 2):
        lanes = slice(p * LANES, (p + 1) * LANES)
        o, _, _ = _pair_attention(mq_ref[0, :, lanes], mk_ref[0, :, lanes], mv_ref[0, :, lanes],
                                  None, None)
        a = hi + p * LANES
        mix[:, a:a + LANES] = (o * _silu(z[:, a:a + LANES])).astype(BF16)
    y_ref[0] = _gate_project_norm(x_ref[0], mix, wout_ref, gf_ref)


def _prompt_finish(x, cb, u, attn, mq, z, mk, mv, conv_w, w_out_bf16, g_final, sample,
                   sample_base):
    B, T, D = x.shape
    tm = ROW_TILE

    def row(width):
        return pl.BlockSpec((1, tm, width), lambda b, i: (b, i, 0))

    def const(shape):
        return pl.BlockSpec(shape, lambda b, i: (0,) * len(shape))

    prev8 = pl.BlockSpec((1, SUBLANES, CONV_W),
                         lambda b, i: (b, jnp.maximum(i * (tm // SUBLANES) - 1, 0), 0))
    memspec = pl.BlockSpec((1,) + mk.shape[1:], lambda b, i: (b, 0, 0))
    in_specs = [row(D), row(CONV_W), row(CONV_W), prev8]
    for _, dil in DILATED_CONFIGS:
        in_specs += [pl.BlockSpec((1, dil, tm // dil, ATTN_W), lambda b, i: (b, 0, i, 0)),
                     pl.BlockSpec((1, dil, tm // dil, LANES), lambda b, i: (b, 0, i, 0))]
    expander = jnp.asarray(_head_expander(), BF16)
    in_specs += [row(MEM_W), row(D_MIX), memspec, memspec,
                 const(conv_w.shape), const(w_out_bf16.shape), const((1, D)),
                 const(expander.shape)]
    assert len(in_specs) == N_FINISH_IN
    o_slabs = pltpu.VMEM((ATTN_W // LANES, tm, LANES), F32)
    l_slabs = pltpu.VMEM((1, tm, LANES), F32)
    nt = T // tm
    s_args, s_in, s_out, s_shape = _sample_row_specs(sample, sample_base, B * nt,
                                                     lambda b, i: b * nt + i)
    return pl.pallas_call(
        _finish_kernel,
        grid=(B, nt),
        in_specs=in_specs + s_in,
        out_specs=tuple([row(D)] + s_out),
        out_shape=tuple([jax.ShapeDtypeStruct((B, T, D), F32)] + s_shape),
        scratch_shapes=[pltpu.VMEM((tm + SUBLANES, CONV_W), F32),
                        pltpu.VMEM((tm, D_MIX), BF16),
                        o_slabs, l_slabs, o_slabs, l_slabs, o_slabs, l_slabs],
        compiler_params=_params(2),
        name="prompt_finish",
    )(x, cb, u, u, *attn, mq, z, mk, mv, conv_w, w_out_bf16, g_final.reshape(1, D), expander,
      *s_args)


def _sample_tables(S, n_buf):
    j = np.arange(S)[:, None]
    t = np.arange(n_buf)[None, :]
    dist = n_buf + j - t
    count = np.zeros(dist.shape)
    for win, dil in DILATED_CONFIGS:
        count += (dist % dil == 0) & (dist <= win)
    bias = np.where(count > 0, -_slopes()[:, None, None] * dist, -np.inf)
    bias = bias.reshape(ATTN_HEADS // 2, 2 * S, n_buf)
    count = np.concatenate([count, count], axis=0)
    return count.astype(np.float32), bias.astype(np.float32)


def _stack_pair(a, p):
    ap = a[:, p * LANES:(p + 1) * LANES]
    lo = lax.broadcasted_iota(jnp.int32, ap.shape, 1) < HEAD_DIM
    zero = jnp.zeros_like(ap)
    return jnp.concatenate([jnp.where(lo, ap, zero), jnp.where(lo, zero, ap)], axis=0)


def _unstack_pair(o, S):
    lo = lax.broadcasted_iota(jnp.int32, (S, LANES), 1) < HEAD_DIM
    return jnp.where(lo, o[:S], o[S:])


def _sample_attn_kernel(q_ref, kn_ref, vn_ref, kt_ref, vt_ref, mq_ref, mkt_ref, mvt_ref,
                        cnt_ref, bias_ref, o_ref, mo_ref):
    R, S = q_ref.shape[0], q_ref.shape[1]
    nt = (((1,), (1,)), ((), ()))
    slopes = _slopes()
    n_cfg = float(len(DILATED_CONFIGS))
    row = lax.broadcasted_iota(jnp.int32, (2 * S, 1), 0)
    upper = row >= S
    rowj = jnp.where(upper, row - S, row)
    cnt = cnt_ref[...]
    rows = range(R)

    def lane_group(ref, r, p):
        return ref[r][:, p * LANES:(p + 1) * LANES]

    def pair_rows(ref, r, p):
        return ref[r, 2 * p:2 * p + 2].reshape(2 * HEAD_DIM, ref.shape[-1]).astype(BF16)

    units = [(r, p) for r in rows for p in range(ATTN_HEADS // 2)]
    munits = [(r, p) for r in rows for p in range(MEM_HEADS // 2)]
    qs = [_stack_pair(q_ref[r], p) for r, p in units]
    scores = [jnp.dot(q8.astype(BF16), pair_rows(kt_ref, r, p),
                      preferred_element_type=F32) + bias_ref[p] for (r, p), q8 in zip(units, qs)]
    mscores = [jnp.dot(_stack_pair(mq_ref[r], p).astype(BF16), pair_rows(mkt_ref, r, p),
                       preferred_element_type=F32) for r, p in munits]
    probs, stats = [], []
    for (r, p), q8, s in zip(units, qs, scores):
        knp = lane_group(kn_ref, r, p)
        slope = jnp.where(upper, float(slopes[2 * p + 1]), float(slopes[2 * p]))
        m = jnp.max(s, axis=-1, keepdims=True)
        s_new = []
        for jp in range(S):
            d = rowj - jp
            sj = jnp.sum(q8 * knp[jp:jp + 1], axis=-1, keepdims=True)
            sj = jnp.where(d >= 0, sj - slope * d.astype(F32), -jnp.inf)
            s_new.append(sj)
            m = jnp.maximum(m, sj)
        pr = cnt * jnp.exp(s - m)
        l = jnp.sum(pr, axis=-1, keepdims=True)
        w_new = []
        for jp in range(S):
            w = jnp.where(rowj == jp, n_cfg, 1.0) * jnp.exp(s_new[jp] - m)
            l = l + w
            w_new.append(w)
        probs.append(pr.astype(BF16))
        stats.append((l, w_new))
    mprobs, mden = [], []
    for s in mscores:
        pr = jnp.exp(s - jnp.max(s, axis=-1, keepdims=True))
        mden.append(jnp.sum(pr, axis=-1, keepdims=True))
        mprobs.append(pr.astype(BF16))
    accs = [lax.dot_general(pr, pair_rows(vt_ref, r, p), nt, preferred_element_type=F32)
            for (r, p), pr in zip(units, probs)]
    maccs = [lax.dot_general(pr, pair_rows(mvt_ref, r, p), nt, preferred_element_type=F32)
             for (r, p), pr in zip(munits, mprobs)]
    outs = []
    for (r, p), acc, (l, w_new) in zip(units, accs, stats):
        vnp = lane_group(vn_ref, r, p)
        for jp in range(S):
            acc = acc + w_new[jp] * vnp[jp:jp + 1]
        outs.append(_unstack_pair(acc / l, S))
    mouts = [_unstack_pair(acc / den, S) for acc, den in zip(maccs, mden)]
    n_p, n_mp = ATTN_HEADS // 2, MEM_HEADS // 2
    for r in rows:
        o_ref[r] = jnp.concatenate(outs[r * n_p:(r + 1) * n_p], axis=-1)
        mo_ref[r] = jnp.concatenate(mouts[r * n_mp:(r + 1) * n_mp], axis=-1)


N_SAMPLE_IN = 10


def _sample_row_specs(sample, base, n_rows, step_of):
    q, mq, cache_kt = sample[0], sample[5], sample[3]
    _, S, W = q.shape
    n_buf = cache_kt.shape[-1]
    assert n_buf >= DILATED_CONFIGS[-1][0] and S <= DILATED_CONFIGS[1][1], \
        "new tokens reach each other through dilation 1 only; the buffer covers every window"
    cnt, bias = (jnp.asarray(t) for t in _sample_tables(S, n_buf))

    def row_in(a):
        return pl.BlockSpec((1,) + a.shape[1:],
                            lambda *g: (base + step_of(*g),) + (0,) * (a.ndim - 1))

    def row_out(a):
        return pl.BlockSpec((1,) + a.shape[1:], lambda *g: (step_of(*g),) + (0,) * (a.ndim - 1))

    def const(a):
        return pl.BlockSpec(a.shape, lambda *g: (0,) * a.ndim)

    args = list(sample) + [cnt, bias]
    assert len(args) == N_SAMPLE_IN
    in_specs = [row_in(a) for a in sample] + [const(cnt), const(bias)]
    out_specs = [row_out(q), row_out(mq)]
    out_shape = [jax.ShapeDtypeStruct((n_rows, S, W), F32),
                 jax.ShapeDtypeStruct((n_rows, S, MEM_W), F32)]
    return args, in_specs, out_specs, out_shape


def _sample_attention(sample, base, n_rows):
    args, in_specs, out_specs, out_shape = _sample_row_specs(sample, base, n_rows, lambda i: i)
    return pl.pallas_call(
        _sample_attn_kernel,
        grid=(n_rows,),
        in_specs=in_specs,
        out_specs=tuple(out_specs),
        out_shape=tuple(out_shape),
        compiler_params=_params(1),
        name="sample_attn",
    )(*args)


def _sample_finish_kernel(x_ref, cb_ref, u0_ref, u1_ref, u2_ref, attn_ref, mem_ref, z_ref,
                          cw_ref, wout_ref, gf_ref, y_ref, mix):
    cw = cw_ref[...]
    conv = cw[0:1] * u0_ref[...] + cw[1:2] * u1_ref[...] + cw[2:3] * u2_ref[...]
    z = z_ref[...].astype(F32)
    lo, hi = CONV_W, CONV_W + ATTN_W
    mix[:, 0:lo] = (cb_ref[...].astype(F32) * conv * _silu(z[:, 0:lo])).astype(BF16)
    mix[:, lo:hi] = (attn_ref[...] * _silu(z[:, lo:hi])).astype(BF16)
    mix[:, hi:] = (mem_ref[...] * _silu(z[:, hi:])).astype(BF16)
    y_ref[...] = _gate_project_norm(x_ref[...], mix, wout_ref, gf_ref)


def _sample_finish(x, cb, u_taps, attn, mem, z, conv_w, w_out_bf16, g_final):
    rows, D = x.shape
    return pl.pallas_call(
        _sample_finish_kernel,
        out_shape=jax.ShapeDtypeStruct((rows, D), F32),
        scratch_shapes=[pltpu.VMEM((rows, D_MIX), BF16)],
        compiler_params=pltpu.CompilerParams(vmem_limit_bytes=VMEM_LIMIT),
        name="sample_finish",
    )(x, cb, *u_taps, attn, mem, z, conv_w, w_out_bf16, g_final.reshape(1, D))


def kernel(x_prompt, x_sample, mem_prompt, cache_win_k, cache_win_v, cache_conv, cache_mem_k,
           cache_mem_v, g_in, w_in, conv_w, g_mem, w_mem_kv, w_out, g_final):
    assert g_in.shape[0] == 1, "the final RMSNorm is fused into the single layer's tail"
    B, T, D = x_prompt.shape
    DB, S, _ = x_sample.shape
    n_keep = min(DILATED_CONFIGS[-1][0], T)
    w_in_b = w_in[0].astype(BF16)
    w_out_b = w_out[0].astype(BF16)
    dils = tuple(d for _, d in DILATED_CONFIGS)
    wkvt_b = _transposed_columns(w_in[0], _C_K, _C_MQ)
    cb, u, q, k, v, kt32, vt32, mq, z = _project(x_prompt, g_in[0], w_in_b, wkvt_b, n_keep, True,
                                                 dils, Q_SCALE * LOG2E)
    mkt32, mvt32, mk, mv = _mem_kv(mem_prompt, g_mem[0], w_mem_kv[0].astype(BF16))

    def heads_last(t, heads):
        return jnp.transpose(t.reshape(t.shape[0], heads, HEAD_DIM, -1), (0, 3, 1, 2))[None]

    def heads_first(c):
        return jnp.transpose(c, (0, 2, 3, 1))

    p_win_k = heads_last(kt32, ATTN_HEADS)
    p_win_v = heads_last(vt32, ATTN_HEADS)
    p_conv = u[None, :, -(CONV_WIDTH - 1):]
    p_mem_k = heads_last(mkt32, MEM_HEADS)
    p_mem_v = heads_last(mvt32, MEM_HEADS)
    rows = DB * S
    cb_s, u_s, q_s, _, _, k32, v32, mq_s, z_s = _project(
        x_sample.reshape(1, rows, D), g_in[0], w_in_b, wkvt_b, rows, False, (1,), Q_SCALE)
    k32 = k32.reshape(DB, S, ATTN_W)
    v32 = v32.reshape(DB, S, ATTN_W)
    sample = (q_s[0].reshape(DB, S, ATTN_W).astype(F32), k32, v32,
              heads_first(cache_win_k[0]), heads_first(cache_win_v[0]),
              mq_s.reshape(DB, S, MEM_W).astype(F32),
              heads_first(cache_mem_k[0]), heads_first(cache_mem_v[0]))
    n_attn = B * (T // ATTN_TILE)
    n_tail = B * (T // ROW_TILE)
    n_own = DB - n_attn - n_tail
    assert n_own > 0
    attn, attn_a, mem_a = _prompt_attention(q, k, v, sample, 0)
    y_prompt, attn_t, mem_t = _prompt_finish(x_prompt, cb, u, attn, mq, z, mk, mv, conv_w[0],
                                             w_out_b, g_final, sample, n_attn)
    attn_o, mem_o = _sample_attention(sample, n_attn + n_tail, n_own)
    attn_s = jnp.concatenate([attn_a, attn_t, attn_o], axis=0)
    mem_s = jnp.concatenate([mem_a, mem_t, mem_o], axis=0)
    full = jnp.concatenate([cache_conv[0], u_s.reshape(DB, S, CONV_W)], axis=1)
    taps = [full[:, t:t + S].reshape(rows, CONV_W) for t in range(CONV_WIDTH)]
    y_sample = _sample_finish(x_sample.reshape(rows, D), cb_s.reshape(rows, CONV_W), taps,
                              attn_s.reshape(rows, ATTN_W), mem_s.reshape(rows, MEM_W),
                              z_s.reshape(rows, D_MIX), conv_w[0], w_out_b,
                              g_final).reshape(DB, S, D)
    s_win_k = k32.reshape(1, DB, S, ATTN_HEADS, HEAD_DIM)
    s_win_v = v32.reshape(1, DB, S, ATTN_HEADS, HEAD_DIM)
    s_conv = full[None, :, -(CONV_WIDTH - 1):]
    return (y_prompt, y_sample, p_win_k, p_win_v, p_conv, p_mem_k, p_mem_v,
            s_win_k, s_win_v, s_conv)
```

```python
import functools

import numpy as np
import jax
import jax.numpy as jnp
from jax import lax
from jax.experimental import pallas as pl
from jax.experimental.pallas import tpu as pltpu

F32 = jnp.float32
BF16 = jnp.bfloat16

HEAD_DIM = 64
ATTN_HEADS = 8
MEM_HEADS = 4
CONV_W = 256
ATTN_W = 512
MEM_W = 256
D_MIX = 1024
CONV_WIDTH = 3
DILATED_CONFIGS = ((128, 1), (512, 4), (2048, 16))
SPAN = 128
ALIBI_MAX = 8.0
RMS_EPS = 1e-6
Q_SCALE = HEAD_DIM ** -0.5
LOG2E = float(np.log2(np.e))

LANES = 128
SUBLANES = 8
VMEM_LIMIT = 56 * 1024 * 1024

_C_CB, _C_CC, _C_Q, _C_K, _C_V, _C_MQ, _C_Z, _C_END = 0, 256, 768, 1280, 1792, 2304, 2560, 3584

ROW_TILE = 512
ATTN_TILE = 512


def _slopes():
    h = np.arange(ATTN_HEADS, dtype=np.float64) + 1.0
    return np.exp2(-ALIBI_MAX * h / ATTN_HEADS)


def _params(n_axes):
    return pltpu.CompilerParams(
        dimension_semantics=("arbitrary",) * n_axes,
        vmem_limit_bytes=VMEM_LIMIT)


def _transpose_cast_kernel(w_ref, o_ref):
    o_ref[...] = w_ref[...].T.astype(BF16)


def _transposed_columns(w, lo, hi):
    K = w.shape[0]
    blk = 2 * LANES
    assert lo % blk == 0 and hi % blk == 0
    return pl.pallas_call(
        _transpose_cast_kernel,
        grid=((hi - lo) // blk,),
        in_specs=[pl.BlockSpec((K, blk), lambda i: (0, lo // blk + i))],
        out_specs=pl.BlockSpec((blk, K), lambda i: (i, 0)),
        out_shape=jax.ShapeDtypeStruct((hi - lo, K), BF16),
        compiler_params=_params(1),
        name="wkv_t",
    )(w)


def _fold_store(res, out_refs, slabs, dils):
    rows = res.shape[0]
    out_refs[0][0, 0] = res.astype(BF16)
    if len(dils) == 1:
        return
    n_slab = res.shape[1] // LANES
    src, prev_d = slabs[0], 1
    for s in range(n_slab):
        src[s] = res[:, s * LANES:(s + 1) * LANES]
    for idx, (d, o_ref) in enumerate(zip(dils[1:], out_refs[1:])):
        step, n = d // prev_d, rows // d
        dst = slabs[idx + 1] if idx + 2 < len(dils) else None
        for rp in range(prev_d):
            for q in range(step):
                r = q * prev_d + rp
                for s in range(n_slab):
                    val = src[s, pl.ds(rp * (rows // prev_d) + q, n, stride=step), :]
                    o_ref[0, r, :, s * LANES:(s + 1) * LANES] = val.astype(BF16)
                    if dst is not None:
                        dst[s, pl.ds(r * n, n), :] = val
        src, prev_d = dst, d


def _proj_kernel(x_ref, g_ref, w_ref, wkvt_ref, *refs, first, keep_transposed, dils, q_scale,
                 hosts_sample):
    n = len(dils)
    if hosts_sample:
        sample_ins, refs = refs[:N_SAMPLE_IN], refs[N_SAMPLE_IN:]
        sample_outs = refs[6 + 3 * n:8 + 3 * n]
        refs = refs[:6 + 3 * n] + refs[8 + 3 * n:]
        _sample_attn_kernel(*sample_ins, *sample_outs)
    cb_ref, u_ref = refs[0:2]
    q_refs, k_refs, v_refs = refs[2:2 + n], refs[2 + n:2 + 2 * n], refs[2 + 2 * n:2 + 3 * n]
    k32_ref, v32_ref, mq_ref, z_ref = refs[2 + 3 * n:6 + 3 * n]
    scratch = refs[6 + 3 * n:]
    slabs = [scratch[a * (n - 1):(a + 1) * (n - 1)] for a in range(3)]
    x = x_ref[0]
    r = lax.rsqrt(jnp.mean(x * x, axis=-1, keepdims=True) + RMS_EPS)
    h = (x * r * g_ref[...]).astype(BF16)

    def seg(lo, hi):
        return jnp.dot(h, w_ref[:, lo:hi], preferred_element_type=F32)

    cb_ref[0] = seg(_C_CB, _C_CC).astype(BF16)
    cch = seg(_C_CC, _C_Q)
    u_ref[0] = cch[:, :CONV_W] * cch[:, CONV_W:]
    _fold_store(seg(_C_Q, _C_K) * q_scale, q_refs, slabs[0], dils)
    k = seg(_C_K, _C_V)
    _fold_store(k, k_refs, slabs[1], dils)
    v = seg(_C_V, _C_MQ)
    _fold_store(v, v_refs, slabs[2], dils)
    mq_ref[0] = (seg(_C_MQ, _C_Z) * q_scale).astype(BF16)
    z_ref[0] = seg(_C_Z, _C_END).astype(BF16)
    if keep_transposed:
        @pl.when(pl.program_id(1) >= first)
        def _():
            nt = (((1,), (1,)), ((), ()))
            k32_ref[0] = lax.dot_general(wkvt_ref[0:ATTN_W], h, nt, preferred_element_type=F32)
            v32_ref[0] = lax.dot_general(wkvt_ref[ATTN_W:], h, nt, preferred_element_type=F32)
    else:
        k32_ref[0] = k
        v32_ref[0] = v


def _project(x, g, w_bf16, wkvt_bf16, n_keep, keep_transposed, dils, q_scale, sample=None,
             sample_base=0):
    B, T, D = x.shape
    tm = ROW_TILE
    nt = T // tm
    first = (T - n_keep) // tm
    n = len(dils)

    def row(width):
        return pl.BlockSpec((1, tm, width), lambda b, i: (b, i, 0))

    folded_specs = [pl.BlockSpec((1, d, tm // d, ATTN_W), lambda b, i: (b, 0, i, 0)) for d in dils]
    folded_shapes = [jax.ShapeDtypeStruct((B, d, T // d, ATTN_W), BF16) for d in dils]

    if keep_transposed:
        keep = pl.BlockSpec((1, ATTN_W, tm), lambda b, i: (b, 0, jnp.maximum(i - first, 0)))
        keep_shape = (B, ATTN_W, n_keep)
    else:
        keep = pl.BlockSpec((1, tm, ATTN_W), lambda b, i: (b, jnp.maximum(i - first, 0), 0))
        keep_shape = (B, n_keep, ATTN_W)
    out_shape = (
        [jax.ShapeDtypeStruct((B, T, CONV_W), BF16),
         jax.ShapeDtypeStruct((B, T, CONV_W), F32)]
        + folded_shapes * 3
        + [jax.ShapeDtypeStruct(keep_shape, F32),
           jax.ShapeDtypeStruct(keep_shape, F32),
           jax.ShapeDtypeStruct((B, T, MEM_W), BF16),
           jax.ShapeDtypeStruct((B, T, D_MIX), BF16)])
    out_specs = ([row(CONV_W), row(CONV_W)] + folded_specs * 3
                 + [keep, keep, row(MEM_W), row(D_MIX)])
    scratch = [pltpu.VMEM((ATTN_W // LANES, tm, LANES), F32)] * (3 * (n - 1))
    once = pl.Buffered(1)
    in_specs = [row(D),
                pl.BlockSpec((1, D), lambda b, i: (0, 0)),
                pl.BlockSpec(w_bf16.shape, lambda b, i: (0, 0), pipeline_mode=once),
                pl.BlockSpec(wkvt_bf16.shape, lambda b, i: (0, 0), pipeline_mode=once)]
    args = [x, g.reshape(1, D), w_bf16, wkvt_bf16]
    if sample is not None:
        s_args, s_in, s_out, s_shape = _sample_row_specs(sample, sample_base, B * nt,
                                                         lambda b, i: b * nt + i)
        args, in_specs = args + s_args, in_specs + s_in
        out_specs, out_shape = out_specs + s_out, out_shape + s_shape
    res = pl.pallas_call(
        functools.partial(_proj_kernel, first=first, keep_transposed=keep_transposed, dils=dils,
                          q_scale=q_scale, hosts_sample=sample is not None),
        grid=(B, nt),
        in_specs=in_specs,
        out_specs=tuple(out_specs),
        out_shape=tuple(out_shape),
        scratch_shapes=scratch,
        compiler_params=_params(2),
        name="proj",
    )(*args)
    cb, u = res[0:2]
    q, k, v = res[2:2 + n], res[2 + n:2 + 2 * n], res[2 + 2 * n:2 + 3 * n]
    k32, v32, mq, z = res[2 + 3 * n:6 + 3 * n]
    return (cb, u, q, k, v, k32, v32, mq, z) + tuple(res[6 + 3 * n:])


def _memkv_kernel(x_ref, g_ref, w_ref, k32_ref, v32_ref, k_ref, v_ref):
    x = x_ref[...]
    r = lax.rsqrt(jnp.mean(x * x, axis=-1, keepdims=True) + RMS_EPS)
    h = (x * r * g_ref[...]).astype(BF16)
    kv = jnp.dot(h, w_ref[...], preferred_element_type=F32)
    k_ref[...] = kv[:, :MEM_W].astype(BF16)
    v_ref[...] = kv[:, MEM_W:].astype(BF16)
    n_mem = k32_ref.shape[2]
    for b in range(k32_ref.shape[0]):
        kvt = kv[b * n_mem:(b + 1) * n_mem].T
        k32_ref[b] = kvt[:MEM_W]
        v32_ref[b] = kvt[MEM_W:]


def _mem_kv(mem, g, w_bf16):
    B, M, D = mem.shape
    rows = B * M
    k32, v32, k, v = pl.pallas_call(
        _memkv_kernel,
        out_shape=(jax.ShapeDtypeStruct((B, MEM_W, M), F32),
                   jax.ShapeDtypeStruct((B, MEM_W, M), F32),
                   jax.ShapeDtypeStruct((rows, MEM_W), BF16),
                   jax.ShapeDtypeStruct((rows, MEM_W), BF16)),
        compiler_params=pltpu.CompilerParams(vmem_limit_bytes=VMEM_LIMIT),
        name="memkv",
    )(mem.reshape(rows, D), g.reshape(1, D), w_bf16)
    return k32, v32, k.reshape(B, M, MEM_W), v.reshape(B, M, MEM_W)


def _band_bias_table():
    qi = np.arange(SPAN)[:, None] + SPAN
    ki = np.arange(2 * SPAN)[None, :]
    dist = qi - ki
    valid = (dist >= 0) & (dist <= SPAN)
    slopes = _slopes()
    tabs = []
    for _, dil in DILATED_CONFIGS:
        for first in (False, True):
            ok = valid & (ki >= SPAN) if first else valid
            for h in range(ATTN_HEADS):
                bias = -(slopes[h] * dil) * dist * LOG2E
                tabs.append(np.where(ok, bias, -np.inf))
    return np.stack(tabs).astype(np.float32)


def _pair_attention(qb, kb, vb, bias_lo, bias_hi):
    lane = lax.broadcasted_iota(jnp.int32, qb.shape, 1)
    lo = lane < HEAD_DIM
    zero = jnp.zeros_like(qb)
    v_ones = jnp.concatenate([vb, jnp.ones_like(vb)], axis=1)
    accs, ms, ls = [], [], []
    for sel, bias in ((lo, bias_lo), (~lo, bias_hi)):
        qm = jnp.where(sel, qb, zero)
        s = lax.dot_general(qm, kb, (((1,), (1,)), ((), ())), preferred_element_type=F32)
        if bias is not None:
            s = s + bias
        m = jnp.max(s, axis=-1, keepdims=True)
        p = jnp.exp2(s - m)
        acc = jnp.dot(p.astype(BF16), v_ones, preferred_element_type=F32)
        ms.append(m)
        accs.append(acc[:, :LANES])
        ls.append(acc[:, LANES:])
    o = jnp.where(lo, accs[0], accs[1]) / jnp.where(lo, ls[0], ls[1])
    return o, ms, ls


def _attn_kernel(*refs, steps):
    ins = refs[:15]
    tab_ref = refs[15]
    sample_ins = refs[16:16 + N_SAMPLE_IN]
    outs = refs[16 + N_SAMPLE_IN:22 + N_SAMPLE_IN]
    sample_outs = refs[22 + N_SAMPLE_IN:24 + N_SAMPLE_IN]
    kcat, vcat = refs[24 + N_SAMPLE_IN:]
    s = pl.program_id(1)
    nblk = ATTN_TILE // SPAN
    _sample_attn_kernel(*sample_ins, *sample_outs)
    for c in range(3):
        _, kp, ko, vp, vo = ins[5 * c:5 * c + 5]
        kcat[c, 0:SPAN] = kp[0, 0]
        kcat[c, SPAN:] = ko[0, 0, 0:SPAN]
        vcat[c, 0:SPAN] = vp[0, 0]
        vcat[c, SPAN:] = vo[0, 0, 0:SPAN]

    for j in range(nblk):
        rows = slice(j * SPAN, (j + 1) * SPAN)
        keys = slice((j - 1) * SPAN, (j + 1) * SPAN)
        for c, (_, dil) in enumerate(DILATED_CONFIGS):
            q_ref, _, ko, _, vo = ins[5 * c:5 * c + 5]
            o_ref, l_ref = outs[2 * c], outs[2 * c + 1]
            units_per_residue = steps // dil
            if j == 0:
                base = (c * 2 + (s % units_per_residue == 0).astype(jnp.int32)) * ATTN_HEADS
            else:
                base = c * 2 * ATTN_HEADS
            lane = lax.broadcasted_iota(jnp.int32, (SPAN, LANES), 1)
            m8 = jnp.zeros((SPAN, LANES), F32)
            l8 = jnp.ones((SPAN, LANES), F32)
            for p in range(ATTN_HEADS // 2):
                lanes = slice(p * LANES, (p + 1) * LANES)
                qb = q_ref[0, 0, rows, lanes]
                if j == 0:
                    kb, vb = kcat[c, :, lanes], vcat[c, :, lanes]
                else:
                    kb, vb = ko[0, 0, keys, lanes], vo[0, 0, keys, lanes]
                o, ms, ls = _pair_attention(qb, kb, vb, tab_ref[base + 2 * p],
                                            tab_ref[base + 2 * p + 1])
                o_ref[0, 0, rows, lanes] = o.astype(BF16)
                for half in range(2):
                    m8 = jnp.where(lane == 2 * p + half, ms[half], m8)
                    l8 = jnp.where(lane == 2 * p + half, ls[half], l8)
            l_ref[0, 0, rows, :] = m8 + jnp.log2(l8)


def _prompt_attention(q, k, v, sample, sample_base):
    B, _, T, W = q[0].shape
    steps = T // ATTN_TILE
    prev_per_tile = ATTN_TILE // SPAN
    args, in_specs, out_specs, out_shape = [], [], [], []
    for c, (_, dil) in enumerate(DILATED_CONFIGS):
        L = T // dil
        upr = L // ATTN_TILE

        def own(b, s, upr=upr):
            return (b, s // upr, s % upr, 0)

        def prev(b, s, upr=upr):
            return (b, s // upr, jnp.maximum((s % upr) * prev_per_tile - 1, 0), 0)

        shape = (B, dil, L, W)
        assert q[c].shape == shape
        args += [q[c], k[c], k[c], v[c], v[c]]
        in_specs += [pl.BlockSpec((1, 1, ATTN_TILE, W), own),
                     pl.BlockSpec((1, 1, SPAN, W), prev),
                     pl.BlockSpec((1, 1, ATTN_TILE, W), own),
                     pl.BlockSpec((1, 1, SPAN, W), prev),
                     pl.BlockSpec((1, 1, ATTN_TILE, W), own)]
        out_specs += [pl.BlockSpec((1, 1, ATTN_TILE, W), own),
                      pl.BlockSpec((1, 1, ATTN_TILE, LANES), own)]
        out_shape += [jax.ShapeDtypeStruct(shape, BF16),
                      jax.ShapeDtypeStruct((B, dil, L, LANES), F32)]
    tab = jnp.asarray(_band_bias_table())
    args.append(tab)
    in_specs.append(pl.BlockSpec(tab.shape, lambda b, s: (0, 0, 0)))
    s_args, s_in, s_out, s_shape = _sample_row_specs(sample, sample_base, B * steps,
                                                     lambda b, s: b * steps + s)
    res = pl.pallas_call(
        functools.partial(_attn_kernel, steps=steps),
        grid=(B, steps),
        in_specs=in_specs + s_in,
        out_specs=tuple(out_specs + s_out),
        out_shape=tuple(out_shape + s_shape),
        scratch_shapes=[pltpu.VMEM((3, 2 * SPAN, W), BF16)] * 2,
        compiler_params=_params(2),
        name="prompt_attn",
    )(*args, *s_args)
    return list(res[:6]), res[6], res[7]


def _silu(z):
    return z / (1.0 + jnp.exp(-z))


def _gate_project_norm(x, mix_ref, wout_ref, gf_ref):
    y = x + jnp.dot(mix_ref[...], wout_ref[...], preferred_element_type=F32)
    r = lax.rsqrt(jnp.mean(y * y, axis=-1, keepdims=True) + RMS_EPS)
    return y * r * gf_ref[...]


def _unfold(src_ref, slabs, d, tmp=None, step=4):
    n_slab, rows = slabs.shape[0], slabs.shape[1]

    def piece(r, s):
        return src_ref[0, r, :, s * LANES:(s + 1) * LANES].astype(F32)

    if d <= step:
        for r in range(d):
            for s in range(n_slab):
                slabs[s, pl.ds(r, rows // d, stride=d), :] = piece(r, s)
        return
    prev_d, n = d // step, rows // d
    per = rows // prev_d
    for rp in range(prev_d):
        for q in range(step):
            for s in range(n_slab):
                tmp[s, pl.ds(rp * per + q, n, stride=step), :] = piece(q * prev_d + rp, s)
    for rp in range(prev_d):
        for s in range(n_slab):
            slabs[s, pl.ds(rp, per, stride=prev_d), :] = tmp[s, pl.ds(rp * per, per), :]


def _head_expander():
    e = np.zeros((2 * LANES, ATTN_W), np.float32)
    for h in range(ATTN_HEADS):
        e[h, h * HEAD_DIM:(h + 1) * HEAD_DIM] = 1.0
        e[LANES + h, h * HEAD_DIM:(h + 1) * HEAD_DIM] = 1.0
    return e


N_FINISH_IN = 18


def _finish_kernel(*refs):
    (x_ref, cb_ref, u_ref, up_ref, o1_ref, l1_ref, o4_ref, l4_ref, o16_ref, l16_ref,
     mq_ref, z_ref, mk_ref, mv_ref, cw_ref, wout_ref, gf_ref, ex_ref) = refs[:N_FINISH_IN]
    sample_ins = refs[N_FINISH_IN:N_FINISH_IN + N_SAMPLE_IN]
    y_ref = refs[N_FINISH_IN + N_SAMPLE_IN]
    sample_outs = refs[N_FINISH_IN + N_SAMPLE_IN + 1:N_FINISH_IN + N_SAMPLE_IN + 3]
    ucat, mix, o4s, l4s, o16s, l16s, otmp, ltmp = refs[N_FINISH_IN + N_SAMPLE_IN + 3:]
    _sample_attn_kernel(*sample_ins, *sample_outs)
    i = pl.program_id(1)
    tm = ROW_TILE
    ucat[0:SUBLANES] = jnp.where(i == 0, 0.0, up_ref[0])
    ucat[SUBLANES:] = u_ref[0]
    cw = cw_ref[...]
    conv = (cw[0:1] * ucat[pl.ds(SUBLANES - 2, tm)]
            + cw[1:2] * ucat[pl.ds(SUBLANES - 1, tm)]
            + cw[2:3] * ucat[pl.ds(SUBLANES, tm)])
    conv_y = cb_ref[0].astype(F32) * conv
    z = z_ref[0].astype(F32)
    mix[:, 0:CONV_W] = (conv_y * _silu(z[:, 0:CONV_W])).astype(BF16)
    dil4, dil16 = DILATED_CONFIGS[1][1], DILATED_CONFIGS[2][1]
    _unfold(o4_ref, o4s, dil4)
    _unfold(l4_ref, l4s, dil4)
    _unfold(o16_ref, o16s, dil16, otmp)
    _unfold(l16_ref, l16s, dil16, ltmp)
    lo, hi = CONV_W, CONV_W + ATTN_W
    l1, l4, l16 = l1_ref[0, 0], l4s[0], l16s[0]
    mx = jnp.maximum(jnp.maximum(l1, l4), l16)
    e1, e4, e16 = jnp.exp2(l1 - mx), jnp.exp2(l4 - mx), jnp.exp2(l16 - mx)
    den = e1 + e4 + e16

    def spread(w):
        w_hi = w.astype(BF16)
        w_lo = (w - w_hi.astype(F32)).astype(BF16)
        return jnp.dot(jnp.concatenate([w_hi, w_lo], axis=1), ex_ref[...],
                       preferred_element_type=F32)

    a1, a4, a16 = spread(e1 / den), spread(e4 / den), spread(e16 / den)
    for s in range(ATTN_W // LANES):
        lanes = slice(s * LANES, (s + 1) * LANES)
        attn = (a1[:, lanes] * o1_ref[0, 0, :, lanes].astype(F32) + a4[:, lanes] * o4s[s]
                + a16[:, lanes] * o16s[s])
        a = lo + s * LANES
        mix[:, a:a + LANES] = (attn * _silu(z[:, a:a + LANES])).astype(BF16)
    for p in range(MEM_HEADS // 2):
        lanes = slice(p * LANES, (p + 1) * LANES)
        o, _, _ = _pair_attention(mq_ref[0, :, lanes], mk_ref[0, :, lanes], mv_ref[0, :, lanes],
                                  None, None)
        a = hi + p * LANES
        mix[:, a:a + LANES] = (o * _silu(z[:, a:a + LANES])).astype(BF16)
    y_ref[0] = _gate_project_norm(x_ref[0], mix, wout_ref, gf_ref)


def _prompt_finish(x, cb, u, attn, mq, z, mk, mv, conv_w, w_out_bf16, g_final, sample,
                   sample_base):
    B, T, D = x.shape
    tm = ROW_TILE

    def row(width):
        return pl.BlockSpec((1, tm, width), lambda b, i: (b, i, 0))

    def const(shape):
        return pl.BlockSpec(shape, lambda b, i: (0,) * len(shape))

    prev8 = pl.BlockSpec((1, SUBLANES, CONV_W),
                         lambda b, i: (b, jnp.maximum(i * (tm // SUBLANES) - 1, 0), 0))
    memspec = pl.BlockSpec((1,) + mk.shape[1:], lambda b, i: (b, 0, 0))
    in_specs = [row(D), row(CONV_W), row(CONV_W), prev8]
    for _, dil in DILATED_CONFIGS:
        in_specs += [pl.BlockSpec((1, dil, tm // dil, ATTN_W), lambda b, i: (b, 0, i, 0)),
                     pl.BlockSpec((1, dil, tm // dil, LANES), lambda b, i: (b, 0, i, 0))]
    expander = jnp.asarray(_head_expander(), BF16)
    in_specs += [row(MEM_W), row(D_MIX), memspec, memspec,
                 const(conv_w.shape), const(w_out_bf16.shape), const((1, D)),
                 const(expander.shape)]
    assert len(in_specs) == N_FINISH_IN
    o_slabs = pltpu.VMEM((ATTN_W // LANES, tm, LANES), F32)
    l_slabs = pltpu.VMEM((1, tm, LANES), F32)
    nt = T // tm
    s_args, s_in, s_out, s_shape = _sample_row_specs(sample, sample_base, B * nt,
                                                     lambda b, i: b * nt + i)
    return pl.pallas_call(
        _finish_kernel,
        grid=(B, nt),
        in_specs=in_specs + s_in,
        out_specs=tuple([row(D)] + s_out),
        out_shape=tuple([jax.ShapeDtypeStruct((B, T, D), F32)] + s_shape),
        scratch_shapes=[pltpu.VMEM((tm + SUBLANES, CONV_W), F32),
                        pltpu.VMEM((tm, D_MIX), BF16),
                        o_slabs, l_slabs, o_slabs, l_slabs, o_slabs, l_slabs],
        compiler_params=_params(2),
        name="prompt_finish",
    )(x, cb, u, u, *attn, mq, z, mk, mv, conv_w, w_out_bf16, g_final.reshape(1, D), expander,
      *s_args)


def _sample_tables(S, n_buf):
    j = np.arange(S)[:, None]
    t = np.arange(n_buf)[None, :]
    dist = n_buf + j - t
    count = np.zeros(dist.shape)
    for win, dil in DILATED_CONFIGS:
        count += (dist % dil == 0) & (dist <= win)
    bias = np.where(count > 0, -_slopes()[:, None, None] * dist, -np.inf)
    bias = bias.reshape(ATTN_HEADS // 2, 2 * S, n_buf)
    count = np.concatenate([count, count], axis=0)
    return count.astype(np.float32), bias.astype(np.float32)


def _stack_pair(a, p):
    ap = a[:, p * LANES:(p + 1) * LANES]
    lo = lax.broadcasted_iota(jnp.int32, ap.shape, 1) < HEAD_DIM
    zero = jnp.zeros_like(ap)
    return jnp.concatenate([jnp.where(lo, ap, zero), jnp.where(lo, zero, ap)], axis=0)


def _unstack_pair(o, S):
    lo = lax.broadcasted_iota(jnp.int32, (S, LANES), 1) < HEAD_DIM
    return jnp.where(lo, o[:S], o[S:])


def _sample_attn_kernel(q_ref, kn_ref, vn_ref, kt_ref, vt_ref, mq_ref, mkt_ref, mvt_ref,
                        cnt_ref, bias_ref, o_ref, mo_ref):
    R, S = q_ref.shape[0], q_ref.shape[1]
    nt = (((1,), (1,)), ((), ()))
    slopes = _slopes()
    n_cfg = float(len(DILATED_CONFIGS))
    row = lax.broadcasted_iota(jnp.int32, (2 * S, 1), 0)
    upper = row >= S
    rowj = jnp.where(upper, row - S, row)
    cnt = cnt_ref[...]
    rows = range(R)

    def lane_group(ref, r, p):
        return ref[r][:, p * LANES:(p + 1) * LANES]

    def pair_rows(ref, r, p):
        return ref[r, 2 * p:2 * p + 2].reshape(2 * HEAD_DIM, ref.shape[-1]).astype(BF16)

    units = [(r, p) for r in rows for p in range(ATTN_HEADS // 2)]
    munits = [(r, p) for r in rows for p in range(MEM_HEADS // 2)]
    qs = [_stack_pair(q_ref[r], p) for r, p in units]
    scores = [jnp.dot(q8.astype(BF16), pair_rows(kt_ref, r, p),
                      preferred_element_type=F32) + bias_ref[p] for (r, p), q8 in zip(units, qs)]
    mscores = [jnp.dot(_stack_pair(mq_ref[r], p).astype(BF16), pair_rows(mkt_ref, r, p),
                       preferred_element_type=F32) for r, p in munits]
    probs, stats = [], []
    for (r, p), q8, s in zip(units, qs, scores):
        knp = lane_group(kn_ref, r, p)
        slope = jnp.where(upper, float(slopes[2 * p + 1]), float(slopes[2 * p]))
        m = jnp.max(s, axis=-1, keepdims=True)
        s_new = []
        for jp in range(S):
            d = rowj - jp
            sj = jnp.sum(q8 * knp[jp:jp + 1], axis=-1, keepdims=True)
            sj = jnp.where(d >= 0, sj - slope * d.astype(F32), -jnp.inf)
            s_new.append(sj)
            m = jnp.maximum(m, sj)
        pr = cnt * jnp.exp(s - m)
        l = jnp.sum(pr, axis=-1, keepdims=True)
        w_new = []
        for jp in range(S):
            w = jnp.where(rowj == jp, n_cfg, 1.0) * jnp.exp(s_new[jp] - m)
            l = l + w
            w_new.append(w)
        probs.append(pr.astype(BF16))
        stats.append((l, w_new))
    mprobs, mden = [], []
    for s in mscores:
        pr = jnp.exp(s - jnp.max(s, axis=-1, keepdims=True))
        mden.append(jnp.sum(pr, axis=-1, keepdims=True))
        mprobs.append(pr.astype(BF16))
    accs = [lax.dot_general(pr, pair_rows(vt_ref, r, p), nt, preferred_element_type=F32)
            for (r, p), pr in zip(units, probs)]
    maccs = [lax.dot_general(pr, pair_rows(mvt_ref, r, p), nt, preferred_element_type=F32)
             for (r, p), pr in zip(munits, mprobs)]
    outs = []
    for (r, p), acc, (l, w_new) in zip(units, accs, stats):
        vnp = lane_group(vn_ref, r, p)
        for jp in range(S):
            acc = acc + w_new[jp] * vnp[jp:jp + 1]
        outs.append(_unstack_pair(acc / l, S))
    mouts = [_unstack_pair(acc / den, S) for acc, den in zip(maccs, mden)]
    n_p, n_mp = ATTN_HEADS // 2, MEM_HEADS // 2
    for r in rows:
        o_ref[r] = jnp.concatenate(outs[r * n_p:(r + 1) * n_p], axis=-1)
        mo_ref[r] = jnp.concatenate(mouts[r * n_mp:(r + 1) * n_mp], axis=-1)


N_SAMPLE_IN = 10


def _sample_row_specs(sample, base, n_rows, step_of):
    q, mq, cache_kt = sample[0], sample[5], sample[3]
    _, S, W = q.shape
    n_buf = cache_kt.shape[-1]
    assert n_buf >= DILATED_CONFIGS[-1][0] and S <= DILATED_CONFIGS[1][1], \
        "new tokens reach each other through dilation 1 only; the buffer covers every window"
    cnt, bias = (jnp.asarray(t) for t in _sample_tables(S, n_buf))

    def row_in(a):
        return pl.BlockSpec((1,) + a.shape[1:],
                            lambda *g: (base + step_of(*g),) + (0,) * (a.ndim - 1))

    def row_out(a):
        return pl.BlockSpec((1,) + a.shape[1:], lambda *g: (step_of(*g),) + (0,) * (a.ndim - 1))

    def const(a):
        return pl.BlockSpec(a.shape, lambda *g: (0,) * a.ndim)

    args = list(sample) + [cnt, bias]
    assert len(args) == N_SAMPLE_IN
    in_specs = [row_in(a) for a in sample] + [const(cnt), const(bias)]
    out_specs = [row_out(q), row_out(mq)]
    out_shape = [jax.ShapeDtypeStruct((n_rows, S, W), F32),
                 jax.ShapeDtypeStruct((n_rows, S, MEM_W), F32)]
    return args, in_specs, out_specs, out_shape


def _sample_attention(sample, base, n_rows):
    args, in_specs, out_specs, out_shape = _sample_row_specs(sample, base, n_rows, lambda i: i)
    return pl.pallas_call(
        _sample_attn_kernel,
        grid=(n_rows,),
        in_specs=in_specs,
        out_specs=tuple(out_specs),
        out_shape=tuple(out_shape),
        compiler_params=_params(1),
        name="sample_attn",
    )(*args)


def _sample_finish_kernel(x_ref, cb_ref, u0_ref, u1_ref, u2_ref, attn_ref, mem_ref, z_ref,
                          cw_ref, wout_ref, gf_ref, y_ref, mix):
    cw = cw_ref[...]
    conv = cw[0:1] * u0_ref[...] + cw[1:2] * u1_ref[...] + cw[2:3] * u2_ref[...]
    z = z_ref[...].astype(F32)
    lo, hi = CONV_W, CONV_W + ATTN_W
    mix[:, 0:lo] = (cb_ref[...].astype(F32) * conv * _silu(z[:, 0:lo])).astype(BF16)
    mix[:, lo:hi] = (attn_ref[...] * _silu(z[:, lo:hi])).astype(BF16)
    mix[:, hi:] = (mem_ref[...] * _silu(z[:, hi:])).astype(BF16)
    y_ref[...] = _gate_project_norm(x_ref[...], mix, wout_ref, gf_ref)


def _sample_finish(x, cb, u_taps, attn, mem, z, conv_w, w_out_bf16, g_final):
    rows, D = x.shape
    return pl.pallas_call(
        _sample_finish_kernel,
        out_shape=jax.ShapeDtypeStruct((rows, D), F32),
        scratch_shapes=[pltpu.VMEM((rows, D_MIX), BF16)],
        compiler_params=pltpu.CompilerParams(vmem_limit_bytes=VMEM_LIMIT),
        name="sample_finish",
    )(x, cb, *u_taps, attn, mem, z, conv_w, w_out_bf16, g_final.reshape(1, D))


def kernel(x_prompt, x_sample, mem_prompt, cache_win_k, cache_win_v, cache_conv, cache_mem_k,
           cache_mem_v, g_in, w_in, conv_w, g_mem, w_mem_kv, w_out, g_final):
    assert g_in.shape[0] == 1, "the final RMSNorm is fused into the single layer's tail"
    B, T, D = x_prompt.shape
    DB, S, _ = x_sample.shape
    n_keep = min(DILATED_CONFIGS[-1][0], T)
    w_in_b = w_in[0].astype(BF16)
    w_out_b = w_out[0].astype(BF16)
    dils = tuple(d for _, d in DILATED_CONFIGS)
    wkvt_b = _transposed_columns(w_in[0], _C_K, _C_MQ)

    def heads_last(t, heads):
        return jnp.transpose(t.reshape(t.shape[0], heads, HEAD_DIM, -1), (0, 3, 1, 2))[None]

    def heads_first(c):
        return jnp.transpose(c, (0, 2, 3, 1))

    rows = DB * S
    cb_s, u_s, q_s, _, _, k32, v32, mq_s, z_s = _project(
        x_sample.reshape(1, rows, D), g_in[0], w_in_b, wkvt_b, rows, False, (1,), Q_SCALE)
    k32 = k32.reshape(DB, S, ATTN_W)
    v32 = v32.reshape(DB, S, ATTN_W)
    sample = (q_s[0].reshape(DB, S, ATTN_W).astype(F32), k32, v32,
              heads_first(cache_win_k[0]), heads_first(cache_win_v[0]),
              mq_s.reshape(DB, S, MEM_W).astype(F32),
              heads_first(cache_mem_k[0]), heads_first(cache_mem_v[0]))
    n_proj = n_tail = B * (T // ROW_TILE)
    n_attn = B * (T // ATTN_TILE)
    n_own = DB - n_proj - n_attn - n_tail
    assert n_own > 0
    cb, u, q, k, v, kt32, vt32, mq, z, attn_p, mem_p = _project(
        x_prompt, g_in[0], w_in_b, wkvt_b, n_keep, True, dils, Q_SCALE * LOG2E, sample, 0)
    mkt32, mvt32, mk, mv = _mem_kv(mem_prompt, g_mem[0], w_mem_kv[0].astype(BF16))
    attn, attn_a, mem_a = _prompt_attention(q, k, v, sample, n_proj)
    y_prompt, attn_t, mem_t = _prompt_finish(x_prompt, cb, u, attn, mq, z, mk, mv, conv_w[0],
                                             w_out_b, g_final, sample, n_proj + n_attn)
    attn_o, mem_o = _sample_attention(sample, n_proj + n_attn + n_tail, n_own)
    attn_s = jnp.concatenate([attn_p, attn_a, attn_t, attn_o], axis=0)
    mem_s = jnp.concatenate([mem_p, mem_a, mem_t, mem_o], axis=0)
    p_win_k = heads_last(kt32, ATTN_HEADS)
    p_win_v = heads_last(vt32, ATTN_HEADS)
    p_conv = u[None, :, -(CONV_WIDTH - 1):]
    p_mem_k = heads_last(mkt32, MEM_HEADS)
    p_mem_v = heads_last(mvt32, MEM_HEADS)
    full = jnp.concatenate([cache_conv[0], u_s.reshape(DB, S, CONV_W)], axis=1)
    taps = [full[:, t:t + S].reshape(rows, CONV_W) for t in range(CONV_WIDTH)]
    y_sample = _sample_finish(x_sample.reshape(rows, D), cb_s.reshape(rows, CONV_W), taps,
                              attn_s.reshape(rows, ATTN_W), mem_s.reshape(rows, MEM_W),
                              z_s.reshape(rows, D_MIX), conv_w[0], w_out_b,
                              g_final).reshape(DB, S, D)
    s_win_k = k32.reshape(1, DB, S, ATTN_HEADS, HEAD_DIM)
    s_win_v = v32.reshape(1, DB, S, ATTN_HEADS, HEAD_DIM)
    s_conv = full[None, :, -(CONV_WIDTH - 1):]
    return (y_prompt, y_sample, p_win_k, p_win_v, p_conv, p_mem_k, p_mem_v,
            s_win_k, s_win_v, s_conv)
```

```python
import functools

import numpy as np
import jax
import jax.numpy as jnp
from jax import lax
from jax.experimental import pallas as pl
from jax.experimental.pallas import tpu as pltpu

F32 = jnp.float32
BF16 = jnp.bfloat16

HEAD_DIM = 64
ATTN_HEADS = 8
MEM_HEADS = 4
CONV_W = 256
ATTN_W = 512
MEM_W = 256
D_MIX = 1024
CONV_WIDTH = 3
DILATED_CONFIGS = ((128, 1), (512, 4), (2048, 16))
SPAN = 128
ALIBI_MAX = 8.0
RMS_EPS = 1e-6
Q_SCALE = HEAD_DIM ** -0.5
LOG2E = float(np.log2(np.e))

LANES = 128
SUBLANES = 8
VMEM_LIMIT = 56 * 1024 * 1024

_C_CB, _C_CC, _C_Q, _C_K, _C_V, _C_MQ, _C_Z, _C_END = 0, 256, 768, 1280, 1792, 2304, 2560, 3584

ROW_TILE = 512
ATTN_TILE = 512


def _slopes():
    h = np.arange(ATTN_HEADS, dtype=np.float64) + 1.0
    return np.exp2(-ALIBI_MAX * h / ATTN_HEADS)


def _params(n_axes):
    return pltpu.CompilerParams(
        dimension_semantics=("arbitrary",) * n_axes,
        vmem_limit_bytes=VMEM_LIMIT)


def _transpose_cast_kernel(w_ref, o_ref):
    o_ref[...] = w_ref[...].T.astype(BF16)


def _transposed_columns(w, lo, hi):
    K = w.shape[0]
    blk = 2 * LANES
    assert lo % blk == 0 and hi % blk == 0
    return pl.pallas_call(
        _transpose_cast_kernel,
        grid=((hi - lo) // blk,),
        in_specs=[pl.BlockSpec((K, blk), lambda i: (0, lo // blk + i))],
        out_specs=pl.BlockSpec((blk, K), lambda i: (i, 0)),
        out_shape=jax.ShapeDtypeStruct((hi - lo, K), BF16),
        compiler_params=_params(1),
        name="wkv_t",
    )(w)


def _fold_store(res, out_refs, slabs, dils):
    rows = res.shape[0]
    out_refs[0][0, 0] = res.astype(BF16)
    if len(dils) == 1:
        return
    n_slab = res.shape[1] // LANES
    src, prev_d = slabs[0], 1
    for s in range(n_slab):
        src[s] = res[:, s * LANES:(s + 1) * LANES]
    for idx, (d, o_ref) in enumerate(zip(dils[1:], out_refs[1:])):
        step, n = d // prev_d, rows // d
        dst = slabs[idx + 1] if idx + 2 < len(dils) else None
        for rp in range(prev_d):
            for q in range(step):
                r = q * prev_d + rp
                for s in range(n_slab):
                    val = src[s, pl.ds(rp * (rows // prev_d) + q, n, stride=step), :]
                    o_ref[0, r, :, s * LANES:(s + 1) * LANES] = val.astype(BF16)
                    if dst is not None:
                        dst[s, pl.ds(r * n, n), :] = val
        src, prev_d = dst, d


def _proj_kernel(x_ref, g_ref, w_ref, wkvt_ref, *refs, first, keep_transposed, dils, q_scale,
                 hosts_sample):
    n = len(dils)
    if hosts_sample:
        sample_ins, refs = refs[:N_SAMPLE_IN], refs[N_SAMPLE_IN:]
        sample_outs = refs[6 + 3 * n:8 + 3 * n]
        refs = refs[:6 + 3 * n] + refs[8 + 3 * n:]
        step = pl.program_id(0) * pl.num_programs(1) + pl.program_id(1)
        _sample_attn_kernel(*sample_ins, *sample_outs, batch_row=hosts_sample[1] + step)
    cb_ref, u_ref = refs[0:2]
    q_refs, k_refs, v_refs = refs[2:2 + n], refs[2 + n:2 + 2 * n], refs[2 + 2 * n:2 + 3 * n]
    k32_ref, v32_ref, mq_ref, z_ref = refs[2 + 3 * n:6 + 3 * n]
    scratch = refs[6 + 3 * n:]
    slabs = [scratch[a * (n - 1):(a + 1) * (n - 1)] for a in range(3)]
    x = x_ref[0]
    r = lax.rsqrt(jnp.mean(x * x, axis=-1, keepdims=True) + RMS_EPS)
    h = (x * r * g_ref[...]).astype(BF16)

    def seg(lo, hi):
        return jnp.dot(h, w_ref[:, lo:hi], preferred_element_type=F32)

    cb_ref[0] = seg(_C_CB, _C_CC).astype(BF16)
    cch = seg(_C_CC, _C_Q)
    u_ref[0] = cch[:, :CONV_W] * cch[:, CONV_W:]
    _fold_store(seg(_C_Q, _C_K) * q_scale, q_refs, slabs[0], dils)
    k = seg(_C_K, _C_V)
    _fold_store(k, k_refs, slabs[1], dils)
    v = seg(_C_V, _C_MQ)
    _fold_store(v, v_refs, slabs[2], dils)
    mq_ref[0] = (seg(_C_MQ, _C_Z) * q_scale).astype(BF16)
    z_ref[0] = seg(_C_Z, _C_END).astype(BF16)
    if keep_transposed:
        @pl.when(pl.program_id(1) >= first)
        def _():
            nt = (((1,), (1,)), ((), ()))
            k32_ref[0] = lax.dot_general(wkvt_ref[0:ATTN_W], h, nt, preferred_element_type=F32)
            v32_ref[0] = lax.dot_general(wkvt_ref[ATTN_W:], h, nt, preferred_element_type=F32)
    else:
        k32_ref[0] = k
        v32_ref[0] = v


def _project(x, g, w_bf16, wkvt_bf16, n_keep, keep_transposed, dils, q_scale, sample=None,
             sample_base=0):
    B, T, D = x.shape
    tm = ROW_TILE
    nt = T // tm
    first = (T - n_keep) // tm
    n = len(dils)

    def row(width):
        return pl.BlockSpec((1, tm, width), lambda b, i: (b, i, 0))

    folded_specs = [pl.BlockSpec((1, d, tm // d, ATTN_W), lambda b, i: (b, 0, i, 0)) for d in dils]
    folded_shapes = [jax.ShapeDtypeStruct((B, d, T // d, ATTN_W), BF16) for d in dils]

    if keep_transposed:
        keep = pl.BlockSpec((1, ATTN_W, tm), lambda b, i: (b, 0, jnp.maximum(i - first, 0)))
        keep_shape = (B, ATTN_W, n_keep)
    else:
        keep = pl.BlockSpec((1, tm, ATTN_W), lambda b, i: (b, jnp.maximum(i - first, 0), 0))
        keep_shape = (B, n_keep, ATTN_W)
    out_shape = (
        [jax.ShapeDtypeStruct((B, T, CONV_W), BF16),
         jax.ShapeDtypeStruct((B, T, CONV_W), F32)]
        + folded_shapes * 3
        + [jax.ShapeDtypeStruct(keep_shape, F32),
           jax.ShapeDtypeStruct(keep_shape, F32),
           jax.ShapeDtypeStruct((B, T, MEM_W), BF16),
           jax.ShapeDtypeStruct((B, T, D_MIX), BF16)])
    out_specs = ([row(CONV_W), row(CONV_W)] + folded_specs * 3
                 + [keep, keep, row(MEM_W), row(D_MIX)])
    scratch = [pltpu.VMEM((ATTN_W // LANES, tm, LANES), F32)] * (3 * (n - 1))
    once = pl.Buffered(1)
    in_specs = [row(D),
                pl.BlockSpec((1, D), lambda b, i: (0, 0)),
                pl.BlockSpec(w_bf16.shape, lambda b, i: (0, 0), pipeline_mode=once),
                pl.BlockSpec(wkvt_bf16.shape, lambda b, i: (0, 0), pipeline_mode=once)]
    args = [x, g.reshape(1, D), w_bf16, wkvt_bf16]
    if sample is not None:
        s_args, s_in, s_out, s_shape = _sample_row_specs(sample, sample_base, B * nt,
                                                         lambda b, i: b * nt + i)
        args, in_specs = args + s_args, in_specs + s_in
        out_specs, out_shape = out_specs + s_out, out_shape + s_shape
    res = pl.pallas_call(
        functools.partial(_proj_kernel, first=first, keep_transposed=keep_transposed, dils=dils,
                          q_scale=q_scale,
                          hosts_sample=(True, sample_base) if sample is not None else None),
        grid=(B, nt),
        in_specs=in_specs,
        out_specs=tuple(out_specs),
        out_shape=tuple(out_shape),
        scratch_shapes=scratch,
        compiler_params=_params(2),
        name="proj",
    )(*args)
    cb, u = res[0:2]
    q, k, v = res[2:2 + n], res[2 + n:2 + 2 * n], res[2 + 2 * n:2 + 3 * n]
    k32, v32, mq, z = res[2 + 3 * n:6 + 3 * n]
    return (cb, u, q, k, v, k32, v32, mq, z) + tuple(res[6 + 3 * n:])


def _memkv_kernel(x_ref, g_ref, w_ref, k32_ref, v32_ref, k_ref, v_ref):
    x = x_ref[...]
    r = lax.rsqrt(jnp.mean(x * x, axis=-1, keepdims=True) + RMS_EPS)
    h = (x * r * g_ref[...]).astype(BF16)
    kv = jnp.dot(h, w_ref[...], preferred_element_type=F32)
    k_ref[...] = kv[:, :MEM_W].astype(BF16)
    v_ref[...] = kv[:, MEM_W:].astype(BF16)
    n_mem = k32_ref.shape[2]
    for b in range(k32_ref.shape[0]):
        kvt = kv[b * n_mem:(b + 1) * n_mem].T
        k32_ref[b] = kvt[:MEM_W]
        v32_ref[b] = kvt[MEM_W:]


def _mem_kv(mem, g, w_bf16):
    B, M, D = mem.shape
    rows = B * M
    k32, v32, k, v = pl.pallas_call(
        _memkv_kernel,
        out_shape=(jax.ShapeDtypeStruct((B, MEM_W, M), F32),
                   jax.ShapeDtypeStruct((B, MEM_W, M), F32),
                   jax.ShapeDtypeStruct((rows, MEM_W), BF16),
                   jax.ShapeDtypeStruct((rows, MEM_W), BF16)),
        compiler_params=pltpu.CompilerParams(vmem_limit_bytes=VMEM_LIMIT),
        name="memkv",
    )(mem.reshape(rows, D), g.reshape(1, D), w_bf16)
    return k32, v32, k.reshape(B, M, MEM_W), v.reshape(B, M, MEM_W)


def _band_bias_table():
    qi = np.arange(SPAN)[:, None] + SPAN
    ki = np.arange(2 * SPAN)[None, :]
    dist = qi - ki
    valid = (dist >= 0) & (dist <= SPAN)
    slopes = _slopes()
    tabs = []
    for _, dil in DILATED_CONFIGS:
        for first in (False, True):
            ok = valid & (ki >= SPAN) if first else valid
            for h in range(ATTN_HEADS):
                bias = -(slopes[h] * dil) * dist * LOG2E
                tabs.append(np.where(ok, bias, -np.inf))
    return np.stack(tabs).astype(np.float32)


def _pair_attention(qb, kb, vb, bias_lo, bias_hi):
    lane = lax.broadcasted_iota(jnp.int32, qb.shape, 1)
    lo = lane < HEAD_DIM
    zero = jnp.zeros_like(qb)
    v_ones = jnp.concatenate([vb, jnp.ones_like(vb)], axis=1)
    accs, ms, ls = [], [], []
    for sel, bias in ((lo, bias_lo), (~lo, bias_hi)):
        qm = jnp.where(sel, qb, zero)
        s = lax.dot_general(qm, kb, (((1,), (1,)), ((), ())), preferred_element_type=F32)
        if bias is not None:
            s = s + bias
        m = jnp.max(s, axis=-1, keepdims=True)
        p = jnp.exp2(s - m)
        acc = jnp.dot(p.astype(BF16), v_ones, preferred_element_type=F32)
        ms.append(m)
        accs.append(acc[:, :LANES])
        ls.append(acc[:, LANES:])
    o = jnp.where(lo, accs[0], accs[1]) / jnp.where(lo, ls[0], ls[1])
    return o, ms, ls


def _attn_kernel(*refs, steps, sample_base):
    ins = refs[:15]
    tab_ref = refs[15]
    sample_ins = refs[16:16 + N_SAMPLE_IN]
    outs = refs[16 + N_SAMPLE_IN:22 + N_SAMPLE_IN]
    sample_outs = refs[22 + N_SAMPLE_IN:24 + N_SAMPLE_IN]
    kcat, vcat = refs[24 + N_SAMPLE_IN:]
    s = pl.program_id(1)
    nblk = ATTN_TILE // SPAN
    _sample_attn_kernel(*sample_ins, *sample_outs,
                        batch_row=sample_base + pl.program_id(0) * steps + s)
    for c in range(3):
        _, kp, ko, vp, vo = ins[5 * c:5 * c + 5]
        kcat[c, 0:SPAN] = kp[0, 0]
        kcat[c, SPAN:] = ko[0, 0, 0:SPAN]
        vcat[c, 0:SPAN] = vp[0, 0]
        vcat[c, SPAN:] = vo[0, 0, 0:SPAN]

    for j in range(nblk):
        rows = slice(j * SPAN, (j + 1) * SPAN)
        keys = slice((j - 1) * SPAN, (j + 1) * SPAN)
        for c, (_, dil) in enumerate(DILATED_CONFIGS):
            q_ref, _, ko, _, vo = ins[5 * c:5 * c + 5]
            o_ref, l_ref = outs[2 * c], outs[2 * c + 1]
            units_per_residue = steps // dil
            if j == 0:
                base = (c * 2 + (s % units_per_residue == 0).astype(jnp.int32)) * ATTN_HEADS
            else:
                base = c * 2 * ATTN_HEADS
            lane = lax.broadcasted_iota(jnp.int32, (SPAN, LANES), 1)
            m8 = jnp.zeros((SPAN, LANES), F32)
            l8 = jnp.ones((SPAN, LANES), F32)
            for p in range(ATTN_HEADS // 2):
                lanes = slice(p * LANES, (p + 1) * LANES)
                qb = q_ref[0, 0, rows, lanes]
                if j == 0:
                    kb, vb = kcat[c, :, lanes], vcat[c, :, lanes]
                else:
                    kb, vb = ko[0, 0, keys, lanes], vo[0, 0, keys, lanes]
                o, ms, ls = _pair_attention(qb, kb, vb, tab_ref[base + 2 * p],
                                            tab_ref[base + 2 * p + 1])
                o_ref[0, 0, rows, lanes] = o.astype(BF16)
                for half in range(2):
                    m8 = jnp.where(lane == 2 * p + half, ms[half], m8)
                    l8 = jnp.where(lane == 2 * p + half, ls[half], l8)
            l_ref[0, 0, rows, :] = m8 + jnp.log2(l8)


def _prompt_attention(q, k, v, sample, sample_base):
    B, _, T, W = q[0].shape
    steps = T // ATTN_TILE
    prev_per_tile = ATTN_TILE // SPAN
    args, in_specs, out_specs, out_shape = [], [], [], []
    for c, (_, dil) in enumerate(DILATED_CONFIGS):
        L = T // dil
        upr = L // ATTN_TILE

        def own(b, s, upr=upr):
            return (b, s // upr, s % upr, 0)

        def prev(b, s, upr=upr):
            return (b, s // upr, jnp.maximum((s % upr) * prev_per_tile - 1, 0), 0)

        shape = (B, dil, L, W)
        assert q[c].shape == shape
        args += [q[c], k[c], k[c], v[c], v[c]]
        in_specs += [pl.BlockSpec((1, 1, ATTN_TILE, W), own),
                     pl.BlockSpec((1, 1, SPAN, W), prev),
                     pl.BlockSpec((1, 1, ATTN_TILE, W), own),
                     pl.BlockSpec((1, 1, SPAN, W), prev),
                     pl.BlockSpec((1, 1, ATTN_TILE, W), own)]
        out_specs += [pl.BlockSpec((1, 1, ATTN_TILE, W), own),
                      pl.BlockSpec((1, 1, ATTN_TILE, LANES), own)]
        out_shape += [jax.ShapeDtypeStruct(shape, BF16),
                      jax.ShapeDtypeStruct((B, dil, L, LANES), F32)]
    tab = jnp.asarray(_band_bias_table())
    args.append(tab)
    in_specs.append(pl.BlockSpec(tab.shape, lambda b, s: (0, 0, 0)))
    s_args, s_in, s_out, s_shape = _sample_row_specs(sample, sample_base, B * steps,
                                                     lambda b, s: b * steps + s)
    res = pl.pallas_call(
        functools.partial(_attn_kernel, steps=steps, sample_base=sample_base),
        grid=(B, steps),
        in_specs=in_specs + s_in,
        out_specs=tuple(out_specs + s_out),
        out_shape=tuple(out_shape + s_shape),
        scratch_shapes=[pltpu.VMEM((3, 2 * SPAN, W), BF16)] * 2,
        compiler_params=_params(2),
        name="prompt_attn",
    )(*args, *s_args)
    return list(res[:6]), res[6], res[7]


def _silu(z):
    return z / (1.0 + jnp.exp(-z))


def _gate_project_norm(x, mix_ref, wout_ref, gf_ref):
    y = x + jnp.dot(mix_ref[...], wout_ref[...], preferred_element_type=F32)
    r = lax.rsqrt(jnp.mean(y * y, axis=-1, keepdims=True) + RMS_EPS)
    return y * r * gf_ref[...]


def _unfold(src_ref, slabs, d, tmp=None, step=4):
    n_slab, rows = slabs.shape[0], slabs.shape[1]

    def piece(r, s):
        return src_ref[0, r, :, s * LANES:(s + 1) * LANES].astype(F32)

    if d <= step:
        for r in range(d):
            for s in range(n_slab):
                slabs[s, pl.ds(r, rows // d, stride=d), :] = piece(r, s)
        return
    prev_d, n = d // step, rows // d
    per = rows // prev_d
    for rp in range(prev_d):
        for q in range(step):
            for s in range(n_slab):
                tmp[s, pl.ds(rp * per + q, n, stride=step), :] = piece(q * prev_d + rp, s)
    for rp in range(prev_d):
        for s in range(n_slab):
            slabs[s, pl.ds(rp, per, stride=prev_d), :] = tmp[s, pl.ds(rp * per, per), :]


def _head_expander():
    e = np.zeros((2 * LANES, ATTN_W), np.float32)
    for h in range(ATTN_HEADS):
        e[h, h * HEAD_DIM:(h + 1) * HEAD_DIM] = 1.0
        e[LANES + h, h * HEAD_DIM:(h + 1) * HEAD_DIM] = 1.0
    return e


N_FINISH_IN = 18


def _finish_kernel(*refs, sample_base):
    (x_ref, cb_ref, u_ref, up_ref, o1_ref, l1_ref, o4_ref, l4_ref, o16_ref, l16_ref,
     mq_ref, z_ref, mk_ref, mv_ref, cw_ref, wout_ref, gf_ref, ex_ref) = refs[:N_FINISH_IN]
    sample_ins = refs[N_FINISH_IN:N_FINISH_IN + N_SAMPLE_IN]
    y_ref = refs[N_FINISH_IN + N_SAMPLE_IN]
    sample_outs = refs[N_FINISH_IN + N_SAMPLE_IN + 1:N_FINISH_IN + N_SAMPLE_IN + 3]
    ucat, mix, o4s, l4s, o16s, l16s, otmp, ltmp = refs[N_FINISH_IN + N_SAMPLE_IN + 3:]
    i = pl.program_id(1)
    _sample_attn_kernel(*sample_ins, *sample_outs,
                        batch_row=sample_base + pl.program_id(0) * pl.num_programs(1) + i)
    tm = ROW_TILE
    ucat[0:SUBLANES] = jnp.where(i == 0, 0.0, up_ref[0])
    ucat[SUBLANES:] = u_ref[0]
    cw = cw_ref[...]
    conv = (cw[0:1] * ucat[pl.ds(SUBLANES - 2, tm)]
            + cw[1:2] * ucat[pl.ds(SUBLANES - 1, tm)]
            + cw[2:3] * ucat[pl.ds(SUBLANES, tm)])
    conv_y = cb_ref[0].astype(F32) * conv
    z = z_ref[0].astype(F32)
    mix[:, 0:CONV_W] = (conv_y * _silu(z[:, 0:CONV_W])).astype(BF16)
    dil4, dil16 = DILATED_CONFIGS[1][1], DILATED_CONFIGS[2][1]
    _unfold(o4_ref, o4s, dil4)
    _unfold(l4_ref, l4s, dil4)
    _unfold(o16_ref, o16s, dil16, otmp)
    _unfold(l16_ref, l16s, dil16, ltmp)
    lo, hi = CONV_W, CONV_W + ATTN_W
    l1, l4, l16 = l1_ref[0, 0], l4s[0], l16s[0]
    mx = jnp.maximum(jnp.maximum(l1, l4), l16)
    e1, e4, e16 = jnp.exp2(l1 - mx), jnp.exp2(l4 - mx), jnp.exp2(l16 - mx)
    den = e1 + e4 + e16

    def spread(w):
        w_hi = w.astype(BF16)
        w_lo = (w - w_hi.astype(F32)).astype(BF16)
        return jnp.dot(jnp.concatenate([w_hi, w_lo], axis=1), ex_ref[...],
                       preferred_element_type=F32)

    a1, a4, a16 = spread(e1 / den), spread(e4 / den), spread(e16 / den)
    for s in range(ATTN_W // LANES):
        lanes = slice(s * LANES, (s + 1) * LANES)
        attn = (a1[:, lanes] * o1_ref[0, 0, :, lanes].astype(F32) + a4[:, lanes] * o4s[s]
                + a16[:, lanes] * o16s[s])
        a = lo + s * LANES
        mix[:, a:a + LANES] = (attn * _silu(z[:, a:a + LANES])).astype(BF16)
    for p in range(MEM_HEADS // 2):
        lanes = slice(p * LANES, (p + 1) * LANES)
        o, _, _ = _pair_attention(mq_ref[0, :, lanes], mk_ref[0, :, lanes], mv_ref[0, :, lanes],
                                  None, None)
        a = hi + p * LANES
        mix[:, a:a + LANES] = (o * _silu(z[:, a:a + LANES])).astype(BF16)
    y_ref[0] = _gate_project_norm(x_ref[0], mix, wout_ref, gf_ref)


def _prompt_finish(x, cb, u, attn, mq, z, mk, mv, conv_w, w_out_bf16, g_final, sample,
                   sample_base):
    B, T, D = x.shape
    tm = ROW_TILE

    def row(width):
        return pl.BlockSpec((1, tm, width), lambda b, i: (b, i, 0))

    def const(shape):
        return pl.BlockSpec(shape, lambda b, i: (0,) * len(shape))

    prev8 = pl.BlockSpec((1, SUBLANES, CONV_W),
                         lambda b, i: (b, jnp.maximum(i * (tm // SUBLANES) - 1, 0), 0))
    memspec = pl.BlockSpec((1,) + mk.shape[1:], lambda b, i: (b, 0, 0))
    in_specs = [row(D), row(CONV_W), row(CONV_W), prev8]
    for _, dil in DILATED_CONFIGS:
        in_specs += [pl.BlockSpec((1, dil, tm // dil, ATTN_W), lambda b, i: (b, 0, i, 0)),
                     pl.BlockSpec((1, dil, tm // dil, LANES), lambda b, i: (b, 0, i, 0))]
    expander = jnp.asarray(_head_expander(), BF16)
    in_specs += [row(MEM_W), row(D_MIX), memspec, memspec,
                 const(conv_w.shape), const(w_out_bf16.shape), const((1, D)),
                 const(expander.shape)]
    assert len(in_specs) == N_FINISH_IN
    o_slabs = pltpu.VMEM((ATTN_W // LANES, tm, LANES), F32)
    l_slabs = pltpu.VMEM((1, tm, LANES), F32)
    nt = T // tm
    s_args, s_in, s_out, s_shape = _sample_row_specs(sample, sample_base, B * nt,
                                                     lambda b, i: b * nt + i)
    return pl.pallas_call(
        functools.partial(_finish_kernel, sample_base=sample_base),
        grid=(B, nt),
        in_specs=in_specs + s_in,
        out_specs=tuple([row(D)] + s_out),
        out_shape=tuple([jax.ShapeDtypeStruct((B, T, D), F32)] + s_shape),
        scratch_shapes=[pltpu.VMEM((tm + SUBLANES, CONV_W), F32),
                        pltpu.VMEM((tm, D_MIX), BF16),
                        o_slabs, l_slabs, o_slabs, l_slabs, o_slabs, l_slabs],
        compiler_params=_params(2),
        name="prompt_finish",
    )(x, cb, u, u, *attn, mq, z, mk, mv, conv_w, w_out_bf16, g_final.reshape(1, D), expander,
      *s_args)


def _sample_tables(S, n_buf):
    j = np.arange(S)[:, None]
    t = np.arange(n_buf)[None, :]
    dist = n_buf + j - t
    count = np.zeros(dist.shape)
    for win, dil in DILATED_CONFIGS:
        count += (dist % dil == 0) & (dist <= win)
    bias = np.where(count > 0, -_slopes()[:, None, None] * dist, -np.inf)
    bias = bias.reshape(ATTN_HEADS // 2, 2 * S, n_buf)
    count = np.concatenate([count, count], axis=0)
    return count.astype(np.float32), bias.astype(np.float32)


def _stack_pair(a, p):
    ap = a[:, p * LANES:(p + 1) * LANES]
    lo = lax.broadcasted_iota(jnp.int32, ap.shape, 1) < HEAD_DIM
    zero = jnp.zeros_like(ap)
    return jnp.concatenate([jnp.where(lo, ap, zero), jnp.where(lo, zero, ap)], axis=0)


def _unstack_pair(o, S):
    lo = lax.broadcasted_iota(jnp.int32, (S, LANES), 1) < HEAD_DIM
    return jnp.where(lo, o[:S], o[S:])


def _sample_attn_kernel(q_ref, kn_ref, vn_ref, kt_ref, vt_ref, mq_ref, mkt_ref, mvt_ref,
                        cnt_ref, bias_ref, o_ref, mo_ref, *, batch_row):
    R, S = kt_ref.shape[0], o_ref.shape[1]
    nt = (((1,), (1,)), ((), ()))
    slopes = _slopes()
    n_cfg = float(len(DILATED_CONFIGS))
    row = lax.broadcasted_iota(jnp.int32, (2 * S, 1), 0)
    upper = row >= S
    rowj = jnp.where(upper, row - S, row)
    cnt = cnt_ref[...]
    rows = range(R)
    which = batch_row % (SUBLANES // S)

    def tokens(ref):
        blk = ref[...]
        tok = blk[0:S]
        for i in range(1, SUBLANES // S):
            tok = jnp.where(which == i, blk[i * S:(i + 1) * S], tok)
        return tok

    q_tok, kn_tok, vn_tok, mq_tok = (tokens(ref) for ref in (q_ref, kn_ref, vn_ref, mq_ref))

    def lane_group(tok, p):
        return tok[:, p * LANES:(p + 1) * LANES]

    def pair_rows(ref, r, p):
        return ref[r, 2 * p:2 * p + 2].reshape(2 * HEAD_DIM, ref.shape[-1]).astype(BF16)

    units = [(r, p) for r in rows for p in range(ATTN_HEADS // 2)]
    munits = [(r, p) for r in rows for p in range(MEM_HEADS // 2)]
    qs = [_stack_pair(q_tok, p) for r, p in units]
    scores = [jnp.dot(q8.astype(BF16), pair_rows(kt_ref, r, p),
                      preferred_element_type=F32) + bias_ref[p] for (r, p), q8 in zip(units, qs)]
    mscores = [jnp.dot(_stack_pair(mq_tok, p).astype(BF16), pair_rows(mkt_ref, r, p),
                       preferred_element_type=F32) for r, p in munits]
    probs, stats = [], []
    for (r, p), q8, s in zip(units, qs, scores):
        knp = lane_group(kn_tok, p)
        slope = jnp.where(upper, float(slopes[2 * p + 1]), float(slopes[2 * p]))
        m = jnp.max(s, axis=-1, keepdims=True)
        s_new = []
        for jp in range(S):
            d = rowj - jp
            sj = jnp.sum(q8 * knp[jp:jp + 1], axis=-1, keepdims=True)
            sj = jnp.where(d >= 0, sj - slope * d.astype(F32), -jnp.inf)
            s_new.append(sj)
            m = jnp.maximum(m, sj)
        pr = cnt * jnp.exp(s - m)
        l = jnp.sum(pr, axis=-1, keepdims=True)
        w_new = []
        for jp in range(S):
            w = jnp.where(rowj == jp, n_cfg, 1.0) * jnp.exp(s_new[jp] - m)
            l = l + w
            w_new.append(w)
        probs.append(pr.astype(BF16))
        stats.append((l, w_new))
    mprobs, mden = [], []
    for s in mscores:
        pr = jnp.exp(s - jnp.max(s, axis=-1, keepdims=True))
        mden.append(jnp.sum(pr, axis=-1, keepdims=True))
        mprobs.append(pr.astype(BF16))
    accs = [lax.dot_general(pr, pair_rows(vt_ref, r, p), nt, preferred_element_type=F32)
            for (r, p), pr in zip(units, probs)]
    maccs = [lax.dot_general(pr, pair_rows(mvt_ref, r, p), nt, preferred_element_type=F32)
             for (r, p), pr in zip(munits, mprobs)]
    outs = []
    for (r, p), acc, (l, w_new) in zip(units, accs, stats):
        vnp = lane_group(vn_tok, p)
        for jp in range(S):
            acc = acc + w_new[jp] * vnp[jp:jp + 1]
        outs.append(_unstack_pair(acc / l, S))
    mouts = [_unstack_pair(acc / den, S) for acc, den in zip(maccs, mden)]
    n_p, n_mp = ATTN_HEADS // 2, MEM_HEADS // 2
    for r in rows:
        o_ref[r] = jnp.concatenate(outs[r * n_p:(r + 1) * n_p], axis=-1)
        mo_ref[r] = jnp.concatenate(mouts[r * n_mp:(r + 1) * n_mp], axis=-1)


N_SAMPLE_IN = 10


def _sample_row_specs(sample, base, n_rows, step_of):
    q, k_new, v_new, cache_kt, cache_vt, mq, mem_kt, mem_vt = sample
    DB, n_buf = cache_kt.shape[0], cache_kt.shape[-1]
    S, W = q.shape[0] // DB, q.shape[1]
    assert SUBLANES % S == 0, "token rows are fetched in 8-row blocks"
    assert n_buf >= DILATED_CONFIGS[-1][0] and S <= DILATED_CONFIGS[1][1], \
        "new tokens reach each other through dilation 1 only; the buffer covers every window"
    cnt, bias = (jnp.asarray(t) for t in _sample_tables(S, n_buf))
    per_block = SUBLANES // S

    def tokens_in(a):
        return pl.BlockSpec((SUBLANES, a.shape[1]),
                            lambda *g: ((base + step_of(*g)) // per_block, 0))

    def row_in(a):
        return pl.BlockSpec((1,) + a.shape[1:],
                            lambda *g: (base + step_of(*g),) + (0,) * (a.ndim - 1))

    def row_out(width):
        return pl.BlockSpec((1, S, width), lambda *g: (step_of(*g), 0, 0))

    def const(a):
        return pl.BlockSpec(a.shape, lambda *g: (0,) * a.ndim)

    args = list(sample) + [cnt, bias]
    assert len(args) == N_SAMPLE_IN
    in_specs = [tokens_in(q), tokens_in(k_new), tokens_in(v_new), row_in(cache_kt),
                row_in(cache_vt), tokens_in(mq), row_in(mem_kt), row_in(mem_vt),
                const(cnt), const(bias)]
    out_specs = [row_out(W), row_out(MEM_W)]
    out_shape = [jax.ShapeDtypeStruct((n_rows, S, W), F32),
                 jax.ShapeDtypeStruct((n_rows, S, MEM_W), F32)]
    return args, in_specs, out_specs, out_shape


def _sample_attention(sample, base, n_rows):
    args, in_specs, out_specs, out_shape = _sample_row_specs(sample, base, n_rows, lambda i: i)

    def body(*refs):
        _sample_attn_kernel(*refs, batch_row=base + pl.program_id(0))

    return pl.pallas_call(
        body,
        grid=(n_rows,),
        in_specs=in_specs,
        out_specs=tuple(out_specs),
        out_shape=tuple(out_shape),
        compiler_params=_params(1),
        name="sample_attn",
    )(*args)


def _sample_finish_kernel(x_ref, cb_ref, u0_ref, u1_ref, u2_ref, attn_ref, mem_ref, z_ref,
                          cw_ref, wout_ref, gf_ref, y_ref, mix):
    cw = cw_ref[...]
    conv = cw[0:1] * u0_ref[...] + cw[1:2] * u1_ref[...] + cw[2:3] * u2_ref[...]
    z = z_ref[...].astype(F32)
    lo, hi = CONV_W, CONV_W + ATTN_W
    mix[:, 0:lo] = (cb_ref[...].astype(F32) * conv * _silu(z[:, 0:lo])).astype(BF16)
    mix[:, lo:hi] = (attn_ref[...] * _silu(z[:, lo:hi])).astype(BF16)
    mix[:, hi:] = (mem_ref[...] * _silu(z[:, hi:])).astype(BF16)
    y_ref[...] = _gate_project_norm(x_ref[...], mix, wout_ref, gf_ref)


def _sample_finish(x, cb, u_taps, attn, mem, z, conv_w, w_out_bf16, g_final):
    rows, D = x.shape
    return pl.pallas_call(
        _sample_finish_kernel,
        out_shape=jax.ShapeDtypeStruct((rows, D), F32),
        scratch_shapes=[pltpu.VMEM((rows, D_MIX), BF16)],
        compiler_params=pltpu.CompilerParams(vmem_limit_bytes=VMEM_LIMIT),
        name="sample_finish",
    )(x, cb, *u_taps, attn, mem, z, conv_w, w_out_bf16, g_final.reshape(1, D))


def kernel(x_prompt, x_sample, mem_prompt, cache_win_k, cache_win_v, cache_conv, cache_mem_k,
           cache_mem_v, g_in, w_in, conv_w, g_mem, w_mem_kv, w_out, g_final):
    assert g_in.shape[0] == 1, "the final RMSNorm is fused into the single layer's tail"
    B, T, D = x_prompt.shape
    DB, S, _ = x_sample.shape
    n_keep = min(DILATED_CONFIGS[-1][0], T)
    w_in_b = w_in[0].astype(BF16)
    w_out_b = w_out[0].astype(BF16)
    dils = tuple(d for _, d in DILATED_CONFIGS)
    wkvt_b = _transposed_columns(w_in[0], _C_K, _C_MQ)

    def heads_last(t, heads):
        return jnp.transpose(t.reshape(t.shape[0], heads, HEAD_DIM, -1), (0, 3, 1, 2))[None]

    def heads_first(c):
        return jnp.transpose(c, (0, 2, 3, 1))

    rows = DB * S
    cb_s, u_s, q_s, _, _, k32, v32, mq_s, z_s = _project(
        x_sample.reshape(1, rows, D), g_in[0], w_in_b, wkvt_b, rows, False, (1,), Q_SCALE)
    k32 = k32.reshape(rows, ATTN_W)
    v32 = v32.reshape(rows, ATTN_W)
    sample = (q_s[0].reshape(rows, ATTN_W).astype(F32), k32, v32,
              heads_first(cache_win_k[0]), heads_first(cache_win_v[0]),
              mq_s.reshape(rows, MEM_W).astype(F32),
              heads_first(cache_mem_k[0]), heads_first(cache_mem_v[0]))
    n_proj = n_tail = B * (T // ROW_TILE)
    n_attn = B * (T // ATTN_TILE)
    n_own = DB - n_proj - n_attn - n_tail
    assert n_own > 0
    cb, u, q, k, v, kt32, vt32, mq, z, attn_p, mem_p = _project(
        x_prompt, g_in[0], w_in_b, wkvt_b, n_keep, True, dils, Q_SCALE * LOG2E, sample, 0)
    mkt32, mvt32, mk, mv = _mem_kv(mem_prompt, g_mem[0], w_mem_kv[0].astype(BF16))
    attn, attn_a, mem_a = _prompt_attention(q, k, v, sample, n_proj)
    y_prompt, attn_t, mem_t = _prompt_finish(x_prompt, cb, u, attn, mq, z, mk, mv, conv_w[0],
                                             w_out_b, g_final, sample, n_proj + n_attn)
    attn_o, mem_o = _sample_attention(sample, n_proj + n_attn + n_tail, n_own)
    attn_s = jnp.concatenate([attn_p, attn_a, attn_t, attn_o], axis=0)
    mem_s = jnp.concatenate([mem_p, mem_a, mem_t, mem_o], axis=0)
    p_win_k = heads_last(kt32, ATTN_HEADS)
    p_win_v = heads_last(vt32, ATTN_HEADS)
    p_conv = u[None, :, -(CONV_WIDTH - 1):]
    p_mem_k = heads_last(mkt32, MEM_HEADS)
    p_mem_v = heads_last(mvt32, MEM_HEADS)
    full = jnp.concatenate([cache_conv[0], u_s.reshape(DB, S, CONV_W)], axis=1)
    taps = [full[:, t:t + S].reshape(rows, CONV_W) for t in range(CONV_WIDTH)]
    y_sample = _sample_finish(x_sample.reshape(rows, D), cb_s.reshape(rows, CONV_W), taps,
                              attn_s.reshape(rows, ATTN_W), mem_s.reshape(rows, MEM_W),
                              z_s.reshape(rows, D_MIX), conv_w[0], w_out_b,
                              g_final).reshape(DB, S, D)
    s_win_k = k32.reshape(1, DB, S, ATTN_HEADS, HEAD_DIM)
    s_win_v = v32.reshape(1, DB, S, ATTN_HEADS, HEAD_DIM)
    s_conv = full[None, :, -(CONV_WIDTH - 1):]
    return (y_prompt, y_sample, p_win_k, p_win_v, p_conv, p_mem_k, p_mem_v,
            s_win_k, s_win_v, s_conv)
```

```python
import functools

import numpy as np
import jax
import jax.numpy as jnp
from jax import lax
from jax.experimental import pallas as pl
from jax.experimental.pallas import tpu as pltpu

F32 = jnp.float32
BF16 = jnp.bfloat16

HEAD_DIM = 64
ATTN_HEADS = 8
MEM_HEADS = 4
CONV_W = 256
ATTN_W = 512
MEM_W = 256
D_MIX = 1024
CONV_WIDTH = 3
DILATED_CONFIGS = ((128, 1), (512, 4), (2048, 16))
SPAN = 128
ALIBI_MAX = 8.0
RMS_EPS = 1e-6
Q_SCALE = HEAD_DIM ** -0.5
LOG2E = float(np.log2(np.e))

LANES = 128
SUBLANES = 8
VMEM_LIMIT = 56 * 1024 * 1024

_C_CB, _C_CC, _C_Q, _C_K, _C_V, _C_MQ, _C_Z, _C_END = 0, 256, 768, 1280, 1792, 2304, 2560, 3584

ROW_TILE = 512
ATTN_TILE = 512


def _slopes():
    h = np.arange(ATTN_HEADS, dtype=np.float64) + 1.0
    return np.exp2(-ALIBI_MAX * h / ATTN_HEADS)


def _params(n_axes):
    return pltpu.CompilerParams(
        dimension_semantics=("arbitrary",) * n_axes,
        vmem_limit_bytes=VMEM_LIMIT)


def _transpose_cast_kernel(w_ref, o_ref):
    o_ref[...] = w_ref[...].T.astype(BF16)


def _transposed_columns(w, lo, hi):
    K = w.shape[0]
    blk = 2 * LANES
    assert lo % blk == 0 and hi % blk == 0
    return pl.pallas_call(
        _transpose_cast_kernel,
        grid=((hi - lo) // blk,),
        in_specs=[pl.BlockSpec((K, blk), lambda i: (0, lo // blk + i))],
        out_specs=pl.BlockSpec((blk, K), lambda i: (i, 0)),
        out_shape=jax.ShapeDtypeStruct((hi - lo, K), BF16),
        compiler_params=_params(1),
        name="wkv_t",
    )(w)


def _fold_store(res, out_refs, slabs, dils):
    rows = res.shape[0]
    out_refs[0][0, 0] = res.astype(BF16)
    if len(dils) == 1:
        return
    n_slab = res.shape[1] // LANES
    src, prev_d = slabs[0], 1
    for s in range(n_slab):
        src[s] = res[:, s * LANES:(s + 1) * LANES]
    for idx, (d, o_ref) in enumerate(zip(dils[1:], out_refs[1:])):
        step, n = d // prev_d, rows // d
        dst = slabs[idx + 1] if idx + 2 < len(dils) else None
        for rp in range(prev_d):
            for q in range(step):
                r = q * prev_d + rp
                for s in range(n_slab):
                    val = src[s, pl.ds(rp * (rows // prev_d) + q, n, stride=step), :]
                    o_ref[0, r, :, s * LANES:(s + 1) * LANES] = val.astype(BF16)
                    if dst is not None:
                        dst[s, pl.ds(r * n, n), :] = val
        src, prev_d = dst, d


def _proj_kernel(x_ref, g_ref, w_ref, wkvt_ref, *refs, first, keep_transposed, dils, q_scale,
                 hosts_sample, fuse_conv):
    n = len(dils)
    if fuse_conv:
        cw_ref, refs, ucat = refs[0], refs[1:-1], refs[-1]

        @pl.when(pl.program_id(1) == 0)
        def _():
            ucat[0:SUBLANES] = jnp.zeros((SUBLANES, CONV_W), F32)
    if hosts_sample:
        sample_ins, refs = refs[:N_SAMPLE_IN], refs[N_SAMPLE_IN:]
        sample_outs = refs[6 + 3 * n:8 + 3 * n]
        refs = refs[:6 + 3 * n] + refs[8 + 3 * n:]
        step = pl.program_id(0) * pl.num_programs(1) + pl.program_id(1)
        _sample_attn_kernel(*sample_ins, *sample_outs, batch_row=hosts_sample[1] + step)
    cb_ref, u_ref = refs[0:2]
    q_refs, k_refs, v_refs = refs[2:2 + n], refs[2 + n:2 + 2 * n], refs[2 + 2 * n:2 + 3 * n]
    k32_ref, v32_ref, mq_ref, z_ref = refs[2 + 3 * n:6 + 3 * n]
    scratch = refs[6 + 3 * n:]
    slabs = [scratch[a * (n - 1):(a + 1) * (n - 1)] for a in range(3)]
    x = x_ref[0]
    r = lax.rsqrt(jnp.mean(x * x, axis=-1, keepdims=True) + RMS_EPS)
    h = (x * r * g_ref[...]).astype(BF16)

    def seg(lo, hi):
        return jnp.dot(h, w_ref[:, lo:hi], preferred_element_type=F32)

    cb = seg(_C_CB, _C_CC)
    cch = seg(_C_CC, _C_Q)
    u = cch[:, :CONV_W] * cch[:, CONV_W:]
    if fuse_conv:
        rows = u.shape[0]
        ucat[SUBLANES:] = u
        cw = cw_ref[...]
        conv = (cw[0:1] * ucat[pl.ds(SUBLANES - 2, rows)]
                + cw[1:2] * ucat[pl.ds(SUBLANES - 1, rows)]
                + cw[2:3] * ucat[pl.ds(SUBLANES, rows)])
        cb_ref[0] = (cb * conv).astype(BF16)
        u_ref[0] = u[rows - SUBLANES:]
        ucat[0:SUBLANES] = u[rows - SUBLANES:]
    else:
        cb_ref[0] = cb.astype(BF16)
        u_ref[0] = u
    _fold_store(seg(_C_Q, _C_K) * q_scale, q_refs, slabs[0], dils)
    k = seg(_C_K, _C_V)
    _fold_store(k, k_refs, slabs[1], dils)
    v = seg(_C_V, _C_MQ)
    _fold_store(v, v_refs, slabs[2], dils)
    mq_ref[0] = (seg(_C_MQ, _C_Z) * q_scale).astype(BF16)
    z_ref[0] = seg(_C_Z, _C_END).astype(BF16)
    if keep_transposed:
        @pl.when(pl.program_id(1) >= first)
        def _():
            nt = (((1,), (1,)), ((), ()))
            k32_ref[0] = lax.dot_general(wkvt_ref[0:ATTN_W], h, nt, preferred_element_type=F32)
            v32_ref[0] = lax.dot_general(wkvt_ref[ATTN_W:], h, nt, preferred_element_type=F32)
    else:
        k32_ref[0] = k
        v32_ref[0] = v


def _project(x, g, w_bf16, wkvt_bf16, n_keep, keep_transposed, dils, q_scale, sample=None,
             sample_base=0, conv_w=None):
    B, T, D = x.shape
    tm = ROW_TILE
    nt = T // tm
    first = (T - n_keep) // tm
    n = len(dils)

    def row(width):
        return pl.BlockSpec((1, tm, width), lambda b, i: (b, i, 0))

    folded_specs = [pl.BlockSpec((1, d, tm // d, ATTN_W), lambda b, i: (b, 0, i, 0)) for d in dils]
    folded_shapes = [jax.ShapeDtypeStruct((B, d, T // d, ATTN_W), BF16) for d in dils]

    if keep_transposed:
        keep = pl.BlockSpec((1, ATTN_W, tm), lambda b, i: (b, 0, jnp.maximum(i - first, 0)))
        keep_shape = (B, ATTN_W, n_keep)
    else:
        keep = pl.BlockSpec((1, tm, ATTN_W), lambda b, i: (b, jnp.maximum(i - first, 0), 0))
        keep_shape = (B, n_keep, ATTN_W)
    fuse_conv = conv_w is not None
    u_rows = SUBLANES if fuse_conv else tm
    out_shape = (
        [jax.ShapeDtypeStruct((B, T, CONV_W), BF16),
         jax.ShapeDtypeStruct((B, nt * u_rows, CONV_W), F32)]
        + folded_shapes * 3
        + [jax.ShapeDtypeStruct(keep_shape, F32),
           jax.ShapeDtypeStruct(keep_shape, F32),
           jax.ShapeDtypeStruct((B, T, MEM_W), BF16),
           jax.ShapeDtypeStruct((B, T, D_MIX), BF16)])
    out_specs = ([row(CONV_W), pl.BlockSpec((1, u_rows, CONV_W), lambda b, i: (b, i, 0))]
                 + folded_specs * 3 + [keep, keep, row(MEM_W), row(D_MIX)])
    scratch = [pltpu.VMEM((ATTN_W // LANES, tm, LANES), F32)] * (3 * (n - 1))
    once = pl.Buffered(1)
    in_specs = [row(D),
                pl.BlockSpec((1, D), lambda b, i: (0, 0)),
                pl.BlockSpec(w_bf16.shape, lambda b, i: (0, 0), pipeline_mode=once),
                pl.BlockSpec(wkvt_bf16.shape, lambda b, i: (0, 0), pipeline_mode=once)]
    args = [x, g.reshape(1, D), w_bf16, wkvt_bf16]
    if fuse_conv:
        args.append(conv_w)
        in_specs.append(pl.BlockSpec(conv_w.shape, lambda b, i: (0, 0)))
        scratch = scratch + [pltpu.VMEM((tm + SUBLANES, CONV_W), F32)]
    if sample is not None:
        s_args, s_in, s_out, s_shape = _sample_row_specs(sample, sample_base, B * nt,
                                                         lambda b, i: b * nt + i)
        args, in_specs = args + s_args, in_specs + s_in
        out_specs, out_shape = out_specs + s_out, out_shape + s_shape
    res = pl.pallas_call(
        functools.partial(_proj_kernel, first=first, keep_transposed=keep_transposed, dils=dils,
                          q_scale=q_scale,
                          hosts_sample=(True, sample_base) if sample is not None else None,
                          fuse_conv=fuse_conv),
        grid=(B, nt),
        in_specs=in_specs,
        out_specs=tuple(out_specs),
        out_shape=tuple(out_shape),
        scratch_shapes=scratch,
        compiler_params=_params(2),
        name="proj",
    )(*args)
    cb, u = res[0:2]
    q, k, v = res[2:2 + n], res[2 + n:2 + 2 * n], res[2 + 2 * n:2 + 3 * n]
    k32, v32, mq, z = res[2 + 3 * n:6 + 3 * n]
    return (cb, u, q, k, v, k32, v32, mq, z) + tuple(res[6 + 3 * n:])


def _memkv_kernel(x_ref, g_ref, w_ref, k32_ref, v32_ref, k_ref, v_ref):
    x = x_ref[...]
    r = lax.rsqrt(jnp.mean(x * x, axis=-1, keepdims=True) + RMS_EPS)
    h = (x * r * g_ref[...]).astype(BF16)
    kv = jnp.dot(h, w_ref[...], preferred_element_type=F32)
    k_ref[...] = kv[:, :MEM_W].astype(BF16)
    v_ref[...] = kv[:, MEM_W:].astype(BF16)
    n_mem = k32_ref.shape[2]
    for b in range(k32_ref.shape[0]):
        kvt = kv[b * n_mem:(b + 1) * n_mem].T
        k32_ref[b] = kvt[:MEM_W]
        v32_ref[b] = kvt[MEM_W:]


def _mem_kv(mem, g, w_bf16):
    B, M, D = mem.shape
    rows = B * M
    k32, v32, k, v = pl.pallas_call(
        _memkv_kernel,
        out_shape=(jax.ShapeDtypeStruct((B, MEM_W, M), F32),
                   jax.ShapeDtypeStruct((B, MEM_W, M), F32),
                   jax.ShapeDtypeStruct((rows, MEM_W), BF16),
                   jax.ShapeDtypeStruct((rows, MEM_W), BF16)),
        compiler_params=pltpu.CompilerParams(vmem_limit_bytes=VMEM_LIMIT),
        name="memkv",
    )(mem.reshape(rows, D), g.reshape(1, D), w_bf16)
    return k32, v32, k.reshape(B, M, MEM_W), v.reshape(B, M, MEM_W)


def _band_bias_table():
    qi = np.arange(SPAN)[:, None] + SPAN
    ki = np.arange(2 * SPAN)[None, :]
    dist = qi - ki
    valid = (dist >= 0) & (dist <= SPAN)
    slopes = _slopes()
    tabs = []
    for _, dil in DILATED_CONFIGS:
        for first in (False, True):
            ok = valid & (ki >= SPAN) if first else valid
            for h in range(ATTN_HEADS):
                bias = -(slopes[h] * dil) * dist * LOG2E
                tabs.append(np.where(ok, bias, -np.inf))
    return np.stack(tabs).astype(np.float32)


def _pair_attention(qb, kb, vb, bias_lo, bias_hi):
    lane = lax.broadcasted_iota(jnp.int32, qb.shape, 1)
    lo = lane < HEAD_DIM
    zero = jnp.zeros_like(qb)
    v_ones = jnp.concatenate([vb, jnp.ones_like(vb)], axis=1)
    accs, ms, ls = [], [], []
    for sel, bias in ((lo, bias_lo), (~lo, bias_hi)):
        qm = jnp.where(sel, qb, zero)
        s = lax.dot_general(qm, kb, (((1,), (1,)), ((), ())), preferred_element_type=F32)
        if bias is not None:
            s = s + bias
        m = jnp.max(s, axis=-1, keepdims=True)
        p = jnp.exp2(s - m)
        acc = jnp.dot(p.astype(BF16), v_ones, preferred_element_type=F32)
        ms.append(m)
        accs.append(acc[:, :LANES])
        ls.append(acc[:, LANES:])
    o = jnp.where(lo, accs[0], accs[1]) / jnp.where(lo, ls[0], ls[1])
    return o, ms, ls


def _attn_kernel(*refs, steps, sample_base):
    ins = refs[:15]
    tab_ref = refs[15]
    sample_ins = refs[16:16 + N_SAMPLE_IN]
    outs = refs[16 + N_SAMPLE_IN:22 + N_SAMPLE_IN]
    sample_outs = refs[22 + N_SAMPLE_IN:24 + N_SAMPLE_IN]
    kcat, vcat = refs[24 + N_SAMPLE_IN:]
    s = pl.program_id(1)
    nblk = ATTN_TILE // SPAN
    _sample_attn_kernel(*sample_ins, *sample_outs,
                        batch_row=sample_base + pl.program_id(0) * steps + s)
    for c in range(3):
        _, kp, ko, vp, vo = ins[5 * c:5 * c + 5]
        kcat[c, 0:SPAN] = kp[0, 0]
        kcat[c, SPAN:] = ko[0, 0, 0:SPAN]
        vcat[c, 0:SPAN] = vp[0, 0]
        vcat[c, SPAN:] = vo[0, 0, 0:SPAN]

    for j in range(nblk):
        rows = slice(j * SPAN, (j + 1) * SPAN)
        keys = slice((j - 1) * SPAN, (j + 1) * SPAN)
        for c, (_, dil) in enumerate(DILATED_CONFIGS):
            q_ref, _, ko, _, vo = ins[5 * c:5 * c + 5]
            o_ref, l_ref = outs[2 * c], outs[2 * c + 1]
            units_per_residue = steps // dil
            if j == 0:
                base = (c * 2 + (s % units_per_residue == 0).astype(jnp.int32)) * ATTN_HEADS
            else:
                base = c * 2 * ATTN_HEADS
            lane = lax.broadcasted_iota(jnp.int32, (SPAN, LANES), 1)
            m8 = jnp.zeros((SPAN, LANES), F32)
            l8 = jnp.ones((SPAN, LANES), F32)
            for p in range(ATTN_HEADS // 2):
                lanes = slice(p * LANES, (p + 1) * LANES)
                qb = q_ref[0, 0, rows, lanes]
                if j == 0:
                    kb, vb = kcat[c, :, lanes], vcat[c, :, lanes]
                else:
                    kb, vb = ko[0, 0, keys, lanes], vo[0, 0, keys, lanes]
                o, ms, ls = _pair_attention(qb, kb, vb, tab_ref[base + 2 * p],
                                            tab_ref[base + 2 * p + 1])
                o_ref[0, 0, rows, lanes] = o.astype(BF16)
                for half in range(2):
                    m8 = jnp.where(lane == 2 * p + half, ms[half], m8)
                    l8 = jnp.where(lane == 2 * p + half, ls[half], l8)
            l_ref[0, 0, rows, :] = m8 + jnp.log2(l8)


def _prompt_attention(q, k, v, sample, sample_base):
    B, _, T, W = q[0].shape
    steps = T // ATTN_TILE
    prev_per_tile = ATTN_TILE // SPAN
    args, in_specs, out_specs, out_shape = [], [], [], []
    for c, (_, dil) in enumerate(DILATED_CONFIGS):
        L = T // dil
        upr = L // ATTN_TILE

        def own(b, s, upr=upr):
            return (b, s // upr, s % upr, 0)

        def prev(b, s, upr=upr):
            return (b, s // upr, jnp.maximum((s % upr) * prev_per_tile - 1, 0), 0)

        shape = (B, dil, L, W)
        assert q[c].shape == shape
        args += [q[c], k[c], k[c], v[c], v[c]]
        in_specs += [pl.BlockSpec((1, 1, ATTN_TILE, W), own),
                     pl.BlockSpec((1, 1, SPAN, W), prev),
                     pl.BlockSpec((1, 1, ATTN_TILE, W), own),
                     pl.BlockSpec((1, 1, SPAN, W), prev),
                     pl.BlockSpec((1, 1, ATTN_TILE, W), own)]
        out_specs += [pl.BlockSpec((1, 1, ATTN_TILE, W), own),
                      pl.BlockSpec((1, 1, ATTN_TILE, LANES), own)]
        out_shape += [jax.ShapeDtypeStruct(shape, BF16),
                      jax.ShapeDtypeStruct((B, dil, L, LANES), F32)]
    tab = jnp.asarray(_band_bias_table())
    args.append(tab)
    in_specs.append(pl.BlockSpec(tab.shape, lambda b, s: (0, 0, 0)))
    s_args, s_in, s_out, s_shape = _sample_row_specs(sample, sample_base, B * steps,
                                                     lambda b, s: b * steps + s)
    res = pl.pallas_call(
        functools.partial(_attn_kernel, steps=steps, sample_base=sample_base),
        grid=(B, steps),
        in_specs=in_specs + s_in,
        out_specs=tuple(out_specs + s_out),
        out_shape=tuple(out_shape + s_shape),
        scratch_shapes=[pltpu.VMEM((3, 2 * SPAN, W), BF16)] * 2,
        compiler_params=_params(2),
        name="prompt_attn",
    )(*args, *s_args)
    return list(res[:6]), res[6], res[7]


def _silu(z):
    return z / (1.0 + jnp.exp(-z))


def _gate_project_norm(x, mix_ref, wout_ref, gf_ref):
    y = x + jnp.dot(mix_ref[...], wout_ref[...], preferred_element_type=F32)
    r = lax.rsqrt(jnp.mean(y * y, axis=-1, keepdims=True) + RMS_EPS)
    return y * r * gf_ref[...]


def _unfold(src_ref, slabs, d, tmp=None, step=4):
    n_slab, rows = slabs.shape[0], slabs.shape[1]

    def piece(r, s):
        return src_ref[0, r, :, s * LANES:(s + 1) * LANES].astype(F32)

    if d <= step:
        for r in range(d):
            for s in range(n_slab):
                slabs[s, pl.ds(r, rows // d, stride=d), :] = piece(r, s)
        return
    prev_d, n = d // step, rows // d
    per = rows // prev_d
    for rp in range(prev_d):
        for q in range(step):
            for s in range(n_slab):
                tmp[s, pl.ds(rp * per + q, n, stride=step), :] = piece(q * prev_d + rp, s)
    for rp in range(prev_d):
        for s in range(n_slab):
            slabs[s, pl.ds(rp, per, stride=prev_d), :] = tmp[s, pl.ds(rp * per, per), :]


def _head_expander():
    e = np.zeros((2 * LANES, ATTN_W), np.float32)
    for h in range(ATTN_HEADS):
        e[h, h * HEAD_DIM:(h + 1) * HEAD_DIM] = 1.0
        e[LANES + h, h * HEAD_DIM:(h + 1) * HEAD_DIM] = 1.0
    return e


N_FINISH_IN = 15


def _finish_kernel(*refs, sample_base):
    (x_ref, cy_ref, o1_ref, l1_ref, o4_ref, l4_ref, o16_ref, l16_ref,
     mq_ref, z_ref, mk_ref, mv_ref, wout_ref, gf_ref, ex_ref) = refs[:N_FINISH_IN]
    sample_ins = refs[N_FINISH_IN:N_FINISH_IN + N_SAMPLE_IN]
    y_ref = refs[N_FINISH_IN + N_SAMPLE_IN]
    sample_outs = refs[N_FINISH_IN + N_SAMPLE_IN + 1:N_FINISH_IN + N_SAMPLE_IN + 3]
    mix, o4s, l4s, o16s, l16s, otmp, ltmp = refs[N_FINISH_IN + N_SAMPLE_IN + 3:]
    i = pl.program_id(1)
    _sample_attn_kernel(*sample_ins, *sample_outs,
                        batch_row=sample_base + pl.program_id(0) * pl.num_programs(1) + i)
    z = z_ref[0].astype(F32)
    mix[:, 0:CONV_W] = (cy_ref[0].astype(F32) * _silu(z[:, 0:CONV_W])).astype(BF16)
    dil4, dil16 = DILATED_CONFIGS[1][1], DILATED_CONFIGS[2][1]
    _unfold(o4_ref, o4s, dil4)
    _unfold(l4_ref, l4s, dil4)
    _unfold(o16_ref, o16s, dil16, otmp)
    _unfold(l16_ref, l16s, dil16, ltmp)
    lo, hi = CONV_W, CONV_W + ATTN_W
    l1, l4, l16 = l1_ref[0, 0], l4s[0], l16s[0]
    mx = jnp.maximum(jnp.maximum(l1, l4), l16)
    e1, e4, e16 = jnp.exp2(l1 - mx), jnp.exp2(l4 - mx), jnp.exp2(l16 - mx)
    den = e1 + e4 + e16

    def spread(w):
        w_hi = w.astype(BF16)
        w_lo = (w - w_hi.astype(F32)).astype(BF16)
        return jnp.dot(jnp.concatenate([w_hi, w_lo], axis=1), ex_ref[...],
                       preferred_element_type=F32)

    a1, a4, a16 = spread(e1 / den), spread(e4 / den), spread(e16 / den)
    for s in range(ATTN_W // LANES):
        lanes = slice(s * LANES, (s + 1) * LANES)
        attn = (a1[:, lanes] * o1_ref[0, 0, :, lanes].astype(F32) + a4[:, lanes] * o4s[s]
                + a16[:, lanes] * o16s[s])
        a = lo + s * LANES
        mix[:, a:a + LANES] = (attn * _silu(z[:, a:a + LANES])).astype(BF16)
    for p in range(MEM_HEADS // 2):
        lanes = slice(p * LANES, (p + 1) * LANES)
        o, _, _ = _pair_attention(mq_ref[0, :, lanes], mk_ref[0, :, lanes], mv_ref[0, :, lanes],
                                  None, None)
        a = hi + p * LANES
        mix[:, a:a + LANES] = (o * _silu(z[:, a:a + LANES])).astype(BF16)
    y_ref[0] = _gate_project_norm(x_ref[0], mix, wout_ref, gf_ref)


def _prompt_finish(x, conv_y, attn, mq, z, mk, mv, w_out_bf16, g_final, sample, sample_base):
    B, T, D = x.shape
    tm = ROW_TILE

    def row(width):
        return pl.BlockSpec((1, tm, width), lambda b, i: (b, i, 0))

    def const(shape):
        return pl.BlockSpec(shape, lambda b, i: (0,) * len(shape))

    memspec = pl.BlockSpec((1,) + mk.shape[1:], lambda b, i: (b, 0, 0))
    in_specs = [row(D), row(CONV_W)]
    for _, dil in DILATED_CONFIGS:
        in_specs += [pl.BlockSpec((1, dil, tm // dil, ATTN_W), lambda b, i: (b, 0, i, 0)),
                     pl.BlockSpec((1, dil, tm // dil, LANES), lambda b, i: (b, 0, i, 0))]
    expander = jnp.asarray(_head_expander(), BF16)
    in_specs += [row(MEM_W), row(D_MIX), memspec, memspec,
                 const(w_out_bf16.shape), const((1, D)), const(expander.shape)]
    assert len(in_specs) == N_FINISH_IN
    o_slabs = pltpu.VMEM((ATTN_W // LANES, tm, LANES), F32)
    l_slabs = pltpu.VMEM((1, tm, LANES), F32)
    nt = T // tm
    s_args, s_in, s_out, s_shape = _sample_row_specs(sample, sample_base, B * nt,
                                                     lambda b, i: b * nt + i)
    return pl.pallas_call(
        functools.partial(_finish_kernel, sample_base=sample_base),
        grid=(B, nt),
        in_specs=in_specs + s_in,
        out_specs=tuple([row(D)] + s_out),
        out_shape=tuple([jax.ShapeDtypeStruct((B, T, D), F32)] + s_shape),
        scratch_shapes=[pltpu.VMEM((tm, D_MIX), BF16),
                        o_slabs, l_slabs, o_slabs, l_slabs, o_slabs, l_slabs],
        compiler_params=_params(2),
        name="prompt_finish",
    )(x, conv_y, *attn, mq, z, mk, mv, w_out_bf16, g_final.reshape(1, D), expander, *s_args)


def _sample_tables(S, n_buf):
    j = np.arange(S)[:, None]
    t = np.arange(n_buf)[None, :]
    dist = n_buf + j - t
    count = np.zeros(dist.shape)
    for win, dil in DILATED_CONFIGS:
        count += (dist % dil == 0) & (dist <= win)
    bias = np.where(count > 0, -_slopes()[:, None, None] * dist, -np.inf)
    bias = bias.reshape(ATTN_HEADS // 2, 2 * S, n_buf)
    count = np.concatenate([count, count], axis=0)
    return count.astype(np.float32), bias.astype(np.float32)


def _stack_pair(a, p):
    ap = a[:, p * LANES:(p + 1) * LANES]
    lo = lax.broadcasted_iota(jnp.int32, ap.shape, 1) < HEAD_DIM
    zero = jnp.zeros_like(ap)
    return jnp.concatenate([jnp.where(lo, ap, zero), jnp.where(lo, zero, ap)], axis=0)


def _unstack_pair(o, S):
    lo = lax.broadcasted_iota(jnp.int32, (S, LANES), 1) < HEAD_DIM
    return jnp.where(lo, o[:S], o[S:])


def _sample_attn_kernel(q_ref, kn_ref, vn_ref, kt_ref, vt_ref, mq_ref, mkt_ref, mvt_ref,
                        cnt_ref, bias_ref, o_ref, mo_ref, *, batch_row):
    R, S = kt_ref.shape[0], o_ref.shape[1]
    nt = (((1,), (1,)), ((), ()))
    slopes = _slopes()
    n_cfg = float(len(DILATED_CONFIGS))
    row = lax.broadcasted_iota(jnp.int32, (2 * S, 1), 0)
    upper = row >= S
    rowj = jnp.where(upper, row - S, row)
    cnt = cnt_ref[...]
    rows = range(R)
    which = batch_row % (SUBLANES // S)

    def tokens(ref):
        blk = ref[...]
        tok = blk[0:S]
        for i in range(1, SUBLANES // S):
            tok = jnp.where(which == i, blk[i * S:(i + 1) * S], tok)
        return tok

    q_tok, kn_tok, vn_tok, mq_tok = (tokens(ref) for ref in (q_ref, kn_ref, vn_ref, mq_ref))

    def lane_group(tok, p):
        return tok[:, p * LANES:(p + 1) * LANES]

    def pair_rows(ref, r, p):
        return ref[r, 2 * p:2 * p + 2].reshape(2 * HEAD_DIM, ref.shape[-1]).astype(BF16)

    units = [(r, p) for r in rows for p in range(ATTN_HEADS // 2)]
    munits = [(r, p) for r in rows for p in range(MEM_HEADS // 2)]
    qs = [_stack_pair(q_tok, p) for r, p in units]
    scores = [jnp.dot(q8.astype(BF16), pair_rows(kt_ref, r, p),
                      preferred_element_type=F32) + bias_ref[p] for (r, p), q8 in zip(units, qs)]
    mscores = [jnp.dot(_stack_pair(mq_tok, p).astype(BF16), pair_rows(mkt_ref, r, p),
                       preferred_element_type=F32) for r, p in munits]
    probs, stats = [], []
    for (r, p), q8, s in zip(units, qs, scores):
        knp = lane_group(kn_tok, p)
        slope = jnp.where(upper, float(slopes[2 * p + 1]), float(slopes[2 * p]))
        m = jnp.max(s, axis=-1, keepdims=True)
        s_new = []
        for jp in range(S):
            d = rowj - jp
            sj = jnp.sum(q8 * knp[jp:jp + 1], axis=-1, keepdims=True)
            sj = jnp.where(d >= 0, sj - slope * d.astype(F32), -jnp.inf)
            s_new.append(sj)
            m = jnp.maximum(m, sj)
        pr = cnt * jnp.exp(s - m)
        l = jnp.sum(pr, axis=-1, keepdims=True)
        w_new = []
        for jp in range(S):
            w = jnp.where(rowj == jp, n_cfg, 1.0) * jnp.exp(s_new[jp] - m)
            l = l + w
            w_new.append(w)
        probs.append(pr.astype(BF16))
        stats.append((l, w_new))
    mprobs, mden = [], []
    for s in mscores:
        pr = jnp.exp(s - jnp.max(s, axis=-1, keepdims=True))
        mden.append(jnp.sum(pr, axis=-1, keepdims=True))
        mprobs.append(pr.astype(BF16))
    accs = [lax.dot_general(pr, pair_rows(vt_ref, r, p), nt, preferred_element_type=F32)
            for (r, p), pr in zip(units, probs)]
    maccs = [lax.dot_general(pr, pair_rows(mvt_ref, r, p), nt, preferred_element_type=F32)
             for (r, p), pr in zip(munits, mprobs)]
    outs = []
    for (r, p), acc, (l, w_new) in zip(units, accs, stats):
        vnp = lane_group(vn_tok, p)
        for jp in range(S):
            acc = acc + w_new[jp] * vnp[jp:jp + 1]
        outs.append(_unstack_pair(acc / l, S))
    mouts = [_unstack_pair(acc / den, S) for acc, den in zip(maccs, mden)]
    n_p, n_mp = ATTN_HEADS // 2, MEM_HEADS // 2
    for r in rows:
        o_ref[r] = jnp.concatenate(outs[r * n_p:(r + 1) * n_p], axis=-1)
        mo_ref[r] = jnp.concatenate(mouts[r * n_mp:(r + 1) * n_mp], axis=-1)


N_SAMPLE_IN = 10


def _sample_row_specs(sample, base, n_rows, step_of):
    q, k_new, v_new, cache_kt, cache_vt, mq, mem_kt, mem_vt = sample
    DB, n_buf = cache_kt.shape[0], cache_kt.shape[-1]
    S, W = q.shape[0] // DB, q.shape[1]
    assert SUBLANES % S == 0, "token rows are fetched in 8-row blocks"
    assert n_buf >= DILATED_CONFIGS[-1][0] and S <= DILATED_CONFIGS[1][1], \
        "new tokens reach each other through dilation 1 only; the buffer covers every window"
    cnt, bias = (jnp.asarray(t) for t in _sample_tables(S, n_buf))
    per_block = SUBLANES // S

    def tokens_in(a):
        return pl.BlockSpec((SUBLANES, a.shape[1]),
                            lambda *g: ((base + step_of(*g)) // per_block, 0))

    def row_in(a):
        return pl.BlockSpec((1,) + a.shape[1:],
                            lambda *g: (base + step_of(*g),) + (0,) * (a.ndim - 1))

    def row_out(width):
        return pl.BlockSpec((1, S, width), lambda *g: (step_of(*g), 0, 0))

    def const(a):
        return pl.BlockSpec(a.shape, lambda *g: (0,) * a.ndim)

    args = list(sample) + [cnt, bias]
    assert len(args) == N_SAMPLE_IN
    in_specs = [tokens_in(q), tokens_in(k_new), tokens_in(v_new), row_in(cache_kt),
                row_in(cache_vt), tokens_in(mq), row_in(mem_kt), row_in(mem_vt),
                const(cnt), const(bias)]
    out_specs = [row_out(W), row_out(MEM_W)]
    out_shape = [jax.ShapeDtypeStruct((n_rows, S, W), F32),
                 jax.ShapeDtypeStruct((n_rows, S, MEM_W), F32)]
    return args, in_specs, out_specs, out_shape


def _sample_attention(sample, base, n_rows):
    args, in_specs, out_specs, out_shape = _sample_row_specs(sample, base, n_rows, lambda i: i)

    def body(*refs):
        _sample_attn_kernel(*refs, batch_row=base + pl.program_id(0))

    return pl.pallas_call(
        body,
        grid=(n_rows,),
        in_specs=in_specs,
        out_specs=tuple(out_specs),
        out_shape=tuple(out_shape),
        compiler_params=_params(1),
        name="sample_attn",
    )(*args)


def _sample_finish_kernel(x_ref, cb_ref, u0_ref, u1_ref, u2_ref, attn_ref, mem_ref, z_ref,
                          cw_ref, wout_ref, gf_ref, y_ref, mix):
    cw = cw_ref[...]
    conv = cw[0:1] * u0_ref[...] + cw[1:2] * u1_ref[...] + cw[2:3] * u2_ref[...]
    z = z_ref[...].astype(F32)
    lo, hi = CONV_W, CONV_W + ATTN_W
    mix[:, 0:lo] = (cb_ref[...].astype(F32) * conv * _silu(z[:, 0:lo])).astype(BF16)
    mix[:, lo:hi] = (attn_ref[...] * _silu(z[:, lo:hi])).astype(BF16)
    mix[:, hi:] = (mem_ref[...] * _silu(z[:, hi:])).astype(BF16)
    y_ref[...] = _gate_project_norm(x_ref[...], mix, wout_ref, gf_ref)


def _sample_finish(x, cb, u_taps, attn, mem, z, conv_w, w_out_bf16, g_final):
    rows, D = x.shape
    return pl.pallas_call(
        _sample_finish_kernel,
        out_shape=jax.ShapeDtypeStruct((rows, D), F32),
        scratch_shapes=[pltpu.VMEM((rows, D_MIX), BF16)],
        compiler_params=pltpu.CompilerParams(vmem_limit_bytes=VMEM_LIMIT),
        name="sample_finish",
    )(x, cb, *u_taps, attn, mem, z, conv_w, w_out_bf16, g_final.reshape(1, D))


def kernel(x_prompt, x_sample, mem_prompt, cache_win_k, cache_win_v, cache_conv, cache_mem_k,
           cache_mem_v, g_in, w_in, conv_w, g_mem, w_mem_kv, w_out, g_final):
    assert g_in.shape[0] == 1, "the final RMSNorm is fused into the single layer's tail"
    B, T, D = x_prompt.shape
    DB, S, _ = x_sample.shape
    n_keep = min(DILATED_CONFIGS[-1][0], T)
    w_in_b = w_in[0].astype(BF16)
    w_out_b = w_out[0].astype(BF16)
    dils = tuple(d for _, d in DILATED_CONFIGS)
    wkvt_b = _transposed_columns(w_in[0], _C_K, _C_MQ)

    def heads_last(t, heads):
        return jnp.transpose(t.reshape(t.shape[0], heads, HEAD_DIM, -1), (0, 3, 1, 2))[None]

    def heads_first(c):
        return jnp.transpose(c, (0, 2, 3, 1))

    rows = DB * S
    cb_s, u_s, q_s, _, _, k32, v32, mq_s, z_s = _project(
        x_sample.reshape(1, rows, D), g_in[0], w_in_b, wkvt_b, rows, False, (1,), Q_SCALE)
    k32 = k32.reshape(rows, ATTN_W)
    v32 = v32.reshape(rows, ATTN_W)
    sample = (q_s[0].reshape(rows, ATTN_W).astype(F32), k32, v32,
              heads_first(cache_win_k[0]), heads_first(cache_win_v[0]),
              mq_s.reshape(rows, MEM_W).astype(F32),
              heads_first(cache_mem_k[0]), heads_first(cache_mem_v[0]))
    n_proj = n_tail = B * (T // ROW_TILE)
    n_attn = B * (T // ATTN_TILE)
    n_own = DB - n_proj - n_attn - n_tail
    assert n_own > 0
    conv_y, u_tails, q, k, v, kt32, vt32, mq, z, attn_p, mem_p = _project(
        x_prompt, g_in[0], w_in_b, wkvt_b, n_keep, True, dils, Q_SCALE * LOG2E, sample, 0,
        conv_w[0])
    mkt32, mvt32, mk, mv = _mem_kv(mem_prompt, g_mem[0], w_mem_kv[0].astype(BF16))
    attn, attn_a, mem_a = _prompt_attention(q, k, v, sample, n_proj)
    y_prompt, attn_t, mem_t = _prompt_finish(x_prompt, conv_y, attn, mq, z, mk, mv, w_out_b,
                                             g_final, sample, n_proj + n_attn)
    attn_o, mem_o = _sample_attention(sample, n_proj + n_attn + n_tail, n_own)
    attn_s = jnp.concatenate([attn_p, attn_a, attn_t, attn_o], axis=0)
    mem_s = jnp.concatenate([mem_p, mem_a, mem_t, mem_o], axis=0)
    p_win_k = heads_last(kt32, ATTN_HEADS)
    p_win_v = heads_last(vt32, ATTN_HEADS)
    p_conv = u_tails[None, :, -(CONV_WIDTH - 1):]
    p_mem_k = heads_last(mkt32, MEM_HEADS)
    p_mem_v = heads_last(mvt32, MEM_HEADS)
    full = jnp.concatenate([cache_conv[0], u_s.reshape(DB, S, CONV_W)], axis=1)
    taps = [full[:, t:t + S].reshape(rows, CONV_W) for t in range(CONV_WIDTH)]
    y_sample = _sample_finish(x_sample.reshape(rows, D), cb_s.reshape(rows, CONV_W), taps,
                              attn_s.reshape(rows, ATTN_W), mem_s.reshape(rows, MEM_W),
                              z_s.reshape(rows, D_MIX), conv_w[0], w_out_b,
                              g_final).reshape(DB, S, D)
    s_win_k = k32.reshape(1, DB, S, ATTN_HEADS, HEAD_DIM)
    s_win_v = v32.reshape(1, DB, S, ATTN_HEADS, HEAD_DIM)
    s_conv = full[None, :, -(CONV_WIDTH - 1):]
    return (y_prompt, y_sample, p_win_k, p_win_v, p_conv, p_mem_k, p_mem_v,
            s_win_k, s_win_v, s_conv)
```

```python
import functools

import numpy as np
import jax
import jax.numpy as jnp
from jax import lax
from jax.experimental import pallas as pl
from jax.experimental.pallas import tpu as pltpu

F32 = jnp.float32
BF16 = jnp.bfloat16

HEAD_DIM = 64
ATTN_HEADS = 8
MEM_HEADS = 4
CONV_W = 256
ATTN_W = 512
MEM_W = 256
D_MIX = 1024
CONV_WIDTH = 3
DILATED_CONFIGS = ((128, 1), (512, 4), (2048, 16))
SPAN = 128
ALIBI_MAX = 8.0
RMS_EPS = 1e-6
Q_SCALE = HEAD_DIM ** -0.5
LOG2E = float(np.log2(np.e))

LANES = 128
SUBLANES = 8
VMEM_LIMIT = 56 * 1024 * 1024

_C_CB, _C_CC, _C_Q, _C_K, _C_V, _C_MQ, _C_Z, _C_END = 0, 256, 768, 1280, 1792, 2304, 2560, 3584

ROW_TILE = 512
ATTN_TILE = 512


def _slopes():
    h = np.arange(ATTN_HEADS, dtype=np.float64) + 1.0
    return np.exp2(-ALIBI_MAX * h / ATTN_HEADS)


def _params(n_axes):
    return pltpu.CompilerParams(
        dimension_semantics=("arbitrary",) * n_axes,
        vmem_limit_bytes=VMEM_LIMIT)


def _transpose_cast_kernel(w_ref, o_ref):
    o_ref[...] = w_ref[...].T.astype(BF16)


def _transposed_columns(w, lo, hi):
    K = w.shape[0]
    blk = 2 * LANES
    assert lo % blk == 0 and hi % blk == 0
    return pl.pallas_call(
        _transpose_cast_kernel,
        grid=((hi - lo) // blk,),
        in_specs=[pl.BlockSpec((K, blk), lambda i: (0, lo // blk + i))],
        out_specs=pl.BlockSpec((blk, K), lambda i: (i, 0)),
        out_shape=jax.ShapeDtypeStruct((hi - lo, K), BF16),
        compiler_params=_params(1),
        name="wkv_t",
    )(w)


def _fold_store(res, out_refs, slabs, dils):
    rows = res.shape[0]
    out_refs[0][0, 0] = res.astype(BF16)
    if len(dils) == 1:
        return
    n_slab = res.shape[1] // LANES
    src, prev_d = slabs[0], 1
    for s in range(n_slab):
        src[s] = res[:, s * LANES:(s + 1) * LANES]
    for idx, (d, o_ref) in enumerate(zip(dils[1:], out_refs[1:])):
        step, n = d // prev_d, rows // d
        dst = slabs[idx + 1] if idx + 2 < len(dils) else None
        for rp in range(prev_d):
            for q in range(step):
                r = q * prev_d + rp
                for s in range(n_slab):
                    val = src[s, pl.ds(rp * (rows // prev_d) + q, n, stride=step), :]
                    o_ref[0, r, :, s * LANES:(s + 1) * LANES] = val.astype(BF16)
                    if dst is not None:
                        dst[s, pl.ds(r * n, n), :] = val
        src, prev_d = dst, d


def _proj_kernel(x_ref, g_ref, w_ref, wkvt_ref, *refs, first, keep_transposed, dils, q_scale,
                 hosts_sample, fuse_conv, emit_z):
    n = len(dils)
    if fuse_conv:
        cw_ref, refs, ucat = refs[0], refs[1:-1], refs[-1]

        @pl.when(pl.program_id(1) == 0)
        def _():
            ucat[0:SUBLANES] = jnp.zeros((SUBLANES, CONV_W), F32)
    n_out = 5 + 3 * n + (1 if emit_z else 0)
    if hosts_sample:
        sample_ins, refs = refs[:N_SAMPLE_IN], refs[N_SAMPLE_IN:]
        sample_outs = refs[n_out:n_out + 2]
        refs = refs[:n_out] + refs[n_out + 2:]
        step = pl.program_id(0) * pl.num_programs(1) + pl.program_id(1)
        _sample_attn_kernel(*sample_ins, *sample_outs, batch_row=hosts_sample[1] + step)
    cb_ref, u_ref = refs[0:2]
    q_refs, k_refs, v_refs = refs[2:2 + n], refs[2 + n:2 + 2 * n], refs[2 + 2 * n:2 + 3 * n]
    k32_ref, v32_ref, mq_ref = refs[2 + 3 * n:5 + 3 * n]
    scratch = refs[n_out:]
    slabs = [scratch[a * (n - 1):(a + 1) * (n - 1)] for a in range(3)]
    x = x_ref[0]
    r = lax.rsqrt(jnp.mean(x * x, axis=-1, keepdims=True) + RMS_EPS)
    h = (x * r * g_ref[...]).astype(BF16)

    def seg(lo, hi):
        return jnp.dot(h, w_ref[:, lo:hi], preferred_element_type=F32)

    cb = seg(_C_CB, _C_CC)
    cch = seg(_C_CC, _C_Q)
    u = cch[:, :CONV_W] * cch[:, CONV_W:]
    if fuse_conv:
        rows = u.shape[0]
        ucat[SUBLANES:] = u
        cw = cw_ref[...]
        conv = (cw[0:1] * ucat[pl.ds(SUBLANES - 2, rows)]
                + cw[1:2] * ucat[pl.ds(SUBLANES - 1, rows)]
                + cw[2:3] * ucat[pl.ds(SUBLANES, rows)])
        cb_ref[0] = (cb * conv).astype(BF16)
        u_ref[0] = u[rows - SUBLANES:]
        ucat[0:SUBLANES] = u[rows - SUBLANES:]
    else:
        cb_ref[0] = cb.astype(BF16)
        u_ref[0] = u
    _fold_store(seg(_C_Q, _C_K) * q_scale, q_refs, slabs[0], dils)
    k = seg(_C_K, _C_V)
    _fold_store(k, k_refs, slabs[1], dils)
    v = seg(_C_V, _C_MQ)
    _fold_store(v, v_refs, slabs[2], dils)
    mq_ref[0] = (seg(_C_MQ, _C_Z) * q_scale).astype(BF16)
    if emit_z:
        refs[5 + 3 * n][0] = seg(_C_Z, _C_END).astype(BF16)
    if keep_transposed:
        @pl.when(pl.program_id(1) >= first)
        def _():
            nt = (((1,), (1,)), ((), ()))
            k32_ref[0] = lax.dot_general(wkvt_ref[0:ATTN_W], h, nt, preferred_element_type=F32)
            v32_ref[0] = lax.dot_general(wkvt_ref[ATTN_W:], h, nt, preferred_element_type=F32)
    else:
        k32_ref[0] = k
        v32_ref[0] = v


def _project(x, g, w_bf16, wkvt_bf16, n_keep, keep_transposed, dils, q_scale, sample=None,
             sample_base=0, conv_w=None, emit_z=True):
    B, T, D = x.shape
    tm = ROW_TILE
    nt = T // tm
    first = (T - n_keep) // tm
    n = len(dils)

    def row(width):
        return pl.BlockSpec((1, tm, width), lambda b, i: (b, i, 0))

    folded_specs = [pl.BlockSpec((1, d, tm // d, ATTN_W), lambda b, i: (b, 0, i, 0)) for d in dils]
    folded_shapes = [jax.ShapeDtypeStruct((B, d, T // d, ATTN_W), BF16) for d in dils]

    if keep_transposed:
        keep = pl.BlockSpec((1, ATTN_W, tm), lambda b, i: (b, 0, jnp.maximum(i - first, 0)))
        keep_shape = (B, ATTN_W, n_keep)
    else:
        keep = pl.BlockSpec((1, tm, ATTN_W), lambda b, i: (b, jnp.maximum(i - first, 0), 0))
        keep_shape = (B, n_keep, ATTN_W)
    fuse_conv = conv_w is not None
    u_rows = SUBLANES if fuse_conv else tm
    out_shape = (
        [jax.ShapeDtypeStruct((B, T, CONV_W), BF16),
         jax.ShapeDtypeStruct((B, nt * u_rows, CONV_W), F32)]
        + folded_shapes * 3
        + [jax.ShapeDtypeStruct(keep_shape, F32),
           jax.ShapeDtypeStruct(keep_shape, F32),
           jax.ShapeDtypeStruct((B, T, MEM_W), BF16)])
    out_specs = ([row(CONV_W), pl.BlockSpec((1, u_rows, CONV_W), lambda b, i: (b, i, 0))]
                 + folded_specs * 3 + [keep, keep, row(MEM_W)])
    if emit_z:
        out_shape.append(jax.ShapeDtypeStruct((B, T, D_MIX), BF16))
        out_specs.append(row(D_MIX))
    scratch = [pltpu.VMEM((ATTN_W // LANES, tm, LANES), F32)] * (3 * (n - 1))
    once = pl.Buffered(1)
    in_specs = [row(D),
                pl.BlockSpec((1, D), lambda b, i: (0, 0)),
                pl.BlockSpec(w_bf16.shape, lambda b, i: (0, 0), pipeline_mode=once),
                pl.BlockSpec(wkvt_bf16.shape, lambda b, i: (0, 0), pipeline_mode=once)]
    args = [x, g.reshape(1, D), w_bf16, wkvt_bf16]
    if fuse_conv:
        args.append(conv_w)
        in_specs.append(pl.BlockSpec(conv_w.shape, lambda b, i: (0, 0)))
        scratch = scratch + [pltpu.VMEM((tm + SUBLANES, CONV_W), F32)]
    if sample is not None:
        s_args, s_in, s_out, s_shape = _sample_row_specs(sample, sample_base, B * nt,
                                                         lambda b, i: b * nt + i)
        args, in_specs = args + s_args, in_specs + s_in
        out_specs, out_shape = out_specs + s_out, out_shape + s_shape
    res = pl.pallas_call(
        functools.partial(_proj_kernel, first=first, keep_transposed=keep_transposed, dils=dils,
                          q_scale=q_scale,
                          hosts_sample=(True, sample_base) if sample is not None else None,
                          fuse_conv=fuse_conv, emit_z=emit_z),
        grid=(B, nt),
        in_specs=in_specs,
        out_specs=tuple(out_specs),
        out_shape=tuple(out_shape),
        scratch_shapes=scratch,
        compiler_params=_params(2),
        name="proj",
    )(*args)
    cb, u = res[0:2]
    q, k, v = res[2:2 + n], res[2 + n:2 + 2 * n], res[2 + 2 * n:2 + 3 * n]
    k32, v32, mq = res[2 + 3 * n:5 + 3 * n]
    n_out = 5 + 3 * n + (1 if emit_z else 0)
    z = res[5 + 3 * n] if emit_z else None
    return (cb, u, q, k, v, k32, v32, mq, z) + tuple(res[n_out:])


def _memkv_kernel(x_ref, g_ref, w_ref, k32_ref, v32_ref, k_ref, v_ref):
    x = x_ref[...]
    r = lax.rsqrt(jnp.mean(x * x, axis=-1, keepdims=True) + RMS_EPS)
    h = (x * r * g_ref[...]).astype(BF16)
    kv = jnp.dot(h, w_ref[...], preferred_element_type=F32)
    k_ref[...] = kv[:, :MEM_W].astype(BF16)
    v_ref[...] = kv[:, MEM_W:].astype(BF16)
    n_mem = k32_ref.shape[2]
    for b in range(k32_ref.shape[0]):
        kvt = kv[b * n_mem:(b + 1) * n_mem].T
        k32_ref[b] = kvt[:MEM_W]
        v32_ref[b] = kvt[MEM_W:]


def _mem_kv(mem, g, w_bf16):
    B, M, D = mem.shape
    rows = B * M
    k32, v32, k, v = pl.pallas_call(
        _memkv_kernel,
        out_shape=(jax.ShapeDtypeStruct((B, MEM_W, M), F32),
                   jax.ShapeDtypeStruct((B, MEM_W, M), F32),
                   jax.ShapeDtypeStruct((rows, MEM_W), BF16),
                   jax.ShapeDtypeStruct((rows, MEM_W), BF16)),
        compiler_params=pltpu.CompilerParams(vmem_limit_bytes=VMEM_LIMIT),
        name="memkv",
    )(mem.reshape(rows, D), g.reshape(1, D), w_bf16)
    return k32, v32, k.reshape(B, M, MEM_W), v.reshape(B, M, MEM_W)


def _band_bias_table():
    qi = np.arange(SPAN)[:, None] + SPAN
    ki = np.arange(2 * SPAN)[None, :]
    dist = qi - ki
    valid = (dist >= 0) & (dist <= SPAN)
    slopes = _slopes()
    tabs = []
    for _, dil in DILATED_CONFIGS:
        for first in (False, True):
            ok = valid & (ki >= SPAN) if first else valid
            for h in range(ATTN_HEADS):
                bias = -(slopes[h] * dil) * dist * LOG2E
                tabs.append(np.where(ok, bias, -np.inf))
    return np.stack(tabs).astype(np.float32)


def _pair_attention(qb, kb, vb, bias_lo, bias_hi):
    lane = lax.broadcasted_iota(jnp.int32, qb.shape, 1)
    lo = lane < HEAD_DIM
    zero = jnp.zeros_like(qb)
    v_ones = jnp.concatenate([vb, jnp.ones_like(vb)], axis=1)
    accs, ms, ls = [], [], []
    for sel, bias in ((lo, bias_lo), (~lo, bias_hi)):
        qm = jnp.where(sel, qb, zero)
        s = lax.dot_general(qm, kb, (((1,), (1,)), ((), ())), preferred_element_type=F32)
        if bias is not None:
            s = s + bias
        m = jnp.max(s, axis=-1, keepdims=True)
        p = jnp.exp2(s - m)
        acc = jnp.dot(p.astype(BF16), v_ones, preferred_element_type=F32)
        ms.append(m)
        accs.append(acc[:, :LANES])
        ls.append(acc[:, LANES:])
    o = jnp.where(lo, accs[0], accs[1]) / jnp.where(lo, ls[0], ls[1])
    return o, ms, ls


def _attn_kernel(*refs, steps, sample_base):
    ins = refs[:15]
    tab_ref = refs[15]
    sample_ins = refs[16:16 + N_SAMPLE_IN]
    outs = refs[16 + N_SAMPLE_IN:22 + N_SAMPLE_IN]
    sample_outs = refs[22 + N_SAMPLE_IN:24 + N_SAMPLE_IN]
    kcat, vcat = refs[24 + N_SAMPLE_IN:]
    s = pl.program_id(1)
    nblk = ATTN_TILE // SPAN
    _sample_attn_kernel(*sample_ins, *sample_outs,
                        batch_row=sample_base + pl.program_id(0) * steps + s)
    for c in range(3):
        _, kp, ko, vp, vo = ins[5 * c:5 * c + 5]
        kcat[c, 0:SPAN] = kp[0, 0]
        kcat[c, SPAN:] = ko[0, 0, 0:SPAN]
        vcat[c, 0:SPAN] = vp[0, 0]
        vcat[c, SPAN:] = vo[0, 0, 0:SPAN]

    for j in range(nblk):
        rows = slice(j * SPAN, (j + 1) * SPAN)
        keys = slice((j - 1) * SPAN, (j + 1) * SPAN)
        for c, (_, dil) in enumerate(DILATED_CONFIGS):
            q_ref, _, ko, _, vo = ins[5 * c:5 * c + 5]
            o_ref, l_ref = outs[2 * c], outs[2 * c + 1]
            units_per_residue = steps // dil
            if j == 0:
                base = (c * 2 + (s % units_per_residue == 0).astype(jnp.int32)) * ATTN_HEADS
            else:
                base = c * 2 * ATTN_HEADS
            lane = lax.broadcasted_iota(jnp.int32, (SPAN, LANES), 1)
            m8 = jnp.zeros((SPAN, LANES), F32)
            l8 = jnp.ones((SPAN, LANES), F32)
            for p in range(ATTN_HEADS // 2):
                lanes = slice(p * LANES, (p + 1) * LANES)
                qb = q_ref[0, 0, rows, lanes]
                if j == 0:
                    kb, vb = kcat[c, :, lanes], vcat[c, :, lanes]
                else:
                    kb, vb = ko[0, 0, keys, lanes], vo[0, 0, keys, lanes]
                o, ms, ls = _pair_attention(qb, kb, vb, tab_ref[base + 2 * p],
                                            tab_ref[base + 2 * p + 1])
                o_ref[0, 0, rows, lanes] = o.astype(BF16)
                for half in range(2):
                    m8 = jnp.where(lane == 2 * p + half, ms[half], m8)
                    l8 = jnp.where(lane == 2 * p + half, ls[half], l8)
            l_ref[0, 0, rows, :] = m8 + jnp.log2(l8)


def _prompt_attention(q, k, v, sample, sample_base):
    B, _, T, W = q[0].shape
    steps = T // ATTN_TILE
    prev_per_tile = ATTN_TILE // SPAN
    args, in_specs, out_specs, out_shape = [], [], [], []
    for c, (_, dil) in enumerate(DILATED_CONFIGS):
        L = T // dil
        upr = L // ATTN_TILE

        def own(b, s, upr=upr):
            return (b, s // upr, s % upr, 0)

        def prev(b, s, upr=upr):
            return (b, s // upr, jnp.maximum((s % upr) * prev_per_tile - 1, 0), 0)

        shape = (B, dil, L, W)
        assert q[c].shape == shape
        args += [q[c], k[c], k[c], v[c], v[c]]
        in_specs += [pl.BlockSpec((1, 1, ATTN_TILE, W), own),
                     pl.BlockSpec((1, 1, SPAN, W), prev),
                     pl.BlockSpec((1, 1, ATTN_TILE, W), own),
                     pl.BlockSpec((1, 1, SPAN, W), prev),
                     pl.BlockSpec((1, 1, ATTN_TILE, W), own)]
        out_specs += [pl.BlockSpec((1, 1, ATTN_TILE, W), own),
                      pl.BlockSpec((1, 1, ATTN_TILE, LANES), own)]
        out_shape += [jax.ShapeDtypeStruct(shape, BF16),
                      jax.ShapeDtypeStruct((B, dil, L, LANES), F32)]
    tab = jnp.asarray(_band_bias_table())
    args.append(tab)
    in_specs.append(pl.BlockSpec(tab.shape, lambda b, s: (0, 0, 0)))
    s_args, s_in, s_out, s_shape = _sample_row_specs(sample, sample_base, B * steps,
                                                     lambda b, s: b * steps + s)
    res = pl.pallas_call(
        functools.partial(_attn_kernel, steps=steps, sample_base=sample_base),
        grid=(B, steps),
        in_specs=in_specs + s_in,
        out_specs=tuple(out_specs + s_out),
        out_shape=tuple(out_shape + s_shape),
        scratch_shapes=[pltpu.VMEM((3, 2 * SPAN, W), BF16)] * 2,
        compiler_params=_params(2),
        name="prompt_attn",
    )(*args, *s_args)
    return list(res[:6]), res[6], res[7]


def _silu(z):
    return z / (1.0 + jnp.exp(-z))


def _gate_project_norm(x, mix_ref, wout_ref, gf_ref):
    y = x + jnp.dot(mix_ref[...], wout_ref[...], preferred_element_type=F32)
    r = lax.rsqrt(jnp.mean(y * y, axis=-1, keepdims=True) + RMS_EPS)
    return y * r * gf_ref[...]


def _unfold(src_ref, slabs, d, tmp=None, step=4):
    n_slab, rows = slabs.shape[0], slabs.shape[1]

    def piece(r, s):
        return src_ref[0, r, :, s * LANES:(s + 1) * LANES].astype(F32)

    if d <= step:
        for r in range(d):
            for s in range(n_slab):
                slabs[s, pl.ds(r, rows // d, stride=d), :] = piece(r, s)
        return
    prev_d, n = d // step, rows // d
    per = rows // prev_d
    for rp in range(prev_d):
        for q in range(step):
            for s in range(n_slab):
                tmp[s, pl.ds(rp * per + q, n, stride=step), :] = piece(q * prev_d + rp, s)
    for rp in range(prev_d):
        for s in range(n_slab):
            slabs[s, pl.ds(rp, per, stride=prev_d), :] = tmp[s, pl.ds(rp * per, per), :]


def _head_expander():
    e = np.zeros((2 * LANES, ATTN_W), np.float32)
    for h in range(ATTN_HEADS):
        e[h, h * HEAD_DIM:(h + 1) * HEAD_DIM] = 1.0
        e[LANES + h, h * HEAD_DIM:(h + 1) * HEAD_DIM] = 1.0
    return e


N_FINISH_IN = 16


def _finish_kernel(*refs, sample_base):
    (x_ref, cy_ref, o1_ref, l1_ref, o4_ref, l4_ref, o16_ref, l16_ref,
     mq_ref, mk_ref, mv_ref, gin_ref, wz_ref, wout_ref, gf_ref, ex_ref) = refs[:N_FINISH_IN]
    sample_ins = refs[N_FINISH_IN:N_FINISH_IN + N_SAMPLE_IN]
    y_ref = refs[N_FINISH_IN + N_SAMPLE_IN]
    sample_outs = refs[N_FINISH_IN + N_SAMPLE_IN + 1:N_FINISH_IN + N_SAMPLE_IN + 3]
    mix, o4s, l4s, o16s, l16s, otmp, ltmp = refs[N_FINISH_IN + N_SAMPLE_IN + 3:]
    i = pl.program_id(1)
    _sample_attn_kernel(*sample_ins, *sample_outs,
                        batch_row=sample_base + pl.program_id(0) * pl.num_programs(1) + i)
    x = x_ref[0]
    r = lax.rsqrt(jnp.mean(x * x, axis=-1, keepdims=True) + RMS_EPS)
    z = jnp.dot((x * r * gin_ref[...]).astype(BF16), wz_ref[...], preferred_element_type=F32)
    mix[:, 0:CONV_W] = (cy_ref[0].astype(F32) * _silu(z[:, 0:CONV_W])).astype(BF16)
    dil4, dil16 = DILATED_CONFIGS[1][1], DILATED_CONFIGS[2][1]
    _unfold(o4_ref, o4s, dil4)
    _unfold(l4_ref, l4s, dil4)
    _unfold(o16_ref, o16s, dil16, otmp)
    _unfold(l16_ref, l16s, dil16, ltmp)
    lo, hi = CONV_W, CONV_W + ATTN_W
    l1, l4, l16 = l1_ref[0, 0], l4s[0], l16s[0]
    mx = jnp.maximum(jnp.maximum(l1, l4), l16)
    e1, e4, e16 = jnp.exp2(l1 - mx), jnp.exp2(l4 - mx), jnp.exp2(l16 - mx)
    den = e1 + e4 + e16

    def spread(w):
        w_hi = w.astype(BF16)
        w_lo = (w - w_hi.astype(F32)).astype(BF16)
        return jnp.dot(jnp.concatenate([w_hi, w_lo], axis=1), ex_ref[...],
                       preferred_element_type=F32)

    a1, a4, a16 = spread(e1 / den), spread(e4 / den), spread(e16 / den)
    for s in range(ATTN_W // LANES):
        lanes = slice(s * LANES, (s + 1) * LANES)
        attn = (a1[:, lanes] * o1_ref[0, 0, :, lanes].astype(F32) + a4[:, lanes] * o4s[s]
                + a16[:, lanes] * o16s[s])
        a = lo + s * LANES
        mix[:, a:a + LANES] = (attn * _silu(z[:, a:a + LANES])).astype(BF16)
    for p in range(MEM_HEADS // 2):
        lanes = slice(p * LANES, (p + 1) * LANES)
        o, _, _ = _pair_attention(mq_ref[0, :, lanes], mk_ref[0, :, lanes], mv_ref[0, :, lanes],
                                  None, None)
        a = hi + p * LANES
        mix[:, a:a + LANES] = (o * _silu(z[:, a:a + LANES])).astype(BF16)
    y_ref[0] = _gate_project_norm(x_ref[0], mix, wout_ref, gf_ref)


def _prompt_finish(x, conv_y, attn, mq, mk, mv, g_in, w_gate_bf16, w_out_bf16, g_final, sample,
                   sample_base):
    B, T, D = x.shape
    tm = ROW_TILE

    def row(width):
        return pl.BlockSpec((1, tm, width), lambda b, i: (b, i, 0))

    def const(shape):
        return pl.BlockSpec(shape, lambda b, i: (0,) * len(shape))

    memspec = pl.BlockSpec((1,) + mk.shape[1:], lambda b, i: (b, 0, 0))
    in_specs = [row(D), row(CONV_W)]
    for _, dil in DILATED_CONFIGS:
        in_specs += [pl.BlockSpec((1, dil, tm // dil, ATTN_W), lambda b, i: (b, 0, i, 0)),
                     pl.BlockSpec((1, dil, tm // dil, LANES), lambda b, i: (b, 0, i, 0))]
    expander = jnp.asarray(_head_expander(), BF16)
    in_specs += [row(MEM_W), memspec, memspec, const((1, D)), const(w_gate_bf16.shape),
                 const(w_out_bf16.shape), const((1, D)), const(expander.shape)]
    assert len(in_specs) == N_FINISH_IN
    o_slabs = pltpu.VMEM((ATTN_W // LANES, tm, LANES), F32)
    l_slabs = pltpu.VMEM((1, tm, LANES), F32)
    nt = T // tm
    s_args, s_in, s_out, s_shape = _sample_row_specs(sample, sample_base, B * nt,
                                                     lambda b, i: b * nt + i)
    return pl.pallas_call(
        functools.partial(_finish_kernel, sample_base=sample_base),
        grid=(B, nt),
        in_specs=in_specs + s_in,
        out_specs=tuple([row(D)] + s_out),
        out_shape=tuple([jax.ShapeDtypeStruct((B, T, D), F32)] + s_shape),
        scratch_shapes=[pltpu.VMEM((tm, D_MIX), BF16),
                        o_slabs, l_slabs, o_slabs, l_slabs, o_slabs, l_slabs],
        compiler_params=_params(2),
        name="prompt_finish",
    )(x, conv_y, *attn, mq, mk, mv, g_in.reshape(1, D), w_gate_bf16, w_out_bf16,
      g_final.reshape(1, D), expander, *s_args)


def _sample_tables(S, n_buf):
    j = np.arange(S)[:, None]
    t = np.arange(n_buf)[None, :]
    dist = n_buf + j - t
    count = np.zeros(dist.shape)
    for win, dil in DILATED_CONFIGS:
        count += (dist % dil == 0) & (dist <= win)
    bias = np.where(count > 0, -_slopes()[:, None, None] * dist, -np.inf)
    bias = bias.reshape(ATTN_HEADS // 2, 2 * S, n_buf)
    count = np.concatenate([count, count], axis=0)
    return count.astype(np.float32), bias.astype(np.float32)


def _stack_pair(a, p):
    ap = a[:, p * LANES:(p + 1) * LANES]
    lo = lax.broadcasted_iota(jnp.int32, ap.shape, 1) < HEAD_DIM
    zero = jnp.zeros_like(ap)
    return jnp.concatenate([jnp.where(lo, ap, zero), jnp.where(lo, zero, ap)], axis=0)


def _unstack_pair(o, S):
    lo = lax.broadcasted_iota(jnp.int32, (S, LANES), 1) < HEAD_DIM
    return jnp.where(lo, o[:S], o[S:])


def _sample_attn_kernel(q_ref, kn_ref, vn_ref, kt_ref, vt_ref, mq_ref, mkt_ref, mvt_ref,
                        cnt_ref, bias_ref, o_ref, mo_ref, *, batch_row):
    R, S = kt_ref.shape[0], o_ref.shape[1]
    nt = (((1,), (1,)), ((), ()))
    slopes = _slopes()
    n_cfg = float(len(DILATED_CONFIGS))
    row = lax.broadcasted_iota(jnp.int32, (2 * S, 1), 0)
    upper = row >= S
    rowj = jnp.where(upper, row - S, row)
    cnt = cnt_ref[...]
    rows = range(R)
    which = batch_row % (SUBLANES // S)

    def tokens(ref):
        blk = ref[...]
        tok = blk[0:S]
        for i in range(1, SUBLANES // S):
            tok = jnp.where(which == i, blk[i * S:(i + 1) * S], tok)
        return tok

    q_tok, kn_tok, vn_tok, mq_tok = (tokens(ref) for ref in (q_ref, kn_ref, vn_ref, mq_ref))

    def lane_group(tok, p):
        return tok[:, p * LANES:(p + 1) * LANES]

    def pair_rows(ref, r, p):
        return ref[r, 2 * p:2 * p + 2].reshape(2 * HEAD_DIM, ref.shape[-1]).astype(BF16)

    units = [(r, p) for r in rows for p in range(ATTN_HEADS // 2)]
    munits = [(r, p) for r in rows for p in range(MEM_HEADS // 2)]
    qs = [_stack_pair(q_tok, p) for r, p in units]
    scores = [jnp.dot(q8.astype(BF16), pair_rows(kt_ref, r, p),
                      preferred_element_type=F32) + bias_ref[p] for (r, p), q8 in zip(units, qs)]
    mscores = [jnp.dot(_stack_pair(mq_tok, p).astype(BF16), pair_rows(mkt_ref, r, p),
                       preferred_element_type=F32) for r, p in munits]
    probs, stats = [], []
    for (r, p), q8, s in zip(units, qs, scores):
        knp = lane_group(kn_tok, p)
        slope = jnp.where(upper, float(slopes[2 * p + 1]), float(slopes[2 * p]))
        m = jnp.max(s, axis=-1, keepdims=True)
        s_new = []
        for jp in range(S):
            d = rowj - jp
            sj = jnp.sum(q8 * knp[jp:jp + 1], axis=-1, keepdims=True)
            sj = jnp.where(d >= 0, sj - slope * d.astype(F32), -jnp.inf)
            s_new.append(sj)
            m = jnp.maximum(m, sj)
        pr = cnt * jnp.exp(s - m)
        l = jnp.sum(pr, axis=-1, keepdims=True)
        w_new = []
        for jp in range(S):
            w = jnp.where(rowj == jp, n_cfg, 1.0) * jnp.exp(s_new[jp] - m)
            l = l + w
            w_new.append(w)
        probs.append(pr.astype(BF16))
        stats.append((l, w_new))
    mprobs, mden = [], []
    for s in mscores:
        pr = jnp.exp(s - jnp.max(s, axis=-1, keepdims=True))
        mden.append(jnp.sum(pr, axis=-1, keepdims=True))
        mprobs.append(pr.astype(BF16))
    accs = [lax.dot_general(pr, pair_rows(vt_ref, r, p), nt, preferred_element_type=F32)
            for (r, p), pr in zip(units, probs)]
    maccs = [lax.dot_general(pr, pair_rows(mvt_ref, r, p), nt, preferred_element_type=F32)
             for (r, p), pr in zip(munits, mprobs)]
    outs = []
    for (r, p), acc, (l, w_new) in zip(units, accs, stats):
        vnp = lane_group(vn_tok, p)
        for jp in range(S):
            acc = acc + w_new[jp] * vnp[jp:jp + 1]
        outs.append(_unstack_pair(acc / l, S))
    mouts = [_unstack_pair(acc / den, S) for acc, den in zip(maccs, mden)]
    n_p, n_mp = ATTN_HEADS // 2, MEM_HEADS // 2
    for r in rows:
        o_ref[r] = jnp.concatenate(outs[r * n_p:(r + 1) * n_p], axis=-1)
        mo_ref[r] = jnp.concatenate(mouts[r * n_mp:(r + 1) * n_mp], axis=-1)


N_SAMPLE_IN = 10


def _sample_row_specs(sample, base, n_rows, step_of):
    q, k_new, v_new, cache_kt, cache_vt, mq, mem_kt, mem_vt = sample
    DB, n_buf = cache_kt.shape[0], cache_kt.shape[-1]
    S, W = q.shape[0] // DB, q.shape[1]
    assert SUBLANES % S == 0, "token rows are fetched in 8-row blocks"
    assert n_buf >= DILATED_CONFIGS[-1][0] and S <= DILATED_CONFIGS[1][1], \
        "new tokens reach each other through dilation 1 only; the buffer covers every window"
    cnt, bias = (jnp.asarray(t) for t in _sample_tables(S, n_buf))
    per_block = SUBLANES // S

    def tokens_in(a):
        return pl.BlockSpec((SUBLANES, a.shape[1]),
                            lambda *g: ((base + step_of(*g)) // per_block, 0))

    def row_in(a):
        return pl.BlockSpec((1,) + a.shape[1:],
                            lambda *g: (base + step_of(*g),) + (0,) * (a.ndim - 1))

    def row_out(width):
        return pl.BlockSpec((1, S, width), lambda *g: (step_of(*g), 0, 0))

    def const(a):
        return pl.BlockSpec(a.shape, lambda *g: (0,) * a.ndim)

    args = list(sample) + [cnt, bias]
    assert len(args) == N_SAMPLE_IN
    in_specs = [tokens_in(q), tokens_in(k_new), tokens_in(v_new), row_in(cache_kt),
                row_in(cache_vt), tokens_in(mq), row_in(mem_kt), row_in(mem_vt),
                const(cnt), const(bias)]
    out_specs = [row_out(W), row_out(MEM_W)]
    out_shape = [jax.ShapeDtypeStruct((n_rows, S, W), F32),
                 jax.ShapeDtypeStruct((n_rows, S, MEM_W), F32)]
    return args, in_specs, out_specs, out_shape


def _sample_attention(sample, base, n_rows):
    args, in_specs, out_specs, out_shape = _sample_row_specs(sample, base, n_rows, lambda i: i)

    def body(*refs):
        _sample_attn_kernel(*refs, batch_row=base + pl.program_id(0))

    return pl.pallas_call(
        body,
        grid=(n_rows,),
        in_specs=in_specs,
        out_specs=tuple(out_specs),
        out_shape=tuple(out_shape),
        compiler_params=_params(1),
        name="sample_attn",
    )(*args)


def _sample_finish_kernel(x_ref, cb_ref, u0_ref, u1_ref, u2_ref, attn_ref, mem_ref, z_ref,
                          cw_ref, wout_ref, gf_ref, y_ref, mix):
    cw = cw_ref[...]
    conv = cw[0:1] * u0_ref[...] + cw[1:2] * u1_ref[...] + cw[2:3] * u2_ref[...]
    z = z_ref[...].astype(F32)
    lo, hi = CONV_W, CONV_W + ATTN_W
    mix[:, 0:lo] = (cb_ref[...].astype(F32) * conv * _silu(z[:, 0:lo])).astype(BF16)
    mix[:, lo:hi] = (attn_ref[...] * _silu(z[:, lo:hi])).astype(BF16)
    mix[:, hi:] = (mem_ref[...] * _silu(z[:, hi:])).astype(BF16)
    y_ref[...] = _gate_project_norm(x_ref[...], mix, wout_ref, gf_ref)


def _sample_finish(x, cb, u_taps, attn, mem, z, conv_w, w_out_bf16, g_final):
    rows, D = x.shape
    return pl.pallas_call(
        _sample_finish_kernel,
        out_shape=jax.ShapeDtypeStruct((rows, D), F32),
        scratch_shapes=[pltpu.VMEM((rows, D_MIX), BF16)],
        compiler_params=pltpu.CompilerParams(vmem_limit_bytes=VMEM_LIMIT),
        name="sample_finish",
    )(x, cb, *u_taps, attn, mem, z, conv_w, w_out_bf16, g_final.reshape(1, D))


def kernel(x_prompt, x_sample, mem_prompt, cache_win_k, cache_win_v, cache_conv, cache_mem_k,
           cache_mem_v, g_in, w_in, conv_w, g_mem, w_mem_kv, w_out, g_final):
    assert g_in.shape[0] == 1, "the final RMSNorm is fused into the single layer's tail"
    B, T, D = x_prompt.shape
    DB, S, _ = x_sample.shape
    n_keep = min(DILATED_CONFIGS[-1][0], T)
    w_in_b = w_in[0].astype(BF16)
    w_out_b = w_out[0].astype(BF16)
    dils = tuple(d for _, d in DILATED_CONFIGS)
    wkvt_b = _transposed_columns(w_in[0], _C_K, _C_MQ)

    def heads_last(t, heads):
        return jnp.transpose(t.reshape(t.shape[0], heads, HEAD_DIM, -1), (0, 3, 1, 2))[None]

    def heads_first(c):
        return jnp.transpose(c, (0, 2, 3, 1))

    rows = DB * S
    cb_s, u_s, q_s, _, _, k32, v32, mq_s, z_s = _project(
        x_sample.reshape(1, rows, D), g_in[0], w_in_b, wkvt_b, rows, False, (1,), Q_SCALE)
    k32 = k32.reshape(rows, ATTN_W)
    v32 = v32.reshape(rows, ATTN_W)
    sample = (q_s[0].reshape(rows, ATTN_W).astype(F32), k32, v32,
              heads_first(cache_win_k[0]), heads_first(cache_win_v[0]),
              mq_s.reshape(rows, MEM_W).astype(F32),
              heads_first(cache_mem_k[0]), heads_first(cache_mem_v[0]))
    n_proj = n_tail = B * (T // ROW_TILE)
    n_attn = B * (T // ATTN_TILE)
    n_own = DB - n_proj - n_attn - n_tail
    assert n_own > 0
    conv_y, u_tails, q, k, v, kt32, vt32, mq, _, attn_p, mem_p = _project(
        x_prompt, g_in[0], w_in_b, wkvt_b, n_keep, True, dils, Q_SCALE * LOG2E, sample, 0,
        conv_w[0], emit_z=False)
    mkt32, mvt32, mk, mv = _mem_kv(mem_prompt, g_mem[0], w_mem_kv[0].astype(BF16))
    attn, attn_a, mem_a = _prompt_attention(q, k, v, sample, n_proj)
    y_prompt, attn_t, mem_t = _prompt_finish(x_prompt, conv_y, attn, mq, mk, mv, g_in[0],
                                             w_in_b[:, _C_Z:_C_END], w_out_b, g_final, sample,
                                             n_proj + n_attn)
    attn_o, mem_o = _sample_attention(sample, n_proj + n_attn + n_tail, n_own)
    attn_s = jnp.concatenate([attn_p, attn_a, attn_t, attn_o], axis=0)
    mem_s = jnp.concatenate([mem_p, mem_a, mem_t, mem_o], axis=0)
    p_win_k = heads_last(kt32, ATTN_HEADS)
    p_win_v = heads_last(vt32, ATTN_HEADS)
    p_conv = u_tails[None, :, -(CONV_WIDTH - 1):]
    p_mem_k = heads_last(mkt32, MEM_HEADS)
    p_mem_v = heads_last(mvt32, MEM_HEADS)
    full = jnp.concatenate([cache_conv[0], u_s.reshape(DB, S, CONV_W)], axis=1)
    taps = [full[:, t:t + S].reshape(rows, CONV_W) for t in range(CONV_WIDTH)]
    y_sample = _sample_finish(x_sample.reshape(rows, D), cb_s.reshape(rows, CONV_W), taps,
                              attn_s.reshape(rows, ATTN_W), mem_s.reshape(rows, MEM_W),
                              z_s.reshape(rows, D_MIX), conv_w[0], w_out_b,
                              g_final).reshape(DB, S, D)
    s_win_k = k32.reshape(1, DB, S, ATTN_HEADS, HEAD_DIM)
    s_win_v = v32.reshape(1, DB, S, ATTN_HEADS, HEAD_DIM)
    s_conv = full[None, :, -(CONV_WIDTH - 1):]
    return (y_prompt, y_sample, p_win_k, p_win_v, p_conv, p_mem_k, p_mem_v,
            s_win_k, s_win_v, s_conv)
```

```python
import functools

import numpy as np
import jax
import jax.numpy as jnp
from jax import lax
from jax.experimental import pallas as pl
from jax.experimental.pallas import tpu as pltpu

F32 = jnp.float32
BF16 = jnp.bfloat16

HEAD_DIM = 64
ATTN_HEADS = 8
MEM_HEADS = 4
CONV_W = 256
ATTN_W = 512
MEM_W = 256
D_MIX = 1024
CONV_WIDTH = 3
DILATED_CONFIGS = ((128, 1), (512, 4), (2048, 16))
SPAN = 128
ALIBI_MAX = 8.0
RMS_EPS = 1e-6
Q_SCALE = HEAD_DIM ** -0.5
LOG2E = float(np.log2(np.e))

LANES = 128
SUBLANES = 8
VMEM_LIMIT = 56 * 1024 * 1024

_C_CB, _C_CC, _C_Q, _C_K, _C_V, _C_MQ, _C_Z, _C_END = 0, 256, 768, 1280, 1792, 2304, 2560, 3584

ROW_TILE = 512
ATTN_TILE = 512


def _slopes():
    h = np.arange(ATTN_HEADS, dtype=np.float64) + 1.0
    return np.exp2(-ALIBI_MAX * h / ATTN_HEADS)


def _params(n_axes):
    return pltpu.CompilerParams(
        dimension_semantics=("arbitrary",) * n_axes,
        vmem_limit_bytes=VMEM_LIMIT)


def _transpose_cast_kernel(w_ref, o_ref):
    o_ref[...] = w_ref[...].T.astype(BF16)


def _transposed_columns(w, lo, hi):
    K = w.shape[0]
    blk = 2 * LANES
    assert lo % blk == 0 and hi % blk == 0
    return pl.pallas_call(
        _transpose_cast_kernel,
        grid=((hi - lo) // blk,),
        in_specs=[pl.BlockSpec((K, blk), lambda i: (0, lo // blk + i))],
        out_specs=pl.BlockSpec((blk, K), lambda i: (i, 0)),
        out_shape=jax.ShapeDtypeStruct((hi - lo, K), BF16),
        compiler_params=_params(1),
        name="wkv_t",
    )(w)


def _fold_store(res, out_refs, slabs, dils):
    rows = res.shape[0]
    out_refs[0][0, 0] = res.astype(BF16)
    if len(dils) == 1:
        return
    n_slab = res.shape[1] // LANES
    src, prev_d = slabs[0], 1
    for s in range(n_slab):
        src[s] = res[:, s * LANES:(s + 1) * LANES]
    for idx, (d, o_ref) in enumerate(zip(dils[1:], out_refs[1:])):
        step, n = d // prev_d, rows // d
        dst = slabs[idx + 1] if idx + 2 < len(dils) else None
        for rp in range(prev_d):
            for q in range(step):
                r = q * prev_d + rp
                for s in range(n_slab):
                    val = src[s, pl.ds(rp * (rows // prev_d) + q, n, stride=step), :]
                    o_ref[0, r, :, s * LANES:(s + 1) * LANES] = val.astype(BF16)
                    if dst is not None:
                        dst[s, pl.ds(r * n, n), :] = val
        src, prev_d = dst, d


def _proj_kernel(x_ref, g_ref, w_ref, wkvt_ref, *refs, first, keep_transposed, dils, q_scale,
                 hosts_sample, fuse_conv, emit_z):
    n = len(dils)
    if fuse_conv:
        cw_ref, refs, ucat = refs[0], refs[1:-1], refs[-1]

        @pl.when(pl.program_id(1) == 0)
        def _():
            ucat[0:SUBLANES] = jnp.zeros((SUBLANES, CONV_W), F32)
    n_out = 5 + 3 * n + (1 if emit_z else 0)
    if hosts_sample:
        sample_ins, refs = refs[:N_SAMPLE_IN], refs[N_SAMPLE_IN:]
        sample_outs = refs[n_out:n_out + 2]
        refs = refs[:n_out] + refs[n_out + 2:]
        step = pl.program_id(0) * pl.num_programs(1) + pl.program_id(1)
        _sample_attn_kernel(*sample_ins, *sample_outs, batch_row=hosts_sample[1] + step)
    cb_ref, u_ref = refs[0:2]
    q_refs, k_refs, v_refs = refs[2:2 + n], refs[2 + n:2 + 2 * n], refs[2 + 2 * n:2 + 3 * n]
    k32_ref, v32_ref, mq_ref = refs[2 + 3 * n:5 + 3 * n]
    scratch = refs[n_out:]
    slabs = [scratch[a * (n - 1):(a + 1) * (n - 1)] for a in range(3)]
    x = x_ref[0]
    r = lax.rsqrt(jnp.mean(x * x, axis=-1, keepdims=True) + RMS_EPS)
    h = (x * r * g_ref[...]).astype(BF16)

    def seg(lo, hi):
        return jnp.dot(h, w_ref[:, lo:hi], preferred_element_type=F32)

    cb = seg(_C_CB, _C_CC)
    cch = seg(_C_CC, _C_Q)
    u = cch[:, :CONV_W] * cch[:, CONV_W:]
    if fuse_conv:
        rows = u.shape[0]
        ucat[SUBLANES:] = u
        cw = cw_ref[...]
        conv = (cw[0:1] * ucat[pl.ds(SUBLANES - 2, rows)]
                + cw[1:2] * ucat[pl.ds(SUBLANES - 1, rows)]
                + cw[2:3] * ucat[pl.ds(SUBLANES, rows)])
        cb_ref[0] = (cb * conv).astype(BF16)
        u_ref[0] = u[rows - SUBLANES:]
        ucat[0:SUBLANES] = u[rows - SUBLANES:]
    else:
        cb_ref[0] = cb.astype(BF16)
        u_ref[0] = u
    _fold_store(seg(_C_Q, _C_K) * q_scale, q_refs, slabs[0], dils)
    k = seg(_C_K, _C_V)
    _fold_store(k, k_refs, slabs[1], dils)
    v = seg(_C_V, _C_MQ)
    _fold_store(v, v_refs, slabs[2], dils)
    mq_ref[0] = (seg(_C_MQ, _C_Z) * q_scale).astype(BF16)
    if emit_z:
        refs[5 + 3 * n][0] = seg(_C_Z, _C_END).astype(BF16)
    if keep_transposed:
        @pl.when(pl.program_id(1) >= first)
        def _():
            nt = (((1,), (1,)), ((), ()))
            k32_ref[0] = lax.dot_general(wkvt_ref[0:ATTN_W], h, nt, preferred_element_type=F32)
            v32_ref[0] = lax.dot_general(wkvt_ref[ATTN_W:], h, nt, preferred_element_type=F32)
    else:
        k32_ref[0] = k
        v32_ref[0] = v


def _project(x, g, w_bf16, wkvt_bf16, n_keep, keep_transposed, dils, q_scale, sample=None,
             sample_base=0, conv_w=None, emit_z=True):
    B, T, D = x.shape
    tm = ROW_TILE
    nt = T // tm
    first = (T - n_keep) // tm
    n = len(dils)

    def row(width):
        return pl.BlockSpec((1, tm, width), lambda b, i: (b, i, 0))

    folded_specs = [pl.BlockSpec((1, d, tm // d, ATTN_W), lambda b, i: (b, 0, i, 0)) for d in dils]
    folded_shapes = [jax.ShapeDtypeStruct((B, d, T // d, ATTN_W), BF16) for d in dils]

    if keep_transposed:
        keep = pl.BlockSpec((1, ATTN_W, tm), lambda b, i: (b, 0, jnp.maximum(i - first, 0)))
        keep_shape = (B, ATTN_W, n_keep)
    else:
        keep = pl.BlockSpec((1, tm, ATTN_W), lambda b, i: (b, jnp.maximum(i - first, 0), 0))
        keep_shape = (B, n_keep, ATTN_W)
    fuse_conv = conv_w is not None
    u_rows = SUBLANES if fuse_conv else tm
    out_shape = (
        [jax.ShapeDtypeStruct((B, T, CONV_W), BF16),
         jax.ShapeDtypeStruct((B, nt * u_rows, CONV_W), F32)]
        + folded_shapes * 3
        + [jax.ShapeDtypeStruct(keep_shape, F32),
           jax.ShapeDtypeStruct(keep_shape, F32),
           jax.ShapeDtypeStruct((B, T, MEM_W), BF16)])
    out_specs = ([row(CONV_W), pl.BlockSpec((1, u_rows, CONV_W), lambda b, i: (b, i, 0))]
                 + folded_specs * 3 + [keep, keep, row(MEM_W)])
    if emit_z:
        out_shape.append(jax.ShapeDtypeStruct((B, T, D_MIX), BF16))
        out_specs.append(row(D_MIX))
    scratch = [pltpu.VMEM((ATTN_W // LANES, tm, LANES), F32)] * (3 * (n - 1))
    once = pl.Buffered(1)
    in_specs = [row(D),
                pl.BlockSpec((1, D), lambda b, i: (0, 0)),
                pl.BlockSpec(w_bf16.shape, lambda b, i: (0, 0), pipeline_mode=once),
                pl.BlockSpec(wkvt_bf16.shape, lambda b, i: (0, 0), pipeline_mode=once)]
    args = [x, g.reshape(1, D), w_bf16, wkvt_bf16]
    if fuse_conv:
        args.append(conv_w)
        in_specs.append(pl.BlockSpec(conv_w.shape, lambda b, i: (0, 0)))
        scratch = scratch + [pltpu.VMEM((tm + SUBLANES, CONV_W), F32)]
    if sample is not None:
        s_args, s_in, s_out, s_shape = _sample_row_specs(sample, sample_base, B * nt,
                                                         lambda b, i: b * nt + i)
        args, in_specs = args + s_args, in_specs + s_in
        out_specs, out_shape = out_specs + s_out, out_shape + s_shape
    res = pl.pallas_call(
        functools.partial(_proj_kernel, first=first, keep_transposed=keep_transposed, dils=dils,
                          q_scale=q_scale,
                          hosts_sample=(True, sample_base) if sample is not None else None,
                          fuse_conv=fuse_conv, emit_z=emit_z),
        grid=(B, nt),
        in_specs=in_specs,
        out_specs=tuple(out_specs),
        out_shape=tuple(out_shape),
        scratch_shapes=scratch,
        compiler_params=_params(2),
        name="proj",
    )(*args)
    cb, u = res[0:2]
    q, k, v = res[2:2 + n], res[2 + n:2 + 2 * n], res[2 + 2 * n:2 + 3 * n]
    k32, v32, mq = res[2 + 3 * n:5 + 3 * n]
    n_out = 5 + 3 * n + (1 if emit_z else 0)
    z = res[5 + 3 * n] if emit_z else None
    return (cb, u, q, k, v, k32, v32, mq, z) + tuple(res[n_out:])


def _memkv_kernel(x_ref, g_ref, w_ref, k32_ref, v32_ref, k_ref, v_ref):
    x = x_ref[...]
    r = lax.rsqrt(jnp.mean(x * x, axis=-1, keepdims=True) + RMS_EPS)
    h = (x * r * g_ref[...]).astype(BF16)
    kv = jnp.dot(h, w_ref[...], preferred_element_type=F32)
    k_ref[...] = kv[:, :MEM_W].astype(BF16)
    v_ref[...] = kv[:, MEM_W:].astype(BF16)
    n_mem = k32_ref.shape[2]
    for b in range(k32_ref.shape[0]):
        kvt = kv[b * n_mem:(b + 1) * n_mem].T
        k32_ref[b] = kvt[:MEM_W]
        v32_ref[b] = kvt[MEM_W:]


def _mem_kv(mem, g, w_bf16):
    B, M, D = mem.shape
    rows = B * M
    k32, v32, k, v = pl.pallas_call(
        _memkv_kernel,
        out_shape=(jax.ShapeDtypeStruct((B, MEM_W, M), F32),
                   jax.ShapeDtypeStruct((B, MEM_W, M), F32),
                   jax.ShapeDtypeStruct((rows, MEM_W), BF16),
                   jax.ShapeDtypeStruct((rows, MEM_W), BF16)),
        compiler_params=pltpu.CompilerParams(vmem_limit_bytes=VMEM_LIMIT),
        name="memkv",
    )(mem.reshape(rows, D), g.reshape(1, D), w_bf16)
    return k32, v32, k.reshape(B, M, MEM_W), v.reshape(B, M, MEM_W)


def _band_bias_table():
    qi = np.arange(SPAN)[:, None] + SPAN
    ki = np.arange(2 * SPAN)[None, :]
    dist = qi - ki
    valid = (dist >= 0) & (dist <= SPAN)
    slopes = _slopes()
    tabs = []
    for _, dil in DILATED_CONFIGS:
        for first in (False, True):
            ok = valid & (ki >= SPAN) if first else valid
            for h in range(ATTN_HEADS):
                bias = -(slopes[h] * dil) * dist * LOG2E
                tabs.append(np.where(ok, bias, -np.inf))
    return np.stack(tabs).astype(np.float32)


def _pair_attention(qb, kb, vb, bias_lo, bias_hi):
    lane = lax.broadcasted_iota(jnp.int32, qb.shape, 1)
    lo = lane < HEAD_DIM
    zero = jnp.zeros_like(qb)
    v_ones = jnp.concatenate([vb, jnp.ones_like(vb)], axis=1)
    accs, ms, ls = [], [], []
    for sel, bias in ((lo, bias_lo), (~lo, bias_hi)):
        qm = jnp.where(sel, qb, zero)
        s = lax.dot_general(qm, kb, (((1,), (1,)), ((), ())), preferred_element_type=F32)
        if bias is not None:
            s = s + bias
        m = jnp.max(s, axis=-1, keepdims=True)
        p = jnp.exp2(s - m)
        acc = jnp.dot(p.astype(BF16), v_ones, preferred_element_type=F32)
        ms.append(m)
        accs.append(acc[:, :LANES])
        ls.append(acc[:, LANES:])
    o = jnp.where(lo, accs[0], accs[1]) / jnp.where(lo, ls[0], ls[1])
    return o, ms, ls


def _attn_kernel(*refs, steps, sample_base):
    ins = refs[:15]
    tab_ref = refs[15]
    sample_ins = refs[16:16 + N_SAMPLE_IN]
    outs = refs[16 + N_SAMPLE_IN:22 + N_SAMPLE_IN]
    sample_outs = refs[22 + N_SAMPLE_IN:24 + N_SAMPLE_IN]
    kcat, vcat = refs[24 + N_SAMPLE_IN:]
    s = pl.program_id(1)
    nblk = ATTN_TILE // SPAN
    _sample_attn_kernel(*sample_ins, *sample_outs,
                        batch_row=sample_base + pl.program_id(0) * steps + s)
    for c in range(3):
        _, kp, ko, vp, vo = ins[5 * c:5 * c + 5]
        kcat[c, 0:SPAN] = kp[0, 0]
        kcat[c, SPAN:] = ko[0, 0, 0:SPAN]
        vcat[c, 0:SPAN] = vp[0, 0]
        vcat[c, SPAN:] = vo[0, 0, 0:SPAN]

    for j in range(nblk):
        rows = slice(j * SPAN, (j + 1) * SPAN)
        keys = slice((j - 1) * SPAN, (j + 1) * SPAN)
        for c, (_, dil) in enumerate(DILATED_CONFIGS):
            q_ref, _, ko, _, vo = ins[5 * c:5 * c + 5]
            o_ref, l_ref = outs[2 * c], outs[2 * c + 1]
            units_per_residue = steps // dil
            if j == 0:
                base = (c * 2 + (s % units_per_residue == 0).astype(jnp.int32)) * ATTN_HEADS
            else:
                base = c * 2 * ATTN_HEADS
            lane = lax.broadcasted_iota(jnp.int32, (SPAN, LANES), 1)
            m8 = jnp.zeros((SPAN, LANES), F32)
            l8 = jnp.ones((SPAN, LANES), F32)
            for p in range(ATTN_HEADS // 2):
                lanes = slice(p * LANES, (p + 1) * LANES)
                qb = q_ref[0, 0, rows, lanes]
                if j == 0:
                    kb, vb = kcat[c, :, lanes], vcat[c, :, lanes]
                else:
                    kb, vb = ko[0, 0, keys, lanes], vo[0, 0, keys, lanes]
                o, ms, ls = _pair_attention(qb, kb, vb, tab_ref[base + 2 * p],
                                            tab_ref[base + 2 * p + 1])
                o_ref[0, 0, rows, lanes] = o.astype(BF16)
                for half in range(2):
                    m8 = jnp.where(lane == 2 * p + half, ms[half], m8)
                    l8 = jnp.where(lane == 2 * p + half, ls[half], l8)
            l_ref[0, 0, rows, :] = m8 + jnp.log2(l8)


def _prompt_attention(q, k, v, sample, sample_base):
    B, _, T, W = q[0].shape
    steps = T // ATTN_TILE
    prev_per_tile = ATTN_TILE // SPAN
    args, in_specs, out_specs, out_shape = [], [], [], []
    for c, (_, dil) in enumerate(DILATED_CONFIGS):
        L = T // dil
        upr = L // ATTN_TILE

        def own(b, s, upr=upr):
            return (b, s // upr, s % upr, 0)

        def prev(b, s, upr=upr):
            return (b, s // upr, jnp.maximum((s % upr) * prev_per_tile - 1, 0), 0)

        shape = (B, dil, L, W)
        assert q[c].shape == shape
        args += [q[c], k[c], k[c], v[c], v[c]]
        in_specs += [pl.BlockSpec((1, 1, ATTN_TILE, W), own),
                     pl.BlockSpec((1, 1, SPAN, W), prev),
                     pl.BlockSpec((1, 1, ATTN_TILE, W), own),
                     pl.BlockSpec((1, 1, SPAN, W), prev),
                     pl.BlockSpec((1, 1, ATTN_TILE, W), own)]
        out_specs += [pl.BlockSpec((1, 1, ATTN_TILE, W), own),
                      pl.BlockSpec((1, 1, ATTN_TILE, LANES), own)]
        out_shape += [jax.ShapeDtypeStruct(shape, BF16),
                      jax.ShapeDtypeStruct((B, dil, L, LANES), F32)]
    tab = jnp.asarray(_band_bias_table())
    args.append(tab)
    in_specs.append(pl.BlockSpec(tab.shape, lambda b, s: (0, 0, 0)))
    s_args, s_in, s_out, s_shape = _sample_row_specs(sample, sample_base, B * steps,
                                                     lambda b, s: b * steps + s)
    res = pl.pallas_call(
        functools.partial(_attn_kernel, steps=steps, sample_base=sample_base),
        grid=(B, steps),
        in_specs=in_specs + s_in,
        out_specs=tuple(out_specs + s_out),
        out_shape=tuple(out_shape + s_shape),
        scratch_shapes=[pltpu.VMEM((3, 2 * SPAN, W), BF16)] * 2,
        compiler_params=_params(2),
        name="prompt_attn",
    )(*args, *s_args)
    return list(res[:6]), res[6], res[7]


def _silu(z):
    return z / (1.0 + jnp.exp(-z))


def _gate_project_norm(x, mix_ref, wout_ref, gf_ref):
    y = x + jnp.dot(mix_ref[...], wout_ref[...], preferred_element_type=F32)
    r = lax.rsqrt(jnp.mean(y * y, axis=-1, keepdims=True) + RMS_EPS)
    return y * r * gf_ref[...]


def _unfold(src_ref, slabs, d, tmp=None, step=4):
    n_slab, rows = slabs.shape[0], slabs.shape[1]

    def piece(r, s):
        return src_ref[0, r, :, s * LANES:(s + 1) * LANES].astype(F32)

    if d <= step:
        for r in range(d):
            for s in range(n_slab):
                slabs[s, pl.ds(r, rows // d, stride=d), :] = piece(r, s)
        return
    prev_d, n = d // step, rows // d
    per = rows // prev_d
    for rp in range(prev_d):
        for q in range(step):
            for s in range(n_slab):
                tmp[s, pl.ds(rp * per + q, n, stride=step), :] = piece(q * prev_d + rp, s)
    for rp in range(prev_d):
        for s in range(n_slab):
            slabs[s, pl.ds(rp, per, stride=prev_d), :] = tmp[s, pl.ds(rp * per, per), :]


def _head_expander():
    e = np.zeros((2 * LANES, ATTN_W), np.float32)
    for h in range(ATTN_HEADS):
        e[h, h * HEAD_DIM:(h + 1) * HEAD_DIM] = 1.0
        e[LANES + h, h * HEAD_DIM:(h + 1) * HEAD_DIM] = 1.0
    return e


N_FINISH_IN = 14


def _finish_kernel(*refs, sample_base):
    (x_ref, cy_ref, o1_ref, l1_ref, o4_ref, l4_ref, o16_ref, l16_ref,
     mem_ref, gin_ref, wz_ref, wout_ref, gf_ref, ex_ref) = refs[:N_FINISH_IN]
    sample_ins = refs[N_FINISH_IN:N_FINISH_IN + N_SAMPLE_IN]
    y_ref = refs[N_FINISH_IN + N_SAMPLE_IN]
    sample_outs = refs[N_FINISH_IN + N_SAMPLE_IN + 1:N_FINISH_IN + N_SAMPLE_IN + 3]
    mix, o4s, l4s, o16s, l16s, otmp, ltmp = refs[N_FINISH_IN + N_SAMPLE_IN + 3:]
    i = pl.program_id(1)
    _sample_attn_kernel(*sample_ins, *sample_outs,
                        batch_row=sample_base + pl.program_id(0) * pl.num_programs(1) + i)
    x = x_ref[0]
    r = lax.rsqrt(jnp.mean(x * x, axis=-1, keepdims=True) + RMS_EPS)
    z = jnp.dot((x * r * gin_ref[...]).astype(BF16), wz_ref[...], preferred_element_type=F32)
    mix[:, 0:CONV_W] = (cy_ref[0].astype(F32) * _silu(z[:, 0:CONV_W])).astype(BF16)
    dil4, dil16 = DILATED_CONFIGS[1][1], DILATED_CONFIGS[2][1]
    _unfold(o4_ref, o4s, dil4)
    _unfold(l4_ref, l4s, dil4)
    _unfold(o16_ref, o16s, dil16, otmp)
    _unfold(l16_ref, l16s, dil16, ltmp)
    lo, hi = CONV_W, CONV_W + ATTN_W
    l1, l4, l16 = l1_ref[0, 0], l4s[0], l16s[0]
    mx = jnp.maximum(jnp.maximum(l1, l4), l16)
    e1, e4, e16 = jnp.exp2(l1 - mx), jnp.exp2(l4 - mx), jnp.exp2(l16 - mx)
    den = e1 + e4 + e16

    def spread(w):
        w_hi = w.astype(BF16)
        w_lo = (w - w_hi.astype(F32)).astype(BF16)
        return jnp.dot(jnp.concatenate([w_hi, w_lo], axis=1), ex_ref[...],
                       preferred_element_type=F32)

    a1, a4, a16 = spread(e1 / den), spread(e4 / den), spread(e16 / den)
    for s in range(ATTN_W // LANES):
        lanes = slice(s * LANES, (s + 1) * LANES)
        attn = (a1[:, lanes] * o1_ref[0, 0, :, lanes].astype(F32) + a4[:, lanes] * o4s[s]
                + a16[:, lanes] * o16s[s])
        a = lo + s * LANES
        mix[:, a:a + LANES] = (attn * _silu(z[:, a:a + LANES])).astype(BF16)
    mix[:, hi:] = (mem_ref[0].astype(F32) * _silu(z[:, hi:])).astype(BF16)
    y_ref[0] = _gate_project_norm(x_ref[0], mix, wout_ref, gf_ref)


def _prompt_finish(x, conv_y, attn, mem_o, g_in, w_gate_bf16, w_out_bf16, g_final, sample,
                   sample_base):
    B, T, D = x.shape
    tm = ROW_TILE

    def row(width):
        return pl.BlockSpec((1, tm, width), lambda b, i: (b, i, 0))

    def const(shape):
        return pl.BlockSpec(shape, lambda b, i: (0,) * len(shape))

    in_specs = [row(D), row(CONV_W)]
    for _, dil in DILATED_CONFIGS:
        in_specs += [pl.BlockSpec((1, dil, tm // dil, ATTN_W), lambda b, i: (b, 0, i, 0)),
                     pl.BlockSpec((1, dil, tm // dil, LANES), lambda b, i: (b, 0, i, 0))]
    expander = jnp.asarray(_head_expander(), BF16)
    in_specs += [row(MEM_W), const((1, D)), const(w_gate_bf16.shape),
                 const(w_out_bf16.shape), const((1, D)), const(expander.shape)]
    assert len(in_specs) == N_FINISH_IN
    o_slabs = pltpu.VMEM((ATTN_W // LANES, tm, LANES), F32)
    l_slabs = pltpu.VMEM((1, tm, LANES), F32)
    nt = T // tm
    s_args, s_in, s_out, s_shape = _sample_row_specs(sample, sample_base, B * nt,
                                                     lambda b, i: b * nt + i)
    return pl.pallas_call(
        functools.partial(_finish_kernel, sample_base=sample_base),
        grid=(B, nt),
        in_specs=in_specs + s_in,
        out_specs=tuple([row(D)] + s_out),
        out_shape=tuple([jax.ShapeDtypeStruct((B, T, D), F32)] + s_shape),
        scratch_shapes=[pltpu.VMEM((tm, D_MIX), BF16),
                        o_slabs, l_slabs, o_slabs, l_slabs, o_slabs, l_slabs],
        compiler_params=_params(2),
        name="prompt_finish",
    )(x, conv_y, *attn, mem_o, g_in.reshape(1, D), w_gate_bf16, w_out_bf16,
      g_final.reshape(1, D), expander, *s_args)


def _sample_tables(S, n_buf):
    j = np.arange(S)[:, None]
    t = np.arange(n_buf)[None, :]
    dist = n_buf + j - t
    count = np.zeros(dist.shape)
    for win, dil in DILATED_CONFIGS:
        count += (dist % dil == 0) & (dist <= win)
    bias = np.where(count > 0, -_slopes()[:, None, None] * dist, -np.inf)
    bias = bias.reshape(ATTN_HEADS // 2, 2 * S, n_buf)
    count = np.concatenate([count, count], axis=0)
    return count.astype(np.float32), bias.astype(np.float32)


def _stack_pair(a, p):
    ap = a[:, p * LANES:(p + 1) * LANES]
    lo = lax.broadcasted_iota(jnp.int32, ap.shape, 1) < HEAD_DIM
    zero = jnp.zeros_like(ap)
    return jnp.concatenate([jnp.where(lo, ap, zero), jnp.where(lo, zero, ap)], axis=0)


def _unstack_pair(o, S):
    lo = lax.broadcasted_iota(jnp.int32, (S, LANES), 1) < HEAD_DIM
    return jnp.where(lo, o[:S], o[S:])


def _sample_attn_kernel(q_ref, kn_ref, vn_ref, kt_ref, vt_ref, mq_ref, mkt_ref, mvt_ref,
                        cnt_ref, bias_ref, o_ref, mo_ref, *, batch_row):
    R, S = kt_ref.shape[0], o_ref.shape[1]
    nt = (((1,), (1,)), ((), ()))
    slopes = _slopes()
    n_cfg = float(len(DILATED_CONFIGS))
    row = lax.broadcasted_iota(jnp.int32, (2 * S, 1), 0)
    upper = row >= S
    rowj = jnp.where(upper, row - S, row)
    cnt = cnt_ref[...]
    rows = range(R)
    which = batch_row % (SUBLANES // S)

    def tokens(ref):
        blk = ref[...]
        tok = blk[0:S]
        for i in range(1, SUBLANES // S):
            tok = jnp.where(which == i, blk[i * S:(i + 1) * S], tok)
        return tok

    q_tok, kn_tok, vn_tok, mq_tok = (tokens(ref) for ref in (q_ref, kn_ref, vn_ref, mq_ref))

    def lane_group(tok, p):
        return tok[:, p * LANES:(p + 1) * LANES]

    def pair_rows(ref, r, p):
        return ref[r, 2 * p:2 * p + 2].reshape(2 * HEAD_DIM, ref.shape[-1]).astype(BF16)

    units = [(r, p) for r in rows for p in range(ATTN_HEADS // 2)]
    munits = [(r, p) for r in rows for p in range(MEM_HEADS // 2)]
    qs = [_stack_pair(q_tok, p) for r, p in units]
    scores = [jnp.dot(q8.astype(BF16), pair_rows(kt_ref, r, p),
                      preferred_element_type=F32) + bias_ref[p] for (r, p), q8 in zip(units, qs)]
    mscores = [jnp.dot(_stack_pair(mq_tok, p).astype(BF16), pair_rows(mkt_ref, r, p),
                       preferred_element_type=F32) for r, p in munits]
    probs, stats = [], []
    for (r, p), q8, s in zip(units, qs, scores):
        knp = lane_group(kn_tok, p)
        slope = jnp.where(upper, float(slopes[2 * p + 1]), float(slopes[2 * p]))
        m = jnp.max(s, axis=-1, keepdims=True)
        s_new = []
        for jp in range(S):
            d = rowj - jp
            sj = jnp.sum(q8 * knp[jp:jp + 1], axis=-1, keepdims=True)
            sj = jnp.where(d >= 0, sj - slope * d.astype(F32), -jnp.inf)
            s_new.append(sj)
            m = jnp.maximum(m, sj)
        pr = cnt * jnp.exp(s - m)
        l = jnp.sum(pr, axis=-1, keepdims=True)
        w_new = []
        for jp in range(S):
            w = jnp.where(rowj == jp, n_cfg, 1.0) * jnp.exp(s_new[jp] - m)
            l = l + w
            w_new.append(w)
        probs.append(pr.astype(BF16))
        stats.append((l, w_new))
    mprobs, mden = [], []
    for s in mscores:
        pr = jnp.exp(s - jnp.max(s, axis=-1, keepdims=True))
        mden.append(jnp.sum(pr, axis=-1, keepdims=True))
        mprobs.append(pr.astype(BF16))
    accs = [lax.dot_general(pr, pair_rows(vt_ref, r, p), nt, preferred_element_type=F32)
            for (r, p), pr in zip(units, probs)]
    maccs = [lax.dot_general(pr, pair_rows(mvt_ref, r, p), nt, preferred_element_type=F32)
             for (r, p), pr in zip(munits, mprobs)]
    outs = []
    for (r, p), acc, (l, w_new) in zip(units, accs, stats):
        vnp = lane_group(vn_tok, p)
        for jp in range(S):
            acc = acc + w_new[jp] * vnp[jp:jp + 1]
        outs.append(_unstack_pair(acc / l, S))
    mouts = [_unstack_pair(acc / den, S) for acc, den in zip(maccs, mden)]
    n_p, n_mp = ATTN_HEADS // 2, MEM_HEADS // 2
    for r in rows:
        o_ref[r] = jnp.concatenate(outs[r * n_p:(r + 1) * n_p], axis=-1)
        mo_ref[r] = jnp.concatenate(mouts[r * n_mp:(r + 1) * n_mp], axis=-1)


N_SAMPLE_IN = 10


def _sample_row_specs(sample, base, n_rows, step_of):
    q, k_new, v_new, cache_kt, cache_vt, mq, mem_kt, mem_vt = sample
    DB, n_buf = cache_kt.shape[0], cache_kt.shape[-1]
    S, W = q.shape[0] // DB, q.shape[1]
    assert SUBLANES % S == 0, "token rows are fetched in 8-row blocks"
    assert n_buf >= DILATED_CONFIGS[-1][0] and S <= DILATED_CONFIGS[1][1], \
        "new tokens reach each other through dilation 1 only; the buffer covers every window"
    cnt, bias = (jnp.asarray(t) for t in _sample_tables(S, n_buf))
    per_block = SUBLANES // S

    def tokens_in(a):
        return pl.BlockSpec((SUBLANES, a.shape[1]),
                            lambda *g: ((base + step_of(*g)) // per_block, 0))

    def row_in(a):
        return pl.BlockSpec((1,) + a.shape[1:],
                            lambda *g: (base + step_of(*g),) + (0,) * (a.ndim - 1))

    def row_out(width):
        return pl.BlockSpec((1, S, width), lambda *g: (step_of(*g), 0, 0))

    def const(a):
        return pl.BlockSpec(a.shape, lambda *g: (0,) * a.ndim)

    args = list(sample) + [cnt, bias]
    assert len(args) == N_SAMPLE_IN
    in_specs = [tokens_in(q), tokens_in(k_new), tokens_in(v_new), row_in(cache_kt),
                row_in(cache_vt), tokens_in(mq), row_in(mem_kt), row_in(mem_vt),
                const(cnt), const(bias)]
    out_specs = [row_out(W), row_out(MEM_W)]
    out_shape = [jax.ShapeDtypeStruct((n_rows, S, W), F32),
                 jax.ShapeDtypeStruct((n_rows, S, MEM_W), F32)]
    return args, in_specs, out_specs, out_shape


def _sample_attention(sample, base, n_rows, prompt_mq, prompt_mk, prompt_mv):
    args, in_specs, out_specs, out_shape = _sample_row_specs(sample, base, n_rows, lambda i: i)
    B, T, _ = prompt_mq.shape
    assert n_rows % B == 0 and T % (n_rows // B) == 0
    per_batch = n_rows // B
    blk = T // per_batch
    pmq = prompt_mq.reshape(n_rows, blk, MEM_W)
    pm_rows = pl.BlockSpec((1, blk, MEM_W), lambda i: (i, 0, 0))
    pm_kv = pl.BlockSpec((1,) + prompt_mk.shape[1:], lambda i: (i // per_batch, 0, 0))

    def body(*refs):
        sample_refs = refs[:N_SAMPLE_IN] + refs[N_SAMPLE_IN + 3:N_SAMPLE_IN + 5]
        pmq_ref, pmk_ref, pmv_ref = refs[N_SAMPLE_IN:N_SAMPLE_IN + 3]
        pmo_ref = refs[N_SAMPLE_IN + 5]
        _sample_attn_kernel(*sample_refs, batch_row=base + pl.program_id(0))
        for p in range(MEM_HEADS // 2):
            lanes = slice(p * LANES, (p + 1) * LANES)
            o, _, _ = _pair_attention(pmq_ref[0, :, lanes], pmk_ref[0, :, lanes],
                                      pmv_ref[0, :, lanes], None, None)
            pmo_ref[0, :, lanes] = o.astype(BF16)

    attn, mem, pmo = pl.pallas_call(
        body,
        grid=(n_rows,),
        in_specs=in_specs + [pm_rows, pm_kv, pm_kv],
        out_specs=tuple(out_specs + [pm_rows]),
        out_shape=tuple(out_shape + [jax.ShapeDtypeStruct(pmq.shape, BF16)]),
        compiler_params=_params(1),
        name="sample_attn",
    )(*args, pmq, prompt_mk, prompt_mv)
    return attn, mem, pmo.reshape(B, T, MEM_W)


def _sample_finish_kernel(x_ref, cb_ref, u0_ref, u1_ref, u2_ref, attn_ref, mem_ref, z_ref,
                          cw_ref, wout_ref, gf_ref, y_ref, mix):
    cw = cw_ref[...]
    conv = cw[0:1] * u0_ref[...] + cw[1:2] * u1_ref[...] + cw[2:3] * u2_ref[...]
    z = z_ref[...].astype(F32)
    lo, hi = CONV_W, CONV_W + ATTN_W
    mix[:, 0:lo] = (cb_ref[...].astype(F32) * conv * _silu(z[:, 0:lo])).astype(BF16)
    mix[:, lo:hi] = (attn_ref[...] * _silu(z[:, lo:hi])).astype(BF16)
    mix[:, hi:] = (mem_ref[...] * _silu(z[:, hi:])).astype(BF16)
    y_ref[...] = _gate_project_norm(x_ref[...], mix, wout_ref, gf_ref)


def _sample_finish(x, cb, u_taps, attn, mem, z, conv_w, w_out_bf16, g_final):
    rows, D = x.shape
    return pl.pallas_call(
        _sample_finish_kernel,
        out_shape=jax.ShapeDtypeStruct((rows, D), F32),
        scratch_shapes=[pltpu.VMEM((rows, D_MIX), BF16)],
        compiler_params=pltpu.CompilerParams(vmem_limit_bytes=VMEM_LIMIT),
        name="sample_finish",
    )(x, cb, *u_taps, attn, mem, z, conv_w, w_out_bf16, g_final.reshape(1, D))


def kernel(x_prompt, x_sample, mem_prompt, cache_win_k, cache_win_v, cache_conv, cache_mem_k,
           cache_mem_v, g_in, w_in, conv_w, g_mem, w_mem_kv, w_out, g_final):
    assert g_in.shape[0] == 1, "the final RMSNorm is fused into the single layer's tail"
    B, T, D = x_prompt.shape
    DB, S, _ = x_sample.shape
    n_keep = min(DILATED_CONFIGS[-1][0], T)
    w_in_b = w_in[0].astype(BF16)
    w_out_b = w_out[0].astype(BF16)
    dils = tuple(d for _, d in DILATED_CONFIGS)
    wkvt_b = _transposed_columns(w_in[0], _C_K, _C_MQ)

    def heads_last(t, heads):
        return jnp.transpose(t.reshape(t.shape[0], heads, HEAD_DIM, -1), (0, 3, 1, 2))[None]

    def heads_first(c):
        return jnp.transpose(c, (0, 2, 3, 1))

    rows = DB * S
    cb_s, u_s, q_s, _, _, k32, v32, mq_s, z_s = _project(
        x_sample.reshape(1, rows, D), g_in[0], w_in_b, wkvt_b, rows, False, (1,), Q_SCALE)
    k32 = k32.reshape(rows, ATTN_W)
    v32 = v32.reshape(rows, ATTN_W)
    sample = (q_s[0].reshape(rows, ATTN_W).astype(F32), k32, v32,
              heads_first(cache_win_k[0]), heads_first(cache_win_v[0]),
              mq_s.reshape(rows, MEM_W).astype(F32),
              heads_first(cache_mem_k[0]), heads_first(cache_mem_v[0]))
    n_proj = n_tail = B * (T // ROW_TILE)
    n_attn = B * (T // ATTN_TILE)
    n_own = DB - n_proj - n_attn - n_tail
    assert n_own > 0
    conv_y, u_tails, q, k, v, kt32, vt32, mq, _, attn_p, mem_p = _project(
        x_prompt, g_in[0], w_in_b, wkvt_b, n_keep, True, dils, Q_SCALE * LOG2E, sample, 0,
        conv_w[0], emit_z=False)
    mkt32, mvt32, mk, mv = _mem_kv(mem_prompt, g_mem[0], w_mem_kv[0].astype(BF16))
    attn, attn_a, mem_a = _prompt_attention(q, k, v, sample, n_proj)
    attn_o, mem_o, mem_prompt_o = _sample_attention(sample, n_proj + n_attn + n_tail, n_own,
                                                    mq, mk, mv)
    y_prompt, attn_t, mem_t = _prompt_finish(x_prompt, conv_y, attn, mem_prompt_o, g_in[0],
                                             w_in_b[:, _C_Z:_C_END], w_out_b, g_final, sample,
                                             n_proj + n_attn)
    attn_s = jnp.concatenate([attn_p, attn_a, attn_t, attn_o], axis=0)
    mem_s = jnp.concatenate([mem_p, mem_a, mem_t, mem_o], axis=0)
    p_win_k = heads_last(kt32, ATTN_HEADS)
    p_win_v = heads_last(vt32, ATTN_HEADS)
    p_conv = u_tails[None, :, -(CONV_WIDTH - 1):]
    p_mem_k = heads_last(mkt32, MEM_HEADS)
    p_mem_v = heads_last(mvt32, MEM_HEADS)
    full = jnp.concatenate([cache_conv[0], u_s.reshape(DB, S, CONV_W)], axis=1)
    taps = [full[:, t:t + S].reshape(rows, CONV_W) for t in range(CONV_WIDTH)]
    y_sample = _sample_finish(x_sample.reshape(rows, D), cb_s.reshape(rows, CONV_W), taps,
                              attn_s.reshape(rows, ATTN_W), mem_s.reshape(rows, MEM_W),
                              z_s.reshape(rows, D_MIX), conv_w[0], w_out_b,
                              g_final).reshape(DB, S, D)
    s_win_k = k32.reshape(1, DB, S, ATTN_HEADS, HEAD_DIM)
    s_win_v = v32.reshape(1, DB, S, ATTN_HEADS, HEAD_DIM)
    s_conv = full[None, :, -(CONV_WIDTH - 1):]
    return (y_prompt, y_sample, p_win_k, p_win_v, p_conv, p_mem_k, p_mem_v,
            s_win_k, s_win_v, s_conv)
```

```python
import functools

import numpy as np
import jax
import jax.numpy as jnp
from jax import lax
from jax.experimental import pallas as pl
from jax.experimental.pallas import tpu as pltpu

F32 = jnp.float32
BF16 = jnp.bfloat16

HEAD_DIM = 64
ATTN_HEADS = 8
MEM_HEADS = 4
CONV_W = 256
ATTN_W = 512
MEM_W = 256
D_MIX = 1024
CONV_WIDTH = 3
DILATED_CONFIGS = ((128, 1), (512, 4), (2048, 16))
SPAN = 128
ALIBI_MAX = 8.0
RMS_EPS = 1e-6
Q_SCALE = HEAD_DIM ** -0.5
LOG2E = float(np.log2(np.e))

LANES = 128
SUBLANES = 8
VMEM_LIMIT = 56 * 1024 * 1024

_C_CB, _C_CC, _C_Q, _C_K, _C_V, _C_MQ, _C_Z, _C_END = 0, 256, 768, 1280, 1792, 2304, 2560, 3584

ROW_TILE = 512
ATTN_TILE = 512


def _slopes():
    h = np.arange(ATTN_HEADS, dtype=np.float64) + 1.0
    return np.exp2(-ALIBI_MAX * h / ATTN_HEADS)


def _params(n_axes):
    return pltpu.CompilerParams(
        dimension_semantics=("arbitrary",) * n_axes,
        vmem_limit_bytes=VMEM_LIMIT)


def _transpose_cast_kernel(w_ref, o_ref):
    o_ref[...] = w_ref[...].T.astype(BF16)


def _transposed_columns(w, lo, hi):
    K = w.shape[0]
    blk = 2 * LANES
    assert lo % blk == 0 and hi % blk == 0
    return pl.pallas_call(
        _transpose_cast_kernel,
        grid=((hi - lo) // blk,),
        in_specs=[pl.BlockSpec((K, blk), lambda i: (0, lo // blk + i))],
        out_specs=pl.BlockSpec((blk, K), lambda i: (i, 0)),
        out_shape=jax.ShapeDtypeStruct((hi - lo, K), BF16),
        compiler_params=_params(1),
        name="wkv_t",
    )(w)


def _fold_store(res, out_refs, slabs, dils):
    rows = res.shape[0]
    out_refs[0][0, 0] = res.astype(BF16)
    if len(dils) == 1:
        return
    n_slab = res.shape[1] // LANES
    src, prev_d = slabs[0], 1
    for s in range(n_slab):
        src[s] = res[:, s * LANES:(s + 1) * LANES]
    for idx, (d, o_ref) in enumerate(zip(dils[1:], out_refs[1:])):
        step, n = d // prev_d, rows // d
        dst = slabs[idx + 1] if idx + 2 < len(dils) else None
        for rp in range(prev_d):
            for q in range(step):
                r = q * prev_d + rp
                for s in range(n_slab):
                    val = src[s, pl.ds(rp * (rows // prev_d) + q, n, stride=step), :]
                    o_ref[0, r, :, s * LANES:(s + 1) * LANES] = val.astype(BF16)
                    if dst is not None:
                        dst[s, pl.ds(r * n, n), :] = val
        src, prev_d = dst, d


def _proj_kernel(x_ref, g_ref, w_ref, wkvt_ref, *refs, first, keep_transposed, dils, q_scale,
                 hosts_sample, fuse_conv, emit_z):
    n = len(dils)
    if fuse_conv:
        cw_ref, refs, ucat = refs[0], refs[1:-1], refs[-1]

        @pl.when(pl.program_id(1) == 0)
        def _():
            ucat[0:SUBLANES] = jnp.zeros((SUBLANES, CONV_W), F32)
    n_out = 5 + 3 * n + (1 if emit_z else 0)
    if hosts_sample:
        sample_ins, refs = refs[:N_SAMPLE_IN], refs[N_SAMPLE_IN:]
        sample_outs = refs[n_out:n_out + 2]
        refs = refs[:n_out] + refs[n_out + 2:]
        step = pl.program_id(0) * pl.num_programs(1) + pl.program_id(1)
        _sample_attn_kernel(*sample_ins, *sample_outs, batch_row=hosts_sample[1] + step)
    cb_ref, u_ref = refs[0:2]
    q_refs, k_refs, v_refs = refs[2:2 + n], refs[2 + n:2 + 2 * n], refs[2 + 2 * n:2 + 3 * n]
    k32_ref, v32_ref, mq_ref = refs[2 + 3 * n:5 + 3 * n]
    scratch = refs[n_out:]
    slabs = [scratch[a * (n - 1):(a + 1) * (n - 1)] for a in range(3)]
    x = x_ref[0]
    r = lax.rsqrt(jnp.mean(x * x, axis=-1, keepdims=True) + RMS_EPS)
    h = (x * r * g_ref[...]).astype(BF16)

    def seg(lo, hi):
        return jnp.dot(h, w_ref[:, lo:hi], preferred_element_type=F32)

    cb = seg(_C_CB, _C_CC)
    cch = seg(_C_CC, _C_Q)
    u = cch[:, :CONV_W] * cch[:, CONV_W:]
    if fuse_conv:
        rows = u.shape[0]
        ucat[SUBLANES:] = u
        cw = cw_ref[...]
        conv = (cw[0:1] * ucat[pl.ds(SUBLANES - 2, rows)]
                + cw[1:2] * ucat[pl.ds(SUBLANES - 1, rows)]
                + cw[2:3] * ucat[pl.ds(SUBLANES, rows)])
        cb_ref[0] = (cb * conv).astype(BF16)
        u_ref[0] = u[rows - SUBLANES:]
        ucat[0:SUBLANES] = u[rows - SUBLANES:]
    else:
        cb_ref[0] = cb.astype(BF16)
        u_ref[0] = u
    _fold_store(seg(_C_Q, _C_K) * q_scale, q_refs, slabs[0], dils)
    k = seg(_C_K, _C_V)
    _fold_store(k, k_refs, slabs[1], dils)
    v = seg(_C_V, _C_MQ)
    _fold_store(v, v_refs, slabs[2], dils)
    mq_ref[0] = (seg(_C_MQ, _C_Z) * q_scale).astype(BF16)
    if emit_z:
        refs[5 + 3 * n][0] = seg(_C_Z, _C_Z + emit_z).astype(BF16)
    if keep_transposed:
        @pl.when(pl.program_id(1) >= first)
        def _():
            nt = (((1,), (1,)), ((), ()))
            k32_ref[0] = lax.dot_general(wkvt_ref[0:ATTN_W], h, nt, preferred_element_type=F32)
            v32_ref[0] = lax.dot_general(wkvt_ref[ATTN_W:], h, nt, preferred_element_type=F32)
    else:
        k32_ref[0] = k
        v32_ref[0] = v


def _project(x, g, w_bf16, wkvt_bf16, n_keep, keep_transposed, dils, q_scale, sample=None,
             sample_base=0, conv_w=None, emit_z=D_MIX):
    B, T, D = x.shape
    tm = ROW_TILE
    nt = T // tm
    first = (T - n_keep) // tm
    n = len(dils)

    def row(width):
        return pl.BlockSpec((1, tm, width), lambda b, i: (b, i, 0))

    folded_specs = [pl.BlockSpec((1, d, tm // d, ATTN_W), lambda b, i: (b, 0, i, 0)) for d in dils]
    folded_shapes = [jax.ShapeDtypeStruct((B, d, T // d, ATTN_W), BF16) for d in dils]

    if keep_transposed:
        keep = pl.BlockSpec((1, ATTN_W, tm), lambda b, i: (b, 0, jnp.maximum(i - first, 0)))
        keep_shape = (B, ATTN_W, n_keep)
    else:
        keep = pl.BlockSpec((1, tm, ATTN_W), lambda b, i: (b, jnp.maximum(i - first, 0), 0))
        keep_shape = (B, n_keep, ATTN_W)
    fuse_conv = conv_w is not None
    u_rows = SUBLANES if fuse_conv else tm
    out_shape = (
        [jax.ShapeDtypeStruct((B, T, CONV_W), BF16),
         jax.ShapeDtypeStruct((B, nt * u_rows, CONV_W), F32)]
        + folded_shapes * 3
        + [jax.ShapeDtypeStruct(keep_shape, F32),
           jax.ShapeDtypeStruct(keep_shape, F32),
           jax.ShapeDtypeStruct((B, T, MEM_W), BF16)])
    out_specs = ([row(CONV_W), pl.BlockSpec((1, u_rows, CONV_W), lambda b, i: (b, i, 0))]
                 + folded_specs * 3 + [keep, keep, row(MEM_W)])
    if emit_z:
        out_shape.append(jax.ShapeDtypeStruct((B, T, emit_z), BF16))
        out_specs.append(row(emit_z))
    scratch = [pltpu.VMEM((ATTN_W // LANES, tm, LANES), F32)] * (3 * (n - 1))
    once = pl.Buffered(1)
    in_specs = [row(D),
                pl.BlockSpec((1, D), lambda b, i: (0, 0)),
                pl.BlockSpec(w_bf16.shape, lambda b, i: (0, 0), pipeline_mode=once),
                pl.BlockSpec(wkvt_bf16.shape, lambda b, i: (0, 0), pipeline_mode=once)]
    args = [x, g.reshape(1, D), w_bf16, wkvt_bf16]
    if fuse_conv:
        args.append(conv_w)
        in_specs.append(pl.BlockSpec(conv_w.shape, lambda b, i: (0, 0)))
        scratch = scratch + [pltpu.VMEM((tm + SUBLANES, CONV_W), F32)]
    if sample is not None:
        s_args, s_in, s_out, s_shape = _sample_row_specs(sample, sample_base, B * nt,
                                                         lambda b, i: b * nt + i)
        args, in_specs = args + s_args, in_specs + s_in
        out_specs, out_shape = out_specs + s_out, out_shape + s_shape
    res = pl.pallas_call(
        functools.partial(_proj_kernel, first=first, keep_transposed=keep_transposed, dils=dils,
                          q_scale=q_scale,
                          hosts_sample=(True, sample_base) if sample is not None else None,
                          fuse_conv=fuse_conv, emit_z=emit_z),
        grid=(B, nt),
        in_specs=in_specs,
        out_specs=tuple(out_specs),
        out_shape=tuple(out_shape),
        scratch_shapes=scratch,
        compiler_params=_params(2),
        name="proj",
    )(*args)
    cb, u = res[0:2]
    q, k, v = res[2:2 + n], res[2 + n:2 + 2 * n], res[2 + 2 * n:2 + 3 * n]
    k32, v32, mq = res[2 + 3 * n:5 + 3 * n]
    n_out = 5 + 3 * n + (1 if emit_z else 0)
    z = res[5 + 3 * n] if emit_z else None
    return (cb, u, q, k, v, k32, v32, mq, z) + tuple(res[n_out:])


def _memkv_kernel(x_ref, g_ref, w_ref, k32_ref, v32_ref, k_ref, v_ref):
    x = x_ref[...]
    r = lax.rsqrt(jnp.mean(x * x, axis=-1, keepdims=True) + RMS_EPS)
    h = (x * r * g_ref[...]).astype(BF16)
    kv = jnp.dot(h, w_ref[...], preferred_element_type=F32)
    k_ref[...] = kv[:, :MEM_W].astype(BF16)
    v_ref[...] = kv[:, MEM_W:].astype(BF16)
    n_mem = k32_ref.shape[2]
    for b in range(k32_ref.shape[0]):
        kvt = kv[b * n_mem:(b + 1) * n_mem].T
        k32_ref[b] = kvt[:MEM_W]
        v32_ref[b] = kvt[MEM_W:]


def _mem_kv(mem, g, w_bf16):
    B, M, D = mem.shape
    rows = B * M
    k32, v32, k, v = pl.pallas_call(
        _memkv_kernel,
        out_shape=(jax.ShapeDtypeStruct((B, MEM_W, M), F32),
                   jax.ShapeDtypeStruct((B, MEM_W, M), F32),
                   jax.ShapeDtypeStruct((rows, MEM_W), BF16),
                   jax.ShapeDtypeStruct((rows, MEM_W), BF16)),
        compiler_params=pltpu.CompilerParams(vmem_limit_bytes=VMEM_LIMIT),
        name="memkv",
    )(mem.reshape(rows, D), g.reshape(1, D), w_bf16)
    return k32, v32, k.reshape(B, M, MEM_W), v.reshape(B, M, MEM_W)


def _band_bias_table():
    qi = np.arange(SPAN)[:, None] + SPAN
    ki = np.arange(2 * SPAN)[None, :]
    dist = qi - ki
    valid = (dist >= 0) & (dist <= SPAN)
    slopes = _slopes()
    tabs = []
    for _, dil in DILATED_CONFIGS:
        for first in (False, True):
            ok = valid & (ki >= SPAN) if first else valid
            for h in range(ATTN_HEADS):
                bias = -(slopes[h] * dil) * dist * LOG2E
                tabs.append(np.where(ok, bias, -np.inf))
    return np.stack(tabs).astype(np.float32)


def _pair_attention(qb, kb, vb, bias_lo, bias_hi):
    lane = lax.broadcasted_iota(jnp.int32, qb.shape, 1)
    lo = lane < HEAD_DIM
    zero = jnp.zeros_like(qb)
    v_ones = jnp.concatenate([vb, jnp.ones_like(vb)], axis=1)
    accs, ms, ls = [], [], []
    for sel, bias in ((lo, bias_lo), (~lo, bias_hi)):
        qm = jnp.where(sel, qb, zero)
        s = lax.dot_general(qm, kb, (((1,), (1,)), ((), ())), preferred_element_type=F32)
        if bias is not None:
            s = s + bias
        m = jnp.max(s, axis=-1, keepdims=True)
        p = jnp.exp2(s - m)
        acc = jnp.dot(p.astype(BF16), v_ones, preferred_element_type=F32)
        ms.append(m)
        accs.append(acc[:, :LANES])
        ls.append(acc[:, LANES:])
    o = jnp.where(lo, accs[0], accs[1]) / jnp.where(lo, ls[0], ls[1])
    return o, ms, ls


def _attn_kernel(*refs, steps, sample_base):
    ins = refs[:15]
    tab_ref = refs[15]
    sample_ins = refs[16:16 + N_SAMPLE_IN]
    outs = refs[16 + N_SAMPLE_IN:22 + N_SAMPLE_IN]
    sample_outs = refs[22 + N_SAMPLE_IN:24 + N_SAMPLE_IN]
    kcat, vcat = refs[24 + N_SAMPLE_IN:]
    s = pl.program_id(1)
    nblk = ATTN_TILE // SPAN
    _sample_attn_kernel(*sample_ins, *sample_outs,
                        batch_row=sample_base + pl.program_id(0) * steps + s)
    for c in range(3):
        _, kp, ko, vp, vo = ins[5 * c:5 * c + 5]
        kcat[c, 0:SPAN] = kp[0, 0]
        kcat[c, SPAN:] = ko[0, 0, 0:SPAN]
        vcat[c, 0:SPAN] = vp[0, 0]
        vcat[c, SPAN:] = vo[0, 0, 0:SPAN]

    for j in range(nblk):
        rows = slice(j * SPAN, (j + 1) * SPAN)
        keys = slice((j - 1) * SPAN, (j + 1) * SPAN)
        for c, (_, dil) in enumerate(DILATED_CONFIGS):
            q_ref, _, ko, _, vo = ins[5 * c:5 * c + 5]
            o_ref, l_ref = outs[2 * c], outs[2 * c + 1]
            units_per_residue = steps // dil
            if j == 0:
                base = (c * 2 + (s % units_per_residue == 0).astype(jnp.int32)) * ATTN_HEADS
            else:
                base = c * 2 * ATTN_HEADS
            lane = lax.broadcasted_iota(jnp.int32, (SPAN, LANES), 1)
            m8 = jnp.zeros((SPAN, LANES), F32)
            l8 = jnp.ones((SPAN, LANES), F32)
            for p in range(ATTN_HEADS // 2):
                lanes = slice(p * LANES, (p + 1) * LANES)
                qb = q_ref[0, 0, rows, lanes]
                if j == 0:
                    kb, vb = kcat[c, :, lanes], vcat[c, :, lanes]
                else:
                    kb, vb = ko[0, 0, keys, lanes], vo[0, 0, keys, lanes]
                o, ms, ls = _pair_attention(qb, kb, vb, tab_ref[base + 2 * p],
                                            tab_ref[base + 2 * p + 1])
                o_ref[0, 0, rows, lanes] = o.astype(BF16)
                for half in range(2):
                    m8 = jnp.where(lane == 2 * p + half, ms[half], m8)
                    l8 = jnp.where(lane == 2 * p + half, ls[half], l8)
            l_ref[0, 0, rows, :] = m8 + jnp.log2(l8)


def _prompt_attention(q, k, v, sample, sample_base):
    B, _, T, W = q[0].shape
    steps = T // ATTN_TILE
    prev_per_tile = ATTN_TILE // SPAN
    args, in_specs, out_specs, out_shape = [], [], [], []
    for c, (_, dil) in enumerate(DILATED_CONFIGS):
        L = T // dil
        upr = L // ATTN_TILE

        def own(b, s, upr=upr):
            return (b, s // upr, s % upr, 0)

        def prev(b, s, upr=upr):
            return (b, s // upr, jnp.maximum((s % upr) * prev_per_tile - 1, 0), 0)

        shape = (B, dil, L, W)
        assert q[c].shape == shape
        args += [q[c], k[c], k[c], v[c], v[c]]
        in_specs += [pl.BlockSpec((1, 1, ATTN_TILE, W), own),
                     pl.BlockSpec((1, 1, SPAN, W), prev),
                     pl.BlockSpec((1, 1, ATTN_TILE, W), own),
                     pl.BlockSpec((1, 1, SPAN, W), prev),
                     pl.BlockSpec((1, 1, ATTN_TILE, W), own)]
        out_specs += [pl.BlockSpec((1, 1, ATTN_TILE, W), own),
                      pl.BlockSpec((1, 1, ATTN_TILE, LANES), own)]
        out_shape += [jax.ShapeDtypeStruct(shape, BF16),
                      jax.ShapeDtypeStruct((B, dil, L, LANES), F32)]
    tab = jnp.asarray(_band_bias_table())
    args.append(tab)
    in_specs.append(pl.BlockSpec(tab.shape, lambda b, s: (0, 0, 0)))
    s_args, s_in, s_out, s_shape = _sample_row_specs(sample, sample_base, B * steps,
                                                     lambda b, s: b * steps + s)
    res = pl.pallas_call(
        functools.partial(_attn_kernel, steps=steps, sample_base=sample_base),
        grid=(B, steps),
        in_specs=in_specs + s_in,
        out_specs=tuple(out_specs + s_out),
        out_shape=tuple(out_shape + s_shape),
        scratch_shapes=[pltpu.VMEM((3, 2 * SPAN, W), BF16)] * 2,
        compiler_params=_params(2),
        name="prompt_attn",
    )(*args, *s_args)
    return list(res[:6]), res[6], res[7]


def _silu(z):
    return z / (1.0 + jnp.exp(-z))


def _gate_project_norm(x, mix_ref, wout_ref, gf_ref):
    y = x + jnp.dot(mix_ref[...], wout_ref[...], preferred_element_type=F32)
    r = lax.rsqrt(jnp.mean(y * y, axis=-1, keepdims=True) + RMS_EPS)
    return y * r * gf_ref[...]


def _unfold(src_ref, slabs, d, tmp=None, step=4):
    n_slab, rows = slabs.shape[0], slabs.shape[1]

    def piece(r, s):
        return src_ref[0, r, :, s * LANES:(s + 1) * LANES].astype(F32)

    if d <= step:
        for r in range(d):
            for s in range(n_slab):
                slabs[s, pl.ds(r, rows // d, stride=d), :] = piece(r, s)
        return
    prev_d, n = d // step, rows // d
    per = rows // prev_d
    for rp in range(prev_d):
        for q in range(step):
            for s in range(n_slab):
                tmp[s, pl.ds(rp * per + q, n, stride=step), :] = piece(q * prev_d + rp, s)
    for rp in range(prev_d):
        for s in range(n_slab):
            slabs[s, pl.ds(rp, per, stride=prev_d), :] = tmp[s, pl.ds(rp * per, per), :]


def _head_expander():
    e = np.zeros((2 * LANES, ATTN_W), np.float32)
    for h in range(ATTN_HEADS):
        e[h, h * HEAD_DIM:(h + 1) * HEAD_DIM] = 1.0
        e[LANES + h, h * HEAD_DIM:(h + 1) * HEAD_DIM] = 1.0
    return e


N_FINISH_IN = 17


def _finish_kernel(*refs, sample_base):
    (x_ref, cy_ref, o1_ref, l1_ref, o4_ref, l4_ref, o16_ref, l16_ref,
     mq_ref, mk_ref, mv_ref, zlo_ref, gin_ref, wz_ref, wout_ref, gf_ref,
     ex_ref) = refs[:N_FINISH_IN]
    sample_ins = refs[N_FINISH_IN:N_FINISH_IN + N_SAMPLE_IN]
    y_ref = refs[N_FINISH_IN + N_SAMPLE_IN]
    sample_outs = refs[N_FINISH_IN + N_SAMPLE_IN + 1:N_FINISH_IN + N_SAMPLE_IN + 3]
    mix, o4s, l4s, o16s, l16s, otmp, ltmp = refs[N_FINISH_IN + N_SAMPLE_IN + 3:]
    i = pl.program_id(1)
    _sample_attn_kernel(*sample_ins, *sample_outs,
                        batch_row=sample_base + pl.program_id(0) * pl.num_programs(1) + i)
    x = x_ref[0]
    r = lax.rsqrt(jnp.mean(x * x, axis=-1, keepdims=True) + RMS_EPS)
    z_hi = jnp.dot((x * r * gin_ref[...]).astype(BF16), wz_ref[...], preferred_element_type=F32)
    z = jnp.concatenate([zlo_ref[0].astype(F32), z_hi], axis=1)
    mix[:, 0:CONV_W] = (cy_ref[0].astype(F32) * _silu(z[:, 0:CONV_W])).astype(BF16)
    dil4, dil16 = DILATED_CONFIGS[1][1], DILATED_CONFIGS[2][1]
    _unfold(o4_ref, o4s, dil4)
    _unfold(l4_ref, l4s, dil4)
    _unfold(o16_ref, o16s, dil16, otmp)
    _unfold(l16_ref, l16s, dil16, ltmp)
    lo, hi = CONV_W, CONV_W + ATTN_W
    l1, l4, l16 = l1_ref[0, 0], l4s[0], l16s[0]
    mx = jnp.maximum(jnp.maximum(l1, l4), l16)
    e1, e4, e16 = jnp.exp2(l1 - mx), jnp.exp2(l4 - mx), jnp.exp2(l16 - mx)
    den = e1 + e4 + e16

    def spread(w):
        w_hi = w.astype(BF16)
        w_lo = (w - w_hi.astype(F32)).astype(BF16)
        return jnp.dot(jnp.concatenate([w_hi, w_lo], axis=1), ex_ref[...],
                       preferred_element_type=F32)

    a1, a4, a16 = spread(e1 / den), spread(e4 / den), spread(e16 / den)
    for s in range(ATTN_W // LANES):
        lanes = slice(s * LANES, (s + 1) * LANES)
        attn = (a1[:, lanes] * o1_ref[0, 0, :, lanes].astype(F32) + a4[:, lanes] * o4s[s]
                + a16[:, lanes] * o16s[s])
        a = lo + s * LANES
        mix[:, a:a + LANES] = (attn * _silu(z[:, a:a + LANES])).astype(BF16)
    for p in range(MEM_HEADS // 2):
        lanes = slice(p * LANES, (p + 1) * LANES)
        o, _, _ = _pair_attention(mq_ref[0, :, lanes], mk_ref[0, :, lanes], mv_ref[0, :, lanes],
                                  None, None)
        a = hi + p * LANES
        mix[:, a:a + LANES] = (o * _silu(z[:, a:a + LANES])).astype(BF16)
    y_ref[0] = _gate_project_norm(x_ref[0], mix, wout_ref, gf_ref)


def _prompt_finish(x, conv_y, attn, mq, mk, mv, z_lo, g_in, w_gate_bf16, w_out_bf16, g_final,
                   sample, sample_base):
    B, T, D = x.shape
    tm = ROW_TILE

    def row(width):
        return pl.BlockSpec((1, tm, width), lambda b, i: (b, i, 0))

    def const(shape):
        return pl.BlockSpec(shape, lambda b, i: (0,) * len(shape))

    memspec = pl.BlockSpec((1,) + mk.shape[1:], lambda b, i: (b, 0, 0))
    in_specs = [row(D), row(CONV_W)]
    for _, dil in DILATED_CONFIGS:
        in_specs += [pl.BlockSpec((1, dil, tm // dil, ATTN_W), lambda b, i: (b, 0, i, 0)),
                     pl.BlockSpec((1, dil, tm // dil, LANES), lambda b, i: (b, 0, i, 0))]
    expander = jnp.asarray(_head_expander(), BF16)
    assert z_lo.shape[-1] + w_gate_bf16.shape[-1] == D_MIX
    in_specs += [row(MEM_W), memspec, memspec, row(z_lo.shape[-1]), const((1, D)),
                 const(w_gate_bf16.shape), const(w_out_bf16.shape), const((1, D)),
                 const(expander.shape)]
    assert len(in_specs) == N_FINISH_IN
    o_slabs = pltpu.VMEM((ATTN_W // LANES, tm, LANES), F32)
    l_slabs = pltpu.VMEM((1, tm, LANES), F32)
    nt = T // tm
    s_args, s_in, s_out, s_shape = _sample_row_specs(sample, sample_base, B * nt,
                                                     lambda b, i: b * nt + i)
    return pl.pallas_call(
        functools.partial(_finish_kernel, sample_base=sample_base),
        grid=(B, nt),
        in_specs=in_specs + s_in,
        out_specs=tuple([row(D)] + s_out),
        out_shape=tuple([jax.ShapeDtypeStruct((B, T, D), F32)] + s_shape),
        scratch_shapes=[pltpu.VMEM((tm, D_MIX), BF16),
                        o_slabs, l_slabs, o_slabs, l_slabs, o_slabs, l_slabs],
        compiler_params=_params(2),
        name="prompt_finish",
    )(x, conv_y, *attn, mq, mk, mv, z_lo, g_in.reshape(1, D), w_gate_bf16, w_out_bf16,
      g_final.reshape(1, D), expander, *s_args)


def _sample_tables(S, n_buf):
    j = np.arange(S)[:, None]
    t = np.arange(n_buf)[None, :]
    dist = n_buf + j - t
    count = np.zeros(dist.shape)
    for win, dil in DILATED_CONFIGS:
        count += (dist % dil == 0) & (dist <= win)
    bias = np.where(count > 0, -_slopes()[:, None, None] * dist, -np.inf)
    bias = bias.reshape(ATTN_HEADS // 2, 2 * S, n_buf)
    count = np.concatenate([count, count], axis=0)
    return count.astype(np.float32), bias.astype(np.float32)


def _stack_pair(a, p):
    ap = a[:, p * LANES:(p + 1) * LANES]
    lo = lax.broadcasted_iota(jnp.int32, ap.shape, 1) < HEAD_DIM
    zero = jnp.zeros_like(ap)
    return jnp.concatenate([jnp.where(lo, ap, zero), jnp.where(lo, zero, ap)], axis=0)


def _unstack_pair(o, S):
    lo = lax.broadcasted_iota(jnp.int32, (S, LANES), 1) < HEAD_DIM
    return jnp.where(lo, o[:S], o[S:])


def _sample_attn_kernel(q_ref, kn_ref, vn_ref, kt_ref, vt_ref, mq_ref, mkt_ref, mvt_ref,
                        cnt_ref, bias_ref, o_ref, mo_ref, *, batch_row):
    R, S = kt_ref.shape[0], o_ref.shape[1]
    nt = (((1,), (1,)), ((), ()))
    slopes = _slopes()
    n_cfg = float(len(DILATED_CONFIGS))
    row = lax.broadcasted_iota(jnp.int32, (2 * S, 1), 0)
    upper = row >= S
    rowj = jnp.where(upper, row - S, row)
    cnt = cnt_ref[...]
    rows = range(R)
    which = batch_row % (SUBLANES // S)

    def tokens(ref):
        blk = ref[...]
        tok = blk[0:S]
        for i in range(1, SUBLANES // S):
            tok = jnp.where(which == i, blk[i * S:(i + 1) * S], tok)
        return tok

    q_tok, kn_tok, vn_tok, mq_tok = (tokens(ref) for ref in (q_ref, kn_ref, vn_ref, mq_ref))

    def lane_group(tok, p):
        return tok[:, p * LANES:(p + 1) * LANES]

    def pair_rows(ref, r, p):
        return ref[r, 2 * p:2 * p + 2].reshape(2 * HEAD_DIM, ref.shape[-1]).astype(BF16)

    units = [(r, p) for r in rows for p in range(ATTN_HEADS // 2)]
    munits = [(r, p) for r in rows for p in range(MEM_HEADS // 2)]
    qs = [_stack_pair(q_tok, p) for r, p in units]
    scores = [jnp.dot(q8.astype(BF16), pair_rows(kt_ref, r, p),
                      preferred_element_type=F32) + bias_ref[p] for (r, p), q8 in zip(units, qs)]
    mscores = [jnp.dot(_stack_pair(mq_tok, p).astype(BF16), pair_rows(mkt_ref, r, p),
                       preferred_element_type=F32) for r, p in munits]
    probs, stats = [], []
    for (r, p), q8, s in zip(units, qs, scores):
        knp = lane_group(kn_tok, p)
        slope = jnp.where(upper, float(slopes[2 * p + 1]), float(slopes[2 * p]))
        m = jnp.max(s, axis=-1, keepdims=True)
        s_new = []
        for jp in range(S):
            d = rowj - jp
            sj = jnp.sum(q8 * knp[jp:jp + 1], axis=-1, keepdims=True)
            sj = jnp.where(d >= 0, sj - slope * d.astype(F32), -jnp.inf)
            s_new.append(sj)
            m = jnp.maximum(m, sj)
        pr = cnt * jnp.exp(s - m)
        l = jnp.sum(pr, axis=-1, keepdims=True)
        w_new = []
        for jp in range(S):
            w = jnp.where(rowj == jp, n_cfg, 1.0) * jnp.exp(s_new[jp] - m)
            l = l + w
            w_new.append(w)
        probs.append(pr.astype(BF16))
        stats.append((l, w_new))
    mprobs, mden = [], []
    for s in mscores:
        pr = jnp.exp(s - jnp.max(s, axis=-1, keepdims=True))
        mden.append(jnp.sum(pr, axis=-1, keepdims=True))
        mprobs.append(pr.astype(BF16))
    accs = [lax.dot_general(pr, pair_rows(vt_ref, r, p), nt, preferred_element_type=F32)
            for (r, p), pr in zip(units, probs)]
    maccs = [lax.dot_general(pr, pair_rows(mvt_ref, r, p), nt, preferred_element_type=F32)
             for (r, p), pr in zip(munits, mprobs)]
    outs = []
    for (r, p), acc, (l, w_new) in zip(units, accs, stats):
        vnp = lane_group(vn_tok, p)
        for jp in range(S):
            acc = acc + w_new[jp] * vnp[jp:jp + 1]
        outs.append(_unstack_pair(acc / l, S))
    mouts = [_unstack_pair(acc / den, S) for acc, den in zip(maccs, mden)]
    n_p, n_mp = ATTN_HEADS // 2, MEM_HEADS // 2
    for r in rows:
        o_ref[r] = jnp.concatenate(outs[r * n_p:(r + 1) * n_p], axis=-1)
        mo_ref[r] = jnp.concatenate(mouts[r * n_mp:(r + 1) * n_mp], axis=-1)


N_SAMPLE_IN = 10


def _sample_row_specs(sample, base, n_rows, step_of):
    q, k_new, v_new, cache_kt, cache_vt, mq, mem_kt, mem_vt = sample
    DB, n_buf = cache_kt.shape[0], cache_kt.shape[-1]
    S, W = q.shape[0] // DB, q.shape[1]
    assert SUBLANES % S == 0, "token rows are fetched in 8-row blocks"
    assert n_buf >= DILATED_CONFIGS[-1][0] and S <= DILATED_CONFIGS[1][1], \
        "new tokens reach each other through dilation 1 only; the buffer covers every window"
    cnt, bias = (jnp.asarray(t) for t in _sample_tables(S, n_buf))
    per_block = SUBLANES // S

    def tokens_in(a):
        return pl.BlockSpec((SUBLANES, a.shape[1]),
                            lambda *g: ((base + step_of(*g)) // per_block, 0))

    def row_in(a):
        return pl.BlockSpec((1,) + a.shape[1:],
                            lambda *g: (base + step_of(*g),) + (0,) * (a.ndim - 1))

    def row_out(width):
        return pl.BlockSpec((1, S, width), lambda *g: (step_of(*g), 0, 0))

    def const(a):
        return pl.BlockSpec(a.shape, lambda *g: (0,) * a.ndim)

    args = list(sample) + [cnt, bias]
    assert len(args) == N_SAMPLE_IN
    in_specs = [tokens_in(q), tokens_in(k_new), tokens_in(v_new), row_in(cache_kt),
                row_in(cache_vt), tokens_in(mq), row_in(mem_kt), row_in(mem_vt),
                const(cnt), const(bias)]
    out_specs = [row_out(W), row_out(MEM_W)]
    out_shape = [jax.ShapeDtypeStruct((n_rows, S, W), F32),
                 jax.ShapeDtypeStruct((n_rows, S, MEM_W), F32)]
    return args, in_specs, out_specs, out_shape


def _sample_attention(sample, base, n_rows):
    args, in_specs, out_specs, out_shape = _sample_row_specs(sample, base, n_rows, lambda i: i)

    def body(*refs):
        _sample_attn_kernel(*refs, batch_row=base + pl.program_id(0))

    return pl.pallas_call(
        body,
        grid=(n_rows,),
        in_specs=in_specs,
        out_specs=tuple(out_specs),
        out_shape=tuple(out_shape),
        compiler_params=_params(1),
        name="sample_attn",
    )(*args)


def _sample_finish_kernel(x_ref, cb_ref, u0_ref, u1_ref, u2_ref, attn_ref, mem_ref, z_ref,
                          cw_ref, wout_ref, gf_ref, y_ref, mix):
    cw = cw_ref[...]
    conv = cw[0:1] * u0_ref[...] + cw[1:2] * u1_ref[...] + cw[2:3] * u2_ref[...]
    z = z_ref[...].astype(F32)
    lo, hi = CONV_W, CONV_W + ATTN_W
    mix[:, 0:lo] = (cb_ref[...].astype(F32) * conv * _silu(z[:, 0:lo])).astype(BF16)
    mix[:, lo:hi] = (attn_ref[...] * _silu(z[:, lo:hi])).astype(BF16)
    mix[:, hi:] = (mem_ref[...] * _silu(z[:, hi:])).astype(BF16)
    y_ref[...] = _gate_project_norm(x_ref[...], mix, wout_ref, gf_ref)


def _sample_finish(x, cb, u_taps, attn, mem, z, conv_w, w_out_bf16, g_final):
    rows, D = x.shape
    return pl.pallas_call(
        _sample_finish_kernel,
        out_shape=jax.ShapeDtypeStruct((rows, D), F32),
        scratch_shapes=[pltpu.VMEM((rows, D_MIX), BF16)],
        compiler_params=pltpu.CompilerParams(vmem_limit_bytes=VMEM_LIMIT),
        name="sample_finish",
    )(x, cb, *u_taps, attn, mem, z, conv_w, w_out_bf16, g_final.reshape(1, D))


def kernel(x_prompt, x_sample, mem_prompt, cache_win_k, cache_win_v, cache_conv, cache_mem_k,
           cache_mem_v, g_in, w_in, conv_w, g_mem, w_mem_kv, w_out, g_final):
    assert g_in.shape[0] == 1, "the final RMSNorm is fused into the single layer's tail"
    B, T, D = x_prompt.shape
    DB, S, _ = x_sample.shape
    n_keep = min(DILATED_CONFIGS[-1][0], T)
    w_in_b = w_in[0].astype(BF16)
    w_out_b = w_out[0].astype(BF16)
    dils = tuple(d for _, d in DILATED_CONFIGS)
    wkvt_b = _transposed_columns(w_in[0], _C_K, _C_MQ)

    def heads_last(t, heads):
        return jnp.transpose(t.reshape(t.shape[0], heads, HEAD_DIM, -1), (0, 3, 1, 2))[None]

    def heads_first(c):
        return jnp.transpose(c, (0, 2, 3, 1))

    rows = DB * S
    cb_s, u_s, q_s, _, _, k32, v32, mq_s, z_s = _project(
        x_sample.reshape(1, rows, D), g_in[0], w_in_b, wkvt_b, rows, False, (1,), Q_SCALE)
    k32 = k32.reshape(rows, ATTN_W)
    v32 = v32.reshape(rows, ATTN_W)
    sample = (q_s[0].reshape(rows, ATTN_W).astype(F32), k32, v32,
              heads_first(cache_win_k[0]), heads_first(cache_win_v[0]),
              mq_s.reshape(rows, MEM_W).astype(F32),
              heads_first(cache_mem_k[0]), heads_first(cache_mem_v[0]))
    n_proj = n_tail = B * (T // ROW_TILE)
    n_attn = B * (T // ATTN_TILE)
    n_own = DB - n_proj - n_attn - n_tail
    assert n_own > 0
    z_split = D_MIX // 2
    conv_y, u_tails, q, k, v, kt32, vt32, mq, z_lo, attn_p, mem_p = _project(
        x_prompt, g_in[0], w_in_b, wkvt_b, n_keep, True, dils, Q_SCALE * LOG2E, sample, 0,
        conv_w[0], emit_z=z_split)
    mkt32, mvt32, mk, mv = _mem_kv(mem_prompt, g_mem[0], w_mem_kv[0].astype(BF16))
    attn, attn_a, mem_a = _prompt_attention(q, k, v, sample, n_proj)
    y_prompt, attn_t, mem_t = _prompt_finish(x_prompt, conv_y, attn, mq, mk, mv, z_lo, g_in[0],
                                             w_in_b[:, _C_Z + z_split:_C_END], w_out_b,
                                             g_final, sample, n_proj + n_attn)
    attn_o, mem_o = _sample_attention(sample, n_proj + n_attn + n_tail, n_own)
    attn_s = jnp.concatenate([attn_p, attn_a, attn_t, attn_o], axis=0)
    mem_s = jnp.concatenate([mem_p, mem_a, mem_t, mem_o], axis=0)
    p_win_k = heads_last(kt32, ATTN_HEADS)
    p_win_v = heads_last(vt32, ATTN_HEADS)
    p_conv = u_tails[None, :, -(CONV_WIDTH - 1):]
    p_mem_k = heads_last(mkt32, MEM_HEADS)
    p_mem_v = heads_last(mvt32, MEM_HEADS)
    full = jnp.concatenate([cache_conv[0], u_s.reshape(DB, S, CONV_W)], axis=1)
    taps = [full[:, t:t + S].reshape(rows, CONV_W) for t in range(CONV_WIDTH)]
    y_sample = _sample_finish(x_sample.reshape(rows, D), cb_s.reshape(rows, CONV_W), taps,
                              attn_s.reshape(rows, ATTN_W), mem_s.reshape(rows, MEM_W),
                              z_s.reshape(rows, D_MIX), conv_w[0], w_out_b,
                              g_final).reshape(DB, S, D)
    s_win_k = k32.reshape(1, DB, S, ATTN_HEADS, HEAD_DIM)
    s_win_v = v32.reshape(1, DB, S, ATTN_HEADS, HEAD_DIM)
    s_conv = full[None, :, -(CONV_WIDTH - 1):]
    return (y_prompt, y_sample, p_win_k, p_win_v, p_conv, p_mem_k, p_mem_v,
            s_win_k, s_win_v, s_conv)
```

```python
import functools

import numpy as np
import jax
import jax.numpy as jnp
from jax import lax
from jax.experimental import pallas as pl
from jax.experimental.pallas import tpu as pltpu

F32 = jnp.float32
BF16 = jnp.bfloat16

HEAD_DIM = 64
ATTN_HEADS = 8
MEM_HEADS = 4
CONV_W = 256
ATTN_W = 512
MEM_W = 256
D_MIX = 1024
CONV_WIDTH = 3
DILATED_CONFIGS = ((128, 1), (512, 4), (2048, 16))
SPAN = 128
ALIBI_MAX = 8.0
RMS_EPS = 1e-6
Q_SCALE = HEAD_DIM ** -0.5
LOG2E = float(np.log2(np.e))

LANES = 128
SUBLANES = 8
VMEM_LIMIT = 56 * 1024 * 1024

_C_CB, _C_CC, _C_Q, _C_K, _C_V, _C_MQ, _C_Z, _C_END = 0, 256, 768, 1280, 1792, 2304, 2560, 3584

ROW_TILE = 512
ATTN_TILE = 512


def _slopes():
    h = np.arange(ATTN_HEADS, dtype=np.float64) + 1.0
    return np.exp2(-ALIBI_MAX * h / ATTN_HEADS)


def _params(n_axes):
    return pltpu.CompilerParams(
        dimension_semantics=("arbitrary",) * n_axes,
        vmem_limit_bytes=VMEM_LIMIT)


def _transpose_cast_kernel(w_ref, o_ref):
    o_ref[...] = w_ref[...].T.astype(BF16)


def _transposed_columns(w, lo, hi):
    K = w.shape[0]
    blk = 2 * LANES
    assert lo % blk == 0 and hi % blk == 0
    return pl.pallas_call(
        _transpose_cast_kernel,
        grid=((hi - lo) // blk,),
        in_specs=[pl.BlockSpec((K, blk), lambda i: (0, lo // blk + i))],
        out_specs=pl.BlockSpec((blk, K), lambda i: (i, 0)),
        out_shape=jax.ShapeDtypeStruct((hi - lo, K), BF16),
        compiler_params=_params(1),
        name="wkv_t",
    )(w)


def _fold_store(res, out_refs, slabs, dils):
    rows = res.shape[0]
    out_refs[0][0, 0] = res.astype(BF16)
    if len(dils) == 1:
        return
    n_slab = res.shape[1] // LANES
    src, prev_d = slabs[0], 1
    for s in range(n_slab):
        src[s] = res[:, s * LANES:(s + 1) * LANES]
    for idx, (d, o_ref) in enumerate(zip(dils[1:], out_refs[1:])):
        step, n = d // prev_d, rows // d
        dst = slabs[idx + 1] if idx + 2 < len(dils) else None
        for rp in range(prev_d):
            for q in range(step):
                r = q * prev_d + rp
                for s in range(n_slab):
                    val = src[s, pl.ds(rp * (rows // prev_d) + q, n, stride=step), :]
                    o_ref[0, r, :, s * LANES:(s + 1) * LANES] = val.astype(BF16)
                    if dst is not None:
                        dst[s, pl.ds(r * n, n), :] = val
        src, prev_d = dst, d


def _proj_kernel(x_ref, g_ref, w_ref, wkvt_ref, *refs, first, keep_transposed, dils, q_scale,
                 hosts_sample, fuse_conv, emit_z):
    n = len(dils)
    if fuse_conv:
        cw_ref, refs, ucat = refs[0], refs[1:-1], refs[-1]

        @pl.when(pl.program_id(1) == 0)
        def _():
            ucat[0:SUBLANES] = jnp.zeros((SUBLANES, CONV_W), F32)
    n_out = 5 + 3 * n + (1 if emit_z else 0)
    if hosts_sample:
        sample_ins, refs = refs[:N_SAMPLE_IN], refs[N_SAMPLE_IN:]
        sample_outs = refs[n_out:n_out + 2]
        refs = refs[:n_out] + refs[n_out + 2:]
        step = pl.program_id(0) * pl.num_programs(1) + pl.program_id(1)
        _sample_attn_kernel(*sample_ins, *sample_outs, batch_row=hosts_sample[1] + step)
    cb_ref, u_ref = refs[0:2]
    q_refs, k_refs, v_refs = refs[2:2 + n], refs[2 + n:2 + 2 * n], refs[2 + 2 * n:2 + 3 * n]
    k32_ref, v32_ref, mq_ref = refs[2 + 3 * n:5 + 3 * n]
    scratch = refs[n_out:]
    slabs = [scratch[a * (n - 1):(a + 1) * (n - 1)] for a in range(3)]
    x = x_ref[0]
    r = lax.rsqrt(jnp.mean(x * x, axis=-1, keepdims=True) + RMS_EPS)
    h = (x * r * g_ref[...]).astype(BF16)

    def seg(lo, hi):
        return jnp.dot(h, w_ref[:, lo:hi], preferred_element_type=F32)

    cb = seg(_C_CB, _C_CC)
    cch = seg(_C_CC, _C_Q)
    u = cch[:, :CONV_W] * cch[:, CONV_W:]
    if fuse_conv:
        rows = u.shape[0]
        ucat[SUBLANES:] = u
        cw = cw_ref[...]
        conv = (cw[0:1] * ucat[pl.ds(SUBLANES - 2, rows)]
                + cw[1:2] * ucat[pl.ds(SUBLANES - 1, rows)]
                + cw[2:3] * ucat[pl.ds(SUBLANES, rows)])
        cb_ref[0] = (cb * conv).astype(BF16)
        u_ref[0] = u[rows - SUBLANES:]
        ucat[0:SUBLANES] = u[rows - SUBLANES:]
    else:
        cb_ref[0] = cb.astype(BF16)
        u_ref[0] = u
    _fold_store(seg(_C_Q, _C_K) * q_scale, q_refs, slabs[0], dils)
    k = seg(_C_K, _C_V)
    _fold_store(k, k_refs, slabs[1], dils)
    v = seg(_C_V, _C_MQ)
    _fold_store(v, v_refs, slabs[2], dils)
    mq_ref[0] = (seg(_C_MQ, _C_Z) * q_scale).astype(BF16)
    if emit_z:
        refs[5 + 3 * n][0] = seg(_C_Z, _C_END).astype(BF16)
    if keep_transposed:
        @pl.when(pl.program_id(1) >= first)
        def _():
            nt = (((1,), (1,)), ((), ()))
            k32_ref[0] = lax.dot_general(wkvt_ref[0:ATTN_W], h, nt, preferred_element_type=F32)
            v32_ref[0] = lax.dot_general(wkvt_ref[ATTN_W:], h, nt, preferred_element_type=F32)
    else:
        k32_ref[0] = k
        v32_ref[0] = v


def _project(x, g, w_bf16, wkvt_bf16, n_keep, keep_transposed, dils, q_scale, sample=None,
             sample_base=0, conv_w=None, emit_z=True):
    B, T, D = x.shape
    tm = ROW_TILE
    nt = T // tm
    first = (T - n_keep) // tm
    n = len(dils)

    def row(width):
        return pl.BlockSpec((1, tm, width), lambda b, i: (b, i, 0))

    folded_specs = [pl.BlockSpec((1, d, tm // d, ATTN_W), lambda b, i: (b, 0, i, 0)) for d in dils]
    folded_shapes = [jax.ShapeDtypeStruct((B, d, T // d, ATTN_W), BF16) for d in dils]

    if keep_transposed:
        keep = pl.BlockSpec((1, ATTN_W, tm), lambda b, i: (b, 0, jnp.maximum(i - first, 0)))
        keep_shape = (B, ATTN_W, n_keep)
    else:
        keep = pl.BlockSpec((1, tm, ATTN_W), lambda b, i: (b, jnp.maximum(i - first, 0), 0))
        keep_shape = (B, n_keep, ATTN_W)
    fuse_conv = conv_w is not None
    u_rows = SUBLANES if fuse_conv else tm
    out_shape = (
        [jax.ShapeDtypeStruct((B, T, CONV_W), BF16),
         jax.ShapeDtypeStruct((B, nt * u_rows, CONV_W), F32)]
        + folded_shapes * 3
        + [jax.ShapeDtypeStruct(keep_shape, F32),
           jax.ShapeDtypeStruct(keep_shape, F32),
           jax.ShapeDtypeStruct((B, T, MEM_W), BF16)])
    out_specs = ([row(CONV_W), pl.BlockSpec((1, u_rows, CONV_W), lambda b, i: (b, i, 0))]
                 + folded_specs * 3 + [keep, keep, row(MEM_W)])
    if emit_z:
        out_shape.append(jax.ShapeDtypeStruct((B, T, D_MIX), BF16))
        out_specs.append(row(D_MIX))
    scratch = [pltpu.VMEM((ATTN_W // LANES, tm, LANES), F32)] * (3 * (n - 1))
    once = pl.Buffered(1)
    in_specs = [row(D),
                pl.BlockSpec((1, D), lambda b, i: (0, 0)),
                pl.BlockSpec(w_bf16.shape, lambda b, i: (0, 0), pipeline_mode=once),
                pl.BlockSpec(wkvt_bf16.shape, lambda b, i: (0, 0), pipeline_mode=once)]
    args = [x, g.reshape(1, D), w_bf16, wkvt_bf16]
    if fuse_conv:
        args.append(conv_w)
        in_specs.append(pl.BlockSpec(conv_w.shape, lambda b, i: (0, 0)))
        scratch = scratch + [pltpu.VMEM((tm + SUBLANES, CONV_W), F32)]
    if sample is not None:
        s_args, s_in, s_out, s_shape = _sample_row_specs(sample, sample_base, B * nt,
                                                         lambda b, i: b * nt + i)
        args, in_specs = args + s_args, in_specs + s_in
        out_specs, out_shape = out_specs + s_out, out_shape + s_shape
    res = pl.pallas_call(
        functools.partial(_proj_kernel, first=first, keep_transposed=keep_transposed, dils=dils,
                          q_scale=q_scale,
                          hosts_sample=(True, sample_base) if sample is not None else None,
                          fuse_conv=fuse_conv, emit_z=emit_z),
        grid=(B, nt),
        in_specs=in_specs,
        out_specs=tuple(out_specs),
        out_shape=tuple(out_shape),
        scratch_shapes=scratch,
        compiler_params=_params(2),
        name="proj",
    )(*args)
    cb, u = res[0:2]
    q, k, v = res[2:2 + n], res[2 + n:2 + 2 * n], res[2 + 2 * n:2 + 3 * n]
    k32, v32, mq = res[2 + 3 * n:5 + 3 * n]
    n_out = 5 + 3 * n + (1 if emit_z else 0)
    z = res[5 + 3 * n] if emit_z else None
    return (cb, u, q, k, v, k32, v32, mq, z) + tuple(res[n_out:])


def _memkv_kernel(x_ref, g_ref, w_ref, k32_ref, v32_ref, k_ref, v_ref):
    x = x_ref[...]
    r = lax.rsqrt(jnp.mean(x * x, axis=-1, keepdims=True) + RMS_EPS)
    h = (x * r * g_ref[...]).astype(BF16)
    kv = jnp.dot(h, w_ref[...], preferred_element_type=F32)
    k_ref[...] = kv[:, :MEM_W].astype(BF16)
    v_ref[...] = kv[:, MEM_W:].astype(BF16)
    n_mem = k32_ref.shape[2]
    for b in range(k32_ref.shape[0]):
        kvt = kv[b * n_mem:(b + 1) * n_mem].T
        k32_ref[b] = kvt[:MEM_W]
        v32_ref[b] = kvt[MEM_W:]


def _mem_kv(mem, g, w_bf16):
    B, M, D = mem.shape
    rows = B * M
    k32, v32, k, v = pl.pallas_call(
        _memkv_kernel,
        out_shape=(jax.ShapeDtypeStruct((B, MEM_W, M), F32),
                   jax.ShapeDtypeStruct((B, MEM_W, M), F32),
                   jax.ShapeDtypeStruct((rows, MEM_W), BF16),
                   jax.ShapeDtypeStruct((rows, MEM_W), BF16)),
        compiler_params=pltpu.CompilerParams(vmem_limit_bytes=VMEM_LIMIT),
        name="memkv",
    )(mem.reshape(rows, D), g.reshape(1, D), w_bf16)
    return k32, v32, k.reshape(B, M, MEM_W), v.reshape(B, M, MEM_W)


def _band_bias_table():
    qi = np.arange(SPAN)[:, None] + SPAN
    ki = np.arange(2 * SPAN)[None, :]
    dist = qi - ki
    valid = (dist >= 0) & (dist <= SPAN)
    slopes = _slopes()
    tabs = []
    for _, dil in DILATED_CONFIGS:
        for first in (False, True):
            ok = valid & (ki >= SPAN) if first else valid
            for h in range(ATTN_HEADS):
                bias = -(slopes[h] * dil) * dist * LOG2E
                tabs.append(np.where(ok, bias, -np.inf))
    return np.stack(tabs).astype(np.float32)


def _pair_attention(qb, kb, vb, bias_lo, bias_hi):
    lane = lax.broadcasted_iota(jnp.int32, qb.shape, 1)
    lo = lane < HEAD_DIM
    zero = jnp.zeros_like(qb)
    v_ones = jnp.concatenate([vb, jnp.ones_like(vb)], axis=1)
    accs, ms, ls = [], [], []
    for sel, bias in ((lo, bias_lo), (~lo, bias_hi)):
        qm = jnp.where(sel, qb, zero)
        s = lax.dot_general(qm, kb, (((1,), (1,)), ((), ())), preferred_element_type=F32)
        if bias is not None:
            s = s + bias
        m = jnp.max(s, axis=-1, keepdims=True)
        p = jnp.exp2(s - m)
        acc = jnp.dot(p.astype(BF16), v_ones, preferred_element_type=F32)
        ms.append(m)
        accs.append(acc[:, :LANES])
        ls.append(acc[:, LANES:])
    o = jnp.where(lo, accs[0], accs[1]) / jnp.where(lo, ls[0], ls[1])
    return o, ms, ls


def _attn_kernel(*refs, steps, sample_base):
    ins = refs[:15]
    tab_ref = refs[15]
    sample_ins = refs[16:16 + N_SAMPLE_IN]
    outs = refs[16 + N_SAMPLE_IN:22 + N_SAMPLE_IN]
    sample_outs = refs[22 + N_SAMPLE_IN:24 + N_SAMPLE_IN]
    kcat, vcat = refs[24 + N_SAMPLE_IN:]
    s = pl.program_id(1)
    nblk = ATTN_TILE // SPAN
    _sample_attn_kernel(*sample_ins, *sample_outs,
                        batch_row=sample_base + pl.program_id(0) * steps + s)
    for c in range(3):
        _, kp, ko, vp, vo = ins[5 * c:5 * c + 5]
        kcat[c, 0:SPAN] = kp[0, 0]
        kcat[c, SPAN:] = ko[0, 0, 0:SPAN]
        vcat[c, 0:SPAN] = vp[0, 0]
        vcat[c, SPAN:] = vo[0, 0, 0:SPAN]

    for j in range(nblk):
        rows = slice(j * SPAN, (j + 1) * SPAN)
        keys = slice((j - 1) * SPAN, (j + 1) * SPAN)
        for c, (_, dil) in enumerate(DILATED_CONFIGS):
            q_ref, _, ko, _, vo = ins[5 * c:5 * c + 5]
            o_ref, l_ref = outs[2 * c], outs[2 * c + 1]
            units_per_residue = steps // dil
            if j == 0:
                base = (c * 2 + (s % units_per_residue == 0).astype(jnp.int32)) * ATTN_HEADS
            else:
                base = c * 2 * ATTN_HEADS
            lane = lax.broadcasted_iota(jnp.int32, (SPAN, LANES), 1)
            m8 = jnp.zeros((SPAN, LANES), F32)
            l8 = jnp.ones((SPAN, LANES), F32)
            for p in range(ATTN_HEADS // 2):
                lanes = slice(p * LANES, (p + 1) * LANES)
                qb = q_ref[0, 0, rows, lanes]
                if j == 0:
                    kb, vb = kcat[c, :, lanes], vcat[c, :, lanes]
                else:
                    kb, vb = ko[0, 0, keys, lanes], vo[0, 0, keys, lanes]
                o, ms, ls = _pair_attention(qb, kb, vb, tab_ref[base + 2 * p],
                                            tab_ref[base + 2 * p + 1])
                o_ref[0, 0, rows, lanes] = o.astype(BF16)
                for half in range(2):
                    m8 = jnp.where(lane == 2 * p + half, ms[half], m8)
                    l8 = jnp.where(lane == 2 * p + half, ls[half], l8)
            l_ref[0, 0, rows, :] = m8 + jnp.log2(l8)


def _prompt_attention(q, k, v, sample, sample_base):
    B, _, T, W = q[0].shape
    steps = T // ATTN_TILE
    prev_per_tile = ATTN_TILE // SPAN
    args, in_specs, out_specs, out_shape = [], [], [], []
    for c, (_, dil) in enumerate(DILATED_CONFIGS):
        L = T // dil
        upr = L // ATTN_TILE

        def own(b, s, upr=upr):
            return (b, s // upr, s % upr, 0)

        def prev(b, s, upr=upr):
            return (b, s // upr, jnp.maximum((s % upr) * prev_per_tile - 1, 0), 0)

        shape = (B, dil, L, W)
        assert q[c].shape == shape
        args += [q[c], k[c], k[c], v[c], v[c]]
        in_specs += [pl.BlockSpec((1, 1, ATTN_TILE, W), own),
                     pl.BlockSpec((1, 1, SPAN, W), prev),
                     pl.BlockSpec((1, 1, ATTN_TILE, W), own),
                     pl.BlockSpec((1, 1, SPAN, W), prev),
                     pl.BlockSpec((1, 1, ATTN_TILE, W), own)]
        out_specs += [pl.BlockSpec((1, 1, ATTN_TILE, W), own),
                      pl.BlockSpec((1, 1, ATTN_TILE, LANES), own)]
        out_shape += [jax.ShapeDtypeStruct(shape, BF16),
                      jax.ShapeDtypeStruct((B, dil, L, LANES), F32)]
    tab = jnp.asarray(_band_bias_table())
    args.append(tab)
    in_specs.append(pl.BlockSpec(tab.shape, lambda b, s: (0, 0, 0)))
    s_args, s_in, s_out, s_shape = _sample_row_specs(sample, sample_base, B * steps,
                                                     lambda b, s: b * steps + s)
    res = pl.pallas_call(
        functools.partial(_attn_kernel, steps=steps, sample_base=sample_base),
        grid=(B, steps),
        in_specs=in_specs + s_in,
        out_specs=tuple(out_specs + s_out),
        out_shape=tuple(out_shape + s_shape),
        scratch_shapes=[pltpu.VMEM((3, 2 * SPAN, W), BF16)] * 2,
        compiler_params=_params(2),
        name="prompt_attn",
    )(*args, *s_args)
    return list(res[:6]), res[6], res[7]


def _silu(z):
    return z / (1.0 + jnp.exp(-z))


def _gate_project_norm(x, mix_ref, wout_ref, gf_ref):
    y = x + jnp.dot(mix_ref[...], wout_ref[...], preferred_element_type=F32)
    r = lax.rsqrt(jnp.mean(y * y, axis=-1, keepdims=True) + RMS_EPS)
    return y * r * gf_ref[...]


def _unfold(src_ref, slabs, d, tmp=None, step=4):
    n_slab, rows = slabs.shape[0], slabs.shape[1]

    def piece(r, s):
        return src_ref[0, r, :, s * LANES:(s + 1) * LANES].astype(F32)

    if d <= step:
        for r in range(d):
            for s in range(n_slab):
                slabs[s, pl.ds(r, rows // d, stride=d), :] = piece(r, s)
        return
    prev_d, n = d // step, rows // d
    per = rows // prev_d
    for rp in range(prev_d):
        for q in range(step):
            for s in range(n_slab):
                tmp[s, pl.ds(rp * per + q, n, stride=step), :] = piece(q * prev_d + rp, s)
    for rp in range(prev_d):
        for s in range(n_slab):
            slabs[s, pl.ds(rp, per, stride=prev_d), :] = tmp[s, pl.ds(rp * per, per), :]


def _head_expander():
    e = np.zeros((2 * LANES, ATTN_W), np.float32)
    for h in range(ATTN_HEADS):
        e[h, h * HEAD_DIM:(h + 1) * HEAD_DIM] = 1.0
        e[LANES + h, h * HEAD_DIM:(h + 1) * HEAD_DIM] = 1.0
    return e


N_FINISH_IN = 17


def _finish_kernel(*refs, sample_base):
    (x_ref, cy_ref, o1_ref, l1_ref, o4_ref, l4_ref, o16_ref, l16_ref,
     mq_ref, mk_ref, mv_ref, gin_ref, wz0_ref, wz1_ref, wout_ref, gf_ref,
     ex_ref) = refs[:N_FINISH_IN]
    sample_ins = refs[N_FINISH_IN:N_FINISH_IN + N_SAMPLE_IN]
    y_ref = refs[N_FINISH_IN + N_SAMPLE_IN]
    sample_outs = refs[N_FINISH_IN + N_SAMPLE_IN + 1:N_FINISH_IN + N_SAMPLE_IN + 3]
    mix, o4s, l4s, o16s, l16s, otmp, ltmp = refs[N_FINISH_IN + N_SAMPLE_IN + 3:]
    i = pl.program_id(1)
    _sample_attn_kernel(*sample_ins, *sample_outs,
                        batch_row=sample_base + pl.program_id(0) * pl.num_programs(1) + i)
    x = x_ref[0]
    r = lax.rsqrt(jnp.mean(x * x, axis=-1, keepdims=True) + RMS_EPS)
    h = (x * r * gin_ref[...]).astype(BF16)
    z = jnp.concatenate([jnp.dot(h, wz0_ref[...], preferred_element_type=F32),
                         jnp.dot(h, wz1_ref[...], preferred_element_type=F32)], axis=1)
    mix[:, 0:CONV_W] = (cy_ref[0].astype(F32) * _silu(z[:, 0:CONV_W])).astype(BF16)
    dil4, dil16 = DILATED_CONFIGS[1][1], DILATED_CONFIGS[2][1]
    _unfold(o4_ref, o4s, dil4)
    _unfold(l4_ref, l4s, dil4)
    _unfold(o16_ref, o16s, dil16, otmp)
    _unfold(l16_ref, l16s, dil16, ltmp)
    lo, hi = CONV_W, CONV_W + ATTN_W
    l1, l4, l16 = l1_ref[0, 0], l4s[0], l16s[0]
    mx = jnp.maximum(jnp.maximum(l1, l4), l16)
    e1, e4, e16 = jnp.exp2(l1 - mx), jnp.exp2(l4 - mx), jnp.exp2(l16 - mx)
    den = e1 + e4 + e16

    def spread(w):
        w_hi = w.astype(BF16)
        w_lo = (w - w_hi.astype(F32)).astype(BF16)
        return jnp.dot(jnp.concatenate([w_hi, w_lo], axis=1), ex_ref[...],
                       preferred_element_type=F32)

    a1, a4, a16 = spread(e1 / den), spread(e4 / den), spread(e16 / den)
    for s in range(ATTN_W // LANES):
        lanes = slice(s * LANES, (s + 1) * LANES)
        attn = (a1[:, lanes] * o1_ref[0, 0, :, lanes].astype(F32) + a4[:, lanes] * o4s[s]
                + a16[:, lanes] * o16s[s])
        a = lo + s * LANES
        mix[:, a:a + LANES] = (attn * _silu(z[:, a:a + LANES])).astype(BF16)
    for p in range(MEM_HEADS // 2):
        lanes = slice(p * LANES, (p + 1) * LANES)
        o, _, _ = _pair_attention(mq_ref[0, :, lanes], mk_ref[0, :, lanes], mv_ref[0, :, lanes],
                                  None, None)
        a = hi + p * LANES
        mix[:, a:a + LANES] = (o * _silu(z[:, a:a + LANES])).astype(BF16)
    y_ref[0] = _gate_project_norm(x_ref[0], mix, wout_ref, gf_ref)


def _prompt_finish(x, conv_y, attn, mq, mk, mv, g_in, w_in_bf16, w_out_bf16, g_final, sample,
                   sample_base):
    B, T, D = x.shape
    tm = ROW_TILE

    def row(width):
        return pl.BlockSpec((1, tm, width), lambda b, i: (b, i, 0))

    def const(shape):
        return pl.BlockSpec(shape, lambda b, i: (0,) * len(shape))

    memspec = pl.BlockSpec((1,) + mk.shape[1:], lambda b, i: (b, 0, 0))
    in_specs = [row(D), row(CONV_W)]
    for _, dil in DILATED_CONFIGS:
        in_specs += [pl.BlockSpec((1, dil, tm // dil, ATTN_W), lambda b, i: (b, 0, i, 0)),
                     pl.BlockSpec((1, dil, tm // dil, LANES), lambda b, i: (b, 0, i, 0))]
    expander = jnp.asarray(_head_expander(), BF16)
    half = D_MIX // 2
    assert _C_Z % half == 0 and _C_END - _C_Z == 2 * half
    gate_cols = [pl.BlockSpec((D, half), lambda b, i, c=_C_Z // half + j: (0, c))
                 for j in range(2)]
    in_specs += ([row(MEM_W), memspec, memspec, const((1, D))] + gate_cols
                 + [const(w_out_bf16.shape), const((1, D)), const(expander.shape)])
    assert len(in_specs) == N_FINISH_IN
    o_slabs = pltpu.VMEM((ATTN_W // LANES, tm, LANES), F32)
    l_slabs = pltpu.VMEM((1, tm, LANES), F32)
    nt = T // tm
    s_args, s_in, s_out, s_shape = _sample_row_specs(sample, sample_base, B * nt,
                                                     lambda b, i: b * nt + i)
    return pl.pallas_call(
        functools.partial(_finish_kernel, sample_base=sample_base),
        grid=(B, nt),
        in_specs=in_specs + s_in,
        out_specs=tuple([row(D)] + s_out),
        out_shape=tuple([jax.ShapeDtypeStruct((B, T, D), F32)] + s_shape),
        scratch_shapes=[pltpu.VMEM((tm, D_MIX), BF16),
                        o_slabs, l_slabs, o_slabs, l_slabs, o_slabs, l_slabs],
        compiler_params=_params(2),
        name="prompt_finish",
    )(x, conv_y, *attn, mq, mk, mv, g_in.reshape(1, D), w_in_bf16, w_in_bf16, w_out_bf16,
      g_final.reshape(1, D), expander, *s_args)


def _sample_tables(S, n_buf):
    j = np.arange(S)[:, None]
    t = np.arange(n_buf)[None, :]
    dist = n_buf + j - t
    count = np.zeros(dist.shape)
    for win, dil in DILATED_CONFIGS:
        count += (dist % dil == 0) & (dist <= win)
    bias = np.where(count > 0, -_slopes()[:, None, None] * dist, -np.inf)
    bias = bias.reshape(ATTN_HEADS // 2, 2 * S, n_buf)
    count = np.concatenate([count, count], axis=0)
    return count.astype(np.float32), bias.astype(np.float32)


def _stack_pair(a, p):
    ap = a[:, p * LANES:(p + 1) * LANES]
    lo = lax.broadcasted_iota(jnp.int32, ap.shape, 1) < HEAD_DIM
    zero = jnp.zeros_like(ap)
    return jnp.concatenate([jnp.where(lo, ap, zero), jnp.where(lo, zero, ap)], axis=0)


def _unstack_pair(o, S):
    lo = lax.broadcasted_iota(jnp.int32, (S, LANES), 1) < HEAD_DIM
    return jnp.where(lo, o[:S], o[S:])


def _sample_attn_kernel(q_ref, kn_ref, vn_ref, kt_ref, vt_ref, mq_ref, mkt_ref, mvt_ref,
                        cnt_ref, bias_ref, o_ref, mo_ref, *, batch_row):
    R, S = kt_ref.shape[0], o_ref.shape[1]
    nt = (((1,), (1,)), ((), ()))
    slopes = _slopes()
    n_cfg = float(len(DILATED_CONFIGS))
    row = lax.broadcasted_iota(jnp.int32, (2 * S, 1), 0)
    upper = row >= S
    rowj = jnp.where(upper, row - S, row)
    cnt = cnt_ref[...]
    rows = range(R)
    which = batch_row % (SUBLANES // S)

    def tokens(ref):
        blk = ref[...]
        tok = blk[0:S]
        for i in range(1, SUBLANES // S):
            tok = jnp.where(which == i, blk[i * S:(i + 1) * S], tok)
        return tok

    q_tok, kn_tok, vn_tok, mq_tok = (tokens(ref) for ref in (q_ref, kn_ref, vn_ref, mq_ref))

    def lane_group(tok, p):
        return tok[:, p * LANES:(p + 1) * LANES]

    def pair_rows(ref, r, p):
        return ref[r, 2 * p:2 * p + 2].reshape(2 * HEAD_DIM, ref.shape[-1]).astype(BF16)

    units = [(r, p) for r in rows for p in range(ATTN_HEADS // 2)]
    munits = [(r, p) for r in rows for p in range(MEM_HEADS // 2)]
    qs = [_stack_pair(q_tok, p) for r, p in units]
    scores = [jnp.dot(q8.astype(BF16), pair_rows(kt_ref, r, p),
                      preferred_element_type=F32) + bias_ref[p] for (r, p), q8 in zip(units, qs)]
    mscores = [jnp.dot(_stack_pair(mq_tok, p).astype(BF16), pair_rows(mkt_ref, r, p),
                       preferred_element_type=F32) for r, p in munits]
    probs, stats = [], []
    for (r, p), q8, s in zip(units, qs, scores):
        knp = lane_group(kn_tok, p)
        slope = jnp.where(upper, float(slopes[2 * p + 1]), float(slopes[2 * p]))
        m = jnp.max(s, axis=-1, keepdims=True)
        s_new = []
        for jp in range(S):
            d = rowj - jp
            sj = jnp.sum(q8 * knp[jp:jp + 1], axis=-1, keepdims=True)
            sj = jnp.where(d >= 0, sj - slope * d.astype(F32), -jnp.inf)
            s_new.append(sj)
            m = jnp.maximum(m, sj)
        pr = cnt * jnp.exp(s - m)
        l = jnp.sum(pr, axis=-1, keepdims=True)
        w_new = []
        for jp in range(S):
            w = jnp.where(rowj == jp, n_cfg, 1.0) * jnp.exp(s_new[jp] - m)
            l = l + w
            w_new.append(w)
        probs.append(pr.astype(BF16))
        stats.append((l, w_new))
    mprobs, mden = [], []
    for s in mscores:
        pr = jnp.exp(s - jnp.max(s, axis=-1, keepdims=True))
        mden.append(jnp.sum(pr, axis=-1, keepdims=True))
        mprobs.append(pr.astype(BF16))
    accs = [lax.dot_general(pr, pair_rows(vt_ref, r, p), nt, preferred_element_type=F32)
            for (r, p), pr in zip(units, probs)]
    maccs = [lax.dot_general(pr, pair_rows(mvt_ref, r, p), nt, preferred_element_type=F32)
             for (r, p), pr in zip(munits, mprobs)]
    outs = []
    for (r, p), acc, (l, w_new) in zip(units, accs, stats):
        vnp = lane_group(vn_tok, p)
        for jp in range(S):
            acc = acc + w_new[jp] * vnp[jp:jp + 1]
        outs.append(_unstack_pair(acc / l, S))
    mouts = [_unstack_pair(acc / den, S) for acc, den in zip(maccs, mden)]
    n_p, n_mp = ATTN_HEADS // 2, MEM_HEADS // 2
    for r in rows:
        o_ref[r] = jnp.concatenate(outs[r * n_p:(r + 1) * n_p], axis=-1)
        mo_ref[r] = jnp.concatenate(mouts[r * n_mp:(r + 1) * n_mp], axis=-1)


N_SAMPLE_IN = 10


def _sample_row_specs(sample, base, n_rows, step_of):
    q, k_new, v_new, cache_kt, cache_vt, mq, mem_kt, mem_vt = sample
    DB, n_buf = cache_kt.shape[0], cache_kt.shape[-1]
    S, W = q.shape[0] // DB, q.shape[1]
    assert SUBLANES % S == 0, "token rows are fetched in 8-row blocks"
    assert n_buf >= DILATED_CONFIGS[-1][0] and S <= DILATED_CONFIGS[1][1], \
        "new tokens reach each other through dilation 1 only; the buffer covers every window"
    cnt, bias = (jnp.asarray(t) for t in _sample_tables(S, n_buf))
    per_block = SUBLANES // S

    def tokens_in(a):
        return pl.BlockSpec((SUBLANES, a.shape[1]),
                            lambda *g: ((base + step_of(*g)) // per_block, 0))

    def row_in(a):
        return pl.BlockSpec((1,) + a.shape[1:],
                            lambda *g: (base + step_of(*g),) + (0,) * (a.ndim - 1))

    def row_out(width):
        return pl.BlockSpec((1, S, width), lambda *g: (step_of(*g), 0, 0))

    def const(a):
        return pl.BlockSpec(a.shape, lambda *g: (0,) * a.ndim)

    args = list(sample) + [cnt, bias]
    assert len(args) == N_SAMPLE_IN
    in_specs = [tokens_in(q), tokens_in(k_new), tokens_in(v_new), row_in(cache_kt),
                row_in(cache_vt), tokens_in(mq), row_in(mem_kt), row_in(mem_vt),
                const(cnt), const(bias)]
    out_specs = [row_out(W), row_out(MEM_W)]
    out_shape = [jax.ShapeDtypeStruct((n_rows, S, W), F32),
                 jax.ShapeDtypeStruct((n_rows, S, MEM_W), F32)]
    return args, in_specs, out_specs, out_shape


def _sample_attention(sample, base, n_rows):
    args, in_specs, out_specs, out_shape = _sample_row_specs(sample, base, n_rows, lambda i: i)

    def body(*refs):
        _sample_attn_kernel(*refs, batch_row=base + pl.program_id(0))

    return pl.pallas_call(
        body,
        grid=(n_rows,),
        in_specs=in_specs,
        out_specs=tuple(out_specs),
        out_shape=tuple(out_shape),
        compiler_params=_params(1),
        name="sample_attn",
    )(*args)


def _sample_finish_kernel(x_ref, cb_ref, u0_ref, u1_ref, u2_ref, attn_ref, mem_ref, z_ref,
                          cw_ref, wout_ref, gf_ref, y_ref, mix):
    cw = cw_ref[...]
    conv = cw[0:1] * u0_ref[...] + cw[1:2] * u1_ref[...] + cw[2:3] * u2_ref[...]
    z = z_ref[...].astype(F32)
    lo, hi = CONV_W, CONV_W + ATTN_W
    mix[:, 0:lo] = (cb_ref[...].astype(F32) * conv * _silu(z[:, 0:lo])).astype(BF16)
    mix[:, lo:hi] = (attn_ref[...] * _silu(z[:, lo:hi])).astype(BF16)
    mix[:, hi:] = (mem_ref[...] * _silu(z[:, hi:])).astype(BF16)
    y_ref[...] = _gate_project_norm(x_ref[...], mix, wout_ref, gf_ref)


def _sample_finish(x, cb, u_taps, attn, mem, z, conv_w, w_out_bf16, g_final):
    rows, D = x.shape
    return pl.pallas_call(
        _sample_finish_kernel,
        out_shape=jax.ShapeDtypeStruct((rows, D), F32),
        scratch_shapes=[pltpu.VMEM((rows, D_MIX), BF16)],
        compiler_params=pltpu.CompilerParams(vmem_limit_bytes=VMEM_LIMIT),
        name="sample_finish",
    )(x, cb, *u_taps, attn, mem, z, conv_w, w_out_bf16, g_final.reshape(1, D))


def kernel(x_prompt, x_sample, mem_prompt, cache_win_k, cache_win_v, cache_conv, cache_mem_k,
           cache_mem_v, g_in, w_in, conv_w, g_mem, w_mem_kv, w_out, g_final):
    assert g_in.shape[0] == 1, "the final RMSNorm is fused into the single layer's tail"
    B, T, D = x_prompt.shape
    DB, S, _ = x_sample.shape
    n_keep = min(DILATED_CONFIGS[-1][0], T)
    w_in_b = w_in[0].astype(BF16)
    w_out_b = w_out[0].astype(BF16)
    dils = tuple(d for _, d in DILATED_CONFIGS)
    wkvt_b = _transposed_columns(w_in[0], _C_K, _C_MQ)

    def heads_last(t, heads):
        return jnp.transpose(t.reshape(t.shape[0], heads, HEAD_DIM, -1), (0, 3, 1, 2))[None]

    def heads_first(c):
        return jnp.transpose(c, (0, 2, 3, 1))

    rows = DB * S
    cb_s, u_s, q_s, _, _, k32, v32, mq_s, z_s = _project(
        x_sample.reshape(1, rows, D), g_in[0], w_in_b, wkvt_b, rows, False, (1,), Q_SCALE)
    k32 = k32.reshape(rows, ATTN_W)
    v32 = v32.reshape(rows, ATTN_W)
    sample = (q_s[0].reshape(rows, ATTN_W).astype(F32), k32, v32,
              heads_first(cache_win_k[0]), heads_first(cache_win_v[0]),
              mq_s.reshape(rows, MEM_W).astype(F32),
              heads_first(cache_mem_k[0]), heads_first(cache_mem_v[0]))
    n_proj = n_tail = B * (T // ROW_TILE)
    n_attn = B * (T // ATTN_TILE)
    n_own = DB - n_proj - n_attn - n_tail
    assert n_own > 0
    conv_y, u_tails, q, k, v, kt32, vt32, mq, _, attn_p, mem_p = _project(
        x_prompt, g_in[0], w_in_b, wkvt_b, n_keep, True, dils, Q_SCALE * LOG2E, sample, 0,
        conv_w[0], emit_z=False)
    mkt32, mvt32, mk, mv = _mem_kv(mem_prompt, g_mem[0], w_mem_kv[0].astype(BF16))
    attn, attn_a, mem_a = _prompt_attention(q, k, v, sample, n_proj)
    y_prompt, attn_t, mem_t = _prompt_finish(x_prompt, conv_y, attn, mq, mk, mv, g_in[0],
                                             w_in_b, w_out_b, g_final, sample, n_proj + n_attn)
    attn_o, mem_o = _sample_attention(sample, n_proj + n_attn + n_tail, n_own)
    attn_s = jnp.concatenate([attn_p, attn_a, attn_t, attn_o], axis=0)
    mem_s = jnp.concatenate([mem_p, mem_a, mem_t, mem_o], axis=0)
    p_win_k = heads_last(kt32, ATTN_HEADS)
    p_win_v = heads_last(vt32, ATTN_HEADS)
    p_conv = u_tails[None, :, -(CONV_WIDTH - 1):]
    p_mem_k = heads_last(mkt32, MEM_HEADS)
    p_mem_v = heads_last(mvt32, MEM_HEADS)
    full = jnp.concatenate([cache_conv[0], u_s.reshape(DB, S, CONV_W)], axis=1)
    taps = [full[:, t:t + S].reshape(rows, CONV_W) for t in range(CONV_WIDTH)]
    y_sample = _sample_finish(x_sample.reshape(rows, D), cb_s.reshape(rows, CONV_W), taps,
                              attn_s.reshape(rows, ATTN_W), mem_s.reshape(rows, MEM_W),
                              z_s.reshape(rows, D_MIX), conv_w[0], w_out_b,
                              g_final).reshape(DB, S, D)
    s_win_k = k32.reshape(1, DB, S, ATTN_HEADS, HEAD_DIM)
    s_win_v = v32.reshape(1, DB, S, ATTN_HEADS, HEAD_DIM)
    s_conv = full[None, :, -(CONV_WIDTH - 1):]
    return (y_prompt, y_sample, p_win_k, p_win_v, p_conv, p_mem_k, p_mem_v,
            s_win_k, s_win_v, s_conv)
```

```python
import functools

import numpy as np
import jax
import jax.numpy as jnp
from jax import lax
from jax.experimental import pallas as pl
from jax.experimental.pallas import tpu as pltpu

F32 = jnp.float32
BF16 = jnp.bfloat16

HEAD_DIM = 64
ATTN_HEADS = 8
MEM_HEADS = 4
CONV_W = 256
ATTN_W = 512
MEM_W = 256
D_MIX = 1024
CONV_WIDTH = 3
DILATED_CONFIGS = ((128, 1), (512, 4), (2048, 16))
SPAN = 128
ALIBI_MAX = 8.0
RMS_EPS = 1e-6
Q_SCALE = HEAD_DIM ** -0.5
LOG2E = float(np.log2(np.e))

LANES = 128
SUBLANES = 8
VMEM_LIMIT = 56 * 1024 * 1024

_C_CB, _C_CC, _C_Q, _C_K, _C_V, _C_MQ, _C_Z, _C_END = 0, 256, 768, 1280, 1792, 2304, 2560, 3584

ROW_TILE = 512
ATTN_TILE = 512


def _slopes():
    h = np.arange(ATTN_HEADS, dtype=np.float64) + 1.0
    return np.exp2(-ALIBI_MAX * h / ATTN_HEADS)


def _params(n_axes):
    return pltpu.CompilerParams(
        dimension_semantics=("arbitrary",) * n_axes,
        vmem_limit_bytes=VMEM_LIMIT)


def _transpose_cast_kernel(w_ref, o_ref):
    o_ref[...] = w_ref[...].T.astype(BF16)


def _transposed_columns(w, lo, hi):
    K = w.shape[0]
    blk = 2 * LANES
    assert lo % blk == 0 and hi % blk == 0
    return pl.pallas_call(
        _transpose_cast_kernel,
        grid=((hi - lo) // blk,),
        in_specs=[pl.BlockSpec((K, blk), lambda i: (0, lo // blk + i))],
        out_specs=pl.BlockSpec((blk, K), lambda i: (i, 0)),
        out_shape=jax.ShapeDtypeStruct((hi - lo, K), BF16),
        compiler_params=_params(1),
        name="wkv_t",
    )(w)


def _fold_store(res, out_refs, slabs, dils):
    rows = res.shape[0]
    out_refs[0][0, 0] = res.astype(BF16)
    if len(dils) == 1:
        return
    n_slab = res.shape[1] // LANES
    src, prev_d = slabs[0], 1
    for s in range(n_slab):
        src[s] = res[:, s * LANES:(s + 1) * LANES]
    for idx, (d, o_ref) in enumerate(zip(dils[1:], out_refs[1:])):
        step, n = d // prev_d, rows // d
        dst = slabs[idx + 1] if idx + 2 < len(dils) else None
        for rp in range(prev_d):
            for q in range(step):
                r = q * prev_d + rp
                for s in range(n_slab):
                    val = src[s, pl.ds(rp * (rows // prev_d) + q, n, stride=step), :]
                    o_ref[0, r, :, s * LANES:(s + 1) * LANES] = val.astype(BF16)
                    if dst is not None:
                        dst[s, pl.ds(r * n, n), :] = val
        src, prev_d = dst, d


def _proj_kernel(x_ref, g_ref, w_ref, wkvt_ref, *refs, first, keep_transposed, dils, q_scale,
                 hosts_sample, fuse_conv, emit_z):
    n = len(dils)
    if fuse_conv:
        cw_ref, refs, ucat = refs[0], refs[1:-1], refs[-1]

        @pl.when(pl.program_id(1) == 0)
        def _():
            ucat[0:SUBLANES] = jnp.zeros((SUBLANES, CONV_W), F32)
    n_out = 5 + 3 * n + (1 if emit_z else 0)
    if hosts_sample:
        sample_ins, refs = refs[:N_SAMPLE_IN], refs[N_SAMPLE_IN:]
        sample_outs = refs[n_out:n_out + 2]
        refs = refs[:n_out] + refs[n_out + 2:]
        step = pl.program_id(0) * pl.num_programs(1) + pl.program_id(1)
        _sample_attn_kernel(*sample_ins, *sample_outs, batch_row=hosts_sample[1] + step)
    cb_ref, u_ref = refs[0:2]
    q_refs, k_refs, v_refs = refs[2:2 + n], refs[2 + n:2 + 2 * n], refs[2 + 2 * n:2 + 3 * n]
    k32_ref, v32_ref, mq_ref = refs[2 + 3 * n:5 + 3 * n]
    scratch = refs[n_out:]
    slabs = [scratch[a * (n - 1):(a + 1) * (n - 1)] for a in range(3)]
    x = x_ref[0]
    r = lax.rsqrt(jnp.mean(x * x, axis=-1, keepdims=True) + RMS_EPS)
    h = (x * r * g_ref[...]).astype(BF16)

    def seg(lo, hi):
        return jnp.dot(h, w_ref[:, lo:hi], preferred_element_type=F32)

    cb = seg(_C_CB, _C_CC)
    cch = seg(_C_CC, _C_Q)
    u = cch[:, :CONV_W] * cch[:, CONV_W:]
    if fuse_conv:
        rows = u.shape[0]
        ucat[SUBLANES:] = u
        cw = cw_ref[...]
        conv = (cw[0:1] * ucat[pl.ds(SUBLANES - 2, rows)]
                + cw[1:2] * ucat[pl.ds(SUBLANES - 1, rows)]
                + cw[2:3] * ucat[pl.ds(SUBLANES, rows)])
        cb_ref[0] = (cb * conv).astype(BF16)
        u_ref[0] = u[rows - SUBLANES:]
        ucat[0:SUBLANES] = u[rows - SUBLANES:]
    else:
        cb_ref[0] = cb.astype(BF16)
        u_ref[0] = u
    _fold_store(seg(_C_Q, _C_K) * q_scale, q_refs, slabs[0], dils)
    k = seg(_C_K, _C_V)
    _fold_store(k, k_refs, slabs[1], dils)
    v = seg(_C_V, _C_MQ)
    _fold_store(v, v_refs, slabs[2], dils)
    mq_ref[0] = (seg(_C_MQ, _C_Z) * q_scale).astype(BF16)
    if emit_z:
        refs[5 + 3 * n][0] = seg(_C_Z, _C_END).astype(BF16)
    if keep_transposed:
        @pl.when(pl.program_id(1) >= first)
        def _():
            nt = (((1,), (1,)), ((), ()))
            k32_ref[0] = lax.dot_general(wkvt_ref[0:ATTN_W], h, nt, preferred_element_type=F32)
            v32_ref[0] = lax.dot_general(wkvt_ref[ATTN_W:], h, nt, preferred_element_type=F32)
    else:
        k32_ref[0] = k
        v32_ref[0] = v


def _project(x, g, w_bf16, wkvt_bf16, n_keep, keep_transposed, dils, q_scale, sample=None,
             sample_base=0, conv_w=None, emit_z=True):
    B, T, D = x.shape
    tm = ROW_TILE
    nt = T // tm
    first = (T - n_keep) // tm
    n = len(dils)

    def row(width):
        return pl.BlockSpec((1, tm, width), lambda b, i: (b, i, 0))

    folded_specs = [pl.BlockSpec((1, d, tm // d, ATTN_W), lambda b, i: (b, 0, i, 0)) for d in dils]
    folded_shapes = [jax.ShapeDtypeStruct((B, d, T // d, ATTN_W), BF16) for d in dils]

    if keep_transposed:
        keep = pl.BlockSpec((1, ATTN_W, tm), lambda b, i: (b, 0, jnp.maximum(i - first, 0)))
        keep_shape = (B, ATTN_W, n_keep)
    else:
        keep = pl.BlockSpec((1, tm, ATTN_W), lambda b, i: (b, jnp.maximum(i - first, 0), 0))
        keep_shape = (B, n_keep, ATTN_W)
    fuse_conv = conv_w is not None
    u_rows = SUBLANES if fuse_conv else tm
    out_shape = (
        [jax.ShapeDtypeStruct((B, T, CONV_W), BF16),
         jax.ShapeDtypeStruct((B, nt * u_rows, CONV_W), F32)]
        + folded_shapes * 3
        + [jax.ShapeDtypeStruct(keep_shape, F32),
           jax.ShapeDtypeStruct(keep_shape, F32),
           jax.ShapeDtypeStruct((B, T, MEM_W), BF16)])
    out_specs = ([row(CONV_W), pl.BlockSpec((1, u_rows, CONV_W), lambda b, i: (b, i, 0))]
                 + folded_specs * 3 + [keep, keep, row(MEM_W)])
    if emit_z:
        out_shape.append(jax.ShapeDtypeStruct((B, T, D_MIX), BF16))
        out_specs.append(row(D_MIX))
    scratch = [pltpu.VMEM((ATTN_W // LANES, tm, LANES), F32)] * (3 * (n - 1))
    once = pl.Buffered(1)
    in_specs = [row(D),
                pl.BlockSpec((1, D), lambda b, i: (0, 0)),
                pl.BlockSpec(w_bf16.shape, lambda b, i: (0, 0), pipeline_mode=once),
                pl.BlockSpec(wkvt_bf16.shape, lambda b, i: (0, 0), pipeline_mode=once)]
    args = [x, g.reshape(1, D), w_bf16, wkvt_bf16]
    if fuse_conv:
        args.append(conv_w)
        in_specs.append(pl.BlockSpec(conv_w.shape, lambda b, i: (0, 0)))
        scratch = scratch + [pltpu.VMEM((tm + SUBLANES, CONV_W), F32)]
    if sample is not None:
        s_args, s_in, s_out, s_shape = _sample_row_specs(sample, sample_base, B * nt,
                                                         lambda b, i: b * nt + i)
        args, in_specs = args + s_args, in_specs + s_in
        out_specs, out_shape = out_specs + s_out, out_shape + s_shape
    res = pl.pallas_call(
        functools.partial(_proj_kernel, first=first, keep_transposed=keep_transposed, dils=dils,
                          q_scale=q_scale,
                          hosts_sample=(True, sample_base) if sample is not None else None,
                          fuse_conv=fuse_conv, emit_z=emit_z),
        grid=(B, nt),
        in_specs=in_specs,
        out_specs=tuple(out_specs),
        out_shape=tuple(out_shape),
        scratch_shapes=scratch,
        compiler_params=_params(2),
        name="proj",
    )(*args)
    cb, u = res[0:2]
    q, k, v = res[2:2 + n], res[2 + n:2 + 2 * n], res[2 + 2 * n:2 + 3 * n]
    k32, v32, mq = res[2 + 3 * n:5 + 3 * n]
    n_out = 5 + 3 * n + (1 if emit_z else 0)
    z = res[5 + 3 * n] if emit_z else None
    return (cb, u, q, k, v, k32, v32, mq, z) + tuple(res[n_out:])


def _memkv_kernel(x_ref, g_ref, w_ref, k32_ref, v32_ref, k_ref, v_ref):
    x = x_ref[...]
    r = lax.rsqrt(jnp.mean(x * x, axis=-1, keepdims=True) + RMS_EPS)
    h = (x * r * g_ref[...]).astype(BF16)
    kv = jnp.dot(h, w_ref[...], preferred_element_type=F32)
    k_ref[...] = kv[:, :MEM_W].astype(BF16)
    v_ref[...] = kv[:, MEM_W:].astype(BF16)
    n_mem = k32_ref.shape[2]
    for b in range(k32_ref.shape[0]):
        kvt = kv[b * n_mem:(b + 1) * n_mem].T
        k32_ref[b] = kvt[:MEM_W]
        v32_ref[b] = kvt[MEM_W:]


def _mem_kv(mem, g, w_bf16):
    B, M, D = mem.shape
    rows = B * M
    k32, v32, k, v = pl.pallas_call(
        _memkv_kernel,
        out_shape=(jax.ShapeDtypeStruct((B, MEM_W, M), F32),
                   jax.ShapeDtypeStruct((B, MEM_W, M), F32),
                   jax.ShapeDtypeStruct((rows, MEM_W), BF16),
                   jax.ShapeDtypeStruct((rows, MEM_W), BF16)),
        compiler_params=pltpu.CompilerParams(vmem_limit_bytes=VMEM_LIMIT),
        name="memkv",
    )(mem.reshape(rows, D), g.reshape(1, D), w_bf16)
    return k32, v32, k.reshape(B, M, MEM_W), v.reshape(B, M, MEM_W)


def _band_bias_table():
    qi = np.arange(SPAN)[:, None] + SPAN
    ki = np.arange(2 * SPAN)[None, :]
    dist = qi - ki
    valid = (dist >= 0) & (dist <= SPAN)
    slopes = _slopes()
    tabs = []
    for _, dil in DILATED_CONFIGS:
        for first in (False, True):
            ok = valid & (ki >= SPAN) if first else valid
            for h in range(ATTN_HEADS):
                bias = -(slopes[h] * dil) * dist * LOG2E
                tabs.append(np.where(ok, bias, -np.inf))
    return np.stack(tabs).astype(np.float32)


def _pair_attention(qb, kb, vb, bias_lo, bias_hi):
    lane = lax.broadcasted_iota(jnp.int32, qb.shape, 1)
    lo = lane < HEAD_DIM
    zero = jnp.zeros_like(qb)
    v_ones = jnp.concatenate([vb, jnp.ones_like(vb)], axis=1)
    accs, ms, ls = [], [], []
    for sel, bias in ((lo, bias_lo), (~lo, bias_hi)):
        qm = jnp.where(sel, qb, zero)
        s = lax.dot_general(qm, kb, (((1,), (1,)), ((), ())), preferred_element_type=F32)
        if bias is not None:
            s = s + bias
        m = jnp.max(s, axis=-1, keepdims=True)
        p = jnp.exp2(s - m)
        acc = jnp.dot(p.astype(BF16), v_ones, preferred_element_type=F32)
        ms.append(m)
        accs.append(acc[:, :LANES])
        ls.append(acc[:, LANES:])
    o = jnp.where(lo, accs[0], accs[1]) / jnp.where(lo, ls[0], ls[1])
    return o, ms, ls


def _attn_kernel(*refs, steps, sample_base):
    ins = refs[:15]
    tab_ref = refs[15]
    sample_ins = refs[16:16 + N_SAMPLE_IN]
    outs = refs[16 + N_SAMPLE_IN:22 + N_SAMPLE_IN]
    sample_outs = refs[22 + N_SAMPLE_IN:24 + N_SAMPLE_IN]
    kcat, vcat = refs[24 + N_SAMPLE_IN:]
    s = pl.program_id(1)
    nblk = ATTN_TILE // SPAN
    _sample_attn_kernel(*sample_ins, *sample_outs,
                        batch_row=sample_base + pl.program_id(0) * steps + s)
    for c in range(3):
        _, kp, ko, vp, vo = ins[5 * c:5 * c + 5]
        kcat[c, 0:SPAN] = kp[0, 0]
        kcat[c, SPAN:] = ko[0, 0, 0:SPAN]
        vcat[c, 0:SPAN] = vp[0, 0]
        vcat[c, SPAN:] = vo[0, 0, 0:SPAN]

    for j in range(nblk):
        rows = slice(j * SPAN, (j + 1) * SPAN)
        keys = slice((j - 1) * SPAN, (j + 1) * SPAN)
        for c, (_, dil) in enumerate(DILATED_CONFIGS):
            q_ref, _, ko, _, vo = ins[5 * c:5 * c + 5]
            o_ref, l_ref = outs[2 * c], outs[2 * c + 1]
            units_per_residue = steps // dil
            if j == 0:
                base = (c * 2 + (s % units_per_residue == 0).astype(jnp.int32)) * ATTN_HEADS
            else:
                base = c * 2 * ATTN_HEADS
            lane = lax.broadcasted_iota(jnp.int32, (SPAN, LANES), 1)
            m8 = jnp.zeros((SPAN, LANES), F32)
            l8 = jnp.ones((SPAN, LANES), F32)
            for p in range(ATTN_HEADS // 2):
                lanes = slice(p * LANES, (p + 1) * LANES)
                qb = q_ref[0, 0, rows, lanes]
                if j == 0:
                    kb, vb = kcat[c, :, lanes], vcat[c, :, lanes]
                else:
                    kb, vb = ko[0, 0, keys, lanes], vo[0, 0, keys, lanes]
                o, ms, ls = _pair_attention(qb, kb, vb, tab_ref[base + 2 * p],
                                            tab_ref[base + 2 * p + 1])
                o_ref[0, 0, rows, lanes] = o.astype(BF16)
                for half in range(2):
                    m8 = jnp.where(lane == 2 * p + half, ms[half], m8)
                    l8 = jnp.where(lane == 2 * p + half, ls[half], l8)
            l_ref[0, 0, rows, :] = m8 + jnp.log2(l8)


def _prompt_attention(q, k, v, sample, sample_base):
    B, _, T, W = q[0].shape
    steps = T // ATTN_TILE
    prev_per_tile = ATTN_TILE // SPAN
    args, in_specs, out_specs, out_shape = [], [], [], []
    for c, (_, dil) in enumerate(DILATED_CONFIGS):
        L = T // dil
        upr = L // ATTN_TILE

        def own(b, s, upr=upr):
            return (b, s // upr, s % upr, 0)

        def prev(b, s, upr=upr):
            return (b, s // upr, jnp.maximum((s % upr) * prev_per_tile - 1, 0), 0)

        shape = (B, dil, L, W)
        assert q[c].shape == shape
        args += [q[c], k[c], k[c], v[c], v[c]]
        in_specs += [pl.BlockSpec((1, 1, ATTN_TILE, W), own),
                     pl.BlockSpec((1, 1, SPAN, W), prev),
                     pl.BlockSpec((1, 1, ATTN_TILE, W), own),
                     pl.BlockSpec((1, 1, SPAN, W), prev),
                     pl.BlockSpec((1, 1, ATTN_TILE, W), own)]
        out_specs += [pl.BlockSpec((1, 1, ATTN_TILE, W), own),
                      pl.BlockSpec((1, 1, ATTN_TILE, LANES), own)]
        out_shape += [jax.ShapeDtypeStruct(shape, BF16),
                      jax.ShapeDtypeStruct((B, dil, L, LANES), F32)]
    tab = jnp.asarray(_band_bias_table())
    args.append(tab)
    in_specs.append(pl.BlockSpec(tab.shape, lambda b, s: (0, 0, 0)))
    s_args, s_in, s_out, s_shape = _sample_row_specs(sample, sample_base, B * steps,
                                                     lambda b, s: b * steps + s)
    res = pl.pallas_call(
        functools.partial(_attn_kernel, steps=steps, sample_base=sample_base),
        grid=(B, steps),
        in_specs=in_specs + s_in,
        out_specs=tuple(out_specs + s_out),
        out_shape=tuple(out_shape + s_shape),
        scratch_shapes=[pltpu.VMEM((3, 2 * SPAN, W), BF16)] * 2,
        compiler_params=_params(2),
        name="prompt_attn",
    )(*args, *s_args)
    return list(res[:6]), res[6], res[7]


def _silu(z):
    return z / (1.0 + jnp.exp(-z))


def _gate_project_norm(x, mix_ref, wout_ref, gf_ref):
    y = x + jnp.dot(mix_ref[...], wout_ref[...], preferred_element_type=F32)
    r = lax.rsqrt(jnp.mean(y * y, axis=-1, keepdims=True) + RMS_EPS)
    return y * r * gf_ref[...]


def _unfold(src_ref, slabs, d, tmp=None, step=4):
    n_slab, rows = slabs.shape[0], slabs.shape[1]

    def piece(r, s):
        return src_ref[0, r, :, s * LANES:(s + 1) * LANES].astype(F32)

    if d <= step:
        for r in range(d):
            for s in range(n_slab):
                slabs[s, pl.ds(r, rows // d, stride=d), :] = piece(r, s)
        return
    prev_d, n = d // step, rows // d
    per = rows // prev_d
    for rp in range(prev_d):
        for q in range(step):
            for s in range(n_slab):
                tmp[s, pl.ds(rp * per + q, n, stride=step), :] = piece(q * prev_d + rp, s)
    for rp in range(prev_d):
        for s in range(n_slab):
            slabs[s, pl.ds(rp, per, stride=prev_d), :] = tmp[s, pl.ds(rp * per, per), :]


def _head_expander():
    e = np.zeros((2 * LANES, ATTN_W), np.float32)
    for h in range(ATTN_HEADS):
        e[h, h * HEAD_DIM:(h + 1) * HEAD_DIM] = 1.0
        e[LANES + h, h * HEAD_DIM:(h + 1) * HEAD_DIM] = 1.0
    return e


N_FINISH_IN = 17


def _finish_kernel(*refs, sample_base):
    (x_ref, cy_ref, o1_ref, l1_ref, o4_ref, l4_ref, o16_ref, l16_ref,
     mq_ref, mk_ref, mv_ref, gin_ref, wz0_ref, wz1_ref, wout_ref, gf_ref,
     ex_ref) = refs[:N_FINISH_IN]
    sample_ins = refs[N_FINISH_IN:N_FINISH_IN + N_SAMPLE_IN]
    y_ref = refs[N_FINISH_IN + N_SAMPLE_IN]
    sample_outs = refs[N_FINISH_IN + N_SAMPLE_IN + 1:N_FINISH_IN + N_SAMPLE_IN + 3]
    mix, o4s, l4s, o16s, l16s, otmp, ltmp = refs[N_FINISH_IN + N_SAMPLE_IN + 3:]
    i = pl.program_id(1)
    _sample_attn_kernel(*sample_ins, *sample_outs,
                        batch_row=sample_base + pl.program_id(0) * pl.num_programs(1) + i)
    x = x_ref[0]
    r = lax.rsqrt(jnp.mean(x * x, axis=-1, keepdims=True) + RMS_EPS)
    h = (x * r * gin_ref[...]).astype(BF16)
    z = jnp.concatenate([jnp.dot(h, wz0_ref[...], preferred_element_type=F32),
                         jnp.dot(h, wz1_ref[...], preferred_element_type=F32)], axis=1)
    mix[:, 0:CONV_W] = (cy_ref[0].astype(F32) * _silu(z[:, 0:CONV_W])).astype(BF16)
    dil4, dil16 = DILATED_CONFIGS[1][1], DILATED_CONFIGS[2][1]
    _unfold(o4_ref, o4s, dil4)
    _unfold(l4_ref, l4s, dil4)
    _unfold(o16_ref, o16s, dil16, otmp)
    _unfold(l16_ref, l16s, dil16, ltmp)
    lo, hi = CONV_W, CONV_W + ATTN_W
    l1, l4, l16 = l1_ref[0, 0], l4s[0], l16s[0]
    mx = jnp.maximum(jnp.maximum(l1, l4), l16)
    e1, e4, e16 = jnp.exp2(l1 - mx), jnp.exp2(l4 - mx), jnp.exp2(l16 - mx)
    den = e1 + e4 + e16

    def spread(w):
        w_hi = w.astype(BF16)
        w_lo = (w - w_hi.astype(F32)).astype(BF16)
        return jnp.dot(jnp.concatenate([w_hi, w_lo], axis=1), ex_ref[...],
                       preferred_element_type=F32)

    a1, a4, a16 = spread(e1 / den), spread(e4 / den), spread(e16 / den)
    for s in range(ATTN_W // LANES):
        lanes = slice(s * LANES, (s + 1) * LANES)
        attn = (a1[:, lanes] * o1_ref[0, 0, :, lanes].astype(F32) + a4[:, lanes] * o4s[s]
                + a16[:, lanes] * o16s[s])
        a = lo + s * LANES
        mix[:, a:a + LANES] = (attn * _silu(z[:, a:a + LANES])).astype(BF16)
    for p in range(MEM_HEADS // 2):
        lanes = slice(p * LANES, (p + 1) * LANES)
        o, _, _ = _pair_attention(mq_ref[0, :, lanes], mk_ref[0, :, lanes], mv_ref[0, :, lanes],
                                  None, None)
        a = hi + p * LANES
        mix[:, a:a + LANES] = (o * _silu(z[:, a:a + LANES])).astype(BF16)
    y_ref[0] = _gate_project_norm(x_ref[0], mix, wout_ref, gf_ref)


def _prompt_finish(x, conv_y, attn, mq, mk, mv, g_in, w_in_bf16, w_out_bf16, g_final, sample,
                   sample_base):
    B, T, D = x.shape
    tm = ROW_TILE

    def row(width):
        return pl.BlockSpec((1, tm, width), lambda b, i: (b, i, 0))

    def const(shape):
        return pl.BlockSpec(shape, lambda b, i: (0,) * len(shape))

    memspec = pl.BlockSpec((1,) + mk.shape[1:], lambda b, i: (b, 0, 0))
    in_specs = [row(D), row(CONV_W)]
    for _, dil in DILATED_CONFIGS:
        in_specs += [pl.BlockSpec((1, dil, tm // dil, ATTN_W), lambda b, i: (b, 0, i, 0)),
                     pl.BlockSpec((1, dil, tm // dil, LANES), lambda b, i: (b, 0, i, 0))]
    expander = jnp.asarray(_head_expander(), BF16)
    half = D_MIX // 2
    assert _C_Z % half == 0 and _C_END - _C_Z == 2 * half
    gate_cols = [pl.BlockSpec((D, half), lambda b, i, c=_C_Z // half + j: (0, c))
                 for j in range(2)]
    in_specs += ([row(MEM_W), memspec, memspec, const((1, D))] + gate_cols
                 + [const(w_out_bf16.shape), const((1, D)), const(expander.shape)])
    assert len(in_specs) == N_FINISH_IN
    o_slabs = pltpu.VMEM((ATTN_W // LANES, tm, LANES), F32)
    l_slabs = pltpu.VMEM((1, tm, LANES), F32)
    nt = T // tm
    s_args, s_in, s_out, s_shape = _sample_row_specs(sample, sample_base, B * nt,
                                                     lambda b, i: b * nt + i)
    return pl.pallas_call(
        functools.partial(_finish_kernel, sample_base=sample_base),
        grid=(B, nt),
        in_specs=in_specs + s_in,
        out_specs=tuple([row(D)] + s_out),
        out_shape=tuple([jax.ShapeDtypeStruct((B, T, D), F32)] + s_shape),
        scratch_shapes=[pltpu.VMEM((tm, D_MIX), BF16),
                        o_slabs, l_slabs, o_slabs, l_slabs, o_slabs, l_slabs],
        compiler_params=_params(2),
        name="prompt_finish",
    )(x, conv_y, *attn, mq, mk, mv, g_in.reshape(1, D), w_in_bf16, w_in_bf16, w_out_bf16,
      g_final.reshape(1, D), expander, *s_args)


def _sample_tables(S, n_buf):
    j = np.arange(S)[:, None]
    t = np.arange(n_buf)[None, :]
    dist = n_buf + j - t
    count = np.zeros(dist.shape)
    for win, dil in DILATED_CONFIGS:
        count += (dist % dil == 0) & (dist <= win)
    bias = np.where(count > 0, -_slopes()[:, None, None] * dist, -np.inf)
    bias = bias.reshape(ATTN_HEADS // 2, 2 * S, n_buf)
    count = np.concatenate([count, count], axis=0)
    return count.astype(np.float32), bias.astype(np.float32)


def _stack_pair(a, p):
    ap = a[:, p * LANES:(p + 1) * LANES]
    lo = lax.broadcasted_iota(jnp.int32, ap.shape, 1) < HEAD_DIM
    zero = jnp.zeros_like(ap)
    return jnp.concatenate([jnp.where(lo, ap, zero), jnp.where(lo, zero, ap)], axis=0)


def _unstack_pair(o, S):
    lo = lax.broadcasted_iota(jnp.int32, (S, LANES), 1) < HEAD_DIM
    return jnp.where(lo, o[:S], o[S:])


def _sample_attn_kernel(q_ref, kn_ref, vn_ref, kt_ref, vt_ref, mq_ref, mkt_ref, mvt_ref,
                        cnt_ref, bias_ref, o_ref, mo_ref, *, batch_row):
    R, S = kt_ref.shape[0], o_ref.shape[1]
    nt = (((1,), (1,)), ((), ()))
    slopes = _slopes()
    n_cfg = float(len(DILATED_CONFIGS))
    row = lax.broadcasted_iota(jnp.int32, (2 * S, 1), 0)
    upper = row >= S
    rowj = jnp.where(upper, row - S, row)
    cnt = cnt_ref[...]
    rows = range(R)
    which = batch_row % (SUBLANES // S)

    def tokens(ref):
        blk = ref[...]
        tok = blk[0:S]
        for i in range(1, SUBLANES // S):
            tok = jnp.where(which == i, blk[i * S:(i + 1) * S], tok)
        return tok

    q_tok, kn_tok, vn_tok, mq_tok = (tokens(ref) for ref in (q_ref, kn_ref, vn_ref, mq_ref))

    def lane_group(tok, p):
        return tok[:, p * LANES:(p + 1) * LANES]

    def pair_rows(ref, r, p):
        return ref[r, 2 * p:2 * p + 2].reshape(2 * HEAD_DIM, ref.shape[-1]).astype(BF16)

    units = [(r, p) for r in rows for p in range(ATTN_HEADS // 2)]
    munits = [(r, p) for r in rows for p in range(MEM_HEADS // 2)]
    qs = [_stack_pair(q_tok, p) for r, p in units]
    scores = [jnp.dot(q8.astype(BF16), pair_rows(kt_ref, r, p),
                      preferred_element_type=F32) + bias_ref[p] for (r, p), q8 in zip(units, qs)]
    mscores = [jnp.dot(_stack_pair(mq_tok, p).astype(BF16), pair_rows(mkt_ref, r, p),
                       preferred_element_type=F32) for r, p in munits]
    probs, stats = [], []
    for (r, p), q8, s in zip(units, qs, scores):
        knp = lane_group(kn_tok, p)
        slope = jnp.where(upper, float(slopes[2 * p + 1]), float(slopes[2 * p]))
        m = jnp.max(s, axis=-1, keepdims=True)
        s_new = []
        for jp in range(S):
            d = rowj - jp
            sj = jnp.sum(q8 * knp[jp:jp + 1], axis=-1, keepdims=True)
            sj = jnp.where(d >= 0, sj - slope * d.astype(F32), -jnp.inf)
            s_new.append(sj)
            m = jnp.maximum(m, sj)
        pr = cnt * jnp.exp(s - m)
        l = jnp.sum(pr, axis=-1, keepdims=True)
        w_new = []
        for jp in range(S):
            w = jnp.where(rowj == jp, n_cfg, 1.0) * jnp.exp(s_new[jp] - m)
            l = l + w
            w_new.append(w)
        probs.append(pr.astype(BF16))
        stats.append((l, w_new))
    mprobs, mden = [], []
    for s in mscores:
        pr = jnp.exp(s - jnp.max(s, axis=-1, keepdims=True))
        mden.append(jnp.sum(pr, axis=-1, keepdims=True))
        mprobs.append(pr.astype(BF16))
    accs = [lax.dot_general(pr, pair_rows(vt_ref, r, p), nt, preferred_element_type=F32)
            for (r, p), pr in zip(units, probs)]
    maccs = [lax.dot_general(pr, pair_rows(mvt_ref, r, p), nt, preferred_element_type=F32)
             for (r, p), pr in zip(munits, mprobs)]
    outs = []
    for (r, p), acc, (l, w_new) in zip(units, accs, stats):
        vnp = lane_group(vn_tok, p)
        for jp in range(S):
            acc = acc + w_new[jp] * vnp[jp:jp + 1]
        outs.append(_unstack_pair(acc / l, S))
    mouts = [_unstack_pair(acc / den, S) for acc, den in zip(maccs, mden)]
    n_p, n_mp = ATTN_HEADS // 2, MEM_HEADS // 2
    for r in rows:
        o_ref[r] = jnp.concatenate(outs[r * n_p:(r + 1) * n_p], axis=-1)
        mo_ref[r] = jnp.concatenate(mouts[r * n_mp:(r + 1) * n_mp], axis=-1)


N_SAMPLE_IN = 10


def _sample_row_specs(sample, base, n_rows, step_of):
    q, k_new, v_new, cache_kt, cache_vt, mq, mem_kt, mem_vt = sample
    DB, n_buf = cache_kt.shape[0], cache_kt.shape[-1]
    S, W = q.shape[0] // DB, q.shape[1]
    assert SUBLANES % S == 0, "token rows are fetched in 8-row blocks"
    assert n_buf >= DILATED_CONFIGS[-1][0] and S <= DILATED_CONFIGS[1][1], \
        "new tokens reach each other through dilation 1 only; the buffer covers every window"
    cnt, bias = (jnp.asarray(t) for t in _sample_tables(S, n_buf))
    per_block = SUBLANES // S

    def tokens_in(a):
        return pl.BlockSpec((SUBLANES, a.shape[1]),
                            lambda *g: ((base + step_of(*g)) // per_block, 0))

    def row_in(a):
        return pl.BlockSpec((1,) + a.shape[1:],
                            lambda *g: (base + step_of(*g),) + (0,) * (a.ndim - 1))

    def row_out(width):
        return pl.BlockSpec((1, S, width), lambda *g: (step_of(*g), 0, 0))

    def const(a):
        return pl.BlockSpec(a.shape, lambda *g: (0,) * a.ndim)

    args = list(sample) + [cnt, bias]
    assert len(args) == N_SAMPLE_IN
    in_specs = [tokens_in(q), tokens_in(k_new), tokens_in(v_new), row_in(cache_kt),
                row_in(cache_vt), tokens_in(mq), row_in(mem_kt), row_in(mem_vt),
                const(cnt), const(bias)]
    out_specs = [row_out(W), row_out(MEM_W)]
    out_shape = [jax.ShapeDtypeStruct((n_rows, S, W), F32),
                 jax.ShapeDtypeStruct((n_rows, S, MEM_W), F32)]
    return args, in_specs, out_specs, out_shape


def _sample_attention(sample, base, n_rows):
    args, in_specs, out_specs, out_shape = _sample_row_specs(sample, base, n_rows, lambda i: i)
    n_slot = 3
    in_specs[3] = in_specs[4] = pl.BlockSpec(memory_space=pl.ANY)
    row_shape = (n_slot,) + sample[3].shape[1:]

    def body(*refs):
        kbuf, vbuf, sem = refs[-3:]
        refs = list(refs[:-3])
        kt_hbm, vt_hbm = refs[3], refs[4]
        i = pl.program_id(0)

        def copies(row, slot):
            return (pltpu.make_async_copy(kt_hbm.at[pl.ds(base + row, 1)],
                                          kbuf.at[pl.ds(slot, 1)], sem.at[slot, 0]),
                    pltpu.make_async_copy(vt_hbm.at[pl.ds(base + row, 1)],
                                          vbuf.at[pl.ds(slot, 1)], sem.at[slot, 1]))

        @pl.when(i == 0)
        def _():
            for r in range(min(n_slot - 1, n_rows)):
                for c in copies(r, r):
                    c.start()

        @pl.when(i + n_slot - 1 < n_rows)
        def _():
            for c in copies(i + n_slot - 1, (i + n_slot - 1) % n_slot):
                c.start()

        slot = i % n_slot
        for c in copies(i, slot):
            c.wait()
        refs[3], refs[4] = kbuf.at[pl.ds(slot, 1)], vbuf.at[pl.ds(slot, 1)]
        _sample_attn_kernel(*refs, batch_row=base + i)

    return pl.pallas_call(
        body,
        grid=(n_rows,),
        in_specs=in_specs,
        out_specs=tuple(out_specs),
        out_shape=tuple(out_shape),
        scratch_shapes=[pltpu.VMEM(row_shape, F32), pltpu.VMEM(row_shape, F32),
                        pltpu.SemaphoreType.DMA((n_slot, 2))],
        compiler_params=_params(1),
        name="sample_attn",
    )(*args)


def _sample_finish_kernel(x_ref, cb_ref, u0_ref, u1_ref, u2_ref, attn_ref, mem_ref, z_ref,
                          cw_ref, wout_ref, gf_ref, y_ref, mix):
    cw = cw_ref[...]
    conv = cw[0:1] * u0_ref[...] + cw[1:2] * u1_ref[...] + cw[2:3] * u2_ref[...]
    z = z_ref[...].astype(F32)
    lo, hi = CONV_W, CONV_W + ATTN_W
    mix[:, 0:lo] = (cb_ref[...].astype(F32) * conv * _silu(z[:, 0:lo])).astype(BF16)
    mix[:, lo:hi] = (attn_ref[...] * _silu(z[:, lo:hi])).astype(BF16)
    mix[:, hi:] = (mem_ref[...] * _silu(z[:, hi:])).astype(BF16)
    y_ref[...] = _gate_project_norm(x_ref[...], mix, wout_ref, gf_ref)


def _sample_finish(x, cb, u_taps, attn, mem, z, conv_w, w_out_bf16, g_final):
    rows, D = x.shape
    return pl.pallas_call(
        _sample_finish_kernel,
        out_shape=jax.ShapeDtypeStruct((rows, D), F32),
        scratch_shapes=[pltpu.VMEM((rows, D_MIX), BF16)],
        compiler_params=pltpu.CompilerParams(vmem_limit_bytes=VMEM_LIMIT),
        name="sample_finish",
    )(x, cb, *u_taps, attn, mem, z, conv_w, w_out_bf16, g_final.reshape(1, D))


def kernel(x_prompt, x_sample, mem_prompt, cache_win_k, cache_win_v, cache_conv, cache_mem_k,
           cache_mem_v, g_in, w_in, conv_w, g_mem, w_mem_kv, w_out, g_final):
    assert g_in.shape[0] == 1, "the final RMSNorm is fused into the single layer's tail"
    B, T, D = x_prompt.shape
    DB, S, _ = x_sample.shape
    n_keep = min(DILATED_CONFIGS[-1][0], T)
    w_in_b = w_in[0].astype(BF16)
    w_out_b = w_out[0].astype(BF16)
    dils = tuple(d for _, d in DILATED_CONFIGS)
    wkvt_b = _transposed_columns(w_in[0], _C_K, _C_MQ)

    def heads_last(t, heads):
        return jnp.transpose(t.reshape(t.shape[0], heads, HEAD_DIM, -1), (0, 3, 1, 2))[None]

    def heads_first(c):
        return jnp.transpose(c, (0, 2, 3, 1))

    rows = DB * S
    cb_s, u_s, q_s, _, _, k32, v32, mq_s, z_s = _project(
        x_sample.reshape(1, rows, D), g_in[0], w_in_b, wkvt_b, rows, False, (1,), Q_SCALE)
    k32 = k32.reshape(rows, ATTN_W)
    v32 = v32.reshape(rows, ATTN_W)
    sample = (q_s[0].reshape(rows, ATTN_W).astype(F32), k32, v32,
              heads_first(cache_win_k[0]), heads_first(cache_win_v[0]),
              mq_s.reshape(rows, MEM_W).astype(F32),
              heads_first(cache_mem_k[0]), heads_first(cache_mem_v[0]))
    n_proj = n_tail = B * (T // ROW_TILE)
    n_attn = B * (T // ATTN_TILE)
    n_own = DB - n_proj - n_attn - n_tail
    assert n_own > 0
    conv_y, u_tails, q, k, v, kt32, vt32, mq, _, attn_p, mem_p = _project(
        x_prompt, g_in[0], w_in_b, wkvt_b, n_keep, True, dils, Q_SCALE * LOG2E, sample, 0,
        conv_w[0], emit_z=False)
    mkt32, mvt32, mk, mv = _mem_kv(mem_prompt, g_mem[0], w_mem_kv[0].astype(BF16))
    attn, attn_a, mem_a = _prompt_attention(q, k, v, sample, n_proj)
    y_prompt, attn_t, mem_t = _prompt_finish(x_prompt, conv_y, attn, mq, mk, mv, g_in[0],
                                             w_in_b, w_out_b, g_final, sample, n_proj + n_attn)
    attn_o, mem_o = _sample_attention(sample, n_proj + n_attn + n_tail, n_own)
    attn_s = jnp.concatenate([attn_p, attn_a, attn_t, attn_o], axis=0)
    mem_s = jnp.concatenate([mem_p, mem_a, mem_t, mem_o], axis=0)
    p_win_k = heads_last(kt32, ATTN_HEADS)
    p_win_v = heads_last(vt32, ATTN_HEADS)
    p_conv = u_tails[None, :, -(CONV_WIDTH - 1):]
    p_mem_k = heads_last(mkt32, MEM_HEADS)
    p_mem_v = heads_last(mvt32, MEM_HEADS)
    full = jnp.concatenate([cache_conv[0], u_s.reshape(DB, S, CONV_W)], axis=1)
    taps = [full[:, t:t + S].reshape(rows, CONV_W) for t in range(CONV_WIDTH)]
    y_sample = _sample_finish(x_sample.reshape(rows, D), cb_s.reshape(rows, CONV_W), taps,
                              attn_s.reshape(rows, ATTN_W), mem_s.reshape(rows, MEM_W),
                              z_s.reshape(rows, D_MIX), conv_w[0], w_out_b,
                              g_final).reshape(DB, S, D)
    s_win_k = k32.reshape(1, DB, S, ATTN_HEADS, HEAD_DIM)
    s_win_v = v32.reshape(1, DB, S, ATTN_HEADS, HEAD_DIM)
    s_conv = full[None, :, -(CONV_WIDTH - 1):]
    return (y_prompt, y_sample, p_win_k, p_win_v, p_conv, p_mem_k, p_mem_v,
            s_win_k, s_win_v, s_conv)
```
